```python
import jax
import jax.numpy as jnp
from jax import lax
import numpy as np


D_MODEL = 1024
BATCH = 8
SEQ = 4096
DEPTH = 4

GRID_W = 64
CTX_LEN = 256
N_MIXERS = 4
EPS = 1e-6
POOL_WINDOWS = (2, 4, 8, 16)
POOL_GROUPS = 4
POOL_GW = D_MODEL // POOL_GROUPS
LRU_WIDTH = D_MODEL
LRU_BLOCKS = 8
LRU_BW = LRU_WIDTH // LRU_BLOCKS
CONV_W = 4
LRU_C = 8.0
HEAD_DIM = 64
N_Q_HEADS = D_MODEL // HEAD_DIM
N_KV_HEADS = 4
Q_BLOCK = 128
ROPE_THETA = 10000.0
GMLP_HALF = 3 * D_MODEL
GMLP_GROUPS = 4
GMLP_GW = GMLP_HALF // GMLP_GROUPS
GMLP_CHUNK = 128
N_EXPERTS = 16
EXPERT_FF = D_MODEL
CAPACITY_FACTOR = 2

kernel_name = 'hybrid_interleaved_diffusion_block'


def _n_slots(m):
    return (DEPTH - m + N_MIXERS - 1) // N_MIXERS


def rms_norm(x, gain=None):
    xf = x.astype(jnp.float32)
    y = xf * lax.rsqrt(jnp.mean(xf * xf, axis=-1, keepdims=True) + EPS)
    if gain is not None:
        y = y * gain.astype(jnp.float32)
    return y.astype(x.dtype)


def layer_norm(x, g, b):
    xf = x.astype(jnp.float32)
    mu = jnp.mean(xf, axis=-1, keepdims=True)
    xc = xf - mu
    var = jnp.mean(xc * xc, axis=-1, keepdims=True)
    return (xc * lax.rsqrt(var + EPS) * g.astype(jnp.float32) + b.astype(jnp.float32)).astype(x.dtype)


def modulate(x, shift, scale):
    return rms_norm(x) * (1 + scale) + shift


def centred_window_mean(h, w):
    n = h.shape[1]
    cs = jnp.pad(jnp.cumsum(h.astype(jnp.float32), axis=1), ((0, 0), (1, 0), (0, 0)))
    t = jnp.arange(n)
    lo = jnp.clip(t - w // 2, 0, n)
    hi = jnp.clip(t + w - w // 2, 0, n)
    mean = (cs[:, hi] - cs[:, lo]) / (hi - lo).astype(jnp.float32)[None, :, None]
    return mean.astype(h.dtype)


def pool_mixer(h, w_pool, scale):
    b, n, d = h.shape
    hg = h.reshape(b, n, POOL_GROUPS, POOL_GW)
    pooled = jnp.stack([centred_window_mean(hg[:, :, g], w) for g, w in enumerate(POOL_WINDOWS)], axis=2) - hg
    y = jnp.einsum('bngc,gcd->bngd', pooled, w_pool).reshape(b, n, d)
    return y * scale


def centred_depthwise_conv(x, w, bias):
    n = x.shape[1]
    left = CONV_W // 2
    right = CONV_W - 1 - left
    xp = jnp.pad(x, ((0, 0), (left, right), (0, 0)))
    return sum(xp[:, k:k + n] * w[k] for k in range(CONV_W)) + bias


def _lin_combine(left, right):
    a1, b1 = left
    a2, b2 = right
    return a1 * a2, a2 * b1 + b2


def linear_scan(a, u, h0):
    a_cum, h = lax.associative_scan(_lin_combine, (a, u), axis=1)
    return h + a_cum * h0[:, None, :]


def rglru_gates(x, wa, ba, wx, bx, lam):
    b, n, w = x.shape
    xb = x.reshape(b, n, LRU_BLOCKS, LRU_BW)
    r = jax.nn.sigmoid((jnp.einsum('bnhi,hij->bnhj', xb, wa).reshape(b, n, w) + ba).astype(jnp.float32))
    i = jax.nn.sigmoid((jnp.einsum('bnhi,hij->bnhj', xb, wx).reshape(b, n, w) + bx).astype(jnp.float32))
    log_a = -LRU_C * r * jax.nn.softplus(-lam.astype(jnp.float32))
    a = jnp.exp(log_a)
    mult = jnp.sqrt(-jnp.expm1(2.0 * log_a))
    u = x.astype(jnp.float32) * i * mult
    return a, u


def rglru_mixer(h_lat, h_ctx, w_in, conv_w, conv_b, wa, ba, wx, bx, lam, w_out):
    def branches(h):
        z = h @ w_in
        y = jax.nn.gelu(z[..., :LRU_WIDTH])
        xr = centred_depthwise_conv(z[..., LRU_WIDTH:], conv_w, conv_b)
        return y, xr
    y_l, x_l = branches(h_lat)
    y_c, x_c = branches(h_ctx)
    zero = jnp.zeros((h_lat.shape[0], LRU_WIDTH), jnp.float32)
    out_l = 0.0
    out_c = 0.0
    for d in range(2):
        a_c, u_c = rglru_gates(x_c, wa[d], ba[d], wx[d], bx[d], lam[d])
        a_l, u_l = rglru_gates(x_l, wa[d], ba[d], wx[d], bx[d], lam[d])
        if d == 1:
            a_c, u_c, a_l, u_l = (jnp.flip(t, axis=1) for t in (a_c, u_c, a_l, u_l))
        hc = linear_scan(a_c, u_c, zero)
        hl = linear_scan(a_l, u_l, hc[:, -1])
        if d == 1:
            hc, hl = jnp.flip(hc, axis=1), jnp.flip(hl, axis=1)
        out_c = out_c + hc
        out_l = out_l + hl
    o_l = (out_l.astype(h_lat.dtype) * y_l) @ w_out
    o_c = (out_c.astype(h_ctx.dtype) * y_c) @ w_out
    return o_l, o_c


def rope_2d_tables(n):
    rows = n // GRID_W
    row = jnp.repeat(jnp.arange(rows), GRID_W).astype(jnp.float32)
    col = jnp.tile(jnp.arange(GRID_W), rows).astype(jnp.float32)
    n_freq = HEAD_DIM // 4
    inv = ROPE_THETA ** (-jnp.arange(n_freq, dtype=jnp.float32) / n_freq)
    ang = jnp.concatenate([row[:, None] * inv, col[:, None] * inv], axis=-1)
    return jnp.cos(ang), jnp.sin(ang)


def apply_rope(x, cos, sin):
    b, n, h, d = x.shape
    xp = x.astype(jnp.float32).reshape(b, n, h, d // 2, 2)
    x1, x2 = xp[..., 0], xp[..., 1]
    cs = cos[None, :, None, :]
    sn = sin[None, :, None, :]
    out = jnp.stack([x1 * cs - x2 * sn, x1 * sn + x2 * cs], axis=-1)
    return out.reshape(b, n, h, d).astype(x.dtype)


def attend(q, k, v):
    s = jnp.einsum('bqkgd,bskd->bkgqs', q, k, preferred_element_type=jnp.float32) * (HEAD_DIM ** -0.5)
    p = jax.nn.softmax(s, axis=-1).astype(v.dtype)
    return jnp.einsum('bkgqs,bskd->bqkgd', p, v)


def gqa_mixer(h_lat, h_ctx, w_qkv, q_gain, k_gain, w_o):
    b, n, _ = h_lat.shape
    n_ctx = h_ctx.shape[1]
    grp = N_Q_HEADS // N_KV_HEADS
    qd = N_Q_HEADS * HEAD_DIM
    kd = N_KV_HEADS * HEAD_DIM

    def project(h):
        t = h.shape[1]
        qkv = h @ w_qkv
        q = qkv[..., :qd].reshape(b, t, N_Q_HEADS, HEAD_DIM)
        k = qkv[..., qd:qd + kd].reshape(b, t, N_KV_HEADS, HEAD_DIM)
        v = qkv[..., qd + kd:].reshape(b, t, N_KV_HEADS, HEAD_DIM)
        return rms_norm(q, q_gain), rms_norm(k, k_gain), v

    q_l, k_l, v_l = project(h_lat)
    q_c, k_c, v_c = project(h_ctx)
    cos, sin = rope_2d_tables(n)
    q_l = apply_rope(q_l, cos, sin)
    k_l = apply_rope(k_l, cos, sin)
    k_all = jnp.concatenate([k_c, k_l], axis=1)
    v_all = jnp.concatenate([v_c, v_l], axis=1)
    q_blocks = q_l.reshape(b, n // Q_BLOCK, Q_BLOCK, N_KV_HEADS, grp, HEAD_DIM).transpose(1, 0, 2, 3, 4, 5)
    o_l = lax.map(lambda qb: attend(qb, k_all, v_all), q_blocks)
    o_l = o_l.transpose(1, 0, 2, 3, 4, 5).reshape(b, n, qd)
    o_c = attend(q_c.reshape(b, n_ctx, N_KV_HEADS, grp, HEAD_DIM), k_c, v_c).reshape(b, n_ctx, qd)
    return o_l @ w_o, o_c @ w_o


def gmlp_mixer(h, w_in, ln_g, ln_b, w_s, b_s, w_out):
    b, n, _ = h.shape
    z = jax.nn.gelu(h @ w_in)
    u, v = z[..., :GMLP_HALF], z[..., GMLP_HALF:]
    v = layer_norm(v, ln_g, ln_b).reshape(b, n // GMLP_CHUNK, GMLP_CHUNK, GMLP_GROUPS, GMLP_GW)
    v = jnp.einsum('gpq,bcqgf->bcpgf', w_s, v) + b_s.T[None, None, :, :, None]
    return (u * v.reshape(b, n, GMLP_HALF)) @ w_out


def expert_choice_ffn(h, w_router, w_gate, w_up, w_down):
    b, n, d = h.shape
    cap = CAPACITY_FACTOR * n // N_EXPERTS
    aff = jax.nn.softmax((h @ w_router).astype(jnp.float32), axis=-1)
    gates, idx = lax.top_k(jnp.swapaxes(aff, 1, 2), cap)
    xs = jax.vmap(lambda hb, ib: hb[ib])(h, idx)
    a = jnp.einsum('becd,edf->becf', xs, w_gate)
    up = jnp.einsum('becd,edf->becf', xs, w_up)
    y = jnp.einsum('becf,efd->becd', jax.nn.silu(a) * up, w_down) * gates.astype(h.dtype)[..., None]
    return jax.vmap(lambda ib, yb: jnp.zeros((n, d), yb.dtype).at[ib.reshape(-1)].add(yb.reshape(-1, d)))(idx, y)


def setup_inputs(seed: int = 0) -> dict:
    key = jax.random.key(seed)
    ks = iter(jax.random.split(key, 40))
    f32 = jnp.float32
    D = D_MODEL

    def nrm(shape, scale):
        return jax.random.normal(next(ks), shape, f32) * scale

    L0, L1, L2, L3 = (_n_slots(m) for m in range(N_MIXERS))
    u = jax.random.uniform(next(ks), (L1, 2, LRU_WIDTH), f32, 0.9, 0.999)
    a0 = u ** (1.0 / LRU_C)
    lru_lam = jnp.log(a0) - jnp.log1p(-a0)
    qkv_out = (N_Q_HEADS + 2 * N_KV_HEADS) * HEAD_DIM
    return {
        'x': nrm((BATCH, SEQ, D), 1.0),
        'c': nrm((BATCH, D), 1.0),
        'ctx': nrm((BATCH, CTX_LEN, D), 1.0),
        'c_ctx': nrm((D,), 1.0),
        'mod_w': nrm((DEPTH, D, 6 * D), 0.5 * D ** -0.5),
        'mod_b': nrm((DEPTH, 6 * D), 0.02),
        'pool_w': nrm((L0, POOL_GROUPS, POOL_GW, POOL_GW), POOL_GW ** -0.5),
        'pool_scale': 1.0 + nrm((L0, D), 0.02),
        'lru_w_in': nrm((L1, D, 2 * LRU_WIDTH), D ** -0.5),
        'lru_conv_w': nrm((L1, CONV_W, LRU_WIDTH), CONV_W ** -0.5),
        'lru_conv_b': nrm((L1, LRU_WIDTH), 0.02),
        'lru_wa': nrm((L1, 2, LRU_BLOCKS, LRU_BW, LRU_BW), LRU_BW ** -0.5),
        'lru_ba': nrm((L1, 2, LRU_WIDTH), 0.02),
        'lru_wx': nrm((L1, 2, LRU_BLOCKS, LRU_BW, LRU_BW), LRU_BW ** -0.5),
        'lru_bx': nrm((L1, 2, LRU_WIDTH), 0.02),
        'lru_lam': lru_lam,
        'lru_w_out': nrm((L1, LRU_WIDTH, D), LRU_WIDTH ** -0.5),
        'attn_w_qkv': nrm((L2, D, qkv_out), D ** -0.5),
        'attn_q_gain': 1.0 + nrm((L2, HEAD_DIM), 0.02),
        'attn_k_gain': 1.0 + nrm((L2, HEAD_DIM), 0.02),
        'attn_w_o': nrm((L2, N_Q_HEADS * HEAD_DIM, D), (N_Q_HEADS * HEAD_DIM) ** -0.5),
        'gmlp_w_in': nrm((L3, D, 2 * GMLP_HALF), D ** -0.5),
        'gmlp_ln_g': 1.0 + nrm((L3, GMLP_HALF), 0.02),
        'gmlp_ln_b': nrm((L3, GMLP_HALF), 0.02),
        'gmlp_w_s': nrm((L3, GMLP_GROUPS, GMLP_CHUNK, GMLP_CHUNK), GMLP_CHUNK ** -0.5),
        'gmlp_b_s': 1.0 + nrm((L3, GMLP_GROUPS, GMLP_CHUNK), 0.02),
        'gmlp_w_out': nrm((L3, GMLP_HALF, D), GMLP_HALF ** -0.5),
        'moe_router': nrm((DEPTH, D, N_EXPERTS), D ** -0.5),
        'moe_w_gate': nrm((DEPTH, N_EXPERTS, D, EXPERT_FF), D ** -0.5),
        'moe_w_up': nrm((DEPTH, N_EXPERTS, D, EXPERT_FF), D ** -0.5),
        'moe_w_down': nrm((DEPTH, N_EXPERTS, EXPERT_FF, D), EXPERT_FF ** -0.5),
        'final_gain': 1.0 + nrm((D,), 0.02),
    }


def reference(x, c, ctx, c_ctx, mod_w, mod_b, pool_w, pool_scale, lru_w_in, lru_conv_w, lru_conv_b,
              lru_wa, lru_ba, lru_wx, lru_bx, lru_lam, lru_w_out, attn_w_qkv, attn_q_gain, attn_k_gain,
              attn_w_o, gmlp_w_in, gmlp_ln_g, gmlp_ln_b, gmlp_w_s, gmlp_b_s, gmlp_w_out,
              moe_router, moe_w_gate, moe_w_up, moe_w_down, final_gain):
    xc = ctx
    silu_c = jax.nn.silu(c)
    silu_cc = jax.nn.silu(c_ctx)
    for i in range(DEPTH):
        m, j = i % N_MIXERS, i // N_MIXERS
        last = i == DEPTH - 1
        mod_l = (silu_c @ mod_w[i] + mod_b[i])[:, None, :]
        mod_c = (silu_cc @ mod_w[i] + mod_b[i])[None, None, :]
        sh1_l, sc1_l, g1_l, sh2_l, sc2_l, g2_l = jnp.split(mod_l, 6, axis=-1)
        sh1_c, sc1_c, g1_c, sh2_c, sc2_c, g2_c = jnp.split(mod_c, 6, axis=-1)
        h_l = modulate(x, sh1_l, sc1_l)
        need_ctx = (not last) or m in (1, 2)
        h_c = modulate(xc, sh1_c, sc1_c) if need_ctx else None
        o_c = None
        if m == 0:
            o_l = pool_mixer(h_l, pool_w[j], pool_scale[j])
            if not last:
                o_c = pool_mixer(h_c, pool_w[j], pool_scale[j])
        elif m == 1:
            o_l, o_c = rglru_mixer(h_l, h_c, lru_w_in[j], lru_conv_w[j], lru_conv_b[j], lru_wa[j], lru_ba[j],
                                   lru_wx[j], lru_bx[j], lru_lam[j], lru_w_out[j])
        elif m == 2:
            o_l, o_c = gqa_mixer(h_l, h_c, attn_w_qkv[j], attn_q_gain[j], attn_k_gain[j], attn_w_o[j])
        else:
            o_l = gmlp_mixer(h_l, gmlp_w_in[j], gmlp_ln_g[j], gmlp_ln_b[j], gmlp_w_s[j], gmlp_b_s[j], gmlp_w_out[j])
            if not last:
                o_c = gmlp_mixer(h_c, gmlp_w_in[j], gmlp_ln_g[j], gmlp_ln_b[j], gmlp_w_s[j], gmlp_b_s[j], gmlp_w_out[j])
        x = x + g1_l * o_l
        x = x + g2_l * expert_choice_ffn(modulate(x, sh2_l, sc2_l), moe_router[i], moe_w_gate[i], moe_w_up[i], moe_w_down[i])
        if not last:
            xc = xc + g1_c * o_c
            xc = xc + g2_c * expert_choice_ffn(modulate(xc, sh2_c, sc2_c), moe_router[i], moe_w_gate[i], moe_w_up[i], moe_w_down[i])
    return rms_norm(x, final_gain)
```

```python
import functools

import jax
import jax.numpy as jnp
from jax import lax
from jax.experimental import pallas as pl
from jax.experimental.pallas import tpu as pltpu

F32 = jnp.float32
BF16 = jnp.bfloat16
EPS = 1e-6

N_EXPERTS = 16
CAPACITY_FACTOR = 2
POOL_WINDOWS = (2, 4, 8, 16)
POOL_HALO = 8
LRU_BLOCKS = 8
LRU_C = 8.0
CONV_W = 4
HEAD_DIM = 64
N_KV_HEADS = 4
GRID_W = 64
ROPE_THETA = 10000.0
GMLP_GROUPS = 4
GMLP_CHUNK = 128

LANES = 128
VMEM_LIMIT = 56 * 1024 * 1024


def _cparams(*sem):
    return pltpu.CompilerParams(dimension_semantics=sem, vmem_limit_bytes=VMEM_LIMIT)


def _modulate(x, shift, scale):
    ms = jnp.mean(x * x, axis=-1, keepdims=True)
    return x * lax.rsqrt(ms + EPS) * (1.0 + scale) + shift


def _split_bf16(a):
    hi = a.astype(BF16)
    lo = (a - hi.astype(F32)).astype(BF16)
    return hi, lo


def _dot(a, b):
    return jnp.dot(a, b, preferred_element_type=F32)


def _dot3(a, b):
    a_hi, a_lo = _split_bf16(a)
    b_hi, b_lo = _split_bf16(b)
    return _dot(a_hi, b_hi) + (_dot(a_hi, b_lo) + _dot(a_lo, b_hi))


def _gelu(x):
    return 0.5 * x * (1.0 + jnp.tanh(0.7978845608028654 * (x + 0.044715 * (x * x * x))))


def _silu(x):
    return x * (1.0 / (1.0 + jnp.exp(-x)))


def _sigmoid(x):
    return 1.0 / (1.0 + jnp.exp(-x))


def _mod_spec(mod, tiles_per_batch):
    d = mod.shape[-1]
    if mod.shape[0] == 1:
        return pl.BlockSpec((1, 1, d), lambda i, *_: (0, 0, 0))
    return pl.BlockSpec((1, 1, d), lambda i, *_: (i // tiles_per_batch, 0, 0))


def _row_tile(n, want):
    t = min(n, want)
    assert n % t == 0
    return t


def _mod_kernel(c_ref, w_ref, b_ref, o_ref):
    s = _silu(c_ref[...])
    o_ref[0] = _dot3(s, w_ref[0]) + b_ref[0]


def mod_vectors(cvec, mod_w, mod_b):
    depth, d, n6 = mod_w.shape
    tn = 1024
    return pl.pallas_call(
        _mod_kernel,
        out_shape=jax.ShapeDtypeStruct((depth, cvec.shape[0], n6), F32),
        grid=(depth, n6 // tn),
        in_specs=[
            pl.BlockSpec(cvec.shape, lambda l, j: (0, 0)),
            pl.BlockSpec((1, d, tn), lambda l, j: (l, 0, j)),
            pl.BlockSpec((1, 1, tn), lambda l, j: (l, 0, j)),
        ],
        out_specs=pl.BlockSpec((1, cvec.shape[0], tn), lambda l, j: (l, 0, j)),
        compiler_params=_cparams("parallel", "parallel"),
        name="mod_vectors",
    )(cvec, mod_w, mod_b.reshape(depth, 1, n6))


def _modmm_kernel(x_ref, sh_ref, sc_ref, w_ref, *o_refs, splits, acts):
    h = _modulate(x_ref[...], sh_ref[0], sc_ref[0]).astype(BF16)
    z = _dot(h, w_ref[...])
    off = 0
    for o_ref, width, act in zip(o_refs, splits, acts):
        part = z[:, off:off + width]
        if act == "gelu":
            part = _gelu(part)
        o_ref[...] = part.astype(o_ref.dtype)
        off += width


def modmm(x, shift, scale, w, n_tok, splits, acts, dtypes, tm=512):
    r, d = x.shape
    tm = _row_tile(n_tok, tm)
    tpb = n_tok // tm
    n = w.shape[1]
    assert sum(splits) == n
    return pl.pallas_call(
        functools.partial(_modmm_kernel, splits=splits, acts=acts),
        out_shape=[jax.ShapeDtypeStruct((r, s), dt) for s, dt in zip(splits, dtypes)],
        grid=(r // tm,),
        in_specs=[
            pl.BlockSpec((tm, d), lambda i: (i, 0)),
            _mod_spec(shift, tpb),
            _mod_spec(scale, tpb),
            pl.BlockSpec((d, n), lambda i: (0, 0)),
        ],
        out_specs=[pl.BlockSpec((tm, s), lambda i: (i, 0)) for s in splits],
        compiler_params=_cparams("parallel"),
        name="modmm",
    )(x, shift, scale, w)


def _mmres_kernel(*refs, n_sum, has_mul):
    a = refs[0][...]
    for a_ref in refs[1:n_sum]:
        a = a + a_ref[...]
    refs = refs[n_sum:]
    if has_mul:
        a = a.astype(F32) * refs[0][...].astype(F32)
        refs = refs[1:]
    w_ref, x_ref, g_ref, o_ref = refs
    o_ref[...] = x_ref[...] + g_ref[0] * _dot(a.astype(BF16), w_ref[...])


def mm_residual(a, w, x, gate, n_tok, b=None, tm=512):
    a = a if isinstance(a, (tuple, list)) else (a,)
    r, k = a[0].shape
    n = w.shape[1]
    tm = _row_tile(n_tok, tm)
    tpb = n_tok // tm
    ins = list(a) + ([b] if b is not None else []) + [w, x, gate]
    specs = [pl.BlockSpec((tm, k), lambda i: (i, 0))] * (len(a) + (b is not None))
    specs += [
        pl.BlockSpec((k, n), lambda i: (0, 0)),
        pl.BlockSpec((tm, n), lambda i: (i, 0)),
        _mod_spec(gate, tpb),
    ]
    return pl.pallas_call(
        functools.partial(_mmres_kernel, n_sum=len(a), has_mul=b is not None),
        out_shape=jax.ShapeDtypeStruct((r, n), F32),
        grid=(r // tm,),
        in_specs=specs,
        out_specs=pl.BlockSpec((tm, n), lambda i: (i, 0)),
        compiler_params=_cparams("parallel"),
        name="mm_residual",
    )(*ins)


def _pool_kernel(xp_ref, x_ref, xn_ref, sh_ref, sc_ref, g_ref, w_ref, ps_ref, o_ref, *, tm, n_tok, tpb):
    t0 = (pl.program_id(0) % tpb) * tm
    x = x_ref[...]
    xe = jnp.concatenate([xp_ref[...], x, xn_ref[...]], axis=0)
    rows = tm + 2 * POOL_HALO
    h = _modulate(xe, sh_ref[0], sc_ref[0])
    pos = lax.broadcasted_iota(jnp.int32, (rows, 1), 0) + (t0 - POOL_HALO)
    h = jnp.where((pos >= 0) & (pos < n_tok), h, 0.0)
    posc = pos[POOL_HALO:POOL_HALO + tm]
    gw = h.shape[1] // len(POOL_WINDOWS)
    outs = []
    for g, win in enumerate(POOL_WINDOWS):
        hg = h[:, g * gw:(g + 1) * gw]
        c = hg + pltpu.roll(hg, 1, 0)
        step = 1
        while 2 * step < win:
            c = pltpu.roll(c, step, 0) + pltpu.roll(c, rows - step, 0)
            step *= 2
        cnt = jnp.minimum(posc + (win - win // 2), n_tok) - jnp.maximum(posc - win // 2, 0)
        pooled = c[POOL_HALO:POOL_HALO + tm] / cnt.astype(F32) - hg[POOL_HALO:POOL_HALO + tm]
        outs.append(_dot(pooled.astype(BF16), w_ref[g]))
    y = jnp.concatenate(outs, axis=1) * ps_ref[...]
    o_ref[...] = x + g_ref[0] * y


def pool_mixer_residual(x, shift, scale, gate, w_pool, pool_scale, n_tok, tm=256):
    r, d = x.shape
    tm = _row_tile(n_tok, tm)
    tpb = n_tok // tm
    hb = tm // POOL_HALO
    last = r // POOL_HALO - 1
    groups, gw, _ = w_pool.shape
    return pl.pallas_call(
        functools.partial(_pool_kernel, tm=tm, n_tok=n_tok, tpb=tpb),
        out_shape=jax.ShapeDtypeStruct((r, d), F32),
        grid=(r // tm,),
        in_specs=[
            pl.BlockSpec((POOL_HALO, d), lambda i: (jnp.maximum(i * hb - 1, 0), 0)),
            pl.BlockSpec((tm, d), lambda i: (i, 0)),
            pl.BlockSpec((POOL_HALO, d), lambda i: (jnp.minimum((i + 1) * hb, last), 0)),
            _mod_spec(shift, tpb),
            _mod_spec(scale, tpb),
            _mod_spec(gate, tpb),
            pl.BlockSpec((groups, gw, gw), lambda i: (0, 0, 0)),
            pl.BlockSpec((1, d), lambda i: (0, 0)),
        ],
        out_specs=pl.BlockSpec((tm, d), lambda i: (i, 0)),
        compiler_params=_cparams("parallel"),
        name="pool_mixer",
    )(x, x, x, shift, scale, gate, w_pool, pool_scale.reshape(1, d))


def _router_kernel(x_ref, sh_ref, sc_ref, wr_ref, h_ref, aff_ref):
    h = _modulate(x_ref[...], sh_ref[0], sc_ref[0])
    h_ref[...] = h.astype(h_ref.dtype)
    logits = _dot3(h, wr_ref[...])
    lane = lax.broadcasted_iota(jnp.int32, logits.shape, 1)
    logits = jnp.where(lane < N_EXPERTS, logits, -jnp.inf)
    e = jnp.exp(logits - jnp.max(logits, axis=-1, keepdims=True))
    aff_ref[...] = e / jnp.sum(e, axis=-1, keepdims=True)


def router(x, shift, scale, w_router_padded, n_tok, tm=512):
    r, d = x.shape
    tm = _row_tile(n_tok, tm)
    tpb = n_tok // tm
    return pl.pallas_call(
        _router_kernel,
        out_shape=[jax.ShapeDtypeStruct((r, d), BF16), jax.ShapeDtypeStruct((r, LANES), F32)],
        grid=(r // tm,),
        in_specs=[
            pl.BlockSpec((tm, d), lambda i: (i, 0)),
            _mod_spec(shift, tpb),
            _mod_spec(scale, tpb),
            pl.BlockSpec((d, LANES), lambda i: (0, 0)),
        ],
        out_specs=[pl.BlockSpec((tm, d), lambda i: (i, 0)), pl.BlockSpec((tm, LANES), lambda i: (i, 0))],
        compiler_params=_cparams("parallel"),
        name="moe_router",
    )(x, shift, scale, w_router_padded)


def _cumsum_rows(v, tb):
    n = v.shape[0]
    tri = (lax.broadcasted_iota(jnp.int32, (tb, tb), 0) >= lax.broadcasted_iota(jnp.int32, (tb, tb), 1)).astype(BF16)
    carry = jnp.zeros((1, v.shape[1]), F32)
    outs = []
    for j in range(n // tb):
        c = _dot(tri, v[j * tb:(j + 1) * tb].astype(BF16)) + carry
        carry = c[tb - 1:tb, :]
        outs.append(c)
    return jnp.concatenate(outs, axis=0)


def _route_kernel(aff_ref, idx_ref, gate_ref, pos_scr, sel_scr, *, n, cap, tb):
    def bisect(k, prefix):
        cand = prefix | jnp.left_shift(jnp.int32(1), 30 - k)
        cnt = jnp.sum((pltpu.bitcast(aff_ref[...], jnp.int32) >= cand).astype(F32), axis=0, keepdims=True)
        return jnp.where(cnt >= cap, cand, prefix)

    thr = lax.fori_loop(0, 31, bisect, jnp.zeros((1, LANES), jnp.int32))
    bits = pltpu.bitcast(aff_ref[...], jnp.int32)
    gt = bits > thr
    eq = bits == thr
    need = cap - jnp.sum(gt.astype(F32), axis=0, keepdims=True)
    eq_rank = _cumsum_rows(eq.astype(F32), tb)
    sel = (gt | (eq & (eq_rank <= need))).astype(F32)
    sel_scr[...] = sel
    pos_scr[...] = _cumsum_rows(sel, tb)

    slot = lax.broadcasted_iota(jnp.int32, (1, cap), 1).astype(F32)
    for e in range(N_EXPERTS):
        def blk(j, carry):
            acc_i, acc_g = carry
            rows = pl.ds(pl.multiple_of(j * tb, tb), tb)
            p = pos_scr[rows, e:e + 1]
            s = sel_scr[rows, e:e + 1]
            a = aff_ref[rows, e:e + 1]
            acc_i = acc_i + jnp.sum((p <= slot).astype(F32), axis=0, keepdims=True)
            hit = (p == slot + 1.0) & (s > 0.0)
            acc_g = acc_g + jnp.sum(jnp.where(hit, a, 0.0), axis=0, keepdims=True)
            return acc_i, acc_g

        zero = jnp.zeros((1, cap), F32)
        acc_i, acc_g = lax.fori_loop(0, n // tb, blk, (zero, zero))
        idx_ref[0, e:e + 1, :] = acc_i.astype(jnp.int32)
        gate_ref[0, e:e + 1, :] = acc_g


def route(aff, n_tok):
    r = aff.shape[0]
    b = r // n_tok
    cap = CAPACITY_FACTOR * n_tok // N_EXPERTS
    tb = min(n_tok, 256)
    return pl.pallas_call(
        functools.partial(_route_kernel, n=n_tok, cap=cap, tb=tb),
        out_shape=[jax.ShapeDtypeStruct((b, N_EXPERTS, cap), jnp.int32), jax.ShapeDtypeStruct((b, N_EXPERTS, cap), F32)],
        grid=(b,),
        in_specs=[pl.BlockSpec((n_tok, LANES), lambda i: (i, 0))],
        out_specs=[pl.BlockSpec((1, N_EXPERTS, cap), lambda i: (i, 0, 0))] * 2,
        scratch_shapes=[pltpu.VMEM((n_tok, LANES), F32), pltpu.VMEM((n_tok, LANES), F32)],
        compiler_params=_cparams("parallel"),
        name="moe_route",
    )(aff)


def _ffn_kernel(x_ref, g_ref, wg_ref, wu_ref, wd_ref, o_ref):
    x = x_ref[0]
    a = _dot(x, wg_ref[0])
    u = _dot(x, wu_ref[0])
    hmid = (_silu(a) * u).astype(BF16)
    o_ref[0] = _dot(hmid, wd_ref[0]) * g_ref[0]


def expert_ffn(xs, gates, w_gate, w_up, w_down, tm=512):
    e, m, d = xs.shape
    ff = w_gate.shape[-1]
    tm = _row_tile(m, tm)
    return pl.pallas_call(
        _ffn_kernel,
        out_shape=jax.ShapeDtypeStruct((e, m, d), F32),
        grid=(e, m // tm),
        in_specs=[
            pl.BlockSpec((1, tm, d), lambda k, i: (k, i, 0)),
            pl.BlockSpec((1, tm, 1), lambda k, i: (k, i, 0)),
            pl.BlockSpec((1, d, ff), lambda k, i: (k, 0, 0)),
            pl.BlockSpec((1, d, ff), lambda k, i: (k, 0, 0)),
            pl.BlockSpec((1, ff, d), lambda k, i: (k, 0, 0)),
        ],
        out_specs=pl.BlockSpec((1, tm, d), lambda k, i: (k, i, 0)),
        compiler_params=_cparams("parallel", "parallel"),
        name="moe_expert_ffn",
    )(xs, gates, w_gate, w_up, w_down)


def moe_residual(x, shift, scale, gate2, w_router_padded, w_gate, w_up, w_down, n_tok):
    r, d = x.shape
    b = r // n_tok
    h, aff = router(x, shift, scale, w_router_padded, n_tok)
    idx, gates = route(aff, n_tok)
    cap = idx.shape[-1]
    gidx = idx + (jnp.arange(b, dtype=jnp.int32) * n_tok)[:, None, None]
    gidx = jnp.transpose(gidx, (1, 0, 2)).reshape(N_EXPERTS, b * cap)
    gts = jnp.transpose(gates, (1, 0, 2)).reshape(N_EXPERTS, b * cap, 1)
    xs = jnp.take(h, gidx, axis=0)
    y = expert_ffn(xs, gts, w_gate, w_up, w_down)
    moe = jnp.zeros((r, d), F32).at[gidx.reshape(-1)].add(y.reshape(-1, d))
    g2 = jnp.broadcast_to(gate2, (b, 1, d))
    return (x.reshape(b, n_tok, d) + g2 * moe.reshape(b, n_tok, d)).reshape(r, d)


def _head_rms(t, seg_ones):
    outs = []
    for j in range(t.shape[1] // LANES):
        blk = t[:, j * LANES:(j + 1) * LANES]
        hi, lo = _split_bf16(blk * blk)
        ss = _dot(hi, seg_ones) + _dot(lo, seg_ones)
        outs.append(blk * lax.rsqrt(ss * (1.0 / HEAD_DIM) + EPS))
    return jnp.concatenate(outs, axis=1)


def _rope(t, cos, sin_signed):
    w = t.shape[1]
    half = HEAD_DIM // 2
    lane = lax.broadcasted_iota(jnp.int32, t.shape, 1)
    partner = jnp.where(lane % HEAD_DIM < half, pltpu.roll(t, w - half, 1), pltpu.roll(t, half, 1))
    reps = w // cos.shape[1]
    return t * jnp.concatenate([cos] * reps, axis=1) + partner * jnp.concatenate([sin_signed] * reps, axis=1)


def _qkv_kernel(*refs, qd, kd, rope):
    if rope:
        x_ref, sh_ref, sc_ref, w_ref, qg_ref, kg_ref, cos_ref, sin_ref, q_ref, k_ref, v_ref = refs
    else:
        x_ref, sh_ref, sc_ref, w_ref, qg_ref, kg_ref, q_ref, k_ref, v_ref = refs
    h = _modulate(x_ref[...], sh_ref[0], sc_ref[0]).astype(BF16)
    z = _dot(h, w_ref[...])
    seg = (lax.broadcasted_iota(jnp.int32, (LANES, LANES), 0) // HEAD_DIM
           == lax.broadcasted_iota(jnp.int32, (LANES, LANES), 1) // HEAD_DIM).astype(BF16)
    q = _head_rms(z[:, :qd], seg) * qg_ref[...]
    k = _head_rms(z[:, qd:qd + kd], seg) * kg_ref[...]
    if rope:
        q = _rope(q, cos_ref[...], sin_ref[...])
        k = _rope(k, cos_ref[...], sin_ref[...])
    q_ref[...] = (q * (HEAD_DIM ** -0.5)).astype(q_ref.dtype)
    k_ref[...] = k.astype(k_ref.dtype)
    v_ref[...] = z[:, qd + kd:].astype(v_ref.dtype)


def qkv_project(x, shift, scale, w_qkv, q_gain, k_gain, n_tok, rope_tables=None, tm=256):
    r, d = x.shape
    kd = N_KV_HEADS * HEAD_DIM
    qd = w_qkv.shape[1] - 2 * kd
    tm = _row_tile(n_tok, tm)
    tpb = n_tok // tm
    ins = [x, shift, scale, w_qkv, q_gain, k_gain]
    specs = [
        pl.BlockSpec((tm, d), lambda i: (i, 0)),
        _mod_spec(shift, tpb),
        _mod_spec(scale, tpb),
        pl.BlockSpec(w_qkv.shape, lambda i: (0, 0)),
        pl.BlockSpec((1, qd), lambda i: (0, 0)),
        pl.BlockSpec((1, kd), lambda i: (0, 0)),
    ]
    if rope_tables is not None:
        ins += list(rope_tables)
        specs += [pl.BlockSpec((tm, LANES), lambda i: (i % tpb, 0))] * 2
    return pl.pallas_call(
        functools.partial(_qkv_kernel, qd=qd, kd=kd, rope=rope_tables is not None),
        out_shape=[jax.ShapeDtypeStruct((r, qd), BF16), jax.ShapeDtypeStruct((r, kd), BF16),
                   jax.ShapeDtypeStruct((r, kd), BF16)],
        grid=(r // tm,),
        in_specs=specs,
        out_specs=[pl.BlockSpec((tm, qd), lambda i: (i, 0)), pl.BlockSpec((tm, kd), lambda i: (i, 0)),
                   pl.BlockSpec((tm, kd), lambda i: (i, 0))],
        compiler_params=_cparams("parallel"),
        name="qkv_project",
    )(*ins)


def _attn_kernel(q_ref, k_ref, v_ref, o_ref, *, grp):
    q = q_ref[...]
    qs = jnp.concatenate([q[:, h * HEAD_DIM:(h + 1) * HEAD_DIM] for h in range(grp)], axis=0)
    s = lax.dot_general(qs, k_ref[0, 0], (((1,), (1,)), ((), ())), preferred_element_type=F32)
    p = jnp.exp(s - jnp.max(s, axis=-1, keepdims=True))
    l = jnp.sum(p, axis=-1, keepdims=True)
    o = _dot(p.astype(BF16), v_ref[0, 0]) / l
    tq = q.shape[0]
    o_ref[...] = jnp.concatenate([o[h * tq:(h + 1) * tq] for h in range(grp)], axis=1).astype(o_ref.dtype)


def attention(q, k, v, n_q, tq=128):
    r, qd = q.shape
    _, kvh, n_k, hd = k.shape
    grp = qd // (kvh * hd)
    tq = _row_tile(n_q, tq)
    tpb = n_q // tq
    return pl.pallas_call(
        functools.partial(_attn_kernel, grp=grp),
        out_shape=jax.ShapeDtypeStruct((r, qd), BF16),
        grid=(r // n_q, kvh, tpb),
        in_specs=[
            pl.BlockSpec((tq, grp * hd), lambda b, g, i: (b * tpb + i, g)),
            pl.BlockSpec((1, 1, n_k, hd), lambda b, g, i: (b, g, 0, 0)),
            pl.BlockSpec((1, 1, n_k, hd), lambda b, g, i: (b, g, 0, 0)),
        ],
        out_specs=pl.BlockSpec((tq, grp * hd), lambda b, g, i: (b * tpb + i, g)),
        compiler_params=_cparams("parallel", "parallel", "parallel"),
        name="attention",
    )(q, k, v)


def rope_tables(n_tok):
    t = jnp.arange(n_tok)
    row = (t // GRID_W).astype(F32)
    col = (t % GRID_W).astype(F32)
    n_freq = HEAD_DIM // 4
    inv = ROPE_THETA ** (-jnp.arange(n_freq, dtype=F32) / n_freq)
    ang = jnp.concatenate([row[:, None] * inv, col[:, None] * inv], axis=-1)
    cos, sin = jnp.cos(ang), jnp.sin(ang)
    reps = LANES // HEAD_DIM
    return jnp.tile(jnp.concatenate([cos, cos], -1), (1, reps)), jnp.tile(jnp.concatenate([-sin, sin], -1), (1, reps))


def _deinterleave_heads(w, n_heads):
    lead = w.shape[:-1]
    w = w.reshape(lead + (n_heads, HEAD_DIM // 2, 2))
    return jnp.swapaxes(w, -1, -2).reshape(lead + (n_heads * HEAD_DIM,))


def _split_kv_heads(t, b):
    return jnp.transpose(t.reshape(b, -1, N_KV_HEADS, HEAD_DIM), (0, 2, 1, 3))


def gqa_residual(x, xc, mods_l, mods_c, w_qkv, q_gain, k_gain, w_o, n_tok, n_ctx):
    b = x.shape[0] // n_tok
    kd = N_KV_HEADS * HEAD_DIM
    qd = w_qkv.shape[1] - 2 * kd
    n_qh = qd // HEAD_DIM
    w_perm = jnp.concatenate([_deinterleave_heads(w_qkv[:, :qd], n_qh),
                              _deinterleave_heads(w_qkv[:, qd:qd + kd], N_KV_HEADS), w_qkv[:, qd + kd:]], axis=1)
    w_perm = w_perm.astype(BF16)
    qg = jnp.tile(_deinterleave_heads(q_gain, 1), n_qh).reshape(1, qd)
    kg = jnp.tile(_deinterleave_heads(k_gain, 1), N_KV_HEADS).reshape(1, kd)
    sh_l, sc_l, g_l = mods_l
    sh_c, sc_c, g_c = mods_c
    q_l, k_l, v_l = qkv_project(x, sh_l, sc_l, w_perm, qg, kg, n_tok, rope_tables(n_tok))
    q_c, k_c, v_c = qkv_project(xc, sh_c, sc_c, w_perm, qg, kg, n_ctx)
    k_c4, v_c4 = _split_kv_heads(k_c, b), _split_kv_heads(v_c, b)
    k_all = jnp.concatenate([k_c4, _split_kv_heads(k_l, b)], axis=2)
    v_all = jnp.concatenate([v_c4, _split_kv_heads(v_l, b)], axis=2)
    o_l = attention(q_l, k_all, v_all, n_tok)
    o_c = attention(q_c, k_c4, v_c4, n_ctx)
    w_o = w_o.astype(BF16)
    return mm_residual(o_l, w_o, x, g_l, n_tok), mm_residual(o_c, w_o, xc, g_c, n_ctx)


SUBLANES = 8


def _lru_kernel(xp_ref, x_ref, xn_ref, cw_ref, cb_ref, wa_ref, ba_ref, wx_ref, bx_ref, lam_ref, h0_ref,
                o_ref, hT_ref, a_scr, u_scr, carry_scr, *, tm, n_tok, tpb, reverse):
    step = pl.program_id(1)
    t_idx = (tpb - 1 - step) if reverse else step
    t0 = t_idx * tm
    rows = tm + 2 * SUBLANES

    @pl.when(step == 0)
    def _():
        carry_scr[...] = h0_ref[0]

    xe = jnp.concatenate([xp_ref[...], x_ref[...], xn_ref[...]], axis=0)
    pos = lax.broadcasted_iota(jnp.int32, (rows, 1), 0) + (t0 - SUBLANES)
    xe = jnp.where((pos >= 0) & (pos < n_tok), xe, 0.0)
    left = CONV_W // 2
    conv = cb_ref[...]
    for k in range(CONV_W):
        shift = (left - k) % rows
        tap = xe if shift == 0 else pltpu.roll(xe, shift, 0)
        conv = conv + tap * cw_ref[k:k + 1, :]
    xr = conv[SUBLANES:SUBLANES + tm]

    xb = xr.astype(BF16)
    bw = xr.shape[1] // LRU_BLOCKS
    ra, ia = [], []
    for j in range(LRU_BLOCKS):
        blk = xb[:, j * bw:(j + 1) * bw]
        ra.append(_dot(blk, wa_ref[j]))
        ia.append(_dot(blk, wx_ref[j]))
    r = _sigmoid(jnp.concatenate(ra, axis=1) + ba_ref[...])
    i = _sigmoid(jnp.concatenate(ia, axis=1) + bx_ref[...])
    log_a = -LRU_C * r * jnp.logaddexp(-lam_ref[...], 0.0)
    a = jnp.exp(log_a)
    a_scr[...] = a
    t = jnp.tanh(log_a)
    u_scr[...] = xr * i * jnp.sqrt(-2.0 * t / (1.0 - t))

    n_grp = tm // SUBLANES
    sub = lax.broadcasted_iota(jnp.int32, (SUBLANES, 1), 0)

    def group(j, carry):
        g = (n_grp - 1 - j) if reverse else j
        rws = pl.ds(pl.multiple_of(g * SUBLANES, SUBLANES), SUBLANES)
        ag, ug = a_scr[rws, :], u_scr[rws, :]
        s = 1
        while s < SUBLANES:
            if reverse:
                ok = sub < SUBLANES - s
                sh = SUBLANES - s
            else:
                ok = sub >= s
                sh = s
            u_prev = jnp.where(ok, pltpu.roll(ug, sh, 0), 0.0)
            a_prev = jnp.where(ok, pltpu.roll(ag, sh, 0), 1.0)
            ug = ug + ag * u_prev
            ag = ag * a_prev
            s *= 2
        hg = ug + ag * carry
        o_ref[rws, :] = hg
        return hg[0:1, :] if reverse else hg[SUBLANES - 1:SUBLANES, :]

    carry = lax.fori_loop(0, n_grp, group, carry_scr[...])
    carry_scr[...] = carry
    hT_ref[0] = carry


def lru_scan(xpre, conv_w, conv_b, wa, ba, wx, bx, lam, h0, n_tok, reverse, tm=256):
    r, w = xpre.shape
    b = r // n_tok
    tm = _row_tile(n_tok, tm)
    tpb = n_tok // tm
    hb = tm // SUBLANES
    last = r // SUBLANES - 1

    def tile(bi, s):
        return bi * tpb + ((tpb - 1 - s) if reverse else s)

    vec = pl.BlockSpec((1, w), lambda bi, s: (0, 0))
    blocks = pl.BlockSpec(wa.shape, lambda bi, s: (0, 0, 0))
    return pl.pallas_call(
        functools.partial(_lru_kernel, tm=tm, n_tok=n_tok, tpb=tpb, reverse=reverse),
        out_shape=[jax.ShapeDtypeStruct((r, w), F32), jax.ShapeDtypeStruct((b, 1, w), F32)],
        grid=(b, tpb),
        in_specs=[
            pl.BlockSpec((SUBLANES, w), lambda bi, s: (jnp.maximum(tile(bi, s) * hb - 1, 0), 0)),
            pl.BlockSpec((tm, w), lambda bi, s: (tile(bi, s), 0)),
            pl.BlockSpec((SUBLANES, w), lambda bi, s: (jnp.minimum((tile(bi, s) + 1) * hb, last), 0)),
            pl.BlockSpec((CONV_W, w), lambda bi, s: (0, 0)),
            vec, blocks, vec, blocks, vec, vec,
            pl.BlockSpec((1, 1, w), lambda bi, s: (bi, 0, 0)),
        ],
        out_specs=[pl.BlockSpec((tm, w), lambda bi, s: (tile(bi, s), 0)),
                   pl.BlockSpec((1, 1, w), lambda bi, s: (bi, 0, 0))],
        scratch_shapes=[pltpu.VMEM((tm, w), F32), pltpu.VMEM((tm, w), F32), pltpu.VMEM((1, w), F32)],
        compiler_params=_cparams("parallel", "arbitrary"),
        name="lru_scan",
    )(xpre, xpre, xpre, conv_w, conv_b.reshape(1, w), wa, ba.reshape(1, w), wx, bx.reshape(1, w),
      lam.reshape(1, w), h0)


def rglru_residual(x, xc, mods_l, mods_c, w_in, conv_w, conv_b, wa, ba, wx, bx, lam, w_out, n_tok, n_ctx):
    b = x.shape[0] // n_tok
    w = w_in.shape[1] // 2
    w_in = w_in.astype(BF16)
    sh_l, sc_l, g_l = mods_l
    sh_c, sc_c, g_c = mods_c
    y_l, xp_l = modmm(x, sh_l, sc_l, w_in, n_tok, (w, w), ("gelu", None), (F32, F32))
    y_c, xp_c = modmm(xc, sh_c, sc_c, w_in, n_ctx, (w, w), ("gelu", None), (F32, F32))
    zero = jnp.zeros((b, 1, w), F32)
    hs_l, hs_c = [], []
    for d in range(2):
        gate_w = (conv_w, conv_b, wa[d].astype(BF16), ba[d], wx[d].astype(BF16), bx[d], lam[d])
        hc, state = lru_scan(xp_c, *gate_w, zero, n_ctx, reverse=d == 1)
        hl, _ = lru_scan(xp_l, *gate_w, state, n_tok, reverse=d == 1)
        hs_c.append(hc)
        hs_l.append(hl)
    w_out = w_out.astype(BF16)
    return (mm_residual(tuple(hs_l), w_out, x, g_l, n_tok, b=y_l),
            mm_residual(tuple(hs_c), w_out, xc, g_c, n_ctx, b=y_c))


def _gmlp_kernel(x_ref, sh_ref, sc_ref, g_ref, win_ref, lng_ref, lnb_ref, ws_ref, bs_ref, wout_ref, o_ref, *, half):
    x = x_ref[...]
    h = _modulate(x, sh_ref[0], sc_ref[0]).astype(BF16)
    u = _gelu(_dot(h, win_ref[:, :half]))
    v = _gelu(_dot(h, win_ref[:, half:]))
    mu = jnp.mean(v, axis=-1, keepdims=True)
    vc = v - mu
    var = jnp.mean(vc * vc, axis=-1, keepdims=True)
    vn = (vc * lax.rsqrt(var + EPS) * lng_ref[...] + lnb_ref[...]).astype(BF16)
    gw = half // GMLP_GROUPS
    chunks = []
    for c in range(x.shape[0] // GMLP_CHUNK):
        rws = slice(c * GMLP_CHUNK, (c + 1) * GMLP_CHUNK)
        chunks.append(jnp.concatenate(
            [_dot(ws_ref[g], vn[rws, g * gw:(g + 1) * gw]) + bs_ref[g] for g in range(GMLP_GROUPS)], axis=1))
    v2 = jnp.concatenate(chunks, axis=0)
    o_ref[...] = x + g_ref[0] * _dot((u * v2).astype(BF16), wout_ref[...])


def gmlp_residual(x, shift, scale, gate, w_in, ln_g, ln_b, w_s, b_s, w_out, n_tok, tm=256):
    r, d = x.shape
    half = w_in.shape[1] // 2
    tm = _row_tile(n_tok, tm)
    tpb = n_tok // tm
    assert tm % GMLP_CHUNK == 0
    const2 = lambda i: (0, 0)
    const3 = lambda i: (0, 0, 0)
    once = pl.Buffered(1)
    return pl.pallas_call(
        functools.partial(_gmlp_kernel, half=half),
        out_shape=jax.ShapeDtypeStruct((r, d), F32),
        grid=(r // tm,),
        in_specs=[
            pl.BlockSpec((tm, d), lambda i: (i, 0)),
            _mod_spec(shift, tpb),
            _mod_spec(scale, tpb),
            _mod_spec(gate, tpb),
            pl.BlockSpec(w_in.shape, const2, pipeline_mode=once),
            pl.BlockSpec((1, half), const2),
            pl.BlockSpec((1, half), const2),
            pl.BlockSpec(w_s.shape, const3),
            pl.BlockSpec(b_s.shape + (1,), const3),
            pl.BlockSpec(w_out.shape, const2, pipeline_mode=once),
        ],
        out_specs=pl.BlockSpec((tm, d), lambda i: (i, 0)),
        compiler_params=_cparams("parallel"),
        name="gmlp_mixer",
    )(x, shift, scale, gate, w_in.astype(BF16), ln_g.reshape(1, half), ln_b.reshape(1, half),
      w_s.astype(BF16), b_s[..., None], w_out.astype(BF16))


def _final_kernel(x_ref, g_ref, o_ref):
    x = x_ref[...]
    o_ref[...] = x * lax.rsqrt(jnp.mean(x * x, axis=-1, keepdims=True) + EPS) * g_ref[...]


def final_norm(x, gain, tm=512):
    r, d = x.shape
    return pl.pallas_call(
        _final_kernel,
        out_shape=jax.ShapeDtypeStruct((r, d), F32),
        grid=(r // tm,),
        in_specs=[pl.BlockSpec((tm, d), lambda i: (i, 0)), pl.BlockSpec((1, d), lambda i: (0, 0))],
        out_specs=pl.BlockSpec((tm, d), lambda i: (i, 0)),
        compiler_params=_cparams("parallel"),
        name="final_norm",
    )(x, gain.reshape(1, d))


def kernel(x, c, ctx, c_ctx, mod_w, mod_b, pool_w, pool_scale, lru_w_in, lru_conv_w, lru_conv_b, lru_wa, lru_ba,
           lru_wx, lru_bx, lru_lam, lru_w_out, attn_w_qkv, attn_q_gain, attn_k_gain, attn_w_o, gmlp_w_in, gmlp_ln_g,
           gmlp_ln_b, gmlp_w_s, gmlp_b_s, gmlp_w_out, moe_router, moe_w_gate, moe_w_up, moe_w_down, final_gain):
    b, n_tok, d = x.shape
    n_ctx = ctx.shape[1]
    depth = mod_w.shape[0]
    n_mixers = 4
    xl = x.reshape(b * n_tok, d)
    xc = ctx.reshape(b * n_ctx, d)

    pad = -(b + 1) % SUBLANES
    cvec = jnp.concatenate([c, c_ctx[None, :], jnp.zeros((pad, d), F32)], axis=0)
    mods = mod_vectors(cvec, mod_w, mod_b)

    for i in range(depth):
        m, j = i % n_mixers, i // n_mixers
        last = i == depth - 1
        ml = [mods[i, :b, k * d:(k + 1) * d].reshape(b, 1, d) for k in range(6)]
        mc = [mods[i, b:b + 1, k * d:(k + 1) * d].reshape(1, 1, d) for k in range(6)]
        if m == 0:
            pw = pool_w[j].astype(BF16)
            xl_new = pool_mixer_residual(xl, ml[0], ml[1], ml[2], pw, pool_scale[j], n_tok)
            if not last:
                xc = pool_mixer_residual(xc, mc[0], mc[1], mc[2], pw, pool_scale[j], n_ctx)
            xl = xl_new
        elif m == 1:
            xl, xc_new = rglru_residual(xl, xc, ml[:3], mc[:3], lru_w_in[j], lru_conv_w[j], lru_conv_b[j], lru_wa[j],
                                        lru_ba[j], lru_wx[j], lru_bx[j], lru_lam[j], lru_w_out[j], n_tok, n_ctx)
            xc = xc if last else xc_new
        elif m == 2:
            xl, xc_new = gqa_residual(xl, xc, ml[:3], mc[:3], attn_w_qkv[j], attn_q_gain[j], attn_k_gain[j],
                                      attn_w_o[j], n_tok, n_ctx)
            xc = xc if last else xc_new
        else:
            gargs = (gmlp_w_in[j], gmlp_ln_g[j], gmlp_ln_b[j], gmlp_w_s[j], gmlp_b_s[j], gmlp_w_out[j])
            xl_new = gmlp_residual(xl, ml[0], ml[1], ml[2], *gargs, n_tok)
            if not last:
                xc = gmlp_residual(xc, mc[0], mc[1], mc[2], *gargs, n_ctx)
            xl = xl_new
        wr = jnp.pad(moe_router[i], ((0, 0), (0, LANES - N_EXPERTS)))
        wg, wu, wd = moe_w_gate[i].astype(BF16), moe_w_up[i].astype(BF16), moe_w_down[i].astype(BF16)
        xl = moe_residual(xl, ml[3], ml[4], ml[5], wr, wg, wu, wd, n_tok)
        if not last:
            xc = moe_residual(xc, mc[3], mc[4], mc[5], wr, wg, wu, wd, n_ctx)
    return final_norm(xl, final_gain).reshape(b, n_tok, d)
```

```python
import functools

import jax
import jax.numpy as jnp
from jax import lax
from jax.experimental import pallas as pl
from jax.experimental.pallas import tpu as pltpu

F32 = jnp.float32
BF16 = jnp.bfloat16
EPS = 1e-6

N_EXPERTS = 16
CAPACITY_FACTOR = 2
POOL_WINDOWS = (2, 4, 8, 16)
POOL_HALO = 8
LRU_BLOCKS = 8
LRU_C = 8.0
CONV_W = 4
HEAD_DIM = 64
N_KV_HEADS = 4
GRID_W = 64
ROPE_THETA = 10000.0
GMLP_GROUPS = 4
GMLP_CHUNK = 128

LANES = 128
SUBLANES = 8
VMEM_LIMIT = 56 * 1024 * 1024


def _cparams(*sem):
    return pltpu.CompilerParams(dimension_semantics=sem, vmem_limit_bytes=VMEM_LIMIT)


def _modulate(x, shift, scale):
    ms = jnp.mean(x * x, axis=-1, keepdims=True)
    return x * lax.rsqrt(ms + EPS) * (1.0 + scale) + shift


def _split_bf16(a):
    hi = a.astype(BF16)
    lo = (a - hi.astype(F32)).astype(BF16)
    return hi, lo


def _dot(a, b):
    return jnp.dot(a, b, preferred_element_type=F32)


def _dot3(a, b):
    a_hi, a_lo = _split_bf16(a)
    b_hi, b_lo = _split_bf16(b)
    return _dot(a_hi, b_hi) + (_dot(a_hi, b_lo) + _dot(a_lo, b_hi))


def _gelu(x):
    return 0.5 * x * (1.0 + jnp.tanh(0.7978845608028654 * (x + 0.044715 * (x * x * x))))


def _silu(x):
    return x * (1.0 / (1.0 + jnp.exp(-x)))


def _sigmoid(x):
    return 1.0 / (1.0 + jnp.exp(-x))


def _mod_spec(mod, tiles_per_batch):
    d = mod.shape[-1]
    if mod.shape[0] == 1:
        return pl.BlockSpec((1, 1, d), lambda i, *_: (0, 0, 0))
    return pl.BlockSpec((1, 1, d), lambda i, *_: (i // tiles_per_batch, 0, 0))


def _row_tile(n, want):
    t = min(n, want)
    assert n % t == 0
    return t


def _mod_kernel(c_ref, w_ref, b_ref, o_ref):
    s = _silu(c_ref[...])
    o_ref[0] = _dot3(s, w_ref[0]) + b_ref[0]


def mod_vectors(cvec, mod_w, mod_b):
    depth, d, n6 = mod_w.shape
    tn = 1024
    return pl.pallas_call(
        _mod_kernel,
        out_shape=jax.ShapeDtypeStruct((depth, cvec.shape[0], n6), F32),
        grid=(depth, n6 // tn),
        in_specs=[
            pl.BlockSpec(cvec.shape, lambda l, j: (0, 0)),
            pl.BlockSpec((1, d, tn), lambda l, j: (l, 0, j)),
            pl.BlockSpec((1, 1, tn), lambda l, j: (l, 0, j)),
        ],
        out_specs=pl.BlockSpec((1, cvec.shape[0], tn), lambda l, j: (l, 0, j)),
        compiler_params=_cparams("parallel", "parallel"),
        name="mod_vectors",
    )(cvec, mod_w, mod_b.reshape(depth, 1, n6))


def _modmm_kernel(x_ref, sh_ref, sc_ref, w_ref, *o_refs, splits, acts):
    h = _modulate(x_ref[...], sh_ref[0], sc_ref[0]).astype(BF16)
    z = _dot(h, w_ref[...])
    off = 0
    for o_ref, width, act in zip(o_refs, splits, acts):
        part = z[:, off:off + width]
        if act == "gelu":
            part = _gelu(part)
        o_ref[...] = part.astype(o_ref.dtype)
        off += width


def modmm(x, shift, scale, w, n_tok, splits, acts, dtypes, tm=512):
    r, d = x.shape
    tm = _row_tile(n_tok, tm)
    tpb = n_tok // tm
    n = w.shape[1]
    assert sum(splits) == n
    return pl.pallas_call(
        functools.partial(_modmm_kernel, splits=splits, acts=acts),
        out_shape=[jax.ShapeDtypeStruct((r, s), dt) for s, dt in zip(splits, dtypes)],
        grid=(r // tm,),
        in_specs=[
            pl.BlockSpec((tm, d), lambda i: (i, 0)),
            _mod_spec(shift, tpb),
            _mod_spec(scale, tpb),
            pl.BlockSpec((d, n), lambda i: (0, 0)),
        ],
        out_specs=[pl.BlockSpec((tm, s), lambda i: (i, 0)) for s in splits],
        compiler_params=_cparams("parallel"),
        name="modmm",
    )(x, shift, scale, w)


def _mmres_kernel(*refs, n_sum, has_mul):
    a = refs[0][...]
    for a_ref in refs[1:n_sum]:
        a = a + a_ref[...]
    refs = refs[n_sum:]
    if has_mul:
        a = a.astype(F32) * refs[0][...].astype(F32)
        refs = refs[1:]
    w_ref, x_ref, g_ref, o_ref = refs
    o_ref[...] = x_ref[...] + g_ref[0] * _dot(a.astype(BF16), w_ref[...])


def mm_residual(a, w, x, gate, n_tok, b=None, tm=512):
    a = a if isinstance(a, (tuple, list)) else (a,)
    r, k = a[0].shape
    n = w.shape[1]
    tm = _row_tile(n_tok, tm)
    tpb = n_tok // tm
    ins = list(a) + ([b] if b is not None else []) + [w, x, gate]
    specs = [pl.BlockSpec((tm, k), lambda i: (i, 0))] * (len(a) + (b is not None))
    specs += [
        pl.BlockSpec((k, n), lambda i: (0, 0)),
        pl.BlockSpec((tm, n), lambda i: (i, 0)),
        _mod_spec(gate, tpb),
    ]
    return pl.pallas_call(
        functools.partial(_mmres_kernel, n_sum=len(a), has_mul=b is not None),
        out_shape=jax.ShapeDtypeStruct((r, n), F32),
        grid=(r // tm,),
        in_specs=specs,
        out_specs=pl.BlockSpec((tm, n), lambda i: (i, 0)),
        compiler_params=_cparams("parallel"),
        name="mm_residual",
    )(*ins)


def _pool_kernel(xp_ref, x_ref, xn_ref, sh_ref, sc_ref, g_ref, w_ref, ps_ref, o_ref, *, tm, n_tok, tpb):
    t0 = (pl.program_id(0) % tpb) * tm
    x = x_ref[...]
    xe = jnp.concatenate([xp_ref[...], x, xn_ref[...]], axis=0)
    rows = tm + 2 * POOL_HALO
    h = _modulate(xe, sh_ref[0], sc_ref[0])
    pos = lax.broadcasted_iota(jnp.int32, (rows, 1), 0) + (t0 - POOL_HALO)
    h = jnp.where((pos >= 0) & (pos < n_tok), h, 0.0)
    posc = pos[POOL_HALO:POOL_HALO + tm]
    gw = h.shape[1] // len(POOL_WINDOWS)
    outs = []
    for g, win in enumerate(POOL_WINDOWS):
        hg = h[:, g * gw:(g + 1) * gw]
        c = hg + pltpu.roll(hg, 1, 0)
        step = 1
        while 2 * step < win:
            c = pltpu.roll(c, step, 0) + pltpu.roll(c, rows - step, 0)
            step *= 2
        cnt = jnp.minimum(posc + (win - win // 2), n_tok) - jnp.maximum(posc - win // 2, 0)
        pooled = c[POOL_HALO:POOL_HALO + tm] / cnt.astype(F32) - hg[POOL_HALO:POOL_HALO + tm]
        outs.append(_dot(pooled.astype(BF16), w_ref[g]))
    y = jnp.concatenate(outs, axis=1) * ps_ref[...]
    o_ref[...] = x + g_ref[0] * y


def pool_mixer_residual(x, shift, scale, gate, w_pool, pool_scale, n_tok, tm=256):
    r, d = x.shape
    tm = _row_tile(n_tok, tm)
    tpb = n_tok // tm
    hb = tm // POOL_HALO
    last = r // POOL_HALO - 1
    groups, gw, _ = w_pool.shape
    return pl.pallas_call(
        functools.partial(_pool_kernel, tm=tm, n_tok=n_tok, tpb=tpb),
        out_shape=jax.ShapeDtypeStruct((r, d), F32),
        grid=(r // tm,),
        in_specs=[
            pl.BlockSpec((POOL_HALO, d), lambda i: (jnp.maximum(i * hb - 1, 0), 0)),
            pl.BlockSpec((tm, d), lambda i: (i, 0)),
            pl.BlockSpec((POOL_HALO, d), lambda i: (jnp.minimum((i + 1) * hb, last), 0)),
            _mod_spec(shift, tpb),
            _mod_spec(scale, tpb),
            _mod_spec(gate, tpb),
            pl.BlockSpec((groups, gw, gw), lambda i: (0, 0, 0)),
            pl.BlockSpec((1, d), lambda i: (0, 0)),
        ],
        out_specs=pl.BlockSpec((tm, d), lambda i: (i, 0)),
        compiler_params=_cparams("parallel"),
        name="pool_mixer",
    )(x, x, x, shift, scale, gate, w_pool, pool_scale.reshape(1, d))


def _router_kernel(x_ref, sh_ref, sc_ref, wr_ref, h_ref, aff_ref):
    h = _modulate(x_ref[...], sh_ref[0], sc_ref[0])
    h_ref[...] = h.astype(h_ref.dtype)
    logits = _dot3(h, wr_ref[...])
    lane = lax.broadcasted_iota(jnp.int32, logits.shape, 1)
    logits = jnp.where(lane < N_EXPERTS, logits, -jnp.inf)
    e = jnp.exp(logits - jnp.max(logits, axis=-1, keepdims=True))
    aff_ref[...] = e / jnp.sum(e, axis=-1, keepdims=True)


def router(x, shift, scale, w_router_padded, n_tok, tm=512):
    r, d = x.shape
    tm = _row_tile(n_tok, tm)
    tpb = n_tok // tm
    return pl.pallas_call(
        _router_kernel,
        out_shape=[jax.ShapeDtypeStruct((r, d), BF16), jax.ShapeDtypeStruct((r, LANES), F32)],
        grid=(r // tm,),
        in_specs=[
            pl.BlockSpec((tm, d), lambda i: (i, 0)),
            _mod_spec(shift, tpb),
            _mod_spec(scale, tpb),
            pl.BlockSpec((d, LANES), lambda i: (0, 0)),
        ],
        out_specs=[pl.BlockSpec((tm, d), lambda i: (i, 0)), pl.BlockSpec((tm, LANES), lambda i: (i, 0))],
        compiler_params=_cparams("parallel"),
        name="moe_router",
    )(x, shift, scale, w_router_padded)


MOE_CHUNK = 128
PACK = LANES // N_EXPERTS


def _cumsum_rows(v, tb):
    n = v.shape[0]
    tri = (lax.broadcasted_iota(jnp.int32, (tb, tb), 0) >= lax.broadcasted_iota(jnp.int32, (tb, tb), 1)).astype(BF16)
    carry = jnp.zeros((1, v.shape[1]), F32)
    outs, starts = [], []
    for j in range(n // tb):
        starts.append(carry)
        c = _dot(tri, v[j * tb:(j + 1) * tb].astype(BF16)) + carry
        carry = c[tb - 1:tb, :]
        outs.append(c)
    return jnp.concatenate(outs, axis=0), jnp.concatenate(starts, axis=0)


def _route_kernel(aff_ref, affp_ref, idx_ref, slot_ref, start_ref, pos_scr, *, n, cap):
    def bisect(k, prefix):
        cand = prefix | jnp.left_shift(jnp.int32(1), 30 - k)
        cnt = jnp.sum((pltpu.bitcast(affp_ref[...], jnp.int32) >= cand).astype(F32), axis=0, keepdims=True)
        shift = N_EXPERTS
        while shift < LANES:
            cnt = cnt + pltpu.roll(cnt, shift, 1)
            shift *= 2
        return jnp.where(cnt >= cap, cand, prefix)

    thr = lax.fori_loop(0, 31, bisect, jnp.zeros((1, LANES), jnp.int32))
    bits = pltpu.bitcast(aff_ref[...], jnp.int32)
    gt = bits > thr
    eq = bits == thr
    need = cap - jnp.sum(gt.astype(F32), axis=0, keepdims=True)
    eq_rank, _ = _cumsum_rows(eq.astype(F32), MOE_CHUNK)
    sel = gt | (eq & (eq_rank <= need))
    pos, starts = _cumsum_rows(sel.astype(F32), MOE_CHUNK)
    pos_scr[...] = pos
    slot_ref[...] = jnp.where(sel, pos - 1.0, -1.0)
    start_ref[0] = starts.astype(jnp.int32)

    tb = min(n, 256)
    slots = lax.broadcasted_iota(jnp.int32, (1, cap), 1).astype(F32)
    for e in range(N_EXPERTS):
        def blk(j, acc):
            rows = pl.ds(pl.multiple_of(j * tb, tb), tb)
            below = jnp.where(pos_scr[rows, e:e + 1] <= slots, 1.0, 0.0)
            return acc + jnp.sum(below.reshape(tb // SUBLANES, SUBLANES, cap), axis=0)

        acc = lax.fori_loop(0, n // tb, blk, jnp.zeros((SUBLANES, cap), F32))
        idx_ref[0, e:e + 1, :] = jnp.sum(acc, axis=0, keepdims=True).astype(jnp.int32)


def route(aff, n_tok):
    r = aff.shape[0]
    b = r // n_tok
    cap = CAPACITY_FACTOR * n_tok // N_EXPERTS
    aff_packed = aff[:, :N_EXPERTS].reshape(r // PACK, LANES)
    n_chunks = n_tok // MOE_CHUNK
    return pl.pallas_call(
        functools.partial(_route_kernel, n=n_tok, cap=cap),
        out_shape=[jax.ShapeDtypeStruct((b, N_EXPERTS, cap), jnp.int32), jax.ShapeDtypeStruct((r, LANES), F32),
                   jax.ShapeDtypeStruct((b, n_chunks, LANES), jnp.int32)],
        grid=(b,),
        in_specs=[pl.BlockSpec((n_tok, LANES), lambda i: (i, 0)),
                  pl.BlockSpec((n_tok // PACK, LANES), lambda i: (i, 0))],
        out_specs=[pl.BlockSpec((1, N_EXPERTS, cap), lambda i: (i, 0, 0)),
                   pl.BlockSpec((n_tok, LANES), lambda i: (i, 0)),
                   pl.BlockSpec((1, n_chunks, LANES), lambda i: (i, 0, 0))],
        scratch_shapes=[pltpu.VMEM((n_tok, LANES), F32)],
        compiler_params=_cparams("parallel"),
        name="moe_route",
    )(aff, aff_packed)


def _ffn_kernel(x_ref, wg_ref, wu_ref, wd_ref, o_ref, wg_scr, wu_scr, wd_scr):
    @pl.when(pl.program_id(1) == 0)
    def _():
        wg_scr[...] = wg_ref[0, 0].astype(BF16)
        wu_scr[...] = wu_ref[0, 0].astype(BF16)
        wd_scr[...] = wd_ref[0, 0].astype(BF16)

    x = x_ref[0]
    a = _dot(x, wg_scr[...])
    u = _dot(x, wu_scr[...])
    hmid = (_silu(a) * u).astype(BF16)
    o_ref[0] = _dot(hmid, wd_scr[...]).astype(o_ref.dtype)


def expert_ffn(xs, w_gate, w_up, w_down, layer, tm=512):
    e, m, d = xs.shape
    ff = w_gate.shape[-1]
    tm = _row_tile(m, tm)
    return pl.pallas_call(
        _ffn_kernel,
        out_shape=jax.ShapeDtypeStruct((e, m, d), BF16),
        grid=(e, m // tm),
        in_specs=[
            pl.BlockSpec((1, tm, d), lambda k, i: (k, i, 0)),
            pl.BlockSpec((1, 1, d, ff), lambda k, i: (layer, k, 0, 0)),
            pl.BlockSpec((1, 1, d, ff), lambda k, i: (layer, k, 0, 0)),
            pl.BlockSpec((1, 1, ff, d), lambda k, i: (layer, k, 0, 0)),
        ],
        out_specs=pl.BlockSpec((1, tm, d), lambda k, i: (k, i, 0)),
        scratch_shapes=[pltpu.VMEM((d, ff), BF16), pltpu.VMEM((d, ff), BF16), pltpu.VMEM((ff, d), BF16)],
        compiler_params=_cparams("parallel", "arbitrary"),
        name="moe_expert_ffn",
    )(xs, w_gate, w_up, w_down)


def _combine_kernel(off_ref, x_ref, g_ref, slot_ref, aff_ref, y_ref, o_ref, *, win, n_chunks):
    base = (pl.program_id(0) * n_chunks + pl.program_id(1)) * N_EXPERTS
    align = min(win, LANES)
    acc = jnp.zeros(x_ref.shape, F32)
    for e in range(N_EXPERTS):
        off = pl.multiple_of(off_ref[base + e], align)
        cols = (lax.broadcasted_iota(jnp.int32, (1, win), 1) + off).astype(F32)
        onehot = jnp.where(slot_ref[:, e:e + 1] == cols, 1.0, 0.0).astype(BF16)
        acc = acc + aff_ref[:, e:e + 1] * _dot(onehot, y_ref[e, pl.ds(off, win), :])
    o_ref[...] = x_ref[...] + g_ref[0] * acc


def moe_combine(x, gate2, slot, aff, y, offs, n_tok):
    r, d = x.shape
    b = r // n_tok
    n_exp, _, _ = y.shape
    cap = y.shape[1] // b
    n_chunks = n_tok // MOE_CHUNK
    win = min(2 * MOE_CHUNK, cap)
    return pl.pallas_call(
        functools.partial(_combine_kernel, win=win, n_chunks=n_chunks),
        out_shape=jax.ShapeDtypeStruct((r, d), F32),
        grid_spec=pltpu.PrefetchScalarGridSpec(
            num_scalar_prefetch=1,
            grid=(b, n_chunks),
            in_specs=[
                pl.BlockSpec((MOE_CHUNK, d), lambda bi, c, off: (bi * n_chunks + c, 0)),
                _mod_spec(gate2, 1),
                pl.BlockSpec((MOE_CHUNK, LANES), lambda bi, c, off: (bi * n_chunks + c, 0)),
                pl.BlockSpec((MOE_CHUNK, LANES), lambda bi, c, off: (bi * n_chunks + c, 0)),
                pl.BlockSpec((n_exp, cap, d), lambda bi, c, off: (0, bi, 0)),
            ],
            out_specs=pl.BlockSpec((MOE_CHUNK, d), lambda bi, c, off: (bi * n_chunks + c, 0)),
        ),
        compiler_params=_cparams("parallel", "arbitrary"),
        name="moe_combine",
    )(offs, x, gate2, slot, aff, y)


def moe_residual(x, shift, scale, gate2, w_router_padded, w_gate, w_up, w_down, layer, n_tok):
    r, d = x.shape
    b = r // n_tok
    h, aff = router(x, shift, scale, w_router_padded, n_tok)
    idx, slot, start = route(aff, n_tok)
    cap = idx.shape[-1]
    gidx = idx + (jnp.arange(b, dtype=jnp.int32) * n_tok)[:, None, None]
    gidx = jnp.transpose(gidx, (1, 0, 2)).reshape(N_EXPERTS, b * cap)
    xs = jnp.take(h, gidx, axis=0)
    y = expert_ffn(xs, w_gate, w_up, w_down, layer)
    win = min(2 * MOE_CHUNK, cap)
    align = min(win, LANES)
    offs = jnp.clip(start[:, :, :N_EXPERTS] // align * align, 0, cap - win).reshape(-1)
    return moe_combine(x, gate2, slot, aff, y, offs, n_tok)


def _head_rms(t, seg_ones):
    outs = []
    for j in range(t.shape[1] // LANES):
        blk = t[:, j * LANES:(j + 1) * LANES]
        hi, lo = _split_bf16(blk * blk)
        ss = _dot(hi, seg_ones) + _dot(lo, seg_ones)
        outs.append(blk * lax.rsqrt(ss * (1.0 / HEAD_DIM) + EPS))
    return jnp.concatenate(outs, axis=1)


def _rope(t, cos, sin_signed):
    w = t.shape[1]
    half = HEAD_DIM // 2
    lane = lax.broadcasted_iota(jnp.int32, t.shape, 1)
    partner = jnp.where(lane % HEAD_DIM < half, pltpu.roll(t, w - half, 1), pltpu.roll(t, half, 1))
    reps = w // cos.shape[1]
    return t * jnp.concatenate([cos] * reps, axis=1) + partner * jnp.concatenate([sin_signed] * reps, axis=1)


def _qkv_kernel(*refs, qd, kd, rope):
    if rope:
        x_ref, sh_ref, sc_ref, w_ref, qg_ref, kg_ref, cos_ref, sin_ref, q_ref, k_ref, v_ref = refs
    else:
        x_ref, sh_ref, sc_ref, w_ref, qg_ref, kg_ref, q_ref, k_ref, v_ref = refs
    h = _modulate(x_ref[...], sh_ref[0], sc_ref[0]).astype(BF16)
    z = _dot(h, w_ref[...])
    seg = (lax.broadcasted_iota(jnp.int32, (LANES, LANES), 0) // HEAD_DIM
           == lax.broadcasted_iota(jnp.int32, (LANES, LANES), 1) // HEAD_DIM).astype(BF16)
    q = _head_rms(z[:, :qd], seg) * qg_ref[...]
    k = _head_rms(z[:, qd:qd + kd], seg) * kg_ref[...]
    if rope:
        q = _rope(q, cos_ref[...], sin_ref[...])
        k = _rope(k, cos_ref[...], sin_ref[...])
    q_ref[...] = (q * (HEAD_DIM ** -0.5 * 1.4426950408889634)).astype(q_ref.dtype)
    k_ref[...] = k.astype(k_ref.dtype)
    v_ref[...] = z[:, qd + kd:].astype(v_ref.dtype)


def qkv_project(x, shift, scale, w_qkv, q_gain, k_gain, n_tok, rope_tables=None, tm=256):
    r, d = x.shape
    kd = N_KV_HEADS * HEAD_DIM
    qd = w_qkv.shape[1] - 2 * kd
    tm = _row_tile(n_tok, tm)
    tpb = n_tok // tm
    ins = [x, shift, scale, w_qkv, q_gain, k_gain]
    specs = [
        pl.BlockSpec((tm, d), lambda i: (i, 0)),
        _mod_spec(shift, tpb),
        _mod_spec(scale, tpb),
        pl.BlockSpec(w_qkv.shape, lambda i: (0, 0)),
        pl.BlockSpec((1, qd), lambda i: (0, 0)),
        pl.BlockSpec((1, kd), lambda i: (0, 0)),
    ]
    if rope_tables is not None:
        ins += list(rope_tables)
        specs += [pl.BlockSpec((tm, LANES), lambda i: (i % tpb, 0))] * 2
    return pl.pallas_call(
        functools.partial(_qkv_kernel, qd=qd, kd=kd, rope=rope_tables is not None),
        out_shape=[jax.ShapeDtypeStruct((r, qd), BF16), jax.ShapeDtypeStruct((r, kd), BF16),
                   jax.ShapeDtypeStruct((r, kd), BF16)],
        grid=(r // tm,),
        in_specs=specs,
        out_specs=[pl.BlockSpec((tm, qd), lambda i: (i, 0)), pl.BlockSpec((tm, kd), lambda i: (i, 0)),
                   pl.BlockSpec((tm, kd), lambda i: (i, 0))],
        compiler_params=_cparams("parallel"),
        name="qkv_project",
    )(*ins)


def _attn_kernel(q_ref, k_ref, v_ref, o_ref, *, grp):
    k = k_ref[0, 0]
    v1 = v_ref[0, 0]
    outs = []
    for h in range(grp):
        q = q_ref[:, h * HEAD_DIM:(h + 1) * HEAD_DIM]
        s = lax.dot_general(q, k, (((1,), (1,)), ((), ())), preferred_element_type=F32)
        p = jnp.exp2(s - jnp.max(s, axis=-1, keepdims=True))
        ov = _dot(p.astype(BF16), v1)
        outs.append(ov[:, :HEAD_DIM] / ov[:, HEAD_DIM:])
    o_ref[...] = jnp.concatenate(outs, axis=1).astype(o_ref.dtype)


def attention(q, k, v, n_q, tq=128):
    r, qd = q.shape
    _, kvh, n_k, hd = k.shape
    grp = qd // (kvh * hd)
    tq = _row_tile(n_q, tq)
    tpb = n_q // tq
    return pl.pallas_call(
        functools.partial(_attn_kernel, grp=grp),
        out_shape=jax.ShapeDtypeStruct((r, qd), BF16),
        grid=(r // n_q, kvh, tpb),
        in_specs=[
            pl.BlockSpec((tq, grp * hd), lambda b, g, i: (b * tpb + i, g)),
            pl.BlockSpec((1, 1, n_k, hd), lambda b, g, i: (b, g, 0, 0)),
            pl.BlockSpec((1, 1, n_k, 2 * hd), lambda b, g, i: (b, g, 0, 0)),
        ],
        out_specs=pl.BlockSpec((tq, grp * hd), lambda b, g, i: (b * tpb + i, g)),
        compiler_params=_cparams("parallel", "parallel", "parallel"),
        name="attention",
    )(q, k, v)


def rope_tables(n_tok):
    t = jnp.arange(n_tok)
    row = (t // GRID_W).astype(F32)
    col = (t % GRID_W).astype(F32)
    n_freq = HEAD_DIM // 4
    inv = ROPE_THETA ** (-jnp.arange(n_freq, dtype=F32) / n_freq)
    ang = jnp.concatenate([row[:, None] * inv, col[:, None] * inv], axis=-1)
    cos, sin = jnp.cos(ang), jnp.sin(ang)
    reps = LANES // HEAD_DIM
    return jnp.tile(jnp.concatenate([cos, cos], -1), (1, reps)), jnp.tile(jnp.concatenate([-sin, sin], -1), (1, reps))


def _deinterleave_heads(w, n_heads):
    lead = w.shape[:-1]
    w = w.reshape(lead + (n_heads, HEAD_DIM // 2, 2))
    return jnp.swapaxes(w, -1, -2).reshape(lead + (n_heads * HEAD_DIM,))


def _split_kv_heads(t, b):
    return jnp.transpose(t.reshape(b, -1, N_KV_HEADS, HEAD_DIM), (0, 2, 1, 3))


def gqa_residual(x, xc, mods_l, mods_c, w_qkv, q_gain, k_gain, w_o, n_tok, n_ctx):
    b = x.shape[0] // n_tok
    kd = N_KV_HEADS * HEAD_DIM
    qd = w_qkv.shape[1] - 2 * kd
    n_qh = qd // HEAD_DIM
    w_perm = jnp.concatenate([_deinterleave_heads(w_qkv[:, :qd], n_qh),
                              _deinterleave_heads(w_qkv[:, qd:qd + kd], N_KV_HEADS), w_qkv[:, qd + kd:]], axis=1)
    w_perm = w_perm.astype(BF16)
    qg = jnp.tile(_deinterleave_heads(q_gain, 1), n_qh).reshape(1, qd)
    kg = jnp.tile(_deinterleave_heads(k_gain, 1), N_KV_HEADS).reshape(1, kd)
    sh_l, sc_l, g_l = mods_l
    sh_c, sc_c, g_c = mods_c
    q_l, k_l, v_l = qkv_project(x, sh_l, sc_l, w_perm, qg, kg, n_tok, rope_tables(n_tok))
    q_c, k_c, v_c = qkv_project(xc, sh_c, sc_c, w_perm, qg, kg, n_ctx)
    k_c4, v_c4 = _split_kv_heads(k_c, b), _split_kv_heads(v_c, b)
    v_c4 = jnp.concatenate([v_c4, jnp.ones_like(v_c4)], axis=3)
    k_all = jnp.concatenate([k_c4, _split_kv_heads(k_l, b)], axis=2)
    v_l4 = _split_kv_heads(v_l, b)
    v_all = jnp.concatenate([v_c4, jnp.concatenate([v_l4, jnp.ones_like(v_l4)], axis=3)], axis=2)
    o_l = attention(q_l, k_all, v_all, n_tok)
    o_c = attention(q_c, k_c4, v_c4, n_ctx)
    w_o = w_o.astype(BF16)
    return mm_residual(o_l, w_o, x, g_l, n_tok), mm_residual(o_c, w_o, xc, g_c, n_ctx)


def _lru_kernel(xp_ref, x_ref, xn_ref, cw_ref, cb_ref, wa_ref, ba_ref, wx_ref, bx_ref, lam_ref, h0_ref,
                o_ref, hT_ref, a_scr, u_scr, carry_scr, *, tm, n_tok, tpb, reverse):
    step = pl.program_id(1)
    t_idx = (tpb - 1 - step) if reverse else step
    t0 = t_idx * tm
    rows = tm + 2 * SUBLANES

    @pl.when(step == 0)
    def _():
        carry_scr[...] = h0_ref[0]

    xe = jnp.concatenate([xp_ref[...], x_ref[...], xn_ref[...]], axis=0)
    pos = lax.broadcasted_iota(jnp.int32, (rows, 1), 0) + (t0 - SUBLANES)
    xe = jnp.where((pos >= 0) & (pos < n_tok), xe, 0.0)
    left = CONV_W // 2
    conv = cb_ref[...]
    for k in range(CONV_W):
        shift = (left - k) % rows
        tap = xe if shift == 0 else pltpu.roll(xe, shift, 0)
        conv = conv + tap * cw_ref[k:k + 1, :]
    xr = conv[SUBLANES:SUBLANES + tm]

    xb = xr.astype(BF16)
    bw = xr.shape[1] // LRU_BLOCKS
    ra, ia = [], []
    for j in range(LRU_BLOCKS):
        blk = xb[:, j * bw:(j + 1) * bw]
        ra.append(_dot(blk, wa_ref[j]))
        ia.append(_dot(blk, wx_ref[j]))
    r = _sigmoid(jnp.concatenate(ra, axis=1) + ba_ref[...])
    i = _sigmoid(jnp.concatenate(ia, axis=1) + bx_ref[...])
    log_a = -LRU_C * r * jnp.logaddexp(-lam_ref[...], 0.0)
    a = jnp.exp(log_a)
    a_scr[...] = a
    t = jnp.tanh(log_a)
    u_scr[...] = xr * i * jnp.sqrt(-2.0 * t / (1.0 - t))

    n_grp = tm // SUBLANES
    sub = lax.broadcasted_iota(jnp.int32, (SUBLANES, 1), 0)

    def group(j, carry):
        g = (n_grp - 1 - j) if reverse else j
        rws = pl.ds(pl.multiple_of(g * SUBLANES, SUBLANES), SUBLANES)
        ag, ug = a_scr[rws, :], u_scr[rws, :]
        s = 1
        while s < SUBLANES:
            if reverse:
                ok = sub < SUBLANES - s
                sh = SUBLANES - s
            else:
                ok = sub >= s
                sh = s
            u_prev = jnp.where(ok, pltpu.roll(ug, sh, 0), 0.0)
            a_prev = jnp.where(ok, pltpu.roll(ag, sh, 0), 1.0)
            ug = ug + ag * u_prev
            ag = ag * a_prev
            s *= 2
        hg = ug + ag * carry
        o_ref[rws, :] = hg
        return hg[0:1, :] if reverse else hg[SUBLANES - 1:SUBLANES, :]

    carry = lax.fori_loop(0, n_grp, group, carry_scr[...])
    carry_scr[...] = carry
    hT_ref[0] = carry


def lru_scan(xpre, conv_w, conv_b, wa, ba, wx, bx, lam, h0, n_tok, reverse, tm=256):
    r, w = xpre.shape
    b = r // n_tok
    tm = _row_tile(n_tok, tm)
    tpb = n_tok // tm
    hb = tm // SUBLANES
    last = r // SUBLANES - 1

    def tile(bi, s):
        return bi * tpb + ((tpb - 1 - s) if reverse else s)

    vec = pl.BlockSpec((1, w), lambda bi, s: (0, 0))
    blocks = pl.BlockSpec(wa.shape, lambda bi, s: (0, 0, 0))
    return pl.pallas_call(
        functools.partial(_lru_kernel, tm=tm, n_tok=n_tok, tpb=tpb, reverse=reverse),
        out_shape=[jax.ShapeDtypeStruct((r, w), F32), jax.ShapeDtypeStruct((b, 1, w), F32)],
        grid=(b, tpb),
        in_specs=[
            pl.BlockSpec((SUBLANES, w), lambda bi, s: (jnp.maximum(tile(bi, s) * hb - 1, 0), 0)),
            pl.BlockSpec((tm, w), lambda bi, s: (tile(bi, s), 0)),
            pl.BlockSpec((SUBLANES, w), lambda bi, s: (jnp.minimum((tile(bi, s) + 1) * hb, last), 0)),
            pl.BlockSpec((CONV_W, w), lambda bi, s: (0, 0)),
            vec, blocks, vec, blocks, vec, vec,
            pl.BlockSpec((1, 1, w), lambda bi, s: (bi, 0, 0)),
        ],
        out_specs=[pl.BlockSpec((tm, w), lambda bi, s: (tile(bi, s), 0)),
                   pl.BlockSpec((1, 1, w), lambda bi, s: (bi, 0, 0))],
        scratch_shapes=[pltpu.VMEM((tm, w), F32), pltpu.VMEM((tm, w), F32), pltpu.VMEM((1, w), F32)],
        compiler_params=_cparams("parallel", "arbitrary"),
        name="lru_scan",
    )(xpre, xpre, xpre, conv_w, conv_b.reshape(1, w), wa, ba.reshape(1, w), wx, bx.reshape(1, w),
      lam.reshape(1, w), h0)


def rglru_residual(x, xc, mods_l, mods_c, w_in, conv_w, conv_b, wa, ba, wx, bx, lam, w_out, n_tok, n_ctx):
    b = x.shape[0] // n_tok
    w = w_in.shape[1] // 2
    w_in = w_in.astype(BF16)
    sh_l, sc_l, g_l = mods_l
    sh_c, sc_c, g_c = mods_c
    y_l, xp_l = modmm(x, sh_l, sc_l, w_in, n_tok, (w, w), ("gelu", None), (F32, F32))
    y_c, xp_c = modmm(xc, sh_c, sc_c, w_in, n_ctx, (w, w), ("gelu", None), (F32, F32))
    zero = jnp.zeros((b, 1, w), F32)
    hs_l, hs_c = [], []
    for d in range(2):
        gate_w = (conv_w, conv_b, wa[d].astype(BF16), ba[d], wx[d].astype(BF16), bx[d], lam[d])
        hc, state = lru_scan(xp_c, *gate_w, zero, n_ctx, reverse=d == 1)
        hl, _ = lru_scan(xp_l, *gate_w, state, n_tok, reverse=d == 1)
        hs_c.append(hc)
        hs_l.append(hl)
    w_out = w_out.astype(BF16)
    return (mm_residual(tuple(hs_l), w_out, x, g_l, n_tok, b=y_l),
            mm_residual(tuple(hs_c), w_out, xc, g_c, n_ctx, b=y_c))


def _gmlp_kernel(x_ref, sh_ref, sc_ref, g_ref, win_ref, lng_ref, lnb_ref, ws_ref, bs_ref, wout_ref, o_ref, *, half):
    x = x_ref[...]
    h = _modulate(x, sh_ref[0], sc_ref[0]).astype(BF16)
    u = _gelu(_dot(h, win_ref[:, :half]))
    v = _gelu(_dot(h, win_ref[:, half:]))
    mu = jnp.mean(v, axis=-1, keepdims=True)
    vc = v - mu
    var = jnp.mean(vc * vc, axis=-1, keepdims=True)
    vn = (vc * lax.rsqrt(var + EPS) * lng_ref[...] + lnb_ref[...]).astype(BF16)
    gw = half // GMLP_GROUPS
    chunks = []
    for c in range(x.shape[0] // GMLP_CHUNK):
        rws = slice(c * GMLP_CHUNK, (c + 1) * GMLP_CHUNK)
        chunks.append(jnp.concatenate(
            [_dot(ws_ref[g], vn[rws, g * gw:(g + 1) * gw]) + bs_ref[g] for g in range(GMLP_GROUPS)], axis=1))
    v2 = jnp.concatenate(chunks, axis=0)
    o_ref[...] = x + g_ref[0] * _dot((u * v2).astype(BF16), wout_ref[...])


def gmlp_residual(x, shift, scale, gate, w_in, ln_g, ln_b, w_s, b_s, w_out, n_tok, tm=256):
    r, d = x.shape
    half = w_in.shape[1] // 2
    tm = _row_tile(n_tok, tm)
    tpb = n_tok // tm
    assert tm % GMLP_CHUNK == 0
    const2 = lambda i: (0, 0)
    const3 = lambda i: (0, 0, 0)
    once = pl.Buffered(1)
    return pl.pallas_call(
        functools.partial(_gmlp_kernel, half=half),
        out_shape=jax.ShapeDtypeStruct((r, d), F32),
        grid=(r // tm,),
        in_specs=[
            pl.BlockSpec((tm, d), lambda i: (i, 0)),
            _mod_spec(shift, tpb),
            _mod_spec(scale, tpb),
            _mod_spec(gate, tpb),
            pl.BlockSpec(w_in.shape, const2, pipeline_mode=once),
            pl.BlockSpec((1, half), const2),
            pl.BlockSpec((1, half), const2),
            pl.BlockSpec(w_s.shape, const3),
            pl.BlockSpec(b_s.shape + (1,), const3),
            pl.BlockSpec(w_out.shape, const2, pipeline_mode=once),
        ],
        out_specs=pl.BlockSpec((tm, d), lambda i: (i, 0)),
        compiler_params=_cparams("parallel"),
        name="gmlp_mixer",
    )(x, shift, scale, gate, w_in.astype(BF16), ln_g.reshape(1, half), ln_b.reshape(1, half),
      w_s.astype(BF16), b_s[..., None], w_out.astype(BF16))


def _final_kernel(x_ref, g_ref, o_ref):
    x = x_ref[...]
    o_ref[...] = x * lax.rsqrt(jnp.mean(x * x, axis=-1, keepdims=True) + EPS) * g_ref[...]


def final_norm(x, gain, tm=512):
    r, d = x.shape
    return pl.pallas_call(
        _final_kernel,
        out_shape=jax.ShapeDtypeStruct((r, d), F32),
        grid=(r // tm,),
        in_specs=[pl.BlockSpec((tm, d), lambda i: (i, 0)), pl.BlockSpec((1, d), lambda i: (0, 0))],
        out_specs=pl.BlockSpec((tm, d), lambda i: (i, 0)),
        compiler_params=_cparams("parallel"),
        name="final_norm",
    )(x, gain.reshape(1, d))


def kernel(x, c, ctx, c_ctx, mod_w, mod_b, pool_w, pool_scale, lru_w_in, lru_conv_w, lru_conv_b, lru_wa, lru_ba,
           lru_wx, lru_bx, lru_lam, lru_w_out, attn_w_qkv, attn_q_gain, attn_k_gain, attn_w_o, gmlp_w_in, gmlp_ln_g,
           gmlp_ln_b, gmlp_w_s, gmlp_b_s, gmlp_w_out, moe_router, moe_w_gate, moe_w_up, moe_w_down, final_gain):
    b, n_tok, d = x.shape
    n_ctx = ctx.shape[1]
    depth = mod_w.shape[0]
    n_mixers = 4
    xl = x.reshape(b * n_tok, d)
    xc = ctx.reshape(b * n_ctx, d)

    pad = -(b + 1) % SUBLANES
    cvec = jnp.concatenate([c, c_ctx[None, :], jnp.zeros((pad, d), F32)], axis=0)
    mods = mod_vectors(cvec, mod_w, mod_b)

    for i in range(depth):
        m, j = i % n_mixers, i // n_mixers
        last = i == depth - 1
        ml = [mods[i, :b, k * d:(k + 1) * d].reshape(b, 1, d) for k in range(6)]
        mc = [mods[i, b:b + 1, k * d:(k + 1) * d].reshape(1, 1, d) for k in range(6)]
        if m == 0:
            pw = pool_w[j].astype(BF16)
            xl_new = pool_mixer_residual(xl, ml[0], ml[1], ml[2], pw, pool_scale[j], n_tok)
            if not last:
                xc = pool_mixer_residual(xc, mc[0], mc[1], mc[2], pw, pool_scale[j], n_ctx)
            xl = xl_new
        elif m == 1:
            xl, xc_new = rglru_residual(xl, xc, ml[:3], mc[:3], lru_w_in[j], lru_conv_w[j], lru_conv_b[j], lru_wa[j],
                                        lru_ba[j], lru_wx[j], lru_bx[j], lru_lam[j], lru_w_out[j], n_tok, n_ctx)
            xc = xc if last else xc_new
        elif m == 2:
            xl, xc_new = gqa_residual(xl, xc, ml[:3], mc[:3], attn_w_qkv[j], attn_q_gain[j], attn_k_gain[j],
                                      attn_w_o[j], n_tok, n_ctx)
            xc = xc if last else xc_new
        else:
            gargs = (gmlp_w_in[j], gmlp_ln_g[j], gmlp_ln_b[j], gmlp_w_s[j], gmlp_b_s[j], gmlp_w_out[j])
            xl_new = gmlp_residual(xl, ml[0], ml[1], ml[2], *gargs, n_tok)
            if not last:
                xc = gmlp_residual(xc, mc[0], mc[1], mc[2], *gargs, n_ctx)
            xl = xl_new
        wr = jnp.pad(moe_router[i], ((0, 0), (0, LANES - N_EXPERTS)))
        xl = moe_residual(xl, ml[3], ml[4], ml[5], wr, moe_w_gate, moe_w_up, moe_w_down, i, n_tok)
        if not last:
            xc = moe_residual(xc, mc[3], mc[4], mc[5], wr, moe_w_gate, moe_w_up, moe_w_down, i, n_ctx)
    return final_norm(xl, final_gain).reshape(b, n_tok, d)
```

```python
import functools

import jax
import jax.numpy as jnp
from jax import lax
from jax.experimental import pallas as pl
from jax.experimental.pallas import tpu as pltpu

F32 = jnp.float32
BF16 = jnp.bfloat16
EPS = 1e-6

N_EXPERTS = 16
CAPACITY_FACTOR = 2
POOL_WINDOWS = (2, 4, 8, 16)
POOL_HALO = 8
LRU_BLOCKS = 8
LRU_C = 8.0
CONV_W = 4
HEAD_DIM = 64
N_KV_HEADS = 4
GRID_W = 64
ROPE_THETA = 10000.0
GMLP_GROUPS = 4
GMLP_CHUNK = 128

LANES = 128
SUBLANES = 8
VMEM_LIMIT = 56 * 1024 * 1024


def _cparams(*sem):
    return pltpu.CompilerParams(dimension_semantics=sem, vmem_limit_bytes=VMEM_LIMIT)


def _modulate(x, shift, scale):
    ms = jnp.mean(x * x, axis=-1, keepdims=True)
    return x * lax.rsqrt(ms + EPS) * (1.0 + scale) + shift


def _split_bf16(a):
    hi = a.astype(BF16)
    lo = (a - hi.astype(F32)).astype(BF16)
    return hi, lo


def _dot(a, b):
    return jnp.dot(a, b, preferred_element_type=F32)


def _dot3(a, b):
    a_hi, a_lo = _split_bf16(a)
    b_hi, b_lo = _split_bf16(b)
    return _dot(a_hi, b_hi) + (_dot(a_hi, b_lo) + _dot(a_lo, b_hi))


def _gelu(x):
    return 0.5 * x * (1.0 + jnp.tanh(0.7978845608028654 * (x + 0.044715 * (x * x * x))))


def _silu(x):
    return x * (1.0 / (1.0 + jnp.exp(-x)))


def _sigmoid(x):
    return 1.0 / (1.0 + jnp.exp(-x))


def _mod_spec(mod, tiles_per_batch):
    d = mod.shape[-1]
    if mod.shape[0] == 1:
        return pl.BlockSpec((1, 1, d), lambda i, *_: (0, 0, 0))
    return pl.BlockSpec((1, 1, d), lambda i, *_: (i // tiles_per_batch, 0, 0))


def _row_tile(n, want):
    t = min(n, want)
    assert n % t == 0
    return t


def _mod_kernel(c_ref, w_ref, b_ref, o_ref):
    s = _silu(c_ref[...])
    o_ref[0] = _dot3(s, w_ref[0]) + b_ref[0]


def mod_vectors(cvec, mod_w, mod_b):
    depth, d, n6 = mod_w.shape
    tn = 1024
    return pl.pallas_call(
        _mod_kernel,
        out_shape=jax.ShapeDtypeStruct((depth, cvec.shape[0], n6), F32),
        grid=(depth, n6 // tn),
        in_specs=[
            pl.BlockSpec(cvec.shape, lambda l, j: (0, 0)),
            pl.BlockSpec((1, d, tn), lambda l, j: (l, 0, j)),
            pl.BlockSpec((1, 1, tn), lambda l, j: (l, 0, j)),
        ],
        out_specs=pl.BlockSpec((1, cvec.shape[0], tn), lambda l, j: (l, 0, j)),
        compiler_params=_cparams("parallel", "parallel"),
        name="mod_vectors",
    )(cvec, mod_w, mod_b.reshape(depth, 1, n6))


def _modmm_kernel(x_ref, sh_ref, sc_ref, w_ref, *o_refs, splits, acts):
    h = _modulate(x_ref[...], sh_ref[0], sc_ref[0]).astype(BF16)
    z = _dot(h, w_ref[...])
    off = 0
    for o_ref, width, act in zip(o_refs, splits, acts):
        part = z[:, off:off + width]
        if act == "gelu":
            part = _gelu(part)
        o_ref[...] = part.astype(o_ref.dtype)
        off += width


def modmm(x, shift, scale, w, n_tok, splits, acts, dtypes, tm=512):
    r, d = x.shape
    tm = _row_tile(n_tok, tm)
    tpb = n_tok // tm
    n = w.shape[1]
    assert sum(splits) == n
    return pl.pallas_call(
        functools.partial(_modmm_kernel, splits=splits, acts=acts),
        out_shape=[jax.ShapeDtypeStruct((r, s), dt) for s, dt in zip(splits, dtypes)],
        grid=(r // tm,),
        in_specs=[
            pl.BlockSpec((tm, d), lambda i: (i, 0)),
            _mod_spec(shift, tpb),
            _mod_spec(scale, tpb),
            pl.BlockSpec((d, n), lambda i: (0, 0)),
        ],
        out_specs=[pl.BlockSpec((tm, s), lambda i: (i, 0)) for s in splits],
        compiler_params=_cparams("parallel"),
        name="modmm",
    )(x, shift, scale, w)


def _mmres_kernel(*refs, n_sum, has_mul):
    a = refs[0][...]
    for a_ref in refs[1:n_sum]:
        a = a + a_ref[...]
    refs = refs[n_sum:]
    if has_mul:
        a = a.astype(F32) * refs[0][...].astype(F32)
        refs = refs[1:]
    w_ref, x_ref, g_ref, o_ref = refs
    o_ref[...] = x_ref[...] + g_ref[0] * _dot(a.astype(BF16), w_ref[...])


def mm_residual(a, w, x, gate, n_tok, b=None, tm=512):
    a = a if isinstance(a, (tuple, list)) else (a,)
    r, k = a[0].shape
    n = w.shape[1]
    tm = _row_tile(n_tok, tm)
    tpb = n_tok // tm
    ins = list(a) + ([b] if b is not None else []) + [w, x, gate]
    specs = [pl.BlockSpec((tm, k), lambda i: (i, 0))] * (len(a) + (b is not None))
    specs += [
        pl.BlockSpec((k, n), lambda i: (0, 0)),
        pl.BlockSpec((tm, n), lambda i: (i, 0)),
        _mod_spec(gate, tpb),
    ]
    return pl.pallas_call(
        functools.partial(_mmres_kernel, n_sum=len(a), has_mul=b is not None),
        out_shape=jax.ShapeDtypeStruct((r, n), F32),
        grid=(r // tm,),
        in_specs=specs,
        out_specs=pl.BlockSpec((tm, n), lambda i: (i, 0)),
        compiler_params=_cparams("parallel"),
        name="mm_residual",
    )(*ins)


def _pool_kernel(xp_ref, x_ref, xn_ref, sh_ref, sc_ref, g_ref, w_ref, ps_ref, o_ref, *, tm, n_tok, tpb):
    t0 = (pl.program_id(0) % tpb) * tm
    x = x_ref[...]
    xe = jnp.concatenate([xp_ref[...], x, xn_ref[...]], axis=0)
    rows = tm + 2 * POOL_HALO
    h = _modulate(xe, sh_ref[0], sc_ref[0])
    pos = lax.broadcasted_iota(jnp.int32, (rows, 1), 0) + (t0 - POOL_HALO)
    h = jnp.where((pos >= 0) & (pos < n_tok), h, 0.0)
    posc = pos[POOL_HALO:POOL_HALO + tm]
    gw = h.shape[1] // len(POOL_WINDOWS)
    outs = []
    for g, win in enumerate(POOL_WINDOWS):
        hg = h[:, g * gw:(g + 1) * gw]
        c = hg + pltpu.roll(hg, 1, 0)
        step = 1
        while 2 * step < win:
            c = pltpu.roll(c, step, 0) + pltpu.roll(c, rows - step, 0)
            step *= 2
        cnt = jnp.minimum(posc + (win - win // 2), n_tok) - jnp.maximum(posc - win // 2, 0)
        pooled = c[POOL_HALO:POOL_HALO + tm] / cnt.astype(F32) - hg[POOL_HALO:POOL_HALO + tm]
        outs.append(_dot(pooled.astype(BF16), w_ref[g]))
    y = jnp.concatenate(outs, axis=1) * ps_ref[...]
    o_ref[...] = x + g_ref[0] * y


def pool_mixer_residual(x, shift, scale, gate, w_pool, pool_scale, n_tok, tm=256):
    r, d = x.shape
    tm = _row_tile(n_tok, tm)
    tpb = n_tok // tm
    hb = tm // POOL_HALO
    last = r // POOL_HALO - 1
    groups, gw, _ = w_pool.shape
    return pl.pallas_call(
        functools.partial(_pool_kernel, tm=tm, n_tok=n_tok, tpb=tpb),
        out_shape=jax.ShapeDtypeStruct((r, d), F32),
        grid=(r // tm,),
        in_specs=[
            pl.BlockSpec((POOL_HALO, d), lambda i: (jnp.maximum(i * hb - 1, 0), 0)),
            pl.BlockSpec((tm, d), lambda i: (i, 0)),
            pl.BlockSpec((POOL_HALO, d), lambda i: (jnp.minimum((i + 1) * hb, last), 0)),
            _mod_spec(shift, tpb),
            _mod_spec(scale, tpb),
            _mod_spec(gate, tpb),
            pl.BlockSpec((groups, gw, gw), lambda i: (0, 0, 0)),
            pl.BlockSpec((1, d), lambda i: (0, 0)),
        ],
        out_specs=pl.BlockSpec((tm, d), lambda i: (i, 0)),
        compiler_params=_cparams("parallel"),
        name="pool_mixer",
    )(x, x, x, shift, scale, gate, w_pool, pool_scale.reshape(1, d))


def _router_kernel(x_ref, sh_ref, sc_ref, wr_ref, h_ref, aff_ref):
    h = _modulate(x_ref[...], sh_ref[0], sc_ref[0])
    h_ref[...] = h.astype(h_ref.dtype)
    logits = _dot3(h, wr_ref[...])
    lane = lax.broadcasted_iota(jnp.int32, logits.shape, 1)
    logits = jnp.where(lane < N_EXPERTS, logits, -jnp.inf)
    e = jnp.exp(logits - jnp.max(logits, axis=-1, keepdims=True))
    aff_ref[...] = e / jnp.sum(e, axis=-1, keepdims=True)


def router(x, shift, scale, w_router_padded, n_tok, tm=512):
    r, d = x.shape
    tm = _row_tile(n_tok, tm)
    tpb = n_tok // tm
    return pl.pallas_call(
        _router_kernel,
        out_shape=[jax.ShapeDtypeStruct((r, d), BF16), jax.ShapeDtypeStruct((r, LANES), F32)],
        grid=(r // tm,),
        in_specs=[
            pl.BlockSpec((tm, d), lambda i: (i, 0)),
            _mod_spec(shift, tpb),
            _mod_spec(scale, tpb),
            pl.BlockSpec((d, LANES), lambda i: (0, 0)),
        ],
        out_specs=[pl.BlockSpec((tm, d), lambda i: (i, 0)), pl.BlockSpec((tm, LANES), lambda i: (i, 0))],
        compiler_params=_cparams("parallel"),
        name="moe_router",
    )(x, shift, scale, w_router_padded)


MOE_CHUNK = 128
PACK = LANES // N_EXPERTS


def _cumsum_rows(v, tb):
    n = v.shape[0]
    tri = (lax.broadcasted_iota(jnp.int32, (tb, tb), 0) >= lax.broadcasted_iota(jnp.int32, (tb, tb), 1)).astype(BF16)
    carry = jnp.zeros((1, v.shape[1]), F32)
    outs, starts = [], []
    for j in range(n // tb):
        starts.append(carry)
        c = _dot(tri, v[j * tb:(j + 1) * tb].astype(BF16)) + carry
        carry = c[tb - 1:tb, :]
        outs.append(c)
    return jnp.concatenate(outs, axis=0), jnp.concatenate(starts, axis=0)


def _route_kernel(aff_ref, affp_ref, idx_ref, slot_ref, start_ref, slot_scr, acc_scr, *, n, cap):
    def bisect(k, prefix):
        cand = prefix | jnp.left_shift(jnp.int32(1), 30 - k)
        cnt = jnp.sum((pltpu.bitcast(affp_ref[...], jnp.int32) >= cand).astype(F32), axis=0, keepdims=True)
        shift = N_EXPERTS
        while shift < LANES:
            cnt = cnt + pltpu.roll(cnt, shift, 1)
            shift *= 2
        return jnp.where(cnt >= cap, cand, prefix)

    thr = lax.fori_loop(0, 31, bisect, jnp.zeros((1, LANES), jnp.int32))
    bits = pltpu.bitcast(aff_ref[...], jnp.int32)
    gt = bits > thr
    eq = bits == thr
    need = cap - jnp.sum(gt.astype(F32), axis=0, keepdims=True)
    eq_rank, _ = _cumsum_rows(eq.astype(F32), MOE_CHUNK)
    sel = gt | (eq & (eq_rank <= need))
    pos, starts = _cumsum_rows(sel.astype(F32), MOE_CHUNK)
    slot_scr[...] = jnp.where(sel, pos - 1.0, -1.0)
    slot_ref[...] = slot_scr[...]
    start_ref[0] = starts.astype(jnp.int32)

    slots = lax.broadcasted_iota(jnp.int32, (1, cap), 1).astype(F32)
    sub = lax.broadcasted_iota(jnp.int32, (SUBLANES, MOE_CHUNK), 0)
    local = lax.broadcasted_iota(jnp.int32, (SUBLANES, MOE_CHUNK), 1).astype(F32)
    acc_scr[...] = jnp.zeros(acc_scr.shape, F32)

    def chunk(c, carry):
        rows = pl.ds(pl.multiple_of(c * MOE_CHUNK, MOE_CHUNK), MOE_CHUNK)
        lhs = jnp.where(sub == 0, local, jnp.where(sub == 1, c.astype(F32), 0.0)).astype(BF16)
        for e in range(N_EXPERTS):
            onehot = jnp.where(slot_scr[rows, e:e + 1] == slots, 1.0, 0.0).astype(BF16)
            acc_scr[e] += _dot(lhs, onehot)
        return carry

    lax.fori_loop(0, n // MOE_CHUNK, chunk, 0)
    for e in range(N_EXPERTS):
        idx_ref[0, e:e + 1, :] = (acc_scr[e, 0:1, :] + MOE_CHUNK * acc_scr[e, 1:2, :]).astype(jnp.int32)


def route(aff, n_tok):
    r = aff.shape[0]
    b = r // n_tok
    cap = CAPACITY_FACTOR * n_tok // N_EXPERTS
    aff_packed = aff[:, :N_EXPERTS].reshape(r // PACK, LANES)
    n_chunks = n_tok // MOE_CHUNK
    return pl.pallas_call(
        functools.partial(_route_kernel, n=n_tok, cap=cap),
        out_shape=[jax.ShapeDtypeStruct((b, N_EXPERTS, cap), jnp.int32), jax.ShapeDtypeStruct((r, LANES), F32),
                   jax.ShapeDtypeStruct((b, n_chunks, LANES), jnp.int32)],
        grid=(b,),
        in_specs=[pl.BlockSpec((n_tok, LANES), lambda i: (i, 0)),
                  pl.BlockSpec((n_tok // PACK, LANES), lambda i: (i, 0))],
        out_specs=[pl.BlockSpec((1, N_EXPERTS, cap), lambda i: (i, 0, 0)),
                   pl.BlockSpec((n_tok, LANES), lambda i: (i, 0)),
                   pl.BlockSpec((1, n_chunks, LANES), lambda i: (i, 0, 0))],
        scratch_shapes=[pltpu.VMEM((n_tok, LANES), F32), pltpu.VMEM((N_EXPERTS, SUBLANES, cap), F32)],
        compiler_params=_cparams("parallel"),
        name="moe_route",
    )(aff, aff_packed)


def _ffn_kernel(x_ref, wg_ref, wu_ref, wd_ref, o_ref, wg_scr, wu_scr, wd_scr):
    @pl.when(pl.program_id(1) == 0)
    def _():
        wg_scr[...] = wg_ref[0, 0].astype(BF16)
        wu_scr[...] = wu_ref[0, 0].astype(BF16)
        wd_scr[...] = wd_ref[0, 0].astype(BF16)

    x = x_ref[0]
    a = _dot(x, wg_scr[...])
    u = _dot(x, wu_scr[...])
    hmid = (_silu(a) * u).astype(BF16)
    o_ref[0] = _dot(hmid, wd_scr[...]).astype(o_ref.dtype)


def expert_ffn(xs, w_gate, w_up, w_down, layer, tm=512):
    e, m, d = xs.shape
    ff = w_gate.shape[-1]
    tm = _row_tile(m, tm)
    return pl.pallas_call(
        _ffn_kernel,
        out_shape=jax.ShapeDtypeStruct((e, m, d), BF16),
        grid=(e, m // tm),
        in_specs=[
            pl.BlockSpec((1, tm, d), lambda k, i: (k, i, 0)),
            pl.BlockSpec((1, 1, d, ff), lambda k, i: (layer, k, 0, 0)),
            pl.BlockSpec((1, 1, d, ff), lambda k, i: (layer, k, 0, 0)),
            pl.BlockSpec((1, 1, ff, d), lambda k, i: (layer, k, 0, 0)),
        ],
        out_specs=pl.BlockSpec((1, tm, d), lambda k, i: (k, i, 0)),
        scratch_shapes=[pltpu.VMEM((d, ff), BF16), pltpu.VMEM((d, ff), BF16), pltpu.VMEM((ff, d), BF16)],
        compiler_params=_cparams("parallel", "arbitrary"),
        name="moe_expert_ffn",
    )(xs, w_gate, w_up, w_down)


def _combine_kernel(off_ref, x_ref, g_ref, slot_ref, aff_ref, y_ref, o_ref, *, win, n_chunks):
    base = (pl.program_id(0) * n_chunks + pl.program_id(1)) * N_EXPERTS
    align = min(win, LANES)
    acc = jnp.zeros(x_ref.shape, F32)
    for e in range(N_EXPERTS):
        off = pl.multiple_of(off_ref[base + e], align)
        cols = (lax.broadcasted_iota(jnp.int32, (1, win), 1) + off).astype(F32)
        onehot = jnp.where(slot_ref[:, e:e + 1] == cols, 1.0, 0.0).astype(BF16)
        acc = acc + aff_ref[:, e:e + 1] * _dot(onehot, y_ref[e, pl.ds(off, win), :])
    o_ref[...] = x_ref[...] + g_ref[0] * acc


def moe_combine(x, gate2, slot, aff, y, offs, n_tok):
    r, d = x.shape
    b = r // n_tok
    n_exp, _, _ = y.shape
    cap = y.shape[1] // b
    n_chunks = n_tok // MOE_CHUNK
    win = min(2 * MOE_CHUNK, cap)
    return pl.pallas_call(
        functools.partial(_combine_kernel, win=win, n_chunks=n_chunks),
        out_shape=jax.ShapeDtypeStruct((r, d), F32),
        grid_spec=pltpu.PrefetchScalarGridSpec(
            num_scalar_prefetch=1,
            grid=(b, n_chunks),
            in_specs=[
                pl.BlockSpec((MOE_CHUNK, d), lambda bi, c, off: (bi * n_chunks + c, 0)),
                _mod_spec(gate2, 1),
                pl.BlockSpec((MOE_CHUNK, LANES), lambda bi, c, off: (bi * n_chunks + c, 0)),
                pl.BlockSpec((MOE_CHUNK, LANES), lambda bi, c, off: (bi * n_chunks + c, 0)),
                pl.BlockSpec((n_exp, cap, d), lambda bi, c, off: (0, bi, 0)),
            ],
            out_specs=pl.BlockSpec((MOE_CHUNK, d), lambda bi, c, off: (bi * n_chunks + c, 0)),
        ),
        compiler_params=_cparams("parallel", "arbitrary"),
        name="moe_combine",
    )(offs, x, gate2, slot, aff, y)


def moe_residual(x, shift, scale, gate2, w_router_padded, w_gate, w_up, w_down, layer, n_tok):
    r, d = x.shape
    b = r // n_tok
    h, aff = router(x, shift, scale, w_router_padded, n_tok)
    idx, slot, start = route(aff, n_tok)
    cap = idx.shape[-1]
    gidx = idx + (jnp.arange(b, dtype=jnp.int32) * n_tok)[:, None, None]
    gidx = jnp.transpose(gidx, (1, 0, 2)).reshape(N_EXPERTS, b * cap)
    xs = h.at[gidx].get(mode="promise_in_bounds")
    y = expert_ffn(xs, w_gate, w_up, w_down, layer)
    win = min(2 * MOE_CHUNK, cap)
    align = min(win, LANES)
    offs = jnp.clip(start[:, :, :N_EXPERTS] // align * align, 0, cap - win).reshape(-1)
    return moe_combine(x, gate2, slot, aff, y, offs, n_tok)


def _head_rms(t, seg_ones):
    outs = []
    for j in range(t.shape[1] // LANES):
        blk = t[:, j * LANES:(j + 1) * LANES]
        hi, lo = _split_bf16(blk * blk)
        ss = _dot(hi, seg_ones) + _dot(lo, seg_ones)
        outs.append(blk * lax.rsqrt(ss * (1.0 / HEAD_DIM) + EPS))
    return jnp.concatenate(outs, axis=1)


def _rope(t, cos, sin_signed):
    w = t.shape[1]
    half = HEAD_DIM // 2
    lane = lax.broadcasted_iota(jnp.int32, t.shape, 1)
    partner = jnp.where(lane % HEAD_DIM < half, pltpu.roll(t, w - half, 1), pltpu.roll(t, half, 1))
    reps = w // cos.shape[1]
    return t * jnp.concatenate([cos] * reps, axis=1) + partner * jnp.concatenate([sin_signed] * reps, axis=1)


def _qkv_kernel(*refs, qd, kd, rope):
    if rope:
        x_ref, sh_ref, sc_ref, w_ref, qg_ref, kg_ref, cos_ref, sin_ref, q_ref, k_ref, v_ref = refs
    else:
        x_ref, sh_ref, sc_ref, w_ref, qg_ref, kg_ref, q_ref, k_ref, v_ref = refs
    h = _modulate(x_ref[...], sh_ref[0], sc_ref[0]).astype(BF16)
    z = _dot(h, w_ref[...])
    seg = (lax.broadcasted_iota(jnp.int32, (LANES, LANES), 0) // HEAD_DIM
           == lax.broadcasted_iota(jnp.int32, (LANES, LANES), 1) // HEAD_DIM).astype(BF16)
    q = _head_rms(z[:, :qd], seg) * qg_ref[...]
    k = _head_rms(z[:, qd:qd + kd], seg) * kg_ref[...]
    if rope:
        q = _rope(q, cos_ref[...], sin_ref[...])
        k = _rope(k, cos_ref[...], sin_ref[...])
    q_ref[...] = (q * (HEAD_DIM ** -0.5 * 1.4426950408889634)).astype(q_ref.dtype)
    k_ref[...] = k.astype(k_ref.dtype)
    v_ref[...] = z[:, qd + kd:].astype(v_ref.dtype)


def qkv_project(x, shift, scale, w_qkv, q_gain, k_gain, n_tok, rope_tables=None, tm=256):
    r, d = x.shape
    kd = N_KV_HEADS * HEAD_DIM
    qd = w_qkv.shape[1] - 2 * kd
    tm = _row_tile(n_tok, tm)
    tpb = n_tok // tm
    ins = [x, shift, scale, w_qkv, q_gain, k_gain]
    specs = [
        pl.BlockSpec((tm, d), lambda i: (i, 0)),
        _mod_spec(shift, tpb),
        _mod_spec(scale, tpb),
        pl.BlockSpec(w_qkv.shape, lambda i: (0, 0)),
        pl.BlockSpec((1, qd), lambda i: (0, 0)),
        pl.BlockSpec((1, kd), lambda i: (0, 0)),
    ]
    if rope_tables is not None:
        ins += list(rope_tables)
        specs += [pl.BlockSpec((tm, LANES), lambda i: (i % tpb, 0))] * 2
    return pl.pallas_call(
        functools.partial(_qkv_kernel, qd=qd, kd=kd, rope=rope_tables is not None),
        out_shape=[jax.ShapeDtypeStruct((r, qd), BF16), jax.ShapeDtypeStruct((r, kd), BF16),
                   jax.ShapeDtypeStruct((r, kd), BF16)],
        grid=(r // tm,),
        in_specs=specs,
        out_specs=[pl.BlockSpec((tm, qd), lambda i: (i, 0)), pl.BlockSpec((tm, kd), lambda i: (i, 0)),
                   pl.BlockSpec((tm, kd), lambda i: (i, 0))],
        compiler_params=_cparams("parallel"),
        name="qkv_project",
    )(*ins)


def _attn_kernel(q_ref, k_ref, vt_ref, o_ref, *scr, grp):
    k = k_ref[0, 0]
    vt = vt_ref[0, 0]
    tq = q_ref.shape[0]
    outs = []
    for pair in range(grp // 2):
        heads = (2 * pair, 2 * pair + 1)
        q2 = jnp.concatenate([q_ref[:, h * HEAD_DIM:(h + 1) * HEAD_DIM] for h in heads], axis=0)
        scr[2 * pair][...] = lax.dot_general(k, q2, (((1,), (1,)), ((), ())), preferred_element_type=F32)
    for pair in range(grp // 2):
        st_scr, p_scr = scr[2 * pair], scr[2 * pair + 1]
        p_scr[...] = jnp.exp2(st_scr[...] - jnp.max(st_scr[...], axis=0, keepdims=True)).astype(BF16)
        ot = _dot(vt, p_scr[...])
        ot = ot[:HEAD_DIM] / ot[HEAD_DIM:]
        o = jnp.concatenate([ot, ot], axis=0).T
        outs += [o[:tq, :HEAD_DIM], o[tq:, :HEAD_DIM]]
    o_ref[...] = jnp.concatenate(outs, axis=1).astype(o_ref.dtype)


def attention(q, k, vt, n_q, tq=256):
    r, qd = q.shape
    _, kvh, n_k, hd = k.shape
    grp = qd // (kvh * hd)
    tq = _row_tile(n_q, tq)
    tpb = n_q // tq
    return pl.pallas_call(
        functools.partial(_attn_kernel, grp=grp),
        out_shape=jax.ShapeDtypeStruct((r, qd), BF16),
        grid=(r // n_q, kvh, tpb),
        in_specs=[
            pl.BlockSpec((tq, grp * hd), lambda b, g, i: (b * tpb + i, g)),
            pl.BlockSpec((1, 1, n_k, hd), lambda b, g, i: (b, g, 0, 0)),
            pl.BlockSpec((1, 1, 2 * hd, n_k), lambda b, g, i: (b, g, 0, 0)),
        ],
        out_specs=pl.BlockSpec((tq, grp * hd), lambda b, g, i: (b * tpb + i, g)),
        scratch_shapes=[pltpu.VMEM((n_k, 2 * tq), dt) for _ in range(grp // 2) for dt in (F32, BF16)],
        compiler_params=_cparams("parallel", "parallel", "parallel"),
        name="attention",
    )(q, k, vt)


def rope_tables(n_tok):
    t = jnp.arange(n_tok)
    row = (t // GRID_W).astype(F32)
    col = (t % GRID_W).astype(F32)
    n_freq = HEAD_DIM // 4
    inv = ROPE_THETA ** (-jnp.arange(n_freq, dtype=F32) / n_freq)
    ang = jnp.concatenate([row[:, None] * inv, col[:, None] * inv], axis=-1)
    cos, sin = jnp.cos(ang), jnp.sin(ang)
    reps = LANES // HEAD_DIM
    return jnp.tile(jnp.concatenate([cos, cos], -1), (1, reps)), jnp.tile(jnp.concatenate([-sin, sin], -1), (1, reps))


def _deinterleave_heads(w, n_heads):
    lead = w.shape[:-1]
    w = w.reshape(lead + (n_heads, HEAD_DIM // 2, 2))
    return jnp.swapaxes(w, -1, -2).reshape(lead + (n_heads * HEAD_DIM,))


def _split_kv_heads(t, b):
    return jnp.transpose(t.reshape(b, -1, N_KV_HEADS, HEAD_DIM), (0, 2, 1, 3))


def gqa_residual(x, xc, mods_l, mods_c, w_qkv, q_gain, k_gain, w_o, n_tok, n_ctx):
    b = x.shape[0] // n_tok
    kd = N_KV_HEADS * HEAD_DIM
    qd = w_qkv.shape[1] - 2 * kd
    n_qh = qd // HEAD_DIM
    w_perm = jnp.concatenate([_deinterleave_heads(w_qkv[:, :qd], n_qh),
                              _deinterleave_heads(w_qkv[:, qd:qd + kd], N_KV_HEADS), w_qkv[:, qd + kd:]], axis=1)
    w_perm = w_perm.astype(BF16)
    qg = jnp.tile(_deinterleave_heads(q_gain, 1), n_qh).reshape(1, qd)
    kg = jnp.tile(_deinterleave_heads(k_gain, 1), N_KV_HEADS).reshape(1, kd)
    sh_l, sc_l, g_l = mods_l
    sh_c, sc_c, g_c = mods_c
    q_l, k_l, v_l = qkv_project(x, sh_l, sc_l, w_perm, qg, kg, n_tok, rope_tables(n_tok))
    q_c, k_c, v_c = qkv_project(xc, sh_c, sc_c, w_perm, qg, kg, n_ctx)
    k_c4 = _split_kv_heads(k_c, b)
    k_all = jnp.concatenate([k_c4, _split_kv_heads(k_l, b)], axis=2)

    def values_t(v):
        vt = jnp.transpose(v.reshape(b, -1, N_KV_HEADS, HEAD_DIM), (0, 2, 3, 1))
        return jnp.concatenate([vt, jnp.ones_like(vt)], axis=2)

    vt_c = values_t(v_c)
    vt_all = jnp.concatenate([vt_c, values_t(v_l)], axis=3)
    o_l = attention(q_l, k_all, vt_all, n_tok)
    o_c = attention(q_c, k_c4, vt_c, n_ctx)
    w_o = w_o.astype(BF16)
    return mm_residual(o_l, w_o, x, g_l, n_tok), mm_residual(o_c, w_o, xc, g_c, n_ctx)


def _lru_kernel(xp_ref, x_ref, xn_ref, cw_ref, cb_ref, wa_ref, ba_ref, wx_ref, bx_ref, lam_ref, h0_ref,
                o_ref, hT_ref, a_scr, u_scr, carry_scr, *, tm, n_tok, tpb, reverse):
    step = pl.program_id(1)
    t_idx = (tpb - 1 - step) if reverse else step
    t0 = t_idx * tm
    rows = tm + 2 * SUBLANES

    @pl.when(step == 0)
    def _():
        carry_scr[...] = h0_ref[0]

    xe = jnp.concatenate([xp_ref[...], x_ref[...], xn_ref[...]], axis=0)
    pos = lax.broadcasted_iota(jnp.int32, (rows, 1), 0) + (t0 - SUBLANES)
    xe = jnp.where((pos >= 0) & (pos < n_tok), xe, 0.0)
    left = CONV_W // 2
    conv = cb_ref[...]
    for k in range(CONV_W):
        shift = (left - k) % rows
        tap = xe if shift == 0 else pltpu.roll(xe, shift, 0)
        conv = conv + tap * cw_ref[k:k + 1, :]
    xr = conv[SUBLANES:SUBLANES + tm]

    xb = xr.astype(BF16)
    bw = xr.shape[1] // LRU_BLOCKS
    ra, ia = [], []
    for j in range(LRU_BLOCKS):
        blk = xb[:, j * bw:(j + 1) * bw]
        ra.append(_dot(blk, wa_ref[j]))
        ia.append(_dot(blk, wx_ref[j]))
    r = _sigmoid(jnp.concatenate(ra, axis=1) + ba_ref[...])
    i = _sigmoid(jnp.concatenate(ia, axis=1) + bx_ref[...])
    log_a = -LRU_C * r * jnp.logaddexp(-lam_ref[...], 0.0)
    a = jnp.exp(log_a)
    a_scr[...] = a
    t = jnp.tanh(log_a)
    u_scr[...] = xr * i * jnp.sqrt(-2.0 * t / (1.0 - t))

    n_grp = tm // SUBLANES
    sub = lax.broadcasted_iota(jnp.int32, (SUBLANES, 1), 0)

    def group(j, carry):
        g = (n_grp - 1 - j) if reverse else j
        rws = pl.ds(pl.multiple_of(g * SUBLANES, SUBLANES), SUBLANES)
        ag, ug = a_scr[rws, :], u_scr[rws, :]
        s = 1
        while s < SUBLANES:
            if reverse:
                ok = sub < SUBLANES - s
                sh = SUBLANES - s
            else:
                ok = sub >= s
                sh = s
            u_prev = jnp.where(ok, pltpu.roll(ug, sh, 0), 0.0)
            a_prev = jnp.where(ok, pltpu.roll(ag, sh, 0), 1.0)
            ug = ug + ag * u_prev
            ag = ag * a_prev
            s *= 2
        hg = ug + ag * carry
        o_ref[rws, :] = hg
        return hg[0:1, :] if reverse else hg[SUBLANES - 1:SUBLANES, :]

    carry = lax.fori_loop(0, n_grp, group, carry_scr[...])
    carry_scr[...] = carry
    hT_ref[0] = carry


def lru_scan(xpre, conv_w, conv_b, wa, ba, wx, bx, lam, h0, n_tok, reverse, tm=256):
    r, w = xpre.shape
    b = r // n_tok
    tm = _row_tile(n_tok, tm)
    tpb = n_tok // tm
    hb = tm // SUBLANES
    last = r // SUBLANES - 1

    def tile(bi, s):
        return bi * tpb + ((tpb - 1 - s) if reverse else s)

    vec = pl.BlockSpec((1, w), lambda bi, s: (0, 0))
    blocks = pl.BlockSpec(wa.shape, lambda bi, s: (0, 0, 0))
    return pl.pallas_call(
        functools.partial(_lru_kernel, tm=tm, n_tok=n_tok, tpb=tpb, reverse=reverse),
        out_shape=[jax.ShapeDtypeStruct((r, w), F32), jax.ShapeDtypeStruct((b, 1, w), F32)],
        grid=(b, tpb),
        in_specs=[
            pl.BlockSpec((SUBLANES, w), lambda bi, s: (jnp.maximum(tile(bi, s) * hb - 1, 0), 0)),
            pl.BlockSpec((tm, w), lambda bi, s: (tile(bi, s), 0)),
            pl.BlockSpec((SUBLANES, w), lambda bi, s: (jnp.minimum((tile(bi, s) + 1) * hb, last), 0)),
            pl.BlockSpec((CONV_W, w), lambda bi, s: (0, 0)),
            vec, blocks, vec, blocks, vec, vec,
            pl.BlockSpec((1, 1, w), lambda bi, s: (bi, 0, 0)),
        ],
        out_specs=[pl.BlockSpec((tm, w), lambda bi, s: (tile(bi, s), 0)),
                   pl.BlockSpec((1, 1, w), lambda bi, s: (bi, 0, 0))],
        scratch_shapes=[pltpu.VMEM((tm, w), F32), pltpu.VMEM((tm, w), F32), pltpu.VMEM((1, w), F32)],
        compiler_params=_cparams("parallel", "arbitrary"),
        name="lru_scan",
    )(xpre, xpre, xpre, conv_w, conv_b.reshape(1, w), wa, ba.reshape(1, w), wx, bx.reshape(1, w),
      lam.reshape(1, w), h0)


def rglru_residual(x, xc, mods_l, mods_c, w_in, conv_w, conv_b, wa, ba, wx, bx, lam, w_out, n_tok, n_ctx):
    b = x.shape[0] // n_tok
    w = w_in.shape[1] // 2
    w_in = w_in.astype(BF16)
    sh_l, sc_l, g_l = mods_l
    sh_c, sc_c, g_c = mods_c
    y_l, xp_l = modmm(x, sh_l, sc_l, w_in, n_tok, (w, w), ("gelu", None), (F32, F32))
    y_c, xp_c = modmm(xc, sh_c, sc_c, w_in, n_ctx, (w, w), ("gelu", None), (F32, F32))
    zero = jnp.zeros((b, 1, w), F32)
    hs_l, hs_c = [], []
    for d in range(2):
        gate_w = (conv_w, conv_b, wa[d].astype(BF16), ba[d], wx[d].astype(BF16), bx[d], lam[d])
        hc, state = lru_scan(xp_c, *gate_w, zero, n_ctx, reverse=d == 1)
        hl, _ = lru_scan(xp_l, *gate_w, state, n_tok, reverse=d == 1)
        hs_c.append(hc)
        hs_l.append(hl)
    w_out = w_out.astype(BF16)
    return (mm_residual(tuple(hs_l), w_out, x, g_l, n_tok, b=y_l),
            mm_residual(tuple(hs_c), w_out, xc, g_c, n_ctx, b=y_c))


def _gmlp_kernel(x_ref, sh_ref, sc_ref, g_ref, win_ref, lng_ref, lnb_ref, ws_ref, bs_ref, wout_ref, o_ref, *, half):
    x = x_ref[...]
    h = _modulate(x, sh_ref[0], sc_ref[0]).astype(BF16)
    u = _gelu(_dot(h, win_ref[:, :half]))
    v = _gelu(_dot(h, win_ref[:, half:]))
    mu = jnp.mean(v, axis=-1, keepdims=True)
    vc = v - mu
    var = jnp.mean(vc * vc, axis=-1, keepdims=True)
    vn = (vc * lax.rsqrt(var + EPS) * lng_ref[...] + lnb_ref[...]).astype(BF16)
    gw = half // GMLP_GROUPS
    chunks = []
    for c in range(x.shape[0] // GMLP_CHUNK):
        rws = slice(c * GMLP_CHUNK, (c + 1) * GMLP_CHUNK)
        chunks.append(jnp.concatenate(
            [_dot(ws_ref[g], vn[rws, g * gw:(g + 1) * gw]) + bs_ref[g] for g in range(GMLP_GROUPS)], axis=1))
    v2 = jnp.concatenate(chunks, axis=0)
    o_ref[...] = x + g_ref[0] * _dot((u * v2).astype(BF16), wout_ref[...])


def gmlp_residual(x, shift, scale, gate, w_in, ln_g, ln_b, w_s, b_s, w_out, n_tok, tm=256):
    r, d = x.shape
    half = w_in.shape[1] // 2
    tm = _row_tile(n_tok, tm)
    tpb = n_tok // tm
    assert tm % GMLP_CHUNK == 0
    const2 = lambda i: (0, 0)
    const3 = lambda i: (0, 0, 0)
    once = pl.Buffered(1)
    return pl.pallas_call(
        functools.partial(_gmlp_kernel, half=half),
        out_shape=jax.ShapeDtypeStruct((r, d), F32),
        grid=(r // tm,),
        in_specs=[
            pl.BlockSpec((tm, d), lambda i: (i, 0)),
            _mod_spec(shift, tpb),
            _mod_spec(scale, tpb),
            _mod_spec(gate, tpb),
            pl.BlockSpec(w_in.shape, const2, pipeline_mode=once),
            pl.BlockSpec((1, half), const2),
            pl.BlockSpec((1, half), const2),
            pl.BlockSpec(w_s.shape, const3),
            pl.BlockSpec(b_s.shape + (1,), const3),
            pl.BlockSpec(w_out.shape, const2, pipeline_mode=once),
        ],
        out_specs=pl.BlockSpec((tm, d), lambda i: (i, 0)),
        compiler_params=_cparams("parallel"),
        name="gmlp_mixer",
    )(x, shift, scale, gate, w_in.astype(BF16), ln_g.reshape(1, half), ln_b.reshape(1, half),
      w_s.astype(BF16), b_s[..., None], w_out.astype(BF16))


def _final_kernel(x_ref, g_ref, o_ref):
    x = x_ref[...]
    o_ref[...] = x * lax.rsqrt(jnp.mean(x * x, axis=-1, keepdims=True) + EPS) * g_ref[...]


def final_norm(x, gain, tm=512):
    r, d = x.shape
    return pl.pallas_call(
        _final_kernel,
        out_shape=jax.ShapeDtypeStruct((r, d), F32),
        grid=(r // tm,),
        in_specs=[pl.BlockSpec((tm, d), lambda i: (i, 0)), pl.BlockSpec((1, d), lambda i: (0, 0))],
        out_specs=pl.BlockSpec((tm, d), lambda i: (i, 0)),
        compiler_params=_cparams("parallel"),
        name="final_norm",
    )(x, gain.reshape(1, d))


def kernel(x, c, ctx, c_ctx, mod_w, mod_b, pool_w, pool_scale, lru_w_in, lru_conv_w, lru_conv_b, lru_wa, lru_ba,
           lru_wx, lru_bx, lru_lam, lru_w_out, attn_w_qkv, attn_q_gain, attn_k_gain, attn_w_o, gmlp_w_in, gmlp_ln_g,
           gmlp_ln_b, gmlp_w_s, gmlp_b_s, gmlp_w_out, moe_router, moe_w_gate, moe_w_up, moe_w_down, final_gain):
    b, n_tok, d = x.shape
    n_ctx = ctx.shape[1]
    depth = mod_w.shape[0]
    n_mixers = 4
    xl = x.reshape(b * n_tok, d)
    xc = ctx.reshape(b * n_ctx, d)

    pad = -(b + 1) % SUBLANES
    cvec = jnp.concatenate([c, c_ctx[None, :], jnp.zeros((pad, d), F32)], axis=0)
    mods = mod_vectors(cvec, mod_w, mod_b)

    for i in range(depth):
        m, j = i % n_mixers, i // n_mixers
        last = i == depth - 1
        ml = [mods[i, :b, k * d:(k + 1) * d].reshape(b, 1, d) for k in range(6)]
        mc = [mods[i, b:b + 1, k * d:(k + 1) * d].reshape(1, 1, d) for k in range(6)]
        if m == 0:
            pw = pool_w[j].astype(BF16)
            xl_new = pool_mixer_residual(xl, ml[0], ml[1], ml[2], pw, pool_scale[j], n_tok)
            if not last:
                xc = pool_mixer_residual(xc, mc[0], mc[1], mc[2], pw, pool_scale[j], n_ctx)
            xl = xl_new
        elif m == 1:
            xl, xc_new = rglru_residual(xl, xc, ml[:3], mc[:3], lru_w_in[j], lru_conv_w[j], lru_conv_b[j], lru_wa[j],
                                        lru_ba[j], lru_wx[j], lru_bx[j], lru_lam[j], lru_w_out[j], n_tok, n_ctx)
            xc = xc if last else xc_new
        elif m == 2:
            xl, xc_new = gqa_residual(xl, xc, ml[:3], mc[:3], attn_w_qkv[j], attn_q_gain[j], attn_k_gain[j],
                                      attn_w_o[j], n_tok, n_ctx)
            xc = xc if last else xc_new
        else:
            gargs = (gmlp_w_in[j], gmlp_ln_g[j], gmlp_ln_b[j], gmlp_w_s[j], gmlp_b_s[j], gmlp_w_out[j])
            xl_new = gmlp_residual(xl, ml[0], ml[1], ml[2], *gargs, n_tok)
            if not last:
                xc = gmlp_residual(xc, mc[0], mc[1], mc[2], *gargs, n_ctx)
            xl = xl_new
        wr = jnp.pad(moe_router[i], ((0, 0), (0, LANES - N_EXPERTS)))
        xl = moe_residual(xl, ml[3], ml[4], ml[5], wr, moe_w_gate, moe_w_up, moe_w_down, i, n_tok)
        if not last:
            xc = moe_residual(xc, mc[3], mc[4], mc[5], wr, moe_w_gate, moe_w_up, moe_w_down, i, n_ctx)
    return final_norm(xl, final_gain).reshape(b, n_tok, d)
```

```python
import functools

import jax
import jax.numpy as jnp
from jax import lax
from jax.experimental import pallas as pl
from jax.experimental.pallas import tpu as pltpu

F32 = jnp.float32
BF16 = jnp.bfloat16
EPS = 1e-6

N_EXPERTS = 16
CAPACITY_FACTOR = 2
POOL_WINDOWS = (2, 4, 8, 16)
POOL_HALO = 8
LRU_BLOCKS = 8
LRU_C = 8.0
CONV_W = 4
HEAD_DIM = 64
N_KV_HEADS = 4
GRID_W = 64
ROPE_THETA = 10000.0
GMLP_GROUPS = 4
GMLP_CHUNK = 128

LANES = 128
SUBLANES = 8
VMEM_LIMIT = 56 * 1024 * 1024


def _cparams(*sem):
    return pltpu.CompilerParams(dimension_semantics=sem, vmem_limit_bytes=VMEM_LIMIT)


def _modulate(x, shift, scale):
    ms = jnp.mean(x * x, axis=-1, keepdims=True)
    return x * lax.rsqrt(ms + EPS) * (1.0 + scale) + shift


def _split_bf16(a):
    hi = a.astype(BF16)
    lo = (a - hi.astype(F32)).astype(BF16)
    return hi, lo


def _dot(a, b):
    return jnp.dot(a, b, preferred_element_type=F32)


def _dot3(a, b):
    a_hi, a_lo = _split_bf16(a)
    b_hi, b_lo = _split_bf16(b)
    return _dot(a_hi, b_hi) + (_dot(a_hi, b_lo) + _dot(a_lo, b_hi))


def _gelu(x):
    return 0.5 * x * (1.0 + jnp.tanh(0.7978845608028654 * (x + 0.044715 * (x * x * x))))


def _silu(x):
    return x * (1.0 / (1.0 + jnp.exp(-x)))


def _sigmoid(x):
    return 1.0 / (1.0 + jnp.exp(-x))


def _mod_spec(mod, tiles_per_batch):
    d = mod.shape[-1]
    if mod.shape[0] == 1:
        return pl.BlockSpec((1, 1, d), lambda i, *_: (0, 0, 0))
    return pl.BlockSpec((1, 1, d), lambda i, *_: (i // tiles_per_batch, 0, 0))


def _row_tile(n, want):
    t = min(n, want)
    assert n % t == 0
    return t


def _mod_kernel(c_ref, w_ref, b_ref, o_ref):
    s = _silu(c_ref[...])
    o_ref[0] = _dot3(s, w_ref[0]) + b_ref[0]


def mod_vectors(cvec, mod_w, mod_b):
    depth, d, n6 = mod_w.shape
    tn = 1024
    return pl.pallas_call(
        _mod_kernel,
        out_shape=jax.ShapeDtypeStruct((depth, cvec.shape[0], n6), F32),
        grid=(depth, n6 // tn),
        in_specs=[
            pl.BlockSpec(cvec.shape, lambda l, j: (0, 0)),
            pl.BlockSpec((1, d, tn), lambda l, j: (l, 0, j)),
            pl.BlockSpec((1, 1, tn), lambda l, j: (l, 0, j)),
        ],
        out_specs=pl.BlockSpec((1, cvec.shape[0], tn), lambda l, j: (l, 0, j)),
        compiler_params=_cparams("parallel", "parallel"),
        name="mod_vectors",
    )(cvec, mod_w, mod_b.reshape(depth, 1, n6))


def _modmm_kernel(x_ref, sh_ref, sc_ref, w_ref, *o_refs, splits, acts):
    h = _modulate(x_ref[...], sh_ref[0], sc_ref[0]).astype(BF16)
    z = _dot(h, w_ref[...])
    off = 0
    for o_ref, width, act in zip(o_refs, splits, acts):
        part = z[:, off:off + width]
        if act == "gelu":
            part = _gelu(part)
        o_ref[...] = part.astype(o_ref.dtype)
        off += width


def modmm(x, shift, scale, w, n_tok, splits, acts, dtypes, tm=512):
    r, d = x.shape
    tm = _row_tile(n_tok, tm)
    tpb = n_tok // tm
    n = w.shape[1]
    assert sum(splits) == n
    return pl.pallas_call(
        functools.partial(_modmm_kernel, splits=splits, acts=acts),
        out_shape=[jax.ShapeDtypeStruct((r, s), dt) for s, dt in zip(splits, dtypes)],
        grid=(r // tm,),
        in_specs=[
            pl.BlockSpec((tm, d), lambda i: (i, 0)),
            _mod_spec(shift, tpb),
            _mod_spec(scale, tpb),
            pl.BlockSpec((d, n), lambda i: (0, 0)),
        ],
        out_specs=[pl.BlockSpec((tm, s), lambda i: (i, 0)) for s in splits],
        compiler_params=_cparams("parallel"),
        name="modmm",
    )(x, shift, scale, w)


def _mmres_kernel(*refs, n_sum, has_mul):
    a = refs[0][...]
    for a_ref in refs[1:n_sum]:
        a = a + a_ref[...]
    refs = refs[n_sum:]
    if has_mul:
        a = a.astype(F32) * refs[0][...].astype(F32)
        refs = refs[1:]
    w_ref, x_ref, g_ref, o_ref = refs
    o_ref[...] = x_ref[...] + g_ref[0] * _dot(a.astype(BF16), w_ref[...])


def mm_residual(a, w, x, gate, n_tok, b=None, tm=512):
    a = a if isinstance(a, (tuple, list)) else (a,)
    r, k = a[0].shape
    n = w.shape[1]
    tm = _row_tile(n_tok, tm)
    tpb = n_tok // tm
    ins = list(a) + ([b] if b is not None else []) + [w, x, gate]
    specs = [pl.BlockSpec((tm, k), lambda i: (i, 0))] * (len(a) + (b is not None))
    specs += [
        pl.BlockSpec((k, n), lambda i: (0, 0)),
        pl.BlockSpec((tm, n), lambda i: (i, 0)),
        _mod_spec(gate, tpb),
    ]
    return pl.pallas_call(
        functools.partial(_mmres_kernel, n_sum=len(a), has_mul=b is not None),
        out_shape=jax.ShapeDtypeStruct((r, n), F32),
        grid=(r // tm,),
        in_specs=specs,
        out_specs=pl.BlockSpec((tm, n), lambda i: (i, 0)),
        compiler_params=_cparams("parallel"),
        name="mm_residual",
    )(*ins)


def _pool_kernel(xp_ref, x_ref, xn_ref, sh_ref, sc_ref, g_ref, w_ref, ps_ref, o_ref, *, tm, n_tok, tpb):
    t0 = (pl.program_id(0) % tpb) * tm
    x = x_ref[...]
    xe = jnp.concatenate([xp_ref[...], x, xn_ref[...]], axis=0)
    rows = tm + 2 * POOL_HALO
    h = _modulate(xe, sh_ref[0], sc_ref[0])
    pos = lax.broadcasted_iota(jnp.int32, (rows, 1), 0) + (t0 - POOL_HALO)
    h = jnp.where((pos >= 0) & (pos < n_tok), h, 0.0)
    posc = pos[POOL_HALO:POOL_HALO + tm]
    gw = h.shape[1] // len(POOL_WINDOWS)
    outs = []
    for g, win in enumerate(POOL_WINDOWS):
        hg = h[:, g * gw:(g + 1) * gw]
        c = hg + pltpu.roll(hg, 1, 0)
        step = 1
        while 2 * step < win:
            c = pltpu.roll(c, step, 0) + pltpu.roll(c, rows - step, 0)
            step *= 2
        cnt = jnp.minimum(posc + (win - win // 2), n_tok) - jnp.maximum(posc - win // 2, 0)
        pooled = c[POOL_HALO:POOL_HALO + tm] / cnt.astype(F32) - hg[POOL_HALO:POOL_HALO + tm]
        outs.append(_dot(pooled.astype(BF16), w_ref[g]))
    y = jnp.concatenate(outs, axis=1) * ps_ref[...]
    o_ref[...] = x + g_ref[0] * y


def pool_mixer_residual(x, shift, scale, gate, w_pool, pool_scale, n_tok, tm=256):
    r, d = x.shape
    tm = _row_tile(n_tok, tm)
    tpb = n_tok // tm
    hb = tm // POOL_HALO
    last = r // POOL_HALO - 1
    groups, gw, _ = w_pool.shape
    return pl.pallas_call(
        functools.partial(_pool_kernel, tm=tm, n_tok=n_tok, tpb=tpb),
        out_shape=jax.ShapeDtypeStruct((r, d), F32),
        grid=(r // tm,),
        in_specs=[
            pl.BlockSpec((POOL_HALO, d), lambda i: (jnp.maximum(i * hb - 1, 0), 0)),
            pl.BlockSpec((tm, d), lambda i: (i, 0)),
            pl.BlockSpec((POOL_HALO, d), lambda i: (jnp.minimum((i + 1) * hb, last), 0)),
            _mod_spec(shift, tpb),
            _mod_spec(scale, tpb),
            _mod_spec(gate, tpb),
            pl.BlockSpec((groups, gw, gw), lambda i: (0, 0, 0)),
            pl.BlockSpec((1, d), lambda i: (0, 0)),
        ],
        out_specs=pl.BlockSpec((tm, d), lambda i: (i, 0)),
        compiler_params=_cparams("parallel"),
        name="pool_mixer",
    )(x, x, x, shift, scale, gate, w_pool, pool_scale.reshape(1, d))


def _router_kernel(x_ref, sh_ref, sc_ref, wr_ref, h_ref, aff_ref):
    h = _modulate(x_ref[...], sh_ref[0], sc_ref[0])
    h_ref[...] = h.astype(h_ref.dtype)
    logits = _dot3(h, wr_ref[...])
    lane = lax.broadcasted_iota(jnp.int32, logits.shape, 1)
    logits = jnp.where(lane < N_EXPERTS, logits, -jnp.inf)
    e = jnp.exp(logits - jnp.max(logits, axis=-1, keepdims=True))
    aff_ref[...] = e / jnp.sum(e, axis=-1, keepdims=True)


def router(x, shift, scale, w_router_padded, n_tok, tm=512):
    r, d = x.shape
    tm = _row_tile(n_tok, tm)
    tpb = n_tok // tm
    return pl.pallas_call(
        _router_kernel,
        out_shape=[jax.ShapeDtypeStruct((r, d), BF16), jax.ShapeDtypeStruct((r, LANES), F32)],
        grid=(r // tm,),
        in_specs=[
            pl.BlockSpec((tm, d), lambda i: (i, 0)),
            _mod_spec(shift, tpb),
            _mod_spec(scale, tpb),
            pl.BlockSpec((d, LANES), lambda i: (0, 0)),
        ],
        out_specs=[pl.BlockSpec((tm, d), lambda i: (i, 0)), pl.BlockSpec((tm, LANES), lambda i: (i, 0))],
        compiler_params=_cparams("parallel"),
        name="moe_router",
    )(x, shift, scale, w_router_padded)


MOE_CHUNK = 128
PACK = LANES // N_EXPERTS


def _cumsum_rows(v, tb):
    n = v.shape[0]
    tri = (lax.broadcasted_iota(jnp.int32, (tb, tb), 0) >= lax.broadcasted_iota(jnp.int32, (tb, tb), 1)).astype(BF16)
    carry = jnp.zeros((1, v.shape[1]), F32)
    outs, starts = [], []
    for j in range(n // tb):
        starts.append(carry)
        c = _dot(tri, v[j * tb:(j + 1) * tb].astype(BF16)) + carry
        carry = c[tb - 1:tb, :]
        outs.append(c)
    return jnp.concatenate(outs, axis=0), jnp.concatenate(starts, axis=0)


def _route_kernel(aff_ref, affp_ref, idx_ref, slot_ref, start_ref, slot_scr, acc_scr, *, n, cap):
    def bisect(k, prefix):
        cand = prefix | jnp.left_shift(jnp.int32(1), 30 - k)
        cnt = jnp.sum((pltpu.bitcast(affp_ref[...], jnp.int32) >= cand).astype(F32), axis=0, keepdims=True)
        shift = N_EXPERTS
        while shift < LANES:
            cnt = cnt + pltpu.roll(cnt, shift, 1)
            shift *= 2
        return jnp.where(cnt >= cap, cand, prefix)

    thr = lax.fori_loop(0, 31, bisect, jnp.zeros((1, LANES), jnp.int32))
    bits = pltpu.bitcast(aff_ref[...], jnp.int32)
    gt = bits > thr
    eq = bits == thr
    need = cap - jnp.sum(gt.astype(F32), axis=0, keepdims=True)
    eq_rank, _ = _cumsum_rows(eq.astype(F32), MOE_CHUNK)
    sel = gt | (eq & (eq_rank <= need))
    pos, starts = _cumsum_rows(sel.astype(F32), MOE_CHUNK)
    slot_scr[...] = jnp.where(sel, pos - 1.0, -1.0)
    slot_ref[...] = slot_scr[...]
    start_ref[0] = starts.astype(jnp.int32)

    slots = lax.broadcasted_iota(jnp.int32, (1, cap), 1).astype(F32)
    sub = lax.broadcasted_iota(jnp.int32, (SUBLANES, MOE_CHUNK), 0)
    local = lax.broadcasted_iota(jnp.int32, (SUBLANES, MOE_CHUNK), 1).astype(F32)
    acc_scr[...] = jnp.zeros(acc_scr.shape, F32)

    def chunk(c, carry):
        rows = pl.ds(pl.multiple_of(c * MOE_CHUNK, MOE_CHUNK), MOE_CHUNK)
        lhs = jnp.where(sub == 0, local, jnp.where(sub == 1, c.astype(F32), 0.0)).astype(BF16)
        for e in range(N_EXPERTS):
            onehot = jnp.where(slot_scr[rows, e:e + 1] == slots, 1.0, 0.0).astype(BF16)
            acc_scr[e] += _dot(lhs, onehot)
        return carry

    lax.fori_loop(0, n // MOE_CHUNK, chunk, 0)
    for e in range(N_EXPERTS):
        idx_ref[0, e:e + 1, :] = (acc_scr[e, 0:1, :] + MOE_CHUNK * acc_scr[e, 1:2, :]).astype(jnp.int32)


def route(aff, n_tok):
    r = aff.shape[0]
    b = r // n_tok
    cap = CAPACITY_FACTOR * n_tok // N_EXPERTS
    aff_packed = aff[:, :N_EXPERTS].reshape(r // PACK, LANES)
    n_chunks = n_tok // MOE_CHUNK
    return pl.pallas_call(
        functools.partial(_route_kernel, n=n_tok, cap=cap),
        out_shape=[jax.ShapeDtypeStruct((b, N_EXPERTS, cap), jnp.int32), jax.ShapeDtypeStruct((r, LANES), F32),
                   jax.ShapeDtypeStruct((b, n_chunks, LANES), jnp.int32)],
        grid=(b,),
        in_specs=[pl.BlockSpec((n_tok, LANES), lambda i: (i, 0)),
                  pl.BlockSpec((n_tok // PACK, LANES), lambda i: (i, 0))],
        out_specs=[pl.BlockSpec((1, N_EXPERTS, cap), lambda i: (i, 0, 0)),
                   pl.BlockSpec((n_tok, LANES), lambda i: (i, 0)),
                   pl.BlockSpec((1, n_chunks, LANES), lambda i: (i, 0, 0))],
        scratch_shapes=[pltpu.VMEM((n_tok, LANES), F32), pltpu.VMEM((N_EXPERTS, SUBLANES, cap), F32)],
        compiler_params=_cparams("parallel"),
        name="moe_route",
    )(aff, aff_packed)


def _ffn_kernel(x_ref, wg_ref, wu_ref, wd_ref, o_ref, wg_scr, wu_scr, wd_scr):
    @pl.when(pl.program_id(1) == 0)
    def _():
        wg_scr[...] = wg_ref[0, 0].astype(BF16)
        wu_scr[...] = wu_ref[0, 0].astype(BF16)
        wd_scr[...] = wd_ref[0, 0].astype(BF16)

    x = x_ref[0]
    a = _dot(x, wg_scr[...])
    u = _dot(x, wu_scr[...])
    hmid = (_silu(a) * u).astype(BF16)
    o_ref[0] = _dot(hmid, wd_scr[...]).astype(o_ref.dtype)


def expert_ffn(xs, w_gate, w_up, w_down, layer, tm=512):
    e, m, d = xs.shape
    ff = w_gate.shape[-1]
    tm = _row_tile(m, tm)
    return pl.pallas_call(
        _ffn_kernel,
        out_shape=jax.ShapeDtypeStruct((e, m, d), BF16),
        grid=(e, m // tm),
        in_specs=[
            pl.BlockSpec((1, tm, d), lambda k, i: (k, i, 0)),
            pl.BlockSpec((1, 1, d, ff), lambda k, i: (layer, k, 0, 0)),
            pl.BlockSpec((1, 1, d, ff), lambda k, i: (layer, k, 0, 0)),
            pl.BlockSpec((1, 1, ff, d), lambda k, i: (layer, k, 0, 0)),
        ],
        out_specs=pl.BlockSpec((1, tm, d), lambda k, i: (k, i, 0)),
        scratch_shapes=[pltpu.VMEM((d, ff), BF16), pltpu.VMEM((d, ff), BF16), pltpu.VMEM((ff, d), BF16)],
        compiler_params=_cparams("parallel", "arbitrary"),
        name="moe_expert_ffn",
    )(xs, w_gate, w_up, w_down)


def _combine_kernel(off_ref, x_ref, g_ref, slot_ref, aff_ref, y_ref, *rest, win, n_chunks, cps, final):
    if final:
        fg_ref, o_ref = rest
    else:
        (o_ref,) = rest
    align = min(win, LANES)
    for sc in range(cps):
        rows = slice(sc * MOE_CHUNK, (sc + 1) * MOE_CHUNK)
        base = (pl.program_id(0) * n_chunks + pl.program_id(1) * cps + sc) * N_EXPERTS
        acc = jnp.zeros((MOE_CHUNK, x_ref.shape[1]), F32)
        for e in range(N_EXPERTS):
            off = pl.multiple_of(off_ref[base + e], align)
            cols = (lax.broadcasted_iota(jnp.int32, (1, win), 1) + off).astype(F32)
            onehot = jnp.where(slot_ref[rows, e:e + 1] == cols, 1.0, 0.0).astype(BF16)
            acc = acc + aff_ref[rows, e:e + 1] * _dot(onehot, y_ref[e, pl.ds(off, win), :])
        out = x_ref[rows, :] + g_ref[0] * acc
        if final:
            out = out * lax.rsqrt(jnp.mean(out * out, axis=-1, keepdims=True) + EPS) * fg_ref[...]
        o_ref[rows, :] = out


def moe_combine(x, gate2, slot, aff, y, offs, n_tok, final_gain=None):
    r, d = x.shape
    b = r // n_tok
    n_exp, _, _ = y.shape
    cap = y.shape[1] // b
    n_chunks = n_tok // MOE_CHUNK
    cps = 2 if n_chunks % 2 == 0 else 1
    tm = cps * MOE_CHUNK
    steps = n_chunks // cps
    win = min(2 * MOE_CHUNK, cap)
    final = final_gain is not None
    row_spec = lambda w: pl.BlockSpec((tm, w), lambda bi, c, off: (bi * steps + c, 0))
    ins = [offs, x, gate2, slot, aff, y]
    specs = [row_spec(d), _mod_spec(gate2, 1), row_spec(LANES), row_spec(LANES),
             pl.BlockSpec((n_exp, cap, d), lambda bi, c, off: (0, bi, 0))]
    if final:
        ins.append(final_gain.reshape(1, d))
        specs.append(pl.BlockSpec((1, d), lambda bi, c, off: (0, 0)))
    return pl.pallas_call(
        functools.partial(_combine_kernel, win=win, n_chunks=n_chunks, cps=cps, final=final),
        out_shape=jax.ShapeDtypeStruct((r, d), F32),
        grid_spec=pltpu.PrefetchScalarGridSpec(
            num_scalar_prefetch=1,
            grid=(b, steps),
            in_specs=specs,
            out_specs=row_spec(d),
        ),
        compiler_params=_cparams("parallel", "arbitrary"),
        name="moe_combine",
    )(*ins)


def moe_residual(x, shift, scale, gate2, w_router_padded, w_gate, w_up, w_down, layer, n_tok, final_gain=None):
    r, d = x.shape
    b = r // n_tok
    h, aff = router(x, shift, scale, w_router_padded, n_tok)
    idx, slot, start = route(aff, n_tok)
    cap = idx.shape[-1]
    gidx = idx + (jnp.arange(b, dtype=jnp.int32) * n_tok)[:, None, None]
    gidx = jnp.transpose(gidx, (1, 0, 2)).reshape(N_EXPERTS, b * cap)
    xs = h.at[gidx].get(mode="promise_in_bounds")
    y = expert_ffn(xs, w_gate, w_up, w_down, layer)
    win = min(2 * MOE_CHUNK, cap)
    align = min(win, LANES)
    offs = jnp.clip(start[:, :, :N_EXPERTS] // align * align, 0, cap - win).reshape(-1)
    return moe_combine(x, gate2, slot, aff, y, offs, n_tok, final_gain)


def _head_rms(t, seg_ones):
    outs = []
    for j in range(t.shape[1] // LANES):
        blk = t[:, j * LANES:(j + 1) * LANES]
        hi, lo = _split_bf16(blk * blk)
        ss = _dot(hi, seg_ones) + _dot(lo, seg_ones)
        outs.append(blk * lax.rsqrt(ss * (1.0 / HEAD_DIM) + EPS))
    return jnp.concatenate(outs, axis=1)


def _rope(t, cos, sin_signed):
    w = t.shape[1]
    half = HEAD_DIM // 2
    lane = lax.broadcasted_iota(jnp.int32, t.shape, 1)
    partner = jnp.where(lane % HEAD_DIM < half, pltpu.roll(t, w - half, 1), pltpu.roll(t, half, 1))
    reps = w // cos.shape[1]
    return t * jnp.concatenate([cos] * reps, axis=1) + partner * jnp.concatenate([sin_signed] * reps, axis=1)


def _qkv_kernel(*refs, qd, kd, rope):
    if rope:
        x_ref, sh_ref, sc_ref, w_ref, qg_ref, kg_ref, cos_ref, sin_ref, q_ref, k_ref, v_ref = refs
    else:
        x_ref, sh_ref, sc_ref, w_ref, qg_ref, kg_ref, q_ref, k_ref, v_ref = refs
    h = _modulate(x_ref[...], sh_ref[0], sc_ref[0]).astype(BF16)
    z = _dot(h, w_ref[...])
    seg = (lax.broadcasted_iota(jnp.int32, (LANES, LANES), 0) // HEAD_DIM
           == lax.broadcasted_iota(jnp.int32, (LANES, LANES), 1) // HEAD_DIM).astype(BF16)
    q = _head_rms(z[:, :qd], seg) * qg_ref[...]
    k = _head_rms(z[:, qd:qd + kd], seg) * kg_ref[...]
    if rope:
        q = _rope(q, cos_ref[...], sin_ref[...])
        k = _rope(k, cos_ref[...], sin_ref[...])
    q_ref[...] = (q * (HEAD_DIM ** -0.5 * 1.4426950408889634)).astype(q_ref.dtype)
    k_ref[...] = k.astype(k_ref.dtype)
    v_ref[...] = z[:, qd + kd:].astype(v_ref.dtype)


def qkv_project(x, shift, scale, w_qkv, q_gain, k_gain, n_tok, rope_tables=None, tm=256):
    r, d = x.shape
    kd = N_KV_HEADS * HEAD_DIM
    qd = w_qkv.shape[1] - 2 * kd
    tm = _row_tile(n_tok, tm)
    tpb = n_tok // tm
    ins = [x, shift, scale, w_qkv, q_gain, k_gain]
    specs = [
        pl.BlockSpec((tm, d), lambda i: (i, 0)),
        _mod_spec(shift, tpb),
        _mod_spec(scale, tpb),
        pl.BlockSpec(w_qkv.shape, lambda i: (0, 0)),
        pl.BlockSpec((1, qd), lambda i: (0, 0)),
        pl.BlockSpec((1, kd), lambda i: (0, 0)),
    ]
    if rope_tables is not None:
        ins += list(rope_tables)
        specs += [pl.BlockSpec((tm, LANES), lambda i: (i % tpb, 0))] * 2
    return pl.pallas_call(
        functools.partial(_qkv_kernel, qd=qd, kd=kd, rope=rope_tables is not None),
        out_shape=[jax.ShapeDtypeStruct((r, qd), BF16), jax.ShapeDtypeStruct((r, kd), BF16),
                   jax.ShapeDtypeStruct((r, kd), BF16)],
        grid=(r // tm,),
        in_specs=specs,
        out_specs=[pl.BlockSpec((tm, qd), lambda i: (i, 0)), pl.BlockSpec((tm, kd), lambda i: (i, 0)),
                   pl.BlockSpec((tm, kd), lambda i: (i, 0))],
        compiler_params=_cparams("parallel"),
        name="qkv_project",
    )(*ins)


def _attn_kernel(q_ref, k_ref, vt_ref, o_ref, *scr, grp, bounded):
    k = k_ref[0, 0]
    vt = vt_ref[0, 0]
    tq = q_ref.shape[0]
    n_pairs = grp // 2
    outs = []

    def scores(pair):
        heads = (2 * pair, 2 * pair + 1)
        q2 = jnp.concatenate([q_ref[:, h * HEAD_DIM:(h + 1) * HEAD_DIM] for h in heads], axis=0)
        return lax.dot_general(k, q2, (((1,), (1,)), ((), ())), preferred_element_type=F32)

    if bounded:
        for pair in range(n_pairs):
            p = jnp.exp2(scores(pair))
            scr[pair][...] = p.astype(BF16)
            ot = _dot(vt[:HEAD_DIM], scr[pair][...]) / jnp.sum(p, axis=0, keepdims=True)
            o = jnp.concatenate([ot, ot], axis=0).T
            outs += [o[:tq, :HEAD_DIM], o[tq:, :HEAD_DIM]]
        p_refs = ()
    else:
        for pair in range(n_pairs):
            scr[2 * pair][...] = scores(pair)
        for pair in range(n_pairs):
            st_scr, p_scr = scr[2 * pair], scr[2 * pair + 1]
            p_scr[...] = jnp.exp2(st_scr[...] - jnp.max(st_scr[...], axis=0, keepdims=True)).astype(BF16)
        p_refs = scr[1::2]
    for p_scr in p_refs:
        ot = _dot(vt, p_scr[...])
        ot = ot[:HEAD_DIM] / ot[HEAD_DIM:]
        o = jnp.concatenate([ot, ot], axis=0).T
        outs += [o[:tq, :HEAD_DIM], o[tq:, :HEAD_DIM]]
    o_ref[...] = jnp.concatenate(outs, axis=1).astype(o_ref.dtype)


def _attention_call(q, k, vt, n_q, tq, bounded):
    r, qd = q.shape
    _, kvh, n_k, hd = k.shape
    grp = qd // (kvh * hd)
    tq = _row_tile(n_q, tq)
    tpb = n_q // tq
    dts = (BF16,) if bounded else (F32, BF16)
    return pl.pallas_call(
        functools.partial(_attn_kernel, grp=grp, bounded=bounded),
        out_shape=jax.ShapeDtypeStruct((r, qd), BF16),
        grid=(r // n_q, kvh, tpb),
        in_specs=[
            pl.BlockSpec((tq, grp * hd), lambda b, g, i: (b * tpb + i, g)),
            pl.BlockSpec((1, 1, n_k, hd), lambda b, g, i: (b, g, 0, 0)),
            pl.BlockSpec((1, 1, 2 * hd, n_k), lambda b, g, i: (b, g, 0, 0)),
        ],
        out_specs=pl.BlockSpec((tq, grp * hd), lambda b, g, i: (b * tpb + i, g)),
        scratch_shapes=[pltpu.VMEM((n_k, 2 * tq), dt) for _ in range(grp // 2) for dt in dts],
        compiler_params=_cparams("parallel", "parallel", "parallel"),
        name="attention_bounded" if bounded else "attention",
    )(q, k, vt)


SCORE_BOUND = 60.0


def attention(q, k, vt, n_q, score_bound, tq=256):
    return lax.cond(score_bound <= SCORE_BOUND,
                    lambda: _attention_call(q, k, vt, n_q, tq, True),
                    lambda: _attention_call(q, k, vt, n_q, tq, False))


def rope_tables(n_tok):
    t = jnp.arange(n_tok)
    row = (t // GRID_W).astype(F32)
    col = (t % GRID_W).astype(F32)
    n_freq = HEAD_DIM // 4
    inv = ROPE_THETA ** (-jnp.arange(n_freq, dtype=F32) / n_freq)
    ang = jnp.concatenate([row[:, None] * inv, col[:, None] * inv], axis=-1)
    cos, sin = jnp.cos(ang), jnp.sin(ang)
    reps = LANES // HEAD_DIM
    return jnp.tile(jnp.concatenate([cos, cos], -1), (1, reps)), jnp.tile(jnp.concatenate([-sin, sin], -1), (1, reps))


def _deinterleave_heads(w, n_heads):
    lead = w.shape[:-1]
    w = w.reshape(lead + (n_heads, HEAD_DIM // 2, 2))
    return jnp.swapaxes(w, -1, -2).reshape(lead + (n_heads * HEAD_DIM,))


def _split_kv_heads(t, b):
    return jnp.transpose(t.reshape(b, -1, N_KV_HEADS, HEAD_DIM), (0, 2, 1, 3))


def gqa_residual(x, xc, mods_l, mods_c, w_qkv, q_gain, k_gain, w_o, n_tok, n_ctx):
    b = x.shape[0] // n_tok
    kd = N_KV_HEADS * HEAD_DIM
    qd = w_qkv.shape[1] - 2 * kd
    n_qh = qd // HEAD_DIM
    w_perm = jnp.concatenate([_deinterleave_heads(w_qkv[:, :qd], n_qh),
                              _deinterleave_heads(w_qkv[:, qd:qd + kd], N_KV_HEADS), w_qkv[:, qd + kd:]], axis=1)
    w_perm = w_perm.astype(BF16)
    qg = jnp.tile(_deinterleave_heads(q_gain, 1), n_qh).reshape(1, qd)
    kg = jnp.tile(_deinterleave_heads(k_gain, 1), N_KV_HEADS).reshape(1, kd)
    sh_l, sc_l, g_l = mods_l
    sh_c, sc_c, g_c = mods_c
    q_l, k_l, v_l = qkv_project(x, sh_l, sc_l, w_perm, qg, kg, n_tok, rope_tables(n_tok))
    q_c, k_c, v_c = qkv_project(xc, sh_c, sc_c, w_perm, qg, kg, n_ctx)
    k_c4 = _split_kv_heads(k_c, b)
    k_all = jnp.concatenate([k_c4, _split_kv_heads(k_l, b)], axis=2)

    def values_t(v):
        vt = jnp.transpose(v.reshape(b, -1, N_KV_HEADS, HEAD_DIM), (0, 2, 3, 1))
        return jnp.concatenate([vt, jnp.ones_like(vt)], axis=2)

    vt_c = values_t(v_c)
    vt_all = jnp.concatenate([vt_c, values_t(v_l)], axis=3)
    score_bound = (1.01 * HEAD_DIM * HEAD_DIM ** -0.5 * 1.4426950408889634
                   * jnp.max(jnp.abs(q_gain)) * jnp.max(jnp.abs(k_gain)))
    o_l = attention(q_l, k_all, vt_all, n_tok, score_bound)
    o_c = attention(q_c, k_c4, vt_c, n_ctx, score_bound)
    w_o = w_o.astype(BF16)
    return mm_residual(o_l, w_o, x, g_l, n_tok), mm_residual(o_c, w_o, xc, g_c, n_ctx)


def _lru_kernel(xp_ref, x_ref, xn_ref, cw_ref, cb_ref, wa_ref, ba_ref, wx_ref, bx_ref, lam_ref, h0_ref,
                o_ref, hT_ref, a_scr, u_scr, carry_scr, *, tm, n_tok, tpb, reverse):
    step = pl.program_id(1)
    t_idx = (tpb - 1 - step) if reverse else step
    t0 = t_idx * tm
    rows = tm + 2 * SUBLANES

    @pl.when(step == 0)
    def _():
        carry_scr[...] = h0_ref[0]

    xe = jnp.concatenate([xp_ref[...], x_ref[...], xn_ref[...]], axis=0)
    pos = lax.broadcasted_iota(jnp.int32, (rows, 1), 0) + (t0 - SUBLANES)
    xe = jnp.where((pos >= 0) & (pos < n_tok), xe, 0.0)
    left = CONV_W // 2
    conv = cb_ref[...]
    for k in range(CONV_W):
        shift = (left - k) % rows
        tap = xe if shift == 0 else pltpu.roll(xe, shift, 0)
        conv = conv + tap * cw_ref[k:k + 1, :]
    xr = conv[SUBLANES:SUBLANES + tm]

    xb = xr.astype(BF16)
    bw = xr.shape[1] // LRU_BLOCKS
    ra, ia = [], []
    for j in range(LRU_BLOCKS):
        blk = xb[:, j * bw:(j + 1) * bw]
        ra.append(_dot(blk, wa_ref[j]))
        ia.append(_dot(blk, wx_ref[j]))
    r = _sigmoid(jnp.concatenate(ra, axis=1) + ba_ref[...])
    i = _sigmoid(jnp.concatenate(ia, axis=1) + bx_ref[...])
    log_a = -LRU_C * r * jnp.logaddexp(-lam_ref[...], 0.0)
    a = jnp.exp(log_a)
    a_scr[...] = a
    t = jnp.tanh(log_a)
    u_scr[...] = xr * i * jnp.sqrt(-2.0 * t / (1.0 - t))

    n_grp = tm // SUBLANES
    sub = lax.broadcasted_iota(jnp.int32, (SUBLANES, 1), 0)

    def group(j, carry):
        g = (n_grp - 1 - j) if reverse else j
        rws = pl.ds(pl.multiple_of(g * SUBLANES, SUBLANES), SUBLANES)
        ag, ug = a_scr[rws, :], u_scr[rws, :]
        s = 1
        while s < SUBLANES:
            if reverse:
                ok = sub < SUBLANES - s
                sh = SUBLANES - s
            else:
                ok = sub >= s
                sh = s
            u_prev = jnp.where(ok, pltpu.roll(ug, sh, 0), 0.0)
            a_prev = jnp.where(ok, pltpu.roll(ag, sh, 0), 1.0)
            ug = ug + ag * u_prev
            ag = ag * a_prev
            s *= 2
        hg = ug + ag * carry
        o_ref[rws, :] = hg
        return hg[0:1, :] if reverse else hg[SUBLANES - 1:SUBLANES, :]

    carry = lax.fori_loop(0, n_grp, group, carry_scr[...])
    carry_scr[...] = carry
    hT_ref[0] = carry


def lru_scan(xpre, conv_w, conv_b, wa, ba, wx, bx, lam, h0, n_tok, reverse, tm=256):
    r, w = xpre.shape
    b = r // n_tok
    tm = _row_tile(n_tok, tm)
    tpb = n_tok // tm
    hb = tm // SUBLANES
    last = r // SUBLANES - 1

    def tile(bi, s):
        return bi * tpb + ((tpb - 1 - s) if reverse else s)

    vec = pl.BlockSpec((1, w), lambda bi, s: (0, 0))
    blocks = pl.BlockSpec(wa.shape, lambda bi, s: (0, 0, 0))
    return pl.pallas_call(
        functools.partial(_lru_kernel, tm=tm, n_tok=n_tok, tpb=tpb, reverse=reverse),
        out_shape=[jax.ShapeDtypeStruct((r, w), F32), jax.ShapeDtypeStruct((b, 1, w), F32)],
        grid=(b, tpb),
        in_specs=[
            pl.BlockSpec((SUBLANES, w), lambda bi, s: (jnp.maximum(tile(bi, s) * hb - 1, 0), 0)),
            pl.BlockSpec((tm, w), lambda bi, s: (tile(bi, s), 0)),
            pl.BlockSpec((SUBLANES, w), lambda bi, s: (jnp.minimum((tile(bi, s) + 1) * hb, last), 0)),
            pl.BlockSpec((CONV_W, w), lambda bi, s: (0, 0)),
            vec, blocks, vec, blocks, vec, vec,
            pl.BlockSpec((1, 1, w), lambda bi, s: (bi, 0, 0)),
        ],
        out_specs=[pl.BlockSpec((tm, w), lambda bi, s: (tile(bi, s), 0)),
                   pl.BlockSpec((1, 1, w), lambda bi, s: (bi, 0, 0))],
        scratch_shapes=[pltpu.VMEM((tm, w), F32), pltpu.VMEM((tm, w), F32), pltpu.VMEM((1, w), F32)],
        compiler_params=_cparams("parallel", "arbitrary"),
        name="lru_scan",
    )(xpre, xpre, xpre, conv_w, conv_b.reshape(1, w), wa, ba.reshape(1, w), wx, bx.reshape(1, w),
      lam.reshape(1, w), h0)


def rglru_residual(x, xc, mods_l, mods_c, w_in, conv_w, conv_b, wa, ba, wx, bx, lam, w_out, n_tok, n_ctx):
    b = x.shape[0] // n_tok
    w = w_in.shape[1] // 2
    w_in = w_in.astype(BF16)
    sh_l, sc_l, g_l = mods_l
    sh_c, sc_c, g_c = mods_c
    y_l, xp_l = modmm(x, sh_l, sc_l, w_in, n_tok, (w, w), ("gelu", None), (F32, F32))
    y_c, xp_c = modmm(xc, sh_c, sc_c, w_in, n_ctx, (w, w), ("gelu", None), (F32, F32))
    zero = jnp.zeros((b, 1, w), F32)
    hs_l, hs_c = [], []
    for d in range(2):
        gate_w = (conv_w, conv_b, wa[d].astype(BF16), ba[d], wx[d].astype(BF16), bx[d], lam[d])
        hc, state = lru_scan(xp_c, *gate_w, zero, n_ctx, reverse=d == 1)
        hl, _ = lru_scan(xp_l, *gate_w, state, n_tok, reverse=d == 1)
        hs_c.append(hc)
        hs_l.append(hl)
    w_out = w_out.astype(BF16)
    return (mm_residual(tuple(hs_l), w_out, x, g_l, n_tok, b=y_l),
            mm_residual(tuple(hs_c), w_out, xc, g_c, n_ctx, b=y_c))


def _gmlp_kernel(x_ref, sh_ref, sc_ref, g_ref, win_ref, lng_ref, lnb_ref, ws_ref, bs_ref, wout_ref, o_ref, *, half):
    x = x_ref[...]
    h = _modulate(x, sh_ref[0], sc_ref[0]).astype(BF16)
    u = _gelu(_dot(h, win_ref[:, :half]))
    v = _gelu(_dot(h, win_ref[:, half:]))
    mu = jnp.mean(v, axis=-1, keepdims=True)
    vc = v - mu
    var = jnp.mean(vc * vc, axis=-1, keepdims=True)
    vn = (vc * lax.rsqrt(var + EPS) * lng_ref[...] + lnb_ref[...]).astype(BF16)
    gw = half // GMLP_GROUPS
    chunks = []
    for c in range(x.shape[0] // GMLP_CHUNK):
        rws = slice(c * GMLP_CHUNK, (c + 1) * GMLP_CHUNK)
        chunks.append(jnp.concatenate(
            [_dot(ws_ref[g], vn[rws, g * gw:(g + 1) * gw]) + bs_ref[g] for g in range(GMLP_GROUPS)], axis=1))
    v2 = jnp.concatenate(chunks, axis=0)
    o_ref[...] = x + g_ref[0] * _dot((u * v2).astype(BF16), wout_ref[...])


def gmlp_residual(x, shift, scale, gate, w_in, ln_g, ln_b, w_s, b_s, w_out, n_tok, tm=256):
    r, d = x.shape
    half = w_in.shape[1] // 2
    tm = _row_tile(n_tok, tm)
    tpb = n_tok // tm
    assert tm % GMLP_CHUNK == 0
    const2 = lambda i: (0, 0)
    const3 = lambda i: (0, 0, 0)
    once = pl.Buffered(1)
    return pl.pallas_call(
        functools.partial(_gmlp_kernel, half=half),
        out_shape=jax.ShapeDtypeStruct((r, d), F32),
        grid=(r // tm,),
        in_specs=[
            pl.BlockSpec((tm, d), lambda i: (i, 0)),
            _mod_spec(shift, tpb),
            _mod_spec(scale, tpb),
            _mod_spec(gate, tpb),
            pl.BlockSpec(w_in.shape, const2, pipeline_mode=once),
            pl.BlockSpec((1, half), const2),
            pl.BlockSpec((1, half), const2),
            pl.BlockSpec(w_s.shape, const3),
            pl.BlockSpec(b_s.shape + (1,), const3),
            pl.BlockSpec(w_out.shape, const2, pipeline_mode=once),
        ],
        out_specs=pl.BlockSpec((tm, d), lambda i: (i, 0)),
        compiler_params=_cparams("parallel"),
        name="gmlp_mixer",
    )(x, shift, scale, gate, w_in.astype(BF16), ln_g.reshape(1, half), ln_b.reshape(1, half),
      w_s.astype(BF16), b_s[..., None], w_out.astype(BF16))


def kernel(x, c, ctx, c_ctx, mod_w, mod_b, pool_w, pool_scale, lru_w_in, lru_conv_w, lru_conv_b, lru_wa, lru_ba,
           lru_wx, lru_bx, lru_lam, lru_w_out, attn_w_qkv, attn_q_gain, attn_k_gain, attn_w_o, gmlp_w_in, gmlp_ln_g,
           gmlp_ln_b, gmlp_w_s, gmlp_b_s, gmlp_w_out, moe_router, moe_w_gate, moe_w_up, moe_w_down, final_gain):
    b, n_tok, d = x.shape
    n_ctx = ctx.shape[1]
    depth = mod_w.shape[0]
    n_mixers = 4
    xl = x.reshape(b * n_tok, d)
    xc = ctx.reshape(b * n_ctx, d)

    pad = -(b + 1) % SUBLANES
    cvec = jnp.concatenate([c, c_ctx[None, :], jnp.zeros((pad, d), F32)], axis=0)
    mods = mod_vectors(cvec, mod_w, mod_b)

    for i in range(depth):
        m, j = i % n_mixers, i // n_mixers
        last = i == depth - 1
        ml = [mods[i, :b, k * d:(k + 1) * d].reshape(b, 1, d) for k in range(6)]
        mc = [mods[i, b:b + 1, k * d:(k + 1) * d].reshape(1, 1, d) for k in range(6)]
        if m == 0:
            pw = pool_w[j].astype(BF16)
            xl_new = pool_mixer_residual(xl, ml[0], ml[1], ml[2], pw, pool_scale[j], n_tok)
            if not last:
                xc = pool_mixer_residual(xc, mc[0], mc[1], mc[2], pw, pool_scale[j], n_ctx)
            xl = xl_new
        elif m == 1:
            xl, xc_new = rglru_residual(xl, xc, ml[:3], mc[:3], lru_w_in[j], lru_conv_w[j], lru_conv_b[j], lru_wa[j],
                                        lru_ba[j], lru_wx[j], lru_bx[j], lru_lam[j], lru_w_out[j], n_tok, n_ctx)
            xc = xc if last else xc_new
        elif m == 2:
            xl, xc_new = gqa_residual(xl, xc, ml[:3], mc[:3], attn_w_qkv[j], attn_q_gain[j], attn_k_gain[j],
                                      attn_w_o[j], n_tok, n_ctx)
            xc = xc if last else xc_new
        else:
            gargs = (gmlp_w_in[j], gmlp_ln_g[j], gmlp_ln_b[j], gmlp_w_s[j], gmlp_b_s[j], gmlp_w_out[j])
            xl_new = gmlp_residual(xl, ml[0], ml[1], ml[2], *gargs, n_tok)
            if not last:
                xc = gmlp_residual(xc, mc[0], mc[1], mc[2], *gargs, n_ctx)
            xl = xl_new
        wr = jnp.pad(moe_router[i], ((0, 0), (0, LANES - N_EXPERTS)))
        xl = moe_residual(xl, ml[3], ml[4], ml[5], wr, moe_w_gate, moe_w_up, moe_w_down, i, n_tok,
                          final_gain if last else None)
        if not last:
            xc = moe_residual(xc, mc[3], mc[4], mc[5], wr, moe_w_gate, moe_w_up, moe_w_down, i, n_ctx)
    return xl.reshape(b, n_tok, d)
```

```python
import functools

import jax
import jax.numpy as jnp
from jax import lax
from jax.experimental import pallas as pl
from jax.experimental.pallas import tpu as pltpu

F32 = jnp.float32
BF16 = jnp.bfloat16
EPS = 1e-6

N_EXPERTS = 16
CAPACITY_FACTOR = 2
POOL_WINDOWS = (2, 4, 8, 16)
POOL_HALO = 8
LRU_BLOCKS = 8
LRU_C = 8.0
CONV_W = 4
HEAD_DIM = 64
N_KV_HEADS = 4
GRID_W = 64
ROPE_THETA = 10000.0
GMLP_GROUPS = 4
GMLP_CHUNK = 128

LANES = 128
SUBLANES = 8
VMEM_LIMIT = 56 * 1024 * 1024


def _cparams(*sem):
    return pltpu.CompilerParams(dimension_semantics=sem, vmem_limit_bytes=VMEM_LIMIT)


def _modulate(x, shift, scale):
    ms = jnp.mean(x * x, axis=-1, keepdims=True)
    return x * lax.rsqrt(ms + EPS) * (1.0 + scale) + shift


def _split_bf16(a):
    hi = a.astype(BF16)
    lo = (a - hi.astype(F32)).astype(BF16)
    return hi, lo


def _dot(a, b):
    return jnp.dot(a, b, preferred_element_type=F32)


def _dot3(a, b):
    a_hi, a_lo = _split_bf16(a)
    b_hi, b_lo = _split_bf16(b)
    return _dot(a_hi, b_hi) + (_dot(a_hi, b_lo) + _dot(a_lo, b_hi))


def _gelu(x):
    return 0.5 * x * (1.0 + jnp.tanh(0.7978845608028654 * (x + 0.044715 * (x * x * x))))


def _silu(x):
    return x * (1.0 / (1.0 + jnp.exp(-x)))


def _sigmoid(x):
    return 0.5 * jnp.tanh(0.5 * x) + 0.5


def _mod_spec(mod, tiles_per_batch):
    d = mod.shape[-1]
    if mod.shape[0] == 1:
        return pl.BlockSpec((1, 1, d), lambda i, *_: (0, 0, 0))
    return pl.BlockSpec((1, 1, d), lambda i, *_: (i // tiles_per_batch, 0, 0))


def _row_tile(n, want):
    t = min(n, want)
    assert n % t == 0
    return t


def _mod_kernel(c_ref, w_ref, b_ref, o_ref):
    s = _silu(c_ref[...])
    o_ref[0] = _dot3(s, w_ref[0]) + b_ref[0]


def mod_vectors(cvec, mod_w, mod_b):
    depth, d, n6 = mod_w.shape
    tn = 1024
    return pl.pallas_call(
        _mod_kernel,
        out_shape=jax.ShapeDtypeStruct((depth, cvec.shape[0], n6), F32),
        grid=(depth, n6 // tn),
        in_specs=[
            pl.BlockSpec(cvec.shape, lambda l, j: (0, 0)),
            pl.BlockSpec((1, d, tn), lambda l, j: (l, 0, j)),
            pl.BlockSpec((1, 1, tn), lambda l, j: (l, 0, j)),
        ],
        out_specs=pl.BlockSpec((1, cvec.shape[0], tn), lambda l, j: (l, 0, j)),
        compiler_params=_cparams("parallel", "parallel"),
        name="mod_vectors",
    )(cvec, mod_w, mod_b.reshape(depth, 1, n6))


def _modmm_kernel(x_ref, sh_ref, sc_ref, w_ref, *o_refs, splits, acts):
    h = _modulate(x_ref[...], sh_ref[0], sc_ref[0]).astype(BF16)
    z = _dot(h, w_ref[...])
    off = 0
    for o_ref, width, act in zip(o_refs, splits, acts):
        part = z[:, off:off + width]
        if act == "gelu":
            part = _gelu(part)
        o_ref[...] = part.astype(o_ref.dtype)
        off += width


def modmm(x, shift, scale, w, n_tok, splits, acts, dtypes, tm=512):
    r, d = x.shape
    tm = _row_tile(n_tok, tm)
    tpb = n_tok // tm
    n = w.shape[1]
    assert sum(splits) == n
    return pl.pallas_call(
        functools.partial(_modmm_kernel, splits=splits, acts=acts),
        out_shape=[jax.ShapeDtypeStruct((r, s), dt) for s, dt in zip(splits, dtypes)],
        grid=(r // tm,),
        in_specs=[
            pl.BlockSpec((tm, d), lambda i: (i, 0)),
            _mod_spec(shift, tpb),
            _mod_spec(scale, tpb),
            pl.BlockSpec((d, n), lambda i: (0, 0)),
        ],
        out_specs=[pl.BlockSpec((tm, s), lambda i: (i, 0)) for s in splits],
        compiler_params=_cparams("parallel"),
        name="modmm",
    )(x, shift, scale, w)


def _mmres_kernel(*refs, n_sum, has_mul):
    a = refs[0][...]
    for a_ref in refs[1:n_sum]:
        a = a + a_ref[...]
    refs = refs[n_sum:]
    if has_mul:
        a = a.astype(F32) * refs[0][...].astype(F32)
        refs = refs[1:]
    w_ref, x_ref, g_ref, o_ref = refs
    o_ref[...] = x_ref[...] + g_ref[0] * _dot(a.astype(BF16), w_ref[...])


def mm_residual(a, w, x, gate, n_tok, b=None, tm=512):
    a = a if isinstance(a, (tuple, list)) else (a,)
    r, k = a[0].shape
    n = w.shape[1]
    tm = _row_tile(n_tok, tm)
    tpb = n_tok // tm
    ins = list(a) + ([b] if b is not None else []) + [w, x, gate]
    specs = [pl.BlockSpec((tm, k), lambda i: (i, 0))] * (len(a) + (b is not None))
    specs += [
        pl.BlockSpec((k, n), lambda i: (0, 0)),
        pl.BlockSpec((tm, n), lambda i: (i, 0)),
        _mod_spec(gate, tpb),
    ]
    return pl.pallas_call(
        functools.partial(_mmres_kernel, n_sum=len(a), has_mul=b is not None),
        out_shape=jax.ShapeDtypeStruct((r, n), F32),
        grid=(r // tm,),
        in_specs=specs,
        out_specs=pl.BlockSpec((tm, n), lambda i: (i, 0)),
        compiler_params=_cparams("parallel"),
        name="mm_residual",
    )(*ins)


def _pool_kernel(xp_ref, x_ref, xn_ref, sh_ref, sc_ref, g_ref, w_ref, ps_ref, o_ref, *, tm, n_tok, tpb):
    t0 = (pl.program_id(0) % tpb) * tm
    x = x_ref[...]
    xe = jnp.concatenate([xp_ref[...], x, xn_ref[...]], axis=0)
    rows = tm + 2 * POOL_HALO
    h = _modulate(xe, sh_ref[0], sc_ref[0])
    pos = lax.broadcasted_iota(jnp.int32, (rows, 1), 0) + (t0 - POOL_HALO)
    h = jnp.where((pos >= 0) & (pos < n_tok), h, 0.0)
    posc = pos[POOL_HALO:POOL_HALO + tm]
    gw = h.shape[1] // len(POOL_WINDOWS)
    outs = []
    for g, win in enumerate(POOL_WINDOWS):
        hg = h[:, g * gw:(g + 1) * gw]
        c = hg + pltpu.roll(hg, 1, 0)
        step = 1
        while 2 * step < win:
            c = pltpu.roll(c, step, 0) + pltpu.roll(c, rows - step, 0)
            step *= 2
        cnt = jnp.minimum(posc + (win - win // 2), n_tok) - jnp.maximum(posc - win // 2, 0)
        pooled = c[POOL_HALO:POOL_HALO + tm] / cnt.astype(F32) - hg[POOL_HALO:POOL_HALO + tm]
        outs.append(_dot(pooled.astype(BF16), w_ref[g]))
    y = jnp.concatenate(outs, axis=1) * ps_ref[...]
    o_ref[...] = x + g_ref[0] * y


def pool_mixer_residual(x, shift, scale, gate, w_pool, pool_scale, n_tok, tm=256):
    r, d = x.shape
    tm = _row_tile(n_tok, tm)
    tpb = n_tok // tm
    hb = tm // POOL_HALO
    last = r // POOL_HALO - 1
    groups, gw, _ = w_pool.shape
    return pl.pallas_call(
        functools.partial(_pool_kernel, tm=tm, n_tok=n_tok, tpb=tpb),
        out_shape=jax.ShapeDtypeStruct((r, d), F32),
        grid=(r // tm,),
        in_specs=[
            pl.BlockSpec((POOL_HALO, d), lambda i: (jnp.maximum(i * hb - 1, 0), 0)),
            pl.BlockSpec((tm, d), lambda i: (i, 0)),
            pl.BlockSpec((POOL_HALO, d), lambda i: (jnp.minimum((i + 1) * hb, last), 0)),
            _mod_spec(shift, tpb),
            _mod_spec(scale, tpb),
            _mod_spec(gate, tpb),
            pl.BlockSpec((groups, gw, gw), lambda i: (0, 0, 0)),
            pl.BlockSpec((1, d), lambda i: (0, 0)),
        ],
        out_specs=pl.BlockSpec((tm, d), lambda i: (i, 0)),
        compiler_params=_cparams("parallel"),
        name="pool_mixer",
    )(x, x, x, shift, scale, gate, w_pool, pool_scale.reshape(1, d))


def _router_kernel(x_ref, sh_ref, sc_ref, wr_ref, h_ref, aff_ref):
    h = _modulate(x_ref[...], sh_ref[0], sc_ref[0])
    h_ref[...] = h.astype(h_ref.dtype)
    logits = _dot3(h, wr_ref[...])
    lane = lax.broadcasted_iota(jnp.int32, logits.shape, 1)
    logits = jnp.where(lane < N_EXPERTS, logits, -jnp.inf)
    e = jnp.exp(logits - jnp.max(logits, axis=-1, keepdims=True))
    aff_ref[...] = e / jnp.sum(e, axis=-1, keepdims=True)


def router(x, shift, scale, w_router_padded, n_tok, tm=512):
    r, d = x.shape
    tm = _row_tile(n_tok, tm)
    tpb = n_tok // tm
    return pl.pallas_call(
        _router_kernel,
        out_shape=[jax.ShapeDtypeStruct((r, d), BF16), jax.ShapeDtypeStruct((r, LANES), F32)],
        grid=(r // tm,),
        in_specs=[
            pl.BlockSpec((tm, d), lambda i: (i, 0)),
            _mod_spec(shift, tpb),
            _mod_spec(scale, tpb),
            pl.BlockSpec((d, LANES), lambda i: (0, 0)),
        ],
        out_specs=[pl.BlockSpec((tm, d), lambda i: (i, 0)), pl.BlockSpec((tm, LANES), lambda i: (i, 0))],
        compiler_params=_cparams("parallel"),
        name="moe_router",
    )(x, shift, scale, w_router_padded)


MOE_CHUNK = 128
EXPERT_GROUPS = 2
PACK = LANES // N_EXPERTS


def _cumsum_rows(v, tb):
    n = v.shape[0]
    tri = (lax.broadcasted_iota(jnp.int32, (tb, tb), 0) >= lax.broadcasted_iota(jnp.int32, (tb, tb), 1)).astype(BF16)
    carry = jnp.zeros((1, v.shape[1]), F32)
    outs, starts = [], []
    for j in range(n // tb):
        starts.append(carry)
        c = _dot(tri, v[j * tb:(j + 1) * tb].astype(BF16)) + carry
        carry = c[tb - 1:tb, :]
        outs.append(c)
    return jnp.concatenate(outs, axis=0), jnp.concatenate(starts, axis=0)


def _route_kernel(aff_ref, affp_ref, idx_ref, slot_ref, start_ref, slot_scr, acc_scr, *, n, cap):
    def enough(cand):
        cnt = jnp.sum((pltpu.bitcast(affp_ref[...], jnp.int32) >= cand).astype(F32), axis=0, keepdims=True)
        shift = N_EXPERTS
        while shift < LANES:
            cnt = cnt + pltpu.roll(cnt, shift, 1)
            shift *= 2
        return cnt >= cap

    def two_bits(k, prefix):
        low = 28 - 2 * k
        c1, c2, c3 = (prefix | jnp.left_shift(jnp.int32(j), low) for j in (1, 2, 3))
        return jnp.where(enough(c3), c3, jnp.where(enough(c2), c2, jnp.where(enough(c1), c1, prefix)))

    top = jnp.full((1, LANES), 1 << 30, jnp.int32)
    thr = jnp.where(enough(top), top, jnp.zeros_like(top))
    thr = lax.fori_loop(0, 15, two_bits, thr)
    bits = pltpu.bitcast(aff_ref[...], jnp.int32)
    gt = bits > thr
    eq = bits == thr
    need = cap - jnp.sum(gt.astype(F32), axis=0, keepdims=True)
    eq_rank, _ = _cumsum_rows(eq.astype(F32), MOE_CHUNK)
    sel = gt | (eq & (eq_rank <= need))
    pos, starts = _cumsum_rows(sel.astype(F32), MOE_CHUNK)
    slot_scr[...] = jnp.where(sel, pos - 1.0, -1.0)
    slot_ref[...] = slot_scr[...]
    start_ref[0] = starts.astype(jnp.int32)

    slots = lax.broadcasted_iota(jnp.int32, (1, cap), 1).astype(F32)
    sub = lax.broadcasted_iota(jnp.int32, (SUBLANES, MOE_CHUNK), 0)
    local = lax.broadcasted_iota(jnp.int32, (SUBLANES, MOE_CHUNK), 1).astype(F32)
    acc_scr[...] = jnp.zeros(acc_scr.shape, F32)

    def chunk(c, carry):
        rows = pl.ds(pl.multiple_of(c * MOE_CHUNK, MOE_CHUNK), MOE_CHUNK)
        lhs = jnp.where(sub == 0, local, jnp.where(sub == 1, lax.convert_element_type(c, F32), 0.0)).astype(BF16)
        for e in range(N_EXPERTS):
            onehot = jnp.where(slot_scr[rows, e:e + 1] == slots, 1.0, 0.0).astype(BF16)
            acc_scr[e] += _dot(lhs, onehot)
        return carry

    lax.fori_loop(0, n // MOE_CHUNK, chunk, 0)
    for e in range(N_EXPERTS):
        idx_ref[0, e:e + 1, :] = (acc_scr[e, 0:1, :] + MOE_CHUNK * acc_scr[e, 1:2, :]).astype(jnp.int32)


def route(aff, n_tok):
    r = aff.shape[0]
    b = r // n_tok
    cap = CAPACITY_FACTOR * n_tok // N_EXPERTS
    aff_packed = aff[:, :N_EXPERTS].reshape(r // PACK, LANES)
    n_chunks = n_tok // MOE_CHUNK
    return pl.pallas_call(
        functools.partial(_route_kernel, n=n_tok, cap=cap),
        out_shape=[jax.ShapeDtypeStruct((b, N_EXPERTS, cap), jnp.int32), jax.ShapeDtypeStruct((r, LANES), F32),
                   jax.ShapeDtypeStruct((b, n_chunks, LANES), jnp.int32)],
        grid=(b,),
        in_specs=[pl.BlockSpec((n_tok, LANES), lambda i: (i, 0)),
                  pl.BlockSpec((n_tok // PACK, LANES), lambda i: (i, 0))],
        out_specs=[pl.BlockSpec((1, N_EXPERTS, cap), lambda i: (i, 0, 0)),
                   pl.BlockSpec((n_tok, LANES), lambda i: (i, 0)),
                   pl.BlockSpec((1, n_chunks, LANES), lambda i: (i, 0, 0))],
        scratch_shapes=[pltpu.VMEM((n_tok, LANES), F32), pltpu.VMEM((N_EXPERTS, SUBLANES, cap), F32)],
        compiler_params=_cparams("parallel"),
        name="moe_route",
    )(aff, aff_packed)


def _ffn_kernel(x_ref, wg_ref, wu_ref, wd_ref, o_ref, wg_scr, wu_scr, wd_scr):
    @pl.when(pl.program_id(1) == 0)
    def _():
        wg_scr[...] = wg_ref[0, 0].astype(BF16)
        wu_scr[...] = wu_ref[0, 0].astype(BF16)
        wd_scr[...] = wd_ref[0, 0].astype(BF16)

    x = x_ref[0]
    a = _dot(x, wg_scr[...])
    u = _dot(x, wu_scr[...])
    hmid = (_silu(a) * u).astype(BF16)
    o_ref[0] = _dot(hmid, wd_scr[...]).astype(o_ref.dtype)


def expert_ffn(xs, w_gate, w_up, w_down, layer, e0, tm=512):
    e, m, d = xs.shape
    ff = w_gate.shape[-1]
    tm = _row_tile(m, tm)
    return pl.pallas_call(
        _ffn_kernel,
        out_shape=jax.ShapeDtypeStruct((e, m, d), BF16),
        grid=(e, m // tm),
        in_specs=[
            pl.BlockSpec((1, tm, d), lambda k, i: (k, i, 0)),
            pl.BlockSpec((1, 1, d, ff), lambda k, i: (layer, e0 + k, 0, 0)),
            pl.BlockSpec((1, 1, d, ff), lambda k, i: (layer, e0 + k, 0, 0)),
            pl.BlockSpec((1, 1, ff, d), lambda k, i: (layer, e0 + k, 0, 0)),
        ],
        out_specs=pl.BlockSpec((1, tm, d), lambda k, i: (k, i, 0)),
        scratch_shapes=[pltpu.VMEM((d, ff), BF16), pltpu.VMEM((d, ff), BF16), pltpu.VMEM((ff, d), BF16)],
        compiler_params=_cparams("parallel", "arbitrary"),
        name="moe_expert_ffn",
    )(xs, w_gate, w_up, w_down)


def _combine_kernel(off_ref, x_ref, g_ref, slot_ref, aff_ref, *rest, win, n_chunks, cps, n_groups, final):
    y_refs, rest = rest[:n_groups], rest[n_groups:]
    if final:
        fg_ref, o_ref = rest
    else:
        (o_ref,) = rest
    per_group = N_EXPERTS // n_groups
    align = min(win, LANES)
    for sc in range(cps):
        rows = slice(sc * MOE_CHUNK, (sc + 1) * MOE_CHUNK)
        base = (pl.program_id(0) * n_chunks + pl.program_id(1) * cps + sc) * N_EXPERTS
        acc = jnp.zeros((MOE_CHUNK, x_ref.shape[1]), F32)
        for e in range(N_EXPERTS):
            off = pl.multiple_of(off_ref[base + e], align)
            cols = (lax.broadcasted_iota(jnp.int32, (1, win), 1) + off).astype(F32)
            onehot = jnp.where(slot_ref[rows, e:e + 1] == cols, 1.0, 0.0).astype(BF16)
            y_win = y_refs[e // per_group][e % per_group, pl.ds(off, win), :]
            acc = acc + aff_ref[rows, e:e + 1] * _dot(onehot, y_win)
        out = x_ref[rows, :] + g_ref[0] * acc
        if final:
            out = out * lax.rsqrt(jnp.mean(out * out, axis=-1, keepdims=True) + EPS) * fg_ref[...]
        o_ref[rows, :] = out


def moe_combine(x, gate2, slot, aff, ys, offs, n_tok, final_gain=None):
    r, d = x.shape
    b = r // n_tok
    per_group = ys[0].shape[0]
    cap = ys[0].shape[1] // b
    n_chunks = n_tok // MOE_CHUNK
    cps = 2 if n_chunks % 2 == 0 else 1
    tm = cps * MOE_CHUNK
    steps = n_chunks // cps
    win = min(2 * MOE_CHUNK, cap)
    final = final_gain is not None
    row_spec = lambda w: pl.BlockSpec((tm, w), lambda bi, c, off: (bi * steps + c, 0))
    ins = [offs, x, gate2, slot, aff] + list(ys)
    specs = [row_spec(d), _mod_spec(gate2, 1), row_spec(LANES), row_spec(LANES)]
    specs += [pl.BlockSpec((per_group, cap, d), lambda bi, c, off: (0, bi, 0))] * len(ys)
    if final:
        ins.append(final_gain.reshape(1, d))
        specs.append(pl.BlockSpec((1, d), lambda bi, c, off: (0, 0)))
    return pl.pallas_call(
        functools.partial(_combine_kernel, win=win, n_chunks=n_chunks, cps=cps, n_groups=len(ys), final=final),
        out_shape=jax.ShapeDtypeStruct((r, d), F32),
        grid_spec=pltpu.PrefetchScalarGridSpec(
            num_scalar_prefetch=1,
            grid=(b, steps),
            in_specs=specs,
            out_specs=row_spec(d),
        ),
        compiler_params=_cparams("parallel", "arbitrary"),
        name="moe_combine",
    )(*ins)


def moe_residual(x, shift, scale, gate2, w_router_padded, w_gate, w_up, w_down, layer, n_tok, final_gain=None):
    r, d = x.shape
    b = r // n_tok
    h, aff = router(x, shift, scale, w_router_padded, n_tok)
    idx, slot, start = route(aff, n_tok)
    cap = idx.shape[-1]
    gidx = idx + (jnp.arange(b, dtype=jnp.int32) * n_tok)[:, None, None]
    gidx = jnp.transpose(gidx, (1, 0, 2)).reshape(N_EXPERTS, b * cap)
    per_group = N_EXPERTS // EXPERT_GROUPS
    ys = []
    for g in range(EXPERT_GROUPS):
        xs = h.at[gidx[g * per_group:(g + 1) * per_group]].get(mode="promise_in_bounds")
        ys.append(expert_ffn(xs, w_gate, w_up, w_down, layer, g * per_group))
    win = min(2 * MOE_CHUNK, cap)
    align = min(win, LANES)
    offs = jnp.clip(start[:, :, :N_EXPERTS] // align * align, 0, cap - win).reshape(-1)
    return moe_combine(x, gate2, slot, aff, ys, offs, n_tok, final_gain)


def _head_rms(t, seg_ones):
    outs = []
    for j in range(t.shape[1] // LANES):
        blk = t[:, j * LANES:(j + 1) * LANES]
        hi, lo = _split_bf16(blk * blk)
        ss = _dot(hi, seg_ones) + _dot(lo, seg_ones)
        outs.append(blk * lax.rsqrt(ss * (1.0 / HEAD_DIM) + EPS))
    return jnp.concatenate(outs, axis=1)


def _rope(t, cos, sin_signed):
    w = t.shape[1]
    half = HEAD_DIM // 2
    lane = lax.broadcasted_iota(jnp.int32, t.shape, 1)
    partner = jnp.where(lane % HEAD_DIM < half, pltpu.roll(t, w - half, 1), pltpu.roll(t, half, 1))
    reps = w // cos.shape[1]
    return t * jnp.concatenate([cos] * reps, axis=1) + partner * jnp.concatenate([sin_signed] * reps, axis=1)


def _qkv_kernel(*refs, qd, kd, rope):
    if rope:
        x_ref, sh_ref, sc_ref, w_ref, qg_ref, kg_ref, cos_ref, sin_ref, q_ref, k_ref, v_ref = refs
    else:
        x_ref, sh_ref, sc_ref, w_ref, qg_ref, kg_ref, q_ref, k_ref, v_ref = refs
    h = _modulate(x_ref[...], sh_ref[0], sc_ref[0]).astype(BF16)
    z = _dot(h, w_ref[...])
    seg = (lax.broadcasted_iota(jnp.int32, (LANES, LANES), 0) // HEAD_DIM
           == lax.broadcasted_iota(jnp.int32, (LANES, LANES), 1) // HEAD_DIM).astype(BF16)
    q = _head_rms(z[:, :qd], seg) * qg_ref[...]
    k = _head_rms(z[:, qd:qd + kd], seg) * kg_ref[...]
    if rope:
        q = _rope(q, cos_ref[...], sin_ref[...])
        k = _rope(k, cos_ref[...], sin_ref[...])
    q_ref[...] = (q * (HEAD_DIM ** -0.5 * 1.4426950408889634)).astype(q_ref.dtype)
    k_ref[...] = k.astype(k_ref.dtype)
    v_ref[...] = z[:, qd + kd:].astype(v_ref.dtype)


def qkv_project(x, shift, scale, w_qkv, q_gain, k_gain, n_tok, rope_tables=None, tm=256):
    r, d = x.shape
    kd = N_KV_HEADS * HEAD_DIM
    qd = w_qkv.shape[1] - 2 * kd
    tm = _row_tile(n_tok, tm)
    tpb = n_tok // tm
    ins = [x, shift, scale, w_qkv, q_gain, k_gain]
    specs = [
        pl.BlockSpec((tm, d), lambda i: (i, 0)),
        _mod_spec(shift, tpb),
        _mod_spec(scale, tpb),
        pl.BlockSpec(w_qkv.shape, lambda i: (0, 0)),
        pl.BlockSpec((1, qd), lambda i: (0, 0)),
        pl.BlockSpec((1, kd), lambda i: (0, 0)),
    ]
    if rope_tables is not None:
        ins += list(rope_tables)
        specs += [pl.BlockSpec((tm, LANES), lambda i: (i % tpb, 0))] * 2
    return pl.pallas_call(
        functools.partial(_qkv_kernel, qd=qd, kd=kd, rope=rope_tables is not None),
        out_shape=[jax.ShapeDtypeStruct((r, qd), BF16), jax.ShapeDtypeStruct((r, kd), BF16),
                   jax.ShapeDtypeStruct((r, kd), BF16)],
        grid=(r // tm,),
        in_specs=specs,
        out_specs=[pl.BlockSpec((tm, qd), lambda i: (i, 0)), pl.BlockSpec((tm, kd), lambda i: (i, 0)),
                   pl.BlockSpec((tm, kd), lambda i: (i, 0))],
        compiler_params=_cparams("parallel"),
        name="qkv_project",
    )(*ins)


def _attn_kernel(q_ref, k_ref, vt_ref, o_ref, *scr, grp, bounded):
    k = k_ref[0, 0]
    vt = vt_ref[0, 0]
    tq = q_ref.shape[0]
    n_pairs = grp // 2
    outs = []

    def scores(pair):
        heads = (2 * pair, 2 * pair + 1)
        q2 = jnp.concatenate([q_ref[:, h * HEAD_DIM:(h + 1) * HEAD_DIM] for h in heads], axis=0)
        return lax.dot_general(k, q2, (((1,), (1,)), ((), ())), preferred_element_type=F32)

    if bounded:
        for pair in range(n_pairs):
            p = jnp.exp2(scores(pair))
            scr[pair][...] = p.astype(BF16)
            ot = _dot(vt[:HEAD_DIM], scr[pair][...]) / jnp.sum(p, axis=0, keepdims=True)
            o = jnp.concatenate([ot, ot], axis=0).T
            outs += [o[:tq, :HEAD_DIM], o[tq:, :HEAD_DIM]]
        p_refs = ()
    else:
        for pair in range(n_pairs):
            scr[2 * pair][...] = scores(pair)
        for pair in range(n_pairs):
            st_scr, p_scr = scr[2 * pair], scr[2 * pair + 1]
            p_scr[...] = jnp.exp2(st_scr[...] - jnp.max(st_scr[...], axis=0, keepdims=True)).astype(BF16)
        p_refs = scr[1::2]
    for p_scr in p_refs:
        ot = _dot(vt, p_scr[...])
        ot = ot[:HEAD_DIM] / ot[HEAD_DIM:]
        o = jnp.concatenate([ot, ot], axis=0).T
        outs += [o[:tq, :HEAD_DIM], o[tq:, :HEAD_DIM]]
    o_ref[...] = jnp.concatenate(outs, axis=1).astype(o_ref.dtype)


def _attention_call(q, k, vt, n_q, tq, bounded):
    r, qd = q.shape
    _, kvh, n_k, hd = k.shape
    grp = qd // (kvh * hd)
    tq = _row_tile(n_q, tq)
    tpb = n_q // tq
    dts = (BF16,) if bounded else (F32, BF16)
    return pl.pallas_call(
        functools.partial(_attn_kernel, grp=grp, bounded=bounded),
        out_shape=jax.ShapeDtypeStruct((r, qd), BF16),
        grid=(r // n_q, kvh, tpb),
        in_specs=[
            pl.BlockSpec((tq, grp * hd), lambda b, g, i: (b * tpb + i, g)),
            pl.BlockSpec((1, 1, n_k, hd), lambda b, g, i: (b, g, 0, 0)),
            pl.BlockSpec((1, 1, 2 * hd, n_k), lambda b, g, i: (b, g, 0, 0)),
        ],
        out_specs=pl.BlockSpec((tq, grp * hd), lambda b, g, i: (b * tpb + i, g)),
        scratch_shapes=[pltpu.VMEM((n_k, 2 * tq), dt) for _ in range(grp // 2) for dt in dts],
        compiler_params=_cparams("parallel", "parallel", "parallel"),
        name="attention_bounded" if bounded else "attention",
    )(q, k, vt)


SCORE_BOUND = 60.0


def attention(q, k, vt, n_q, score_bound, tq=256):
    return lax.cond(score_bound <= SCORE_BOUND,
                    lambda: _attention_call(q, k, vt, n_q, tq, True),
                    lambda: _attention_call(q, k, vt, n_q, tq, False))


def rope_tables(n_tok):
    t = jnp.arange(n_tok)
    row = (t // GRID_W).astype(F32)
    col = (t % GRID_W).astype(F32)
    n_freq = HEAD_DIM // 4
    inv = ROPE_THETA ** (-jnp.arange(n_freq, dtype=F32) / n_freq)
    ang = jnp.concatenate([row[:, None] * inv, col[:, None] * inv], axis=-1)
    cos, sin = jnp.cos(ang), jnp.sin(ang)
    reps = LANES // HEAD_DIM
    return jnp.tile(jnp.concatenate([cos, cos], -1), (1, reps)), jnp.tile(jnp.concatenate([-sin, sin], -1), (1, reps))


def _deinterleave_heads(w, n_heads):
    lead = w.shape[:-1]
    w = w.reshape(lead + (n_heads, HEAD_DIM // 2, 2))
    return jnp.swapaxes(w, -1, -2).reshape(lead + (n_heads * HEAD_DIM,))


def _split_kv_heads(t, b):
    return jnp.transpose(t.reshape(b, -1, N_KV_HEADS, HEAD_DIM), (0, 2, 1, 3))


def gqa_residual(x, xc, mods_l, mods_c, w_qkv, q_gain, k_gain, w_o, n_tok, n_ctx):
    b = x.shape[0] // n_tok
    kd = N_KV_HEADS * HEAD_DIM
    qd = w_qkv.shape[1] - 2 * kd
    n_qh = qd // HEAD_DIM
    w_perm = jnp.concatenate([_deinterleave_heads(w_qkv[:, :qd], n_qh),
                              _deinterleave_heads(w_qkv[:, qd:qd + kd], N_KV_HEADS), w_qkv[:, qd + kd:]], axis=1)
    w_perm = w_perm.astype(BF16)
    qg = jnp.tile(_deinterleave_heads(q_gain, 1), n_qh).reshape(1, qd)
    kg = jnp.tile(_deinterleave_heads(k_gain, 1), N_KV_HEADS).reshape(1, kd)
    sh_l, sc_l, g_l = mods_l
    sh_c, sc_c, g_c = mods_c
    q_l, k_l, v_l = qkv_project(x, sh_l, sc_l, w_perm, qg, kg, n_tok, rope_tables(n_tok))
    q_c, k_c, v_c = qkv_project(xc, sh_c, sc_c, w_perm, qg, kg, n_ctx)
    k_c4 = _split_kv_heads(k_c, b)
    k_all = jnp.concatenate([k_c4, _split_kv_heads(k_l, b)], axis=2)

    def values_t(v):
        vt = jnp.transpose(v.reshape(b, -1, N_KV_HEADS, HEAD_DIM), (0, 2, 3, 1))
        return jnp.concatenate([vt, jnp.ones_like(vt)], axis=2)

    vt_c = values_t(v_c)
    vt_all = jnp.concatenate([vt_c, values_t(v_l)], axis=3)
    score_bound = (1.01 * HEAD_DIM * HEAD_DIM ** -0.5 * 1.4426950408889634
                   * jnp.max(jnp.abs(q_gain)) * jnp.max(jnp.abs(k_gain)))
    o_l = attention(q_l, k_all, vt_all, n_tok, score_bound)
    o_c = attention(q_c, k_c4, vt_c, n_ctx, score_bound)
    w_o = w_o.astype(BF16)
    return mm_residual(o_l, w_o, x, g_l, n_tok), mm_residual(o_c, w_o, xc, g_c, n_ctx)


def _lru_kernel(xp_ref, x_ref, xn_ref, cw_ref, cb_ref, wa_ref, ba_ref, wx_ref, bx_ref, lam_ref, h0_ref,
                o_ref, hT_ref, a_scr, u_scr, carry_scr, *, tm, n_tok, tpb, reverse):
    step = pl.program_id(1)
    t_idx = (tpb - 1 - step) if reverse else step
    t0 = t_idx * tm
    rows = tm + 2 * SUBLANES

    @pl.when(step == 0)
    def _():
        carry_scr[...] = h0_ref[0]

    xe = jnp.concatenate([xp_ref[...], x_ref[...], xn_ref[...]], axis=0)
    pos = lax.broadcasted_iota(jnp.int32, (rows, 1), 0) + (t0 - SUBLANES)
    xe = jnp.where((pos >= 0) & (pos < n_tok), xe, 0.0)
    left = CONV_W // 2
    conv = cb_ref[...]
    for k in range(CONV_W):
        shift = (left - k) % rows
        tap = xe if shift == 0 else pltpu.roll(xe, shift, 0)
        conv = conv + tap * cw_ref[k:k + 1, :]
    xr = conv[SUBLANES:SUBLANES + tm]

    xb = xr.astype(BF16)
    bw = xr.shape[1] // LRU_BLOCKS
    ra, ia = [], []
    for j in range(LRU_BLOCKS):
        blk = xb[:, j * bw:(j + 1) * bw]
        ra.append(_dot(blk, wa_ref[j]))
        ia.append(_dot(blk, wx_ref[j]))
    r = _sigmoid(jnp.concatenate(ra, axis=1) + ba_ref[...])
    i = _sigmoid(jnp.concatenate(ia, axis=1) + bx_ref[...])
    log_a = -LRU_C * r * jnp.logaddexp(-lam_ref[...], 0.0)
    a = jnp.exp(log_a)
    a_scr[...] = a
    t = jnp.tanh(log_a)
    u_scr[...] = xr * i * jnp.sqrt(-2.0 * t / (1.0 - t))

    n_grp = tm // SUBLANES
    sub = lax.broadcasted_iota(jnp.int32, (SUBLANES, 1), 0)

    def group(j, carry):
        g = (n_grp - 1 - j) if reverse else j
        rws = pl.ds(pl.multiple_of(g * SUBLANES, SUBLANES), SUBLANES)
        ag, ug = a_scr[rws, :], u_scr[rws, :]
        s = 1
        while s < SUBLANES:
            if reverse:
                ok = sub < SUBLANES - s
                sh = SUBLANES - s
            else:
                ok = sub >= s
                sh = s
            u_prev = jnp.where(ok, pltpu.roll(ug, sh, 0), 0.0)
            a_prev = jnp.where(ok, pltpu.roll(ag, sh, 0), 1.0)
            ug = ug + ag * u_prev
            ag = ag * a_prev
            s *= 2
        hg = ug + ag * carry
        o_ref[rws, :] = hg
        return hg[0:1, :] if reverse else hg[SUBLANES - 1:SUBLANES, :]

    carry = lax.fori_loop(0, n_grp, group, carry_scr[...])
    carry_scr[...] = carry
    hT_ref[0] = carry


def lru_scan(xpre, conv_w, conv_b, wa, ba, wx, bx, lam, h0, n_tok, reverse, tm=256):
    r, w = xpre.shape
    b = r // n_tok
    tm = _row_tile(n_tok, tm)
    tpb = n_tok // tm
    hb = tm // SUBLANES
    last = r // SUBLANES - 1

    def tile(bi, s):
        return bi * tpb + ((tpb - 1 - s) if reverse else s)

    vec = pl.BlockSpec((1, w), lambda bi, s: (0, 0))
    blocks = pl.BlockSpec(wa.shape, lambda bi, s: (0, 0, 0))
    return pl.pallas_call(
        functools.partial(_lru_kernel, tm=tm, n_tok=n_tok, tpb=tpb, reverse=reverse),
        out_shape=[jax.ShapeDtypeStruct((r, w), F32), jax.ShapeDtypeStruct((b, 1, w), F32)],
        grid=(b, tpb),
        in_specs=[
            pl.BlockSpec((SUBLANES, w), lambda bi, s: (jnp.maximum(tile(bi, s) * hb - 1, 0), 0)),
            pl.BlockSpec((tm, w), lambda bi, s: (tile(bi, s), 0)),
            pl.BlockSpec((SUBLANES, w), lambda bi, s: (jnp.minimum((tile(bi, s) + 1) * hb, last), 0)),
            pl.BlockSpec((CONV_W, w), lambda bi, s: (0, 0)),
            vec, blocks, vec, blocks, vec, vec,
            pl.BlockSpec((1, 1, w), lambda bi, s: (bi, 0, 0)),
        ],
        out_specs=[pl.BlockSpec((tm, w), lambda bi, s: (tile(bi, s), 0)),
                   pl.BlockSpec((1, 1, w), lambda bi, s: (bi, 0, 0))],
        scratch_shapes=[pltpu.VMEM((tm, w), F32), pltpu.VMEM((tm, w), F32), pltpu.VMEM((1, w), F32)],
        compiler_params=_cparams("parallel", "arbitrary"),
        name="lru_scan",
    )(xpre, xpre, xpre, conv_w, conv_b.reshape(1, w), wa, ba.reshape(1, w), wx, bx.reshape(1, w),
      lam.reshape(1, w), h0)


def rglru_residual(x, xc, mods_l, mods_c, w_in, conv_w, conv_b, wa, ba, wx, bx, lam, w_out, n_tok, n_ctx):
    b = x.shape[0] // n_tok
    w = w_in.shape[1] // 2
    w_in = w_in.astype(BF16)
    sh_l, sc_l, g_l = mods_l
    sh_c, sc_c, g_c = mods_c
    y_l, xp_l = modmm(x, sh_l, sc_l, w_in, n_tok, (w, w), ("gelu", None), (F32, F32))
    y_c, xp_c = modmm(xc, sh_c, sc_c, w_in, n_ctx, (w, w), ("gelu", None), (F32, F32))
    zero = jnp.zeros((b, 1, w), F32)
    hs_l, hs_c = [], []
    for d in range(2):
        gate_w = (conv_w, conv_b, wa[d].astype(BF16), ba[d], wx[d].astype(BF16), bx[d], lam[d])
        hc, state = lru_scan(xp_c, *gate_w, zero, n_ctx, reverse=d == 1)
        hl, _ = lru_scan(xp_l, *gate_w, state, n_tok, reverse=d == 1)
        hs_c.append(hc)
        hs_l.append(hl)
    w_out = w_out.astype(BF16)
    return (mm_residual(tuple(hs_l), w_out, x, g_l, n_tok, b=y_l),
            mm_residual(tuple(hs_c), w_out, xc, g_c, n_ctx, b=y_c))


def _gmlp_kernel(x_ref, sh_ref, sc_ref, g_ref, win_ref, lng_ref, lnb_ref, ws_ref, bs_ref, wout_ref, o_ref, *, half):
    x = x_ref[...]
    h = _modulate(x, sh_ref[0], sc_ref[0]).astype(BF16)
    u = _gelu(_dot(h, win_ref[:, :half]))
    v = _gelu(_dot(h, win_ref[:, half:]))
    mu = jnp.mean(v, axis=-1, keepdims=True)
    vc = v - mu
    var = jnp.mean(vc * vc, axis=-1, keepdims=True)
    vn = (vc * lax.rsqrt(var + EPS) * lng_ref[...] + lnb_ref[...]).astype(BF16)
    gw = half // GMLP_GROUPS
    chunks = []
    for c in range(x.shape[0] // GMLP_CHUNK):
        rws = slice(c * GMLP_CHUNK, (c + 1) * GMLP_CHUNK)
        chunks.append(jnp.concatenate(
            [_dot(ws_ref[g], vn[rws, g * gw:(g + 1) * gw]) + bs_ref[g] for g in range(GMLP_GROUPS)], axis=1))
    v2 = jnp.concatenate(chunks, axis=0)
    o_ref[...] = x + g_ref[0] * _dot((u * v2).astype(BF16), wout_ref[...])


def gmlp_residual(x, shift, scale, gate, w_in, ln_g, ln_b, w_s, b_s, w_out, n_tok, tm=256):
    r, d = x.shape
    half = w_in.shape[1] // 2
    tm = _row_tile(n_tok, tm)
    tpb = n_tok // tm
    assert tm % GMLP_CHUNK == 0
    const2 = lambda i: (0, 0)
    const3 = lambda i: (0, 0, 0)
    once = pl.Buffered(1)
    return pl.pallas_call(
        functools.partial(_gmlp_kernel, half=half),
        out_shape=jax.ShapeDtypeStruct((r, d), F32),
        grid=(r // tm,),
        in_specs=[
            pl.BlockSpec((tm, d), lambda i: (i, 0)),
            _mod_spec(shift, tpb),
            _mod_spec(scale, tpb),
            _mod_spec(gate, tpb),
            pl.BlockSpec(w_in.shape, const2, pipeline_mode=once),
            pl.BlockSpec((1, half), const2),
            pl.BlockSpec((1, half), const2),
            pl.BlockSpec(w_s.shape, const3),
            pl.BlockSpec(b_s.shape + (1,), const3),
            pl.BlockSpec(w_out.shape, const2, pipeline_mode=once),
        ],
        out_specs=pl.BlockSpec((tm, d), lambda i: (i, 0)),
        compiler_params=_cparams("parallel"),
        name="gmlp_mixer",
    )(x, shift, scale, gate, w_in.astype(BF16), ln_g.reshape(1, half), ln_b.reshape(1, half),
      w_s.astype(BF16), b_s[..., None], w_out.astype(BF16))


def kernel(x, c, ctx, c_ctx, mod_w, mod_b, pool_w, pool_scale, lru_w_in, lru_conv_w, lru_conv_b, lru_wa, lru_ba,
           lru_wx, lru_bx, lru_lam, lru_w_out, attn_w_qkv, attn_q_gain, attn_k_gain, attn_w_o, gmlp_w_in, gmlp_ln_g,
           gmlp_ln_b, gmlp_w_s, gmlp_b_s, gmlp_w_out, moe_router, moe_w_gate, moe_w_up, moe_w_down, final_gain):
    b, n_tok, d = x.shape
    n_ctx = ctx.shape[1]
    depth = mod_w.shape[0]
    n_mixers = 4
    xl = x.reshape(b * n_tok, d)
    xc = ctx.reshape(b * n_ctx, d)

    pad = -(b + 1) % SUBLANES
    cvec = jnp.concatenate([c, c_ctx[None, :], jnp.zeros((pad, d), F32)], axis=0)
    mods = mod_vectors(cvec, mod_w, mod_b)

    for i in range(depth):
        m, j = i % n_mixers, i // n_mixers
        last = i == depth - 1
        ml = [mods[i, :b, k * d:(k + 1) * d].reshape(b, 1, d) for k in range(6)]
        mc = [mods[i, b:b + 1, k * d:(k + 1) * d].reshape(1, 1, d) for k in range(6)]
        if m == 0:
            pw = pool_w[j].astype(BF16)
            xl_new = pool_mixer_residual(xl, ml[0], ml[1], ml[2], pw, pool_scale[j], n_tok)
            if not last:
                xc = pool_mixer_residual(xc, mc[0], mc[1], mc[2], pw, pool_scale[j], n_ctx)
            xl = xl_new
        elif m == 1:
            xl, xc_new = rglru_residual(xl, xc, ml[:3], mc[:3], lru_w_in[j], lru_conv_w[j], lru_conv_b[j], lru_wa[j],
                                        lru_ba[j], lru_wx[j], lru_bx[j], lru_lam[j], lru_w_out[j], n_tok, n_ctx)
            xc = xc if last else xc_new
        elif m == 2:
            xl, xc_new = gqa_residual(xl, xc, ml[:3], mc[:3], attn_w_qkv[j], attn_q_gain[j], attn_k_gain[j],
                                      attn_w_o[j], n_tok, n_ctx)
            xc = xc if last else xc_new
        else:
            gargs = (gmlp_w_in[j], gmlp_ln_g[j], gmlp_ln_b[j], gmlp_w_s[j], gmlp_b_s[j], gmlp_w_out[j])
            xl_new = gmlp_residual(xl, ml[0], ml[1], ml[2], *gargs, n_tok)
            if not last:
                xc = gmlp_residual(xc, mc[0], mc[1], mc[2], *gargs, n_ctx)
            xl = xl_new
        wr = jnp.pad(moe_router[i], ((0, 0), (0, LANES - N_EXPERTS)))
        xl = moe_residual(xl, ml[3], ml[4], ml[5], wr, moe_w_gate, moe_w_up, moe_w_down, i, n_tok,
                          final_gain if last else None)
        if not last:
            xc = moe_residual(xc, mc[3], mc[4], mc[5], wr, moe_w_gate, moe_w_up, moe_w_down, i, n_ctx)
    return xl.reshape(b, n_tok, d)
```

```python
import functools

import jax
import jax.numpy as jnp
from jax import lax
from jax.experimental import pallas as pl
from jax.experimental.pallas import tpu as pltpu

F32 = jnp.float32
BF16 = jnp.bfloat16
EPS = 1e-6

N_EXPERTS = 16
CAPACITY_FACTOR = 2
POOL_WINDOWS = (2, 4, 8, 16)
POOL_HALO = 8
LRU_BLOCKS = 8
LRU_C = 8.0
CONV_W = 4
HEAD_DIM = 64
N_KV_HEADS = 4
GRID_W = 64
ROPE_THETA = 10000.0
GMLP_GROUPS = 4
GMLP_CHUNK = 128

LANES = 128
SUBLANES = 8
VMEM_LIMIT = 56 * 1024 * 1024


def _cparams(*sem):
    return pltpu.CompilerParams(dimension_semantics=sem, vmem_limit_bytes=VMEM_LIMIT)


def _modulate(x, shift, scale):
    ms = jnp.mean(x * x, axis=-1, keepdims=True)
    return x * lax.rsqrt(ms + EPS) * (1.0 + scale) + shift


def _split_bf16(a):
    hi = a.astype(BF16)
    lo = (a - hi.astype(F32)).astype(BF16)
    return hi, lo


def _dot(a, b):
    return jnp.dot(a, b, preferred_element_type=F32)


def _dot3(a, b):
    a_hi, a_lo = _split_bf16(a)
    b_hi, b_lo = _split_bf16(b)
    return _dot(a_hi, b_hi) + (_dot(a_hi, b_lo) + _dot(a_lo, b_hi))


def _gelu(x):
    return 0.5 * x * (1.0 + jnp.tanh(0.7978845608028654 * (x + 0.044715 * (x * x * x))))


def _silu(x):
    return x * (1.0 / (1.0 + jnp.exp(-x)))


def _sigmoid(x):
    return 0.5 * jnp.tanh(0.5 * x) + 0.5


def _mod_spec(mod, tiles_per_batch):
    d = mod.shape[-1]
    if mod.shape[0] == 1:
        return pl.BlockSpec((1, 1, d), lambda i, *_: (0, 0, 0))
    return pl.BlockSpec((1, 1, d), lambda i, *_: (i // tiles_per_batch, 0, 0))


def _row_tile(n, want):
    t = min(n, want)
    assert n % t == 0
    return t


def _mod_kernel(c_ref, w_ref, b_ref, o_ref):
    s = _silu(c_ref[...])
    o_ref[0] = _dot3(s, w_ref[0]) + b_ref[0]


def mod_vectors(cvec, mod_w, mod_b):
    depth, d, n6 = mod_w.shape
    tn = 1024
    return pl.pallas_call(
        _mod_kernel,
        out_shape=jax.ShapeDtypeStruct((depth, cvec.shape[0], n6), F32),
        grid=(depth, n6 // tn),
        in_specs=[
            pl.BlockSpec(cvec.shape, lambda l, j: (0, 0)),
            pl.BlockSpec((1, d, tn), lambda l, j: (l, 0, j)),
            pl.BlockSpec((1, 1, tn), lambda l, j: (l, 0, j)),
        ],
        out_specs=pl.BlockSpec((1, cvec.shape[0], tn), lambda l, j: (l, 0, j)),
        compiler_params=_cparams("parallel", "parallel"),
        name="mod_vectors",
    )(cvec, mod_w, mod_b.reshape(depth, 1, n6))


def _modmm_kernel(x_ref, sh_ref, sc_ref, w_ref, *o_refs, splits, acts):
    h = _modulate(x_ref[...], sh_ref[0], sc_ref[0]).astype(BF16)
    z = _dot(h, w_ref[...])
    off = 0
    for o_ref, width, act in zip(o_refs, splits, acts):
        part = z[:, off:off + width]
        if act == "gelu":
            part = _gelu(part)
        o_ref[...] = part.astype(o_ref.dtype)
        off += width


def modmm(x, shift, scale, w, n_tok, splits, acts, dtypes, tm=512):
    r, d = x.shape
    tm = _row_tile(n_tok, tm)
    tpb = n_tok // tm
    n = w.shape[1]
    assert sum(splits) == n
    return pl.pallas_call(
        functools.partial(_modmm_kernel, splits=splits, acts=acts),
        out_shape=[jax.ShapeDtypeStruct((r, s), dt) for s, dt in zip(splits, dtypes)],
        grid=(r // tm,),
        in_specs=[
            pl.BlockSpec((tm, d), lambda i: (i, 0)),
            _mod_spec(shift, tpb),
            _mod_spec(scale, tpb),
            pl.BlockSpec((d, n), lambda i: (0, 0)),
        ],
        out_specs=[pl.BlockSpec((tm, s), lambda i: (i, 0)) for s in splits],
        compiler_params=_cparams("parallel"),
        name="modmm",
    )(x, shift, scale, w)


def _mmres_kernel(*refs, n_sum, has_mul):
    a = refs[0][...]
    for a_ref in refs[1:n_sum]:
        a = a + a_ref[...]
    refs = refs[n_sum:]
    if has_mul:
        a = a.astype(F32) * refs[0][...].astype(F32)
        refs = refs[1:]
    w_ref, x_ref, g_ref, o_ref = refs
    o_ref[...] = x_ref[...] + g_ref[0] * _dot(a.astype(BF16), w_ref[...])


def mm_residual(a, w, x, gate, n_tok, b=None, tm=512):
    a = a if isinstance(a, (tuple, list)) else (a,)
    r, k = a[0].shape
    n = w.shape[1]
    tm = _row_tile(n_tok, tm)
    tpb = n_tok // tm
    ins = list(a) + ([b] if b is not None else []) + [w, x, gate]
    specs = [pl.BlockSpec((tm, k), lambda i: (i, 0))] * (len(a) + (b is not None))
    specs += [
        pl.BlockSpec((k, n), lambda i: (0, 0)),
        pl.BlockSpec((tm, n), lambda i: (i, 0)),
        _mod_spec(gate, tpb),
    ]
    return pl.pallas_call(
        functools.partial(_mmres_kernel, n_sum=len(a), has_mul=b is not None),
        out_shape=jax.ShapeDtypeStruct((r, n), F32),
        grid=(r // tm,),
        in_specs=specs,
        out_specs=pl.BlockSpec((tm, n), lambda i: (i, 0)),
        compiler_params=_cparams("parallel"),
        name="mm_residual",
    )(*ins)


def _pool_kernel(xp_ref, x_ref, xn_ref, sh_ref, sc_ref, g_ref, w_ref, ps_ref, o_ref, *, tm, n_tok, tpb):
    t0 = (pl.program_id(0) % tpb) * tm
    x = x_ref[...]
    xe = jnp.concatenate([xp_ref[...], x, xn_ref[...]], axis=0)
    rows = tm + 2 * POOL_HALO
    h = _modulate(xe, sh_ref[0], sc_ref[0])
    pos = lax.broadcasted_iota(jnp.int32, (rows, 1), 0) + (t0 - POOL_HALO)
    h = jnp.where((pos >= 0) & (pos < n_tok), h, 0.0)
    posc = pos[POOL_HALO:POOL_HALO + tm]
    gw = h.shape[1] // len(POOL_WINDOWS)
    outs = []
    for g, win in enumerate(POOL_WINDOWS):
        hg = h[:, g * gw:(g + 1) * gw]
        c = hg + pltpu.roll(hg, 1, 0)
        step = 1
        while 2 * step < win:
            c = pltpu.roll(c, step, 0) + pltpu.roll(c, rows - step, 0)
            step *= 2
        cnt = jnp.minimum(posc + (win - win // 2), n_tok) - jnp.maximum(posc - win // 2, 0)
        pooled = c[POOL_HALO:POOL_HALO + tm] / cnt.astype(F32) - hg[POOL_HALO:POOL_HALO + tm]
        outs.append(_dot(pooled.astype(BF16), w_ref[g]))
    y = jnp.concatenate(outs, axis=1) * ps_ref[...]
    o_ref[...] = x + g_ref[0] * y


def pool_mixer_residual(x, shift, scale, gate, w_pool, pool_scale, n_tok, tm=256):
    r, d = x.shape
    tm = _row_tile(n_tok, tm)
    tpb = n_tok // tm
    hb = tm // POOL_HALO
    last = r // POOL_HALO - 1
    groups, gw, _ = w_pool.shape
    return pl.pallas_call(
        functools.partial(_pool_kernel, tm=tm, n_tok=n_tok, tpb=tpb),
        out_shape=jax.ShapeDtypeStruct((r, d), F32),
        grid=(r // tm,),
        in_specs=[
            pl.BlockSpec((POOL_HALO, d), lambda i: (jnp.maximum(i * hb - 1, 0), 0)),
            pl.BlockSpec((tm, d), lambda i: (i, 0)),
            pl.BlockSpec((POOL_HALO, d), lambda i: (jnp.minimum((i + 1) * hb, last), 0)),
            _mod_spec(shift, tpb),
            _mod_spec(scale, tpb),
            _mod_spec(gate, tpb),
            pl.BlockSpec((groups, gw, gw), lambda i: (0, 0, 0)),
            pl.BlockSpec((1, d), lambda i: (0, 0)),
        ],
        out_specs=pl.BlockSpec((tm, d), lambda i: (i, 0)),
        compiler_params=_cparams("parallel"),
        name="pool_mixer",
    )(x, x, x, shift, scale, gate, w_pool, pool_scale.reshape(1, d))


def _router_kernel(x_ref, sh_ref, sc_ref, wr_ref, h_ref, aff_ref):
    h = _modulate(x_ref[...], sh_ref[0], sc_ref[0])
    h_ref[...] = h.astype(h_ref.dtype)
    logits = _dot3(h, wr_ref[...])
    lane = lax.broadcasted_iota(jnp.int32, logits.shape, 1)
    logits = jnp.where(lane < N_EXPERTS, logits, -jnp.inf)
    e = jnp.exp(logits - jnp.max(logits, axis=-1, keepdims=True))
    aff_ref[...] = e / jnp.sum(e, axis=-1, keepdims=True)


def router(x, shift, scale, w_router_padded, n_tok, tm=512):
    r, d = x.shape
    tm = _row_tile(n_tok, tm)
    tpb = n_tok // tm
    return pl.pallas_call(
        _router_kernel,
        out_shape=[jax.ShapeDtypeStruct((r, d), BF16), jax.ShapeDtypeStruct((r, LANES), F32)],
        grid=(r // tm,),
        in_specs=[
            pl.BlockSpec((tm, d), lambda i: (i, 0)),
            _mod_spec(shift, tpb),
            _mod_spec(scale, tpb),
            pl.BlockSpec((d, LANES), lambda i: (0, 0)),
        ],
        out_specs=[pl.BlockSpec((tm, d), lambda i: (i, 0)), pl.BlockSpec((tm, LANES), lambda i: (i, 0))],
        compiler_params=_cparams("parallel"),
        name="moe_router",
    )(x, shift, scale, w_router_padded)


MOE_CHUNK = 128
EXPERT_GROUPS = 2
COMBINE_CHUNKS = 2
NARROW_WINDOW = 64
PACK = LANES // N_EXPERTS


def _cumsum_rows(v, tb):
    n = v.shape[0]
    tri = (lax.broadcasted_iota(jnp.int32, (tb, tb), 0) >= lax.broadcasted_iota(jnp.int32, (tb, tb), 1)).astype(BF16)
    carry = jnp.zeros((1, v.shape[1]), F32)
    outs, starts = [], []
    for j in range(n // tb):
        starts.append(carry)
        c = _dot(tri, v[j * tb:(j + 1) * tb].astype(BF16)) + carry
        carry = c[tb - 1:tb, :]
        outs.append(c)
    return jnp.concatenate(outs, axis=0), jnp.concatenate(starts, axis=0)


def _route_kernel(aff_ref, affp_ref, idx_ref, gate_ref, slot_ref, start_ref, slot_scr, acc_scr, *, n, cap):
    def enough(cand):
        cnt = jnp.sum((pltpu.bitcast(affp_ref[...], jnp.int32) >= cand).astype(F32), axis=0, keepdims=True)
        shift = N_EXPERTS
        while shift < LANES:
            cnt = cnt + pltpu.roll(cnt, shift, 1)
            shift *= 2
        return cnt >= cap

    def two_bits(k, prefix):
        low = 28 - 2 * k
        c1, c2, c3 = (prefix | jnp.left_shift(jnp.int32(j), low) for j in (1, 2, 3))
        return jnp.where(enough(c3), c3, jnp.where(enough(c2), c2, jnp.where(enough(c1), c1, prefix)))

    top = jnp.full((1, LANES), 1 << 30, jnp.int32)
    thr = jnp.where(enough(top), top, jnp.zeros_like(top))
    thr = lax.fori_loop(0, 15, two_bits, thr)
    bits = pltpu.bitcast(aff_ref[...], jnp.int32)
    gt = bits > thr
    eq = bits == thr
    need = cap - jnp.sum(gt.astype(F32), axis=0, keepdims=True)
    eq_rank, _ = _cumsum_rows(eq.astype(F32), MOE_CHUNK)
    sel = gt | (eq & (eq_rank <= need))
    pos, starts = _cumsum_rows(sel.astype(F32), MOE_CHUNK)
    slot_scr[...] = jnp.where(sel, pos - 1.0, -1.0)
    slot_ref[...] = slot_scr[...]
    start_ref[0] = starts.astype(jnp.int32)

    slots = lax.broadcasted_iota(jnp.int32, (1, cap), 1).astype(F32)
    sub = lax.broadcasted_iota(jnp.int32, (SUBLANES, MOE_CHUNK), 0)
    local = lax.broadcasted_iota(jnp.int32, (SUBLANES, MOE_CHUNK), 1).astype(F32)
    acc_scr[...] = jnp.zeros(acc_scr.shape, F32)

    def chunk(c, carry):
        rows = pl.ds(pl.multiple_of(c * MOE_CHUNK, MOE_CHUNK), MOE_CHUNK)
        base = jnp.where(sub == 0, local, jnp.where(sub == 1, lax.convert_element_type(c, F32), 0.0))
        aff_t = aff_ref[rows, :].T
        for e in range(N_EXPERTS):
            onehot = jnp.where(slot_scr[rows, e:e + 1] == slots, 1.0, 0.0).astype(BF16)
            g = aff_t[e:e + 1, :]
            g_hi = g.astype(BF16).astype(F32)
            g_mid = (g - g_hi).astype(BF16).astype(F32)
            g_lo = g - g_hi - g_mid
            lhs = jnp.where(sub == 2, g_hi, jnp.where(sub == 3, g_mid, jnp.where(sub == 4, g_lo, base)))
            acc_scr[e] += _dot(lhs.astype(BF16), onehot)
        return carry

    lax.fori_loop(0, n // MOE_CHUNK, chunk, 0)
    for e in range(N_EXPERTS):
        idx_ref[0, e:e + 1, :] = (acc_scr[e, 0:1, :] + MOE_CHUNK * acc_scr[e, 1:2, :]).astype(jnp.int32)
        gate_ref[0, e:e + 1, :] = acc_scr[e, 2:3, :] + acc_scr[e, 3:4, :] + acc_scr[e, 4:5, :]


def route(aff, n_tok):
    r = aff.shape[0]
    b = r // n_tok
    cap = CAPACITY_FACTOR * n_tok // N_EXPERTS
    aff_packed = aff[:, :N_EXPERTS].reshape(r // PACK, LANES)
    n_chunks = n_tok // MOE_CHUNK
    return pl.pallas_call(
        functools.partial(_route_kernel, n=n_tok, cap=cap),
        out_shape=[jax.ShapeDtypeStruct((b, N_EXPERTS, cap), jnp.int32), jax.ShapeDtypeStruct((b, N_EXPERTS, cap), F32),
                   jax.ShapeDtypeStruct((r, LANES), F32), jax.ShapeDtypeStruct((b, n_chunks, LANES), jnp.int32)],
        grid=(b,),
        in_specs=[pl.BlockSpec((n_tok, LANES), lambda i: (i, 0)),
                  pl.BlockSpec((n_tok // PACK, LANES), lambda i: (i, 0))],
        out_specs=[pl.BlockSpec((1, N_EXPERTS, cap), lambda i: (i, 0, 0)),
                   pl.BlockSpec((1, N_EXPERTS, cap), lambda i: (i, 0, 0)),
                   pl.BlockSpec((n_tok, LANES), lambda i: (i, 0)),
                   pl.BlockSpec((1, n_chunks, LANES), lambda i: (i, 0, 0))],
        scratch_shapes=[pltpu.VMEM((n_tok, LANES), F32), pltpu.VMEM((N_EXPERTS, SUBLANES, cap), F32)],
        compiler_params=_cparams("parallel"),
        name="moe_route",
    )(aff, aff_packed)


def _ffn_kernel(doff_ref, x_ref, gt_ref, wg_ref, wu_ref, wd_ref, y_ref, dup_ref, wg_scr, wu_scr, wd_scr, y_scr, *,
                e0, cap, n_steps, dwin):
    @pl.when(pl.program_id(1) == 0)
    def _():
        wg_scr[...] = wg_ref[0, 0].astype(BF16)
        wu_scr[...] = wu_ref[0, 0].astype(BF16)
        wd_scr[...] = wd_ref[0, 0].astype(BF16)

    x = x_ref[0]
    a = _dot(x, wg_scr[...])
    u = _dot(x, wu_scr[...])
    hmid = (_silu(a) * u).astype(BF16)
    y_scr[...] = (_dot(hmid, wd_scr[...]) * gt_ref[0]).astype(BF16)
    y_ref[0] = y_scr[...]
    e = e0 + pl.program_id(0)
    align = min(dwin, 16)
    for bl in range(y_scr.shape[0] // cap):
        b = pl.program_id(1) * (y_scr.shape[0] // cap) + bl
        for s in range(n_steps):
            off = pl.multiple_of(doff_ref[(b * n_steps + s) * N_EXPERTS + e], align)
            dup_ref[bl, s, 0] = y_scr[pl.ds(bl * cap + off, dwin), :]


def expert_ffn(xs, gates, doffs, w_gate, w_up, w_down, layer, e0, cap, n_steps, dwin, tm=512):
    e, m, d = xs.shape
    ff = w_gate.shape[-1]
    tm = max(_row_tile(m, tm), cap)
    bpt = tm // cap
    return pl.pallas_call(
        functools.partial(_ffn_kernel, e0=e0, cap=cap, n_steps=n_steps, dwin=dwin),
        out_shape=[jax.ShapeDtypeStruct((e, m, d), BF16),
                   jax.ShapeDtypeStruct((m // cap, n_steps, e, dwin, d), BF16)],
        grid_spec=pltpu.PrefetchScalarGridSpec(
            num_scalar_prefetch=1,
            grid=(e, m // tm),
            in_specs=[
                pl.BlockSpec((1, tm, d), lambda k, i, off: (k, i, 0)),
                pl.BlockSpec((1, tm, 1), lambda k, i, off: (k, i, 0)),
                pl.BlockSpec((1, 1, d, ff), lambda k, i, off: (layer, e0 + k, 0, 0)),
                pl.BlockSpec((1, 1, d, ff), lambda k, i, off: (layer, e0 + k, 0, 0)),
                pl.BlockSpec((1, 1, ff, d), lambda k, i, off: (layer, e0 + k, 0, 0)),
            ],
            out_specs=[pl.BlockSpec((1, tm, d), lambda k, i, off: (k, i, 0)),
                       pl.BlockSpec((bpt, n_steps, 1, dwin, d), lambda k, i, off: (i, 0, k, 0, 0))],
            scratch_shapes=[pltpu.VMEM((d, ff), BF16), pltpu.VMEM((d, ff), BF16), pltpu.VMEM((ff, d), BF16),
                            pltpu.VMEM((tm, d), BF16)],
        ),
        compiler_params=_cparams("parallel", "arbitrary"),
        name="moe_expert_ffn",
    )(doffs, xs, gates, w_gate, w_up, w_down)


def _combine_kernel(woff_ref, doff_ref, fit_ref, ysel_ref, x_ref, g_ref, slot_ref, *rest, win, dwin, n_chunks, cps,
                    n_groups, final):
    dup_refs, y_refs, rest = rest[:n_groups], rest[n_groups:2 * n_groups], rest[2 * n_groups:]
    if final:
        fg_ref, o_ref = rest
    else:
        (o_ref,) = rest
    per_group = N_EXPERTS // n_groups
    step = pl.program_id(0) * (n_chunks // cps) + pl.program_id(1)

    def finish(rows, acc):
        out = x_ref[rows, :] + g_ref[0] * acc
        if final:
            out = out * lax.rsqrt(jnp.mean(out * out, axis=-1, keepdims=True) + EPS) * fg_ref[...]
        o_ref[rows, :] = out

    @pl.when(fit_ref[step] == 1)
    def _():
        tm = x_ref.shape[0]
        epl = LANES // dwin
        lane = lax.broadcasted_iota(jnp.int32, (1, LANES), 1)
        which = lane // dwin
        within = (lane % dwin).astype(F32)
        acc = jnp.zeros((tm, x_ref.shape[1]), F32)
        for gi, dup_ref in enumerate(dup_refs):
            pieces = []
            for p in range(per_group // epl):
                col, sl = None, None
                for w in range(epl):
                    e = gi * per_group + p * epl + w
                    cw = lax.convert_element_type(doff_ref[step * N_EXPERTS + e], F32) + within
                    sw = jnp.broadcast_to(slot_ref[:, e:e + 1], (tm, LANES))
                    col = cw if w == 0 else jnp.where(which == w, cw, col)
                    sl = sw if w == 0 else jnp.where(which == w, sw, sl)
                pieces.append(jnp.where(sl == col, 1.0, 0.0).astype(BF16))
            onehot = jnp.concatenate(pieces, axis=1)
            acc = acc + _dot(onehot, dup_ref[0, 0].reshape(per_group * dwin, x_ref.shape[1]))
        finish(slice(0, tm), acc)

    @pl.when(fit_ref[step] == 0)
    def _():
        align = min(win, LANES)
        for sc in range(cps):
            rows = slice(sc * MOE_CHUNK, (sc + 1) * MOE_CHUNK)
            base = (step * cps + sc) * N_EXPERTS
            acc = jnp.zeros((MOE_CHUNK, x_ref.shape[1]), F32)
            for e in range(N_EXPERTS):
                off = pl.multiple_of(woff_ref[base + e], align)
                cols = (lax.broadcasted_iota(jnp.int32, (1, win), 1) + off).astype(F32)
                onehot = jnp.where(slot_ref[rows, e:e + 1] == cols, 1.0, 0.0).astype(BF16)
                acc = acc + _dot(onehot, y_refs[e // per_group][e % per_group, pl.ds(off, win), :])
            finish(rows, acc)


def moe_combine(x, gate2, slot, ys, dups, woffs, doffs, fits, n_tok, final_gain=None):
    r, d = x.shape
    b = r // n_tok
    per_group = ys[0].shape[0]
    cap = ys[0].shape[1] // b
    dwin = dups[0].shape[3]
    n_chunks = n_tok // MOE_CHUNK
    cps = COMBINE_CHUNKS if n_chunks % COMBINE_CHUNKS == 0 else 1
    tm = cps * MOE_CHUNK
    steps = n_chunks // cps
    win = min(2 * MOE_CHUNK, cap)
    final = final_gain is not None
    need = jnp.max((1 - fits).reshape(b, steps), axis=1)
    ysel = lax.cummax(need * jnp.arange(b, dtype=jnp.int32))
    row_spec = lambda w: pl.BlockSpec((tm, w), lambda bi, c, *_: (bi * steps + c, 0))
    ins = [woffs, doffs, fits, ysel, x, gate2, slot] + list(dups) + list(ys)
    specs = [row_spec(d), _mod_spec(gate2, 1), row_spec(LANES)]
    specs += [pl.BlockSpec((1, 1, per_group, dwin, d), lambda bi, c, *_: (bi, c, 0, 0, 0))] * len(dups)
    specs += [pl.BlockSpec((per_group, cap, d), lambda bi, c, wo, do, ft, ys_: (0, ys_[bi], 0),
                           pipeline_mode=pl.Buffered(1))] * len(ys)
    if final:
        ins.append(final_gain.reshape(1, d))
        specs.append(pl.BlockSpec((1, d), lambda bi, c, *_: (0, 0)))
    return pl.pallas_call(
        functools.partial(_combine_kernel, win=win, dwin=dwin, n_chunks=n_chunks, cps=cps, n_groups=len(ys),
                          final=final),
        out_shape=jax.ShapeDtypeStruct((r, d), F32),
        grid_spec=pltpu.PrefetchScalarGridSpec(
            num_scalar_prefetch=4,
            grid=(b, steps),
            in_specs=specs,
            out_specs=row_spec(d),
        ),
        compiler_params=_cparams("parallel", "arbitrary"),
        name="moe_combine",
    )(*ins)


def moe_residual(x, shift, scale, gate2, w_router_padded, w_gate, w_up, w_down, layer, n_tok, final_gain=None):
    r, d = x.shape
    b = r // n_tok
    h, aff = router(x, shift, scale, w_router_padded, n_tok)
    idx, gates, slot, start = route(aff, n_tok)
    cap = idx.shape[-1]
    n_chunks = n_tok // MOE_CHUNK
    cps = COMBINE_CHUNKS if n_chunks % COMBINE_CHUNKS == 0 else 1
    steps = n_chunks // cps
    start = start[:, :, :N_EXPERTS]
    win = min(2 * MOE_CHUNK, cap)
    align = min(win, LANES)
    woffs = jnp.clip(start // align * align, 0, cap - win).reshape(-1)
    dwin = min(NARROW_WINDOW, cap)
    dalign = min(dwin, 16)
    s_start = start[:, ::cps]
    s_end = jnp.concatenate([s_start[:, 1:], jnp.full((b, 1, N_EXPERTS), cap, jnp.int32)], axis=1)
    doffs = jnp.clip(s_start // dalign * dalign, 0, cap - dwin)
    fits = jnp.all(s_end - doffs <= dwin, axis=-1).astype(jnp.int32).reshape(-1)
    doffs = doffs.reshape(-1)

    gidx = idx + (jnp.arange(b, dtype=jnp.int32) * n_tok)[:, None, None]
    gidx = jnp.transpose(gidx, (1, 0, 2)).reshape(N_EXPERTS, b * cap)
    gts = jnp.transpose(gates, (1, 0, 2)).reshape(N_EXPERTS, b * cap, 1)
    per_group = N_EXPERTS // EXPERT_GROUPS
    ys, dups = [], []
    for g in range(EXPERT_GROUPS):
        grp = slice(g * per_group, (g + 1) * per_group)
        xs = h.at[gidx[grp]].get(mode="promise_in_bounds")
        y, dup = expert_ffn(xs, gts[grp], doffs, w_gate, w_up, w_down, layer, g * per_group, cap, steps, dwin)
        ys.append(y)
        dups.append(dup)
    return moe_combine(x, gate2, slot, ys, dups, woffs, doffs, fits, n_tok, final_gain)


def _head_rms(t, seg_ones):
    outs = []
    for j in range(t.shape[1] // LANES):
        blk = t[:, j * LANES:(j + 1) * LANES]
        hi, lo = _split_bf16(blk * blk)
        ss = _dot(hi, seg_ones) + _dot(lo, seg_ones)
        outs.append(blk * lax.rsqrt(ss * (1.0 / HEAD_DIM) + EPS))
    return jnp.concatenate(outs, axis=1)


def _rope(t, cos, sin_signed):
    w = t.shape[1]
    half = HEAD_DIM // 2
    lane = lax.broadcasted_iota(jnp.int32, t.shape, 1)
    partner = jnp.where(lane % HEAD_DIM < half, pltpu.roll(t, w - half, 1), pltpu.roll(t, half, 1))
    reps = w // cos.shape[1]
    return t * jnp.concatenate([cos] * reps, axis=1) + partner * jnp.concatenate([sin_signed] * reps, axis=1)


def _qkv_kernel(*refs, qd, kd, rope):
    if rope:
        x_ref, sh_ref, sc_ref, w_ref, qg_ref, kg_ref, cos_ref, sin_ref, q_ref, k_ref, v_ref = refs
    else:
        x_ref, sh_ref, sc_ref, w_ref, qg_ref, kg_ref, q_ref, k_ref, v_ref = refs
    h = _modulate(x_ref[...], sh_ref[0], sc_ref[0]).astype(BF16)
    z = _dot(h, w_ref[...])
    seg = (lax.broadcasted_iota(jnp.int32, (LANES, LANES), 0) // HEAD_DIM
           == lax.broadcasted_iota(jnp.int32, (LANES, LANES), 1) // HEAD_DIM).astype(BF16)
    q = _head_rms(z[:, :qd], seg) * qg_ref[...]
    k = _head_rms(z[:, qd:qd + kd], seg) * kg_ref[...]
    if rope:
        q = _rope(q, cos_ref[...], sin_ref[...])
        k = _rope(k, cos_ref[...], sin_ref[...])
    q_ref[...] = (q * (HEAD_DIM ** -0.5 * 1.4426950408889634)).astype(q_ref.dtype)
    k_ref[...] = k.astype(k_ref.dtype)
    v_ref[...] = z[:, qd + kd:].astype(v_ref.dtype)


def qkv_project(x, shift, scale, w_qkv, q_gain, k_gain, n_tok, rope_tables=None, tm=256):
    r, d = x.shape
    kd = N_KV_HEADS * HEAD_DIM
    qd = w_qkv.shape[1] - 2 * kd
    tm = _row_tile(n_tok, tm)
    tpb = n_tok // tm
    ins = [x, shift, scale, w_qkv, q_gain, k_gain]
    specs = [
        pl.BlockSpec((tm, d), lambda i: (i, 0)),
        _mod_spec(shift, tpb),
        _mod_spec(scale, tpb),
        pl.BlockSpec(w_qkv.shape, lambda i: (0, 0)),
        pl.BlockSpec((1, qd), lambda i: (0, 0)),
        pl.BlockSpec((1, kd), lambda i: (0, 0)),
    ]
    if rope_tables is not None:
        ins += list(rope_tables)
        specs += [pl.BlockSpec((tm, LANES), lambda i: (i % tpb, 0))] * 2
    return pl.pallas_call(
        functools.partial(_qkv_kernel, qd=qd, kd=kd, rope=rope_tables is not None),
        out_shape=[jax.ShapeDtypeStruct((r, qd), BF16), jax.ShapeDtypeStruct((r, kd), BF16),
                   jax.ShapeDtypeStruct((r, kd), BF16)],
        grid=(r // tm,),
        in_specs=specs,
        out_specs=[pl.BlockSpec((tm, qd), lambda i: (i, 0)), pl.BlockSpec((tm, kd), lambda i: (i, 0)),
                   pl.BlockSpec((tm, kd), lambda i: (i, 0))],
        compiler_params=_cparams("parallel"),
        name="qkv_project",
    )(*ins)


def _attn_kernel(q_ref, k_ref, vt_ref, o_ref, *scr, grp, bounded):
    k = k_ref[0, 0]
    vt = vt_ref[0, 0]
    tq = q_ref.shape[0]
    n_pairs = grp // 2
    outs = []

    def scores(pair):
        heads = (2 * pair, 2 * pair + 1)
        q2 = jnp.concatenate([q_ref[:, h * HEAD_DIM:(h + 1) * HEAD_DIM] for h in heads], axis=0)
        return lax.dot_general(k, q2, (((1,), (1,)), ((), ())), preferred_element_type=F32)

    if bounded:
        for pair in range(n_pairs):
            p = jnp.exp2(scores(pair))
            scr[pair][...] = p.astype(BF16)
            ot = _dot(vt[:HEAD_DIM], scr[pair][...]) / jnp.sum(p, axis=0, keepdims=True)
            o = jnp.concatenate([ot, ot], axis=0).T
            outs += [o[:tq, :HEAD_DIM], o[tq:, :HEAD_DIM]]
        p_refs = ()
    else:
        for pair in range(n_pairs):
            scr[2 * pair][...] = scores(pair)
        for pair in range(n_pairs):
            st_scr, p_scr = scr[2 * pair], scr[2 * pair + 1]
            p_scr[...] = jnp.exp2(st_scr[...] - jnp.max(st_scr[...], axis=0, keepdims=True)).astype(BF16)
        p_refs = scr[1::2]
    for p_scr in p_refs:
        ot = _dot(vt, p_scr[...])
        ot = ot[:HEAD_DIM] / ot[HEAD_DIM:]
        o = jnp.concatenate([ot, ot], axis=0).T
        outs += [o[:tq, :HEAD_DIM], o[tq:, :HEAD_DIM]]
    o_ref[...] = jnp.concatenate(outs, axis=1).astype(o_ref.dtype)


def _attention_call(q, k, vt, n_q, tq, bounded):
    r, qd = q.shape
    _, kvh, n_k, hd = k.shape
    grp = qd // (kvh * hd)
    tq = _row_tile(n_q, tq)
    tpb = n_q // tq
    dts = (BF16,) if bounded else (F32, BF16)
    return pl.pallas_call(
        functools.partial(_attn_kernel, grp=grp, bounded=bounded),
        out_shape=jax.ShapeDtypeStruct((r, qd), BF16),
        grid=(r // n_q, kvh, tpb),
        in_specs=[
            pl.BlockSpec((tq, grp * hd), lambda b, g, i: (b * tpb + i, g)),
            pl.BlockSpec((1, 1, n_k, hd), lambda b, g, i: (b, g, 0, 0)),
            pl.BlockSpec((1, 1, 2 * hd, n_k), lambda b, g, i: (b, g, 0, 0)),
        ],
        out_specs=pl.BlockSpec((tq, grp * hd), lambda b, g, i: (b * tpb + i, g)),
        scratch_shapes=[pltpu.VMEM((n_k, 2 * tq), dt) for _ in range(grp // 2) for dt in dts],
        compiler_params=_cparams("parallel", "parallel", "parallel"),
        name="attention_bounded" if bounded else "attention",
    )(q, k, vt)


SCORE_BOUND = 60.0


def attention(q, k, vt, n_q, score_bound, tq=256):
    return lax.cond(score_bound <= SCORE_BOUND,
                    lambda: _attention_call(q, k, vt, n_q, tq, True),
                    lambda: _attention_call(q, k, vt, n_q, tq, False))


def rope_tables(n_tok):
    t = jnp.arange(n_tok)
    row = (t // GRID_W).astype(F32)
    col = (t % GRID_W).astype(F32)
    n_freq = HEAD_DIM // 4
    inv = ROPE_THETA ** (-jnp.arange(n_freq, dtype=F32) / n_freq)
    ang = jnp.concatenate([row[:, None] * inv, col[:, None] * inv], axis=-1)
    cos, sin = jnp.cos(ang), jnp.sin(ang)
    reps = LANES // HEAD_DIM
    return jnp.tile(jnp.concatenate([cos, cos], -1), (1, reps)), jnp.tile(jnp.concatenate([-sin, sin], -1), (1, reps))


def _deinterleave_heads(w, n_heads):
    lead = w.shape[:-1]
    w = w.reshape(lead + (n_heads, HEAD_DIM // 2, 2))
    return jnp.swapaxes(w, -1, -2).reshape(lead + (n_heads * HEAD_DIM,))


def _split_kv_heads(t, b):
    return jnp.transpose(t.reshape(b, -1, N_KV_HEADS, HEAD_DIM), (0, 2, 1, 3))


def gqa_residual(x, xc, mods_l, mods_c, w_qkv, q_gain, k_gain, w_o, n_tok, n_ctx):
    b = x.shape[0] // n_tok
    kd = N_KV_HEADS * HEAD_DIM
    qd = w_qkv.shape[1] - 2 * kd
    n_qh = qd // HEAD_DIM
    w_perm = jnp.concatenate([_deinterleave_heads(w_qkv[:, :qd], n_qh),
                              _deinterleave_heads(w_qkv[:, qd:qd + kd], N_KV_HEADS), w_qkv[:, qd + kd:]], axis=1)
    w_perm = w_perm.astype(BF16)
    qg = jnp.tile(_deinterleave_heads(q_gain, 1), n_qh).reshape(1, qd)
    kg = jnp.tile(_deinterleave_heads(k_gain, 1), N_KV_HEADS).reshape(1, kd)
    sh_l, sc_l, g_l = mods_l
    sh_c, sc_c, g_c = mods_c
    q_l, k_l, v_l = qkv_project(x, sh_l, sc_l, w_perm, qg, kg, n_tok, rope_tables(n_tok))
    q_c, k_c, v_c = qkv_project(xc, sh_c, sc_c, w_perm, qg, kg, n_ctx)
    k_c4 = _split_kv_heads(k_c, b)
    k_all = jnp.concatenate([k_c4, _split_kv_heads(k_l, b)], axis=2)

    def values_t(v):
        vt = jnp.transpose(v.reshape(b, -1, N_KV_HEADS, HEAD_DIM), (0, 2, 3, 1))
        return jnp.concatenate([vt, jnp.ones_like(vt)], axis=2)

    vt_c = values_t(v_c)
    vt_all = jnp.concatenate([vt_c, values_t(v_l)], axis=3)
    score_bound = (1.01 * HEAD_DIM * HEAD_DIM ** -0.5 * 1.4426950408889634
                   * jnp.max(jnp.abs(q_gain)) * jnp.max(jnp.abs(k_gain)))
    o_l = attention(q_l, k_all, vt_all, n_tok, score_bound)
    o_c = attention(q_c, k_c4, vt_c, n_ctx, score_bound)
    w_o = w_o.astype(BF16)
    return mm_residual(o_l, w_o, x, g_l, n_tok), mm_residual(o_c, w_o, xc, g_c, n_ctx)


def _lru_kernel(xp_ref, x_ref, xn_ref, cw_ref, cb_ref, wa_ref, ba_ref, wx_ref, bx_ref, lam_ref, h0_ref,
                o_ref, hT_ref, a_scr, u_scr, carry_scr, *, tm, n_tok, tpb, reverse):
    step = pl.program_id(1)
    t_idx = (tpb - 1 - step) if reverse else step
    t0 = t_idx * tm
    rows = tm + 2 * SUBLANES

    @pl.when(step == 0)
    def _():
        carry_scr[...] = h0_ref[0]

    xe = jnp.concatenate([xp_ref[...], x_ref[...], xn_ref[...]], axis=0)
    pos = lax.broadcasted_iota(jnp.int32, (rows, 1), 0) + (t0 - SUBLANES)
    xe = jnp.where((pos >= 0) & (pos < n_tok), xe, 0.0)
    left = CONV_W // 2
    conv = cb_ref[...]
    for k in range(CONV_W):
        shift = (left - k) % rows
        tap = xe if shift == 0 else pltpu.roll(xe, shift, 0)
        conv = conv + tap * cw_ref[k:k + 1, :]
    xr = conv[SUBLANES:SUBLANES + tm]

    xb = xr.astype(BF16)
    bw = xr.shape[1] // LRU_BLOCKS
    ra, ia = [], []
    for j in range(LRU_BLOCKS):
        blk = xb[:, j * bw:(j + 1) * bw]
        ra.append(_dot(blk, wa_ref[j]))
        ia.append(_dot(blk, wx_ref[j]))
    r = _sigmoid(jnp.concatenate(ra, axis=1) + ba_ref[...])
    i = _sigmoid(jnp.concatenate(ia, axis=1) + bx_ref[...])
    log_a = -LRU_C * r * jnp.logaddexp(-lam_ref[...], 0.0)
    a = jnp.exp(log_a)
    a_scr[...] = a
    t = jnp.tanh(log_a)
    u_scr[...] = xr * i * jnp.sqrt(-2.0 * t / (1.0 - t))

    n_grp = tm // SUBLANES
    sub = lax.broadcasted_iota(jnp.int32, (SUBLANES, 1), 0)

    def group(j, carry):
        g = (n_grp - 1 - j) if reverse else j
        rws = pl.ds(pl.multiple_of(g * SUBLANES, SUBLANES), SUBLANES)
        ag, ug = a_scr[rws, :], u_scr[rws, :]
        s = 1
        while s < SUBLANES:
            if reverse:
                ok = sub < SUBLANES - s
                sh = SUBLANES - s
            else:
                ok = sub >= s
                sh = s
            u_prev = jnp.where(ok, pltpu.roll(ug, sh, 0), 0.0)
            a_prev = jnp.where(ok, pltpu.roll(ag, sh, 0), 1.0)
            ug = ug + ag * u_prev
            ag = ag * a_prev
            s *= 2
        hg = ug + ag * carry
        o_ref[rws, :] = hg
        return hg[0:1, :] if reverse else hg[SUBLANES - 1:SUBLANES, :]

    carry = lax.fori_loop(0, n_grp, group, carry_scr[...])
    carry_scr[...] = carry
    hT_ref[0] = carry


def lru_scan(xpre, conv_w, conv_b, wa, ba, wx, bx, lam, h0, n_tok, reverse, tm=256):
    r, w = xpre.shape
    b = r // n_tok
    tm = _row_tile(n_tok, tm)
    tpb = n_tok // tm
    hb = tm // SUBLANES
    last = r // SUBLANES - 1

    def tile(bi, s):
        return bi * tpb + ((tpb - 1 - s) if reverse else s)

    vec = pl.BlockSpec((1, w), lambda bi, s: (0, 0))
    blocks = pl.BlockSpec(wa.shape, lambda bi, s: (0, 0, 0))
    return pl.pallas_call(
        functools.partial(_lru_kernel, tm=tm, n_tok=n_tok, tpb=tpb, reverse=reverse),
        out_shape=[jax.ShapeDtypeStruct((r, w), F32), jax.ShapeDtypeStruct((b, 1, w), F32)],
        grid=(b, tpb),
        in_specs=[
            pl.BlockSpec((SUBLANES, w), lambda bi, s: (jnp.maximum(tile(bi, s) * hb - 1, 0), 0)),
            pl.BlockSpec((tm, w), lambda bi, s: (tile(bi, s), 0)),
            pl.BlockSpec((SUBLANES, w), lambda bi, s: (jnp.minimum((tile(bi, s) + 1) * hb, last), 0)),
            pl.BlockSpec((CONV_W, w), lambda bi, s: (0, 0)),
            vec, blocks, vec, blocks, vec, vec,
            pl.BlockSpec((1, 1, w), lambda bi, s: (bi, 0, 0)),
        ],
        out_specs=[pl.BlockSpec((tm, w), lambda bi, s: (tile(bi, s), 0)),
                   pl.BlockSpec((1, 1, w), lambda bi, s: (bi, 0, 0))],
        scratch_shapes=[pltpu.VMEM((tm, w), F32), pltpu.VMEM((tm, w), F32), pltpu.VMEM((1, w), F32)],
        compiler_params=_cparams("parallel", "arbitrary"),
        name="lru_scan",
    )(xpre, xpre, xpre, conv_w, conv_b.reshape(1, w), wa, ba.reshape(1, w), wx, bx.reshape(1, w),
      lam.reshape(1, w), h0)


def rglru_residual(x, xc, mods_l, mods_c, w_in, conv_w, conv_b, wa, ba, wx, bx, lam, w_out, n_tok, n_ctx):
    b = x.shape[0] // n_tok
    w = w_in.shape[1] // 2
    w_in = w_in.astype(BF16)
    sh_l, sc_l, g_l = mods_l
    sh_c, sc_c, g_c = mods_c
    y_l, xp_l = modmm(x, sh_l, sc_l, w_in, n_tok, (w, w), ("gelu", None), (F32, F32))
    y_c, xp_c = modmm(xc, sh_c, sc_c, w_in, n_ctx, (w, w), ("gelu", None), (F32, F32))
    zero = jnp.zeros((b, 1, w), F32)
    hs_l, hs_c = [], []
    for d in range(2):
        gate_w = (conv_w, conv_b, wa[d].astype(BF16), ba[d], wx[d].astype(BF16), bx[d], lam[d])
        hc, state = lru_scan(xp_c, *gate_w, zero, n_ctx, reverse=d == 1)
        hl, _ = lru_scan(xp_l, *gate_w, state, n_tok, reverse=d == 1)
        hs_c.append(hc)
        hs_l.append(hl)
    w_out = w_out.astype(BF16)
    return (mm_residual(tuple(hs_l), w_out, x, g_l, n_tok, b=y_l),
            mm_residual(tuple(hs_c), w_out, xc, g_c, n_ctx, b=y_c))


def _gmlp_kernel(x_ref, sh_ref, sc_ref, g_ref, win_ref, lng_ref, lnb_ref, ws_ref, bs_ref, wout_ref, o_ref, *, half):
    x = x_ref[...]
    h = _modulate(x, sh_ref[0], sc_ref[0]).astype(BF16)
    u = _gelu(_dot(h, win_ref[:, :half]))
    v = _gelu(_dot(h, win_ref[:, half:]))
    mu = jnp.mean(v, axis=-1, keepdims=True)
    vc = v - mu
    var = jnp.mean(vc * vc, axis=-1, keepdims=True)
    vn = (vc * lax.rsqrt(var + EPS) * lng_ref[...] + lnb_ref[...]).astype(BF16)
    gw = half // GMLP_GROUPS
    chunks = []
    for c in range(x.shape[0] // GMLP_CHUNK):
        rws = slice(c * GMLP_CHUNK, (c + 1) * GMLP_CHUNK)
        chunks.append(jnp.concatenate(
            [_dot(ws_ref[g], vn[rws, g * gw:(g + 1) * gw]) + bs_ref[g] for g in range(GMLP_GROUPS)], axis=1))
    v2 = jnp.concatenate(chunks, axis=0)
    o_ref[...] = x + g_ref[0] * _dot((u * v2).astype(BF16), wout_ref[...])


def gmlp_residual(x, shift, scale, gate, w_in, ln_g, ln_b, w_s, b_s, w_out, n_tok, tm=256):
    r, d = x.shape
    half = w_in.shape[1] // 2
    tm = _row_tile(n_tok, tm)
    tpb = n_tok // tm
    assert tm % GMLP_CHUNK == 0
    const2 = lambda i: (0, 0)
    const3 = lambda i: (0, 0, 0)
    once = pl.Buffered(1)
    return pl.pallas_call(
        functools.partial(_gmlp_kernel, half=half),
        out_shape=jax.ShapeDtypeStruct((r, d), F32),
        grid=(r // tm,),
        in_specs=[
            pl.BlockSpec((tm, d), lambda i: (i, 0)),
            _mod_spec(shift, tpb),
            _mod_spec(scale, tpb),
            _mod_spec(gate, tpb),
            pl.BlockSpec(w_in.shape, const2, pipeline_mode=once),
            pl.BlockSpec((1, half), const2),
            pl.BlockSpec((1, half), const2),
            pl.BlockSpec(w_s.shape, const3),
            pl.BlockSpec(b_s.shape + (1,), const3),
            pl.BlockSpec(w_out.shape, const2, pipeline_mode=once),
        ],
        out_specs=pl.BlockSpec((tm, d), lambda i: (i, 0)),
        compiler_params=_cparams("parallel"),
        name="gmlp_mixer",
    )(x, shift, scale, gate, w_in.astype(BF16), ln_g.reshape(1, half), ln_b.reshape(1, half),
      w_s.astype(BF16), b_s[..., None], w_out.astype(BF16))


def kernel(x, c, ctx, c_ctx, mod_w, mod_b, pool_w, pool_scale, lru_w_in, lru_conv_w, lru_conv_b, lru_wa, lru_ba,
           lru_wx, lru_bx, lru_lam, lru_w_out, attn_w_qkv, attn_q_gain, attn_k_gain, attn_w_o, gmlp_w_in, gmlp_ln_g,
           gmlp_ln_b, gmlp_w_s, gmlp_b_s, gmlp_w_out, moe_router, moe_w_gate, moe_w_up, moe_w_down, final_gain):
    b, n_tok, d = x.shape
    n_ctx = ctx.shape[1]
    depth = mod_w.shape[0]
    n_mixers = 4
    xl = x.reshape(b * n_tok, d)
    xc = ctx.reshape(b * n_ctx, d)

    pad = -(b + 1) % SUBLANES
    cvec = jnp.concatenate([c, c_ctx[None, :], jnp.zeros((pad, d), F32)], axis=0)
    mods = mod_vectors(cvec, mod_w, mod_b)

    for i in range(depth):
        m, j = i % n_mixers, i // n_mixers
        last = i == depth - 1
        ml = [mods[i, :b, k * d:(k + 1) * d].reshape(b, 1, d) for k in range(6)]
        mc = [mods[i, b:b + 1, k * d:(k + 1) * d].reshape(1, 1, d) for k in range(6)]
        if m == 0:
            pw = pool_w[j].astype(BF16)
            xl_new = pool_mixer_residual(xl, ml[0], ml[1], ml[2], pw, pool_scale[j], n_tok)
            if not last:
                xc = pool_mixer_residual(xc, mc[0], mc[1], mc[2], pw, pool_scale[j], n_ctx)
            xl = xl_new
        elif m == 1:
            xl, xc_new = rglru_residual(xl, xc, ml[:3], mc[:3], lru_w_in[j], lru_conv_w[j], lru_conv_b[j], lru_wa[j],
                                        lru_ba[j], lru_wx[j], lru_bx[j], lru_lam[j], lru_w_out[j], n_tok, n_ctx)
            xc = xc if last else xc_new
        elif m == 2:
            xl, xc_new = gqa_residual(xl, xc, ml[:3], mc[:3], attn_w_qkv[j], attn_q_gain[j], attn_k_gain[j],
                                      attn_w_o[j], n_tok, n_ctx)
            xc = xc if last else xc_new
        else:
            gargs = (gmlp_w_in[j], gmlp_ln_g[j], gmlp_ln_b[j], gmlp_w_s[j], gmlp_b_s[j], gmlp_w_out[j])
            xl_new = gmlp_residual(xl, ml[0], ml[1], ml[2], *gargs, n_tok)
            if not last:
                xc = gmlp_residual(xc, mc[0], mc[1], mc[2], *gargs, n_ctx)
            xl = xl_new
        wr = jnp.pad(moe_router[i], ((0, 0), (0, LANES - N_EXPERTS)))
        xl = moe_residual(xl, ml[3], ml[4], ml[5], wr, moe_w_gate, moe_w_up, moe_w_down, i, n_tok,
                          final_gain if last else None)
        if not last:
            xc = moe_residual(xc, mc[3], mc[4], mc[5], wr, moe_w_gate, moe_w_up, moe_w_down, i, n_ctx)
    return xl.reshape(b, n_tok, d)
```

```python
import functools

import jax
import jax.numpy as jnp
from jax import lax
from jax.experimental import pallas as pl
from jax.experimental.pallas import tpu as pltpu

F32 = jnp.float32
BF16 = jnp.bfloat16
EPS = 1e-6

N_EXPERTS = 16
CAPACITY_FACTOR = 2
POOL_WINDOWS = (2, 4, 8, 16)
POOL_HALO = 8
LRU_BLOCKS = 8
LRU_C = 8.0
CONV_W = 4
HEAD_DIM = 64
N_KV_HEADS = 4
GRID_W = 64
ROPE_THETA = 10000.0
GMLP_GROUPS = 4
GMLP_CHUNK = 128

LANES = 128
SUBLANES = 8
VMEM_LIMIT = 56 * 1024 * 1024


def _cparams(*sem):
    return pltpu.CompilerParams(dimension_semantics=sem, vmem_limit_bytes=VMEM_LIMIT)


def _modulate(x, shift, scale):
    ms = jnp.mean(x * x, axis=-1, keepdims=True)
    return x * lax.rsqrt(ms + EPS) * (1.0 + scale) + shift


def _split_bf16(a):
    hi = a.astype(BF16)
    lo = (a - hi.astype(F32)).astype(BF16)
    return hi, lo


def _dot(a, b):
    return jnp.dot(a, b, preferred_element_type=F32)


def _dot3(a, b):
    a_hi, a_lo = _split_bf16(a)
    b_hi, b_lo = _split_bf16(b)
    return _dot(a_hi, b_hi) + (_dot(a_hi, b_lo) + _dot(a_lo, b_hi))


def _gelu(x):
    return 0.5 * x * (1.0 + jnp.tanh(0.7978845608028654 * (x + 0.044715 * (x * x * x))))


def _silu(x):
    return x * (1.0 / (1.0 + jnp.exp(-x)))


def _sigmoid(x):
    return 0.5 * jnp.tanh(0.5 * x) + 0.5


def _mod_spec(mod, tiles_per_batch):
    d = mod.shape[-1]
    if mod.shape[0] == 1:
        return pl.BlockSpec((1, 1, d), lambda i, *_: (0, 0, 0))
    return pl.BlockSpec((1, 1, d), lambda i, *_: (i // tiles_per_batch, 0, 0))


def _row_tile(n, want):
    t = min(n, want)
    assert n % t == 0
    return t


def _mod_kernel(c_ref, w_ref, b_ref, o_ref):
    s = _silu(c_ref[...])
    o_ref[0] = _dot3(s, w_ref[0]) + b_ref[0]


def mod_vectors(cvec, mod_w, mod_b):
    depth, d, n6 = mod_w.shape
    tn = 1024
    return pl.pallas_call(
        _mod_kernel,
        out_shape=jax.ShapeDtypeStruct((depth, cvec.shape[0], n6), F32),
        grid=(depth, n6 // tn),
        in_specs=[
            pl.BlockSpec(cvec.shape, lambda l, j: (0, 0)),
            pl.BlockSpec((1, d, tn), lambda l, j: (l, 0, j)),
            pl.BlockSpec((1, 1, tn), lambda l, j: (l, 0, j)),
        ],
        out_specs=pl.BlockSpec((1, cvec.shape[0], tn), lambda l, j: (l, 0, j)),
        compiler_params=_cparams("parallel", "parallel"),
        name="mod_vectors",
    )(cvec, mod_w, mod_b.reshape(depth, 1, n6))


def _modmm_kernel(x_ref, sh_ref, sc_ref, w_ref, *o_refs, splits, acts):
    h = _modulate(x_ref[...], sh_ref[0], sc_ref[0]).astype(BF16)
    z = _dot(h, w_ref[...])
    off = 0
    for o_ref, width, act in zip(o_refs, splits, acts):
        part = z[:, off:off + width]
        if act == "gelu":
            part = _gelu(part)
        o_ref[...] = part.astype(o_ref.dtype)
        off += width


def modmm(x, shift, scale, w, n_tok, splits, acts, dtypes, tm=512):
    r, d = x.shape
    tm = _row_tile(n_tok, tm)
    tpb = n_tok // tm
    n = w.shape[1]
    assert sum(splits) == n
    return pl.pallas_call(
        functools.partial(_modmm_kernel, splits=splits, acts=acts),
        out_shape=[jax.ShapeDtypeStruct((r, s), dt) for s, dt in zip(splits, dtypes)],
        grid=(r // tm,),
        in_specs=[
            pl.BlockSpec((tm, d), lambda i: (i, 0)),
            _mod_spec(shift, tpb),
            _mod_spec(scale, tpb),
            pl.BlockSpec((d, n), lambda i: (0, 0)),
        ],
        out_specs=[pl.BlockSpec((tm, s), lambda i: (i, 0)) for s in splits],
        compiler_params=_cparams("parallel"),
        name="modmm",
    )(x, shift, scale, w)


def _mmres_kernel(*refs, n_sum, has_mul):
    a = refs[0][...]
    for a_ref in refs[1:n_sum]:
        a = a + a_ref[...]
    refs = refs[n_sum:]
    if has_mul:
        a = a.astype(F32) * refs[0][...].astype(F32)
        refs = refs[1:]
    w_ref, x_ref, g_ref, o_ref = refs
    o_ref[...] = x_ref[...] + g_ref[0] * _dot(a.astype(BF16), w_ref[...])


def mm_residual(a, w, x, gate, n_tok, b=None, tm=512):
    a = a if isinstance(a, (tuple, list)) else (a,)
    r, k = a[0].shape
    n = w.shape[1]
    tm = _row_tile(n_tok, tm)
    tpb = n_tok // tm
    ins = list(a) + ([b] if b is not None else []) + [w, x, gate]
    specs = [pl.BlockSpec((tm, k), lambda i: (i, 0))] * (len(a) + (b is not None))
    specs += [
        pl.BlockSpec((k, n), lambda i: (0, 0)),
        pl.BlockSpec((tm, n), lambda i: (i, 0)),
        _mod_spec(gate, tpb),
    ]
    return pl.pallas_call(
        functools.partial(_mmres_kernel, n_sum=len(a), has_mul=b is not None),
        out_shape=jax.ShapeDtypeStruct((r, n), F32),
        grid=(r // tm,),
        in_specs=specs,
        out_specs=pl.BlockSpec((tm, n), lambda i: (i, 0)),
        compiler_params=_cparams("parallel"),
        name="mm_residual",
    )(*ins)


def _pool_kernel(xp_ref, x_ref, xn_ref, sh_ref, sc_ref, g_ref, w_ref, ps_ref, o_ref, *, tm, n_tok, tpb):
    t0 = (pl.program_id(0) % tpb) * tm
    x = x_ref[...]
    xe = jnp.concatenate([xp_ref[...], x, xn_ref[...]], axis=0)
    rows = tm + 2 * POOL_HALO
    h = _modulate(xe, sh_ref[0], sc_ref[0])
    pos = lax.broadcasted_iota(jnp.int32, (rows, 1), 0) + (t0 - POOL_HALO)
    h = jnp.where((pos >= 0) & (pos < n_tok), h, 0.0)
    posc = pos[POOL_HALO:POOL_HALO + tm]
    gw = h.shape[1] // len(POOL_WINDOWS)
    outs = []
    for g, win in enumerate(POOL_WINDOWS):
        hg = h[:, g * gw:(g + 1) * gw]
        c = hg + pltpu.roll(hg, 1, 0)
        step = 1
        while 2 * step < win:
            c = pltpu.roll(c, step, 0) + pltpu.roll(c, rows - step, 0)
            step *= 2
        cnt = jnp.minimum(posc + (win - win // 2), n_tok) - jnp.maximum(posc - win // 2, 0)
        pooled = c[POOL_HALO:POOL_HALO + tm] / cnt.astype(F32) - hg[POOL_HALO:POOL_HALO + tm]
        outs.append(_dot(pooled.astype(BF16), w_ref[g]))
    y = jnp.concatenate(outs, axis=1) * ps_ref[...]
    o_ref[...] = x + g_ref[0] * y


def pool_mixer_residual(x, shift, scale, gate, w_pool, pool_scale, n_tok, tm=256):
    r, d = x.shape
    tm = _row_tile(n_tok, tm)
    tpb = n_tok // tm
    hb = tm // POOL_HALO
    last = r // POOL_HALO - 1
    groups, gw, _ = w_pool.shape
    return pl.pallas_call(
        functools.partial(_pool_kernel, tm=tm, n_tok=n_tok, tpb=tpb),
        out_shape=jax.ShapeDtypeStruct((r, d), F32),
        grid=(r // tm,),
        in_specs=[
            pl.BlockSpec((POOL_HALO, d), lambda i: (jnp.maximum(i * hb - 1, 0), 0)),
            pl.BlockSpec((tm, d), lambda i: (i, 0)),
            pl.BlockSpec((POOL_HALO, d), lambda i: (jnp.minimum((i + 1) * hb, last), 0)),
            _mod_spec(shift, tpb),
            _mod_spec(scale, tpb),
            _mod_spec(gate, tpb),
            pl.BlockSpec((groups, gw, gw), lambda i: (0, 0, 0)),
            pl.BlockSpec((1, d), lambda i: (0, 0)),
        ],
        out_specs=pl.BlockSpec((tm, d), lambda i: (i, 0)),
        compiler_params=_cparams("parallel"),
        name="pool_mixer",
    )(x, x, x, shift, scale, gate, w_pool, pool_scale.reshape(1, d))


def _router_kernel(x_ref, sh_ref, sc_ref, wr_ref, h_ref, aff_ref):
    h = _modulate(x_ref[...], sh_ref[0], sc_ref[0])
    h_ref[...] = h.astype(h_ref.dtype)
    logits = _dot3(h, wr_ref[...])
    lane = lax.broadcasted_iota(jnp.int32, logits.shape, 1)
    logits = jnp.where(lane < N_EXPERTS, logits, -jnp.inf)
    e = jnp.exp(logits - jnp.max(logits, axis=-1, keepdims=True))
    aff_ref[...] = e / jnp.sum(e, axis=-1, keepdims=True)


def router(x, shift, scale, w_router_padded, n_tok, tm=512):
    r, d = x.shape
    tm = _row_tile(n_tok, tm)
    tpb = n_tok // tm
    return pl.pallas_call(
        _router_kernel,
        out_shape=[jax.ShapeDtypeStruct((r, d), BF16), jax.ShapeDtypeStruct((r, LANES), F32)],
        grid=(r // tm,),
        in_specs=[
            pl.BlockSpec((tm, d), lambda i: (i, 0)),
            _mod_spec(shift, tpb),
            _mod_spec(scale, tpb),
            pl.BlockSpec((d, LANES), lambda i: (0, 0)),
        ],
        out_specs=[pl.BlockSpec((tm, d), lambda i: (i, 0)), pl.BlockSpec((tm, LANES), lambda i: (i, 0))],
        compiler_params=_cparams("parallel"),
        name="moe_router",
    )(x, shift, scale, w_router_padded)


MOE_CHUNK = 128
EXPERT_GROUPS = 2
COMBINE_CHUNKS = 2
NARROW_WINDOW = 64
PACK = LANES // N_EXPERTS


def _cumsum_rows(v, tb):
    n = v.shape[0]
    tri = (lax.broadcasted_iota(jnp.int32, (tb, tb), 0) >= lax.broadcasted_iota(jnp.int32, (tb, tb), 1)).astype(BF16)
    carry = jnp.zeros((1, v.shape[1]), F32)
    outs, starts = [], []
    for j in range(n // tb):
        starts.append(carry)
        c = _dot(tri, v[j * tb:(j + 1) * tb].astype(BF16)) + carry
        carry = c[tb - 1:tb, :]
        outs.append(c)
    return jnp.concatenate(outs, axis=0), jnp.concatenate(starts, axis=0)


def _route_kernel(aff_ref, affp_ref, idx_ref, gate_ref, slot_ref, start_ref, slot_scr, acc_scr, *, n, cap):
    def enough(cand):
        cnt = jnp.sum((pltpu.bitcast(affp_ref[...], jnp.int32) >= cand).astype(F32), axis=0, keepdims=True)
        shift = N_EXPERTS
        while shift < LANES:
            cnt = cnt + pltpu.roll(cnt, shift, 1)
            shift *= 2
        return cnt >= cap

    def two_bits(k, prefix):
        low = 28 - 2 * k
        c1, c2, c3 = (prefix | jnp.left_shift(jnp.int32(j), low) for j in (1, 2, 3))
        return jnp.where(enough(c3), c3, jnp.where(enough(c2), c2, jnp.where(enough(c1), c1, prefix)))

    top = jnp.full((1, LANES), 1 << 30, jnp.int32)
    thr = jnp.where(enough(top), top, jnp.zeros_like(top))
    thr = lax.fori_loop(0, 15, two_bits, thr)
    bits = pltpu.bitcast(aff_ref[...], jnp.int32)
    gt = bits > thr
    eq = bits == thr
    need = cap - jnp.sum(gt.astype(F32), axis=0, keepdims=True)
    eq_rank, _ = _cumsum_rows(eq.astype(F32), MOE_CHUNK)
    sel = gt | (eq & (eq_rank <= need))
    pos, starts = _cumsum_rows(sel.astype(F32), MOE_CHUNK)
    slot_scr[...] = jnp.where(sel, pos - 1.0, -1.0)
    slot_ref[...] = slot_scr[...]
    start_ref[0] = starts.astype(jnp.int32)

    slots = lax.broadcasted_iota(jnp.int32, (1, cap), 1).astype(F32)
    sub = lax.broadcasted_iota(jnp.int32, (SUBLANES, MOE_CHUNK), 0)
    local = lax.broadcasted_iota(jnp.int32, (SUBLANES, MOE_CHUNK), 1).astype(F32)
    acc_scr[...] = jnp.zeros(acc_scr.shape, F32)

    def chunk(c, carry):
        rows = pl.ds(pl.multiple_of(c * MOE_CHUNK, MOE_CHUNK), MOE_CHUNK)
        base = jnp.where(sub == 0, local, jnp.where(sub == 1, lax.convert_element_type(c, F32), 0.0))
        aff_t = aff_ref[rows, :].T
        for e in range(N_EXPERTS):
            onehot = jnp.where(slot_scr[rows, e:e + 1] == slots, 1.0, 0.0).astype(BF16)
            g = aff_t[e:e + 1, :]
            g_hi = g.astype(BF16).astype(F32)
            g_mid = (g - g_hi).astype(BF16).astype(F32)
            g_lo = g - g_hi - g_mid
            lhs = jnp.where(sub == 2, g_hi, jnp.where(sub == 3, g_mid, jnp.where(sub == 4, g_lo, base)))
            acc_scr[e] += _dot(lhs.astype(BF16), onehot)
        return carry

    lax.fori_loop(0, n // MOE_CHUNK, chunk, 0)
    for e in range(N_EXPERTS):
        idx_ref[0, e:e + 1, :] = (acc_scr[e, 0:1, :] + MOE_CHUNK * acc_scr[e, 1:2, :]).astype(jnp.int32)
        gate_ref[0, e:e + 1, :] = acc_scr[e, 2:3, :] + acc_scr[e, 3:4, :] + acc_scr[e, 4:5, :]


def route(aff, n_tok):
    r = aff.shape[0]
    b = r // n_tok
    cap = CAPACITY_FACTOR * n_tok // N_EXPERTS
    aff_packed = aff[:, :N_EXPERTS].reshape(r // PACK, LANES)
    n_chunks = n_tok // MOE_CHUNK
    return pl.pallas_call(
        functools.partial(_route_kernel, n=n_tok, cap=cap),
        out_shape=[jax.ShapeDtypeStruct((b, N_EXPERTS, cap), jnp.int32), jax.ShapeDtypeStruct((b, N_EXPERTS, cap), F32),
                   jax.ShapeDtypeStruct((r, LANES), F32), jax.ShapeDtypeStruct((b, n_chunks, LANES), jnp.int32)],
        grid=(b,),
        in_specs=[pl.BlockSpec((n_tok, LANES), lambda i: (i, 0)),
                  pl.BlockSpec((n_tok // PACK, LANES), lambda i: (i, 0))],
        out_specs=[pl.BlockSpec((1, N_EXPERTS, cap), lambda i: (i, 0, 0)),
                   pl.BlockSpec((1, N_EXPERTS, cap), lambda i: (i, 0, 0)),
                   pl.BlockSpec((n_tok, LANES), lambda i: (i, 0)),
                   pl.BlockSpec((1, n_chunks, LANES), lambda i: (i, 0, 0))],
        scratch_shapes=[pltpu.VMEM((n_tok, LANES), F32), pltpu.VMEM((N_EXPERTS, SUBLANES, cap), F32)],
        compiler_params=_cparams("parallel"),
        name="moe_route",
    )(aff, aff_packed)


def _ffn_kernel(doff_ref, x_ref, gt_ref, wg_ref, wu_ref, wd_ref, y_ref, dup_ref, wg_scr, wu_scr, wd_scr, y_scr, *,
                e0, cap, n_steps, dwin):
    @pl.when(pl.program_id(1) == 0)
    def _():
        wg_scr[...] = wg_ref[0, 0].astype(BF16)
        wu_scr[...] = wu_ref[0, 0].astype(BF16)
        wd_scr[...] = wd_ref[0, 0].astype(BF16)

    x = x_ref[0]
    a = _dot(x, wg_scr[...])
    u = _dot(x, wu_scr[...])
    hmid = (_silu(a) * u).astype(BF16)
    gate_cols = jnp.broadcast_to(gt_ref[0, 0], (LANES, x.shape[0])).T
    y = _dot(hmid, wd_scr[...])
    y_scr[...] = (y * jnp.concatenate([gate_cols] * (y.shape[1] // LANES), axis=1)).astype(BF16)
    y_ref[0] = y_scr[...]
    e = e0 + pl.program_id(0)
    align = min(dwin, 16)
    for bl in range(y_scr.shape[0] // cap):
        b = pl.program_id(1) * (y_scr.shape[0] // cap) + bl
        for s in range(n_steps):
            off = pl.multiple_of(doff_ref[(b * n_steps + s) * N_EXPERTS + e], align)
            dup_ref[bl, s, 0] = y_scr[pl.ds(bl * cap + off, dwin), :]


def expert_ffn(xs, gates, doffs, w_gate, w_up, w_down, layer, e0, cap, n_steps, dwin, tm=512):
    e, m, d = xs.shape
    ff = w_gate.shape[-1]
    tm = max(_row_tile(m, tm), cap)
    bpt = tm // cap
    return pl.pallas_call(
        functools.partial(_ffn_kernel, e0=e0, cap=cap, n_steps=n_steps, dwin=dwin),
        out_shape=[jax.ShapeDtypeStruct((e, m, d), BF16),
                   jax.ShapeDtypeStruct((m // cap, n_steps, e, dwin, d), BF16)],
        grid_spec=pltpu.PrefetchScalarGridSpec(
            num_scalar_prefetch=1,
            grid=(e, m // tm),
            in_specs=[
                pl.BlockSpec((1, tm, d), lambda k, i, off: (k, i, 0)),
                pl.BlockSpec((1, 1, 1, tm), lambda k, i, off: (k, i, 0, 0)),
                pl.BlockSpec((1, 1, d, ff), lambda k, i, off: (layer, e0 + k, 0, 0)),
                pl.BlockSpec((1, 1, d, ff), lambda k, i, off: (layer, e0 + k, 0, 0)),
                pl.BlockSpec((1, 1, ff, d), lambda k, i, off: (layer, e0 + k, 0, 0)),
            ],
            out_specs=[pl.BlockSpec((1, tm, d), lambda k, i, off: (k, i, 0)),
                       pl.BlockSpec((bpt, n_steps, 1, dwin, d), lambda k, i, off: (i, 0, k, 0, 0))],
            scratch_shapes=[pltpu.VMEM((d, ff), BF16), pltpu.VMEM((d, ff), BF16), pltpu.VMEM((ff, d), BF16),
                            pltpu.VMEM((tm, d), BF16)],
        ),
        compiler_params=_cparams("parallel", "arbitrary"),
        name="moe_expert_ffn",
    )(doffs, xs, gates.reshape(e, m // tm, 1, tm), w_gate, w_up, w_down)


def _combine_kernel(woff_ref, doff_ref, fit_ref, ysel_ref, x_ref, g_ref, slot_ref, *rest, win, dwin, n_chunks, cps,
                    n_groups, final):
    dup_refs, y_refs, rest = rest[:n_groups], rest[n_groups:2 * n_groups], rest[2 * n_groups:]
    if final:
        fg_ref, o_ref = rest
    else:
        (o_ref,) = rest
    per_group = N_EXPERTS // n_groups
    step = pl.program_id(0) * (n_chunks // cps) + pl.program_id(1)

    def finish(rows, acc):
        out = x_ref[rows, :] + g_ref[0] * acc
        if final:
            out = out * lax.rsqrt(jnp.mean(out * out, axis=-1, keepdims=True) + EPS) * fg_ref[...]
        o_ref[rows, :] = out

    @pl.when(fit_ref[step] == 1)
    def _():
        tm = x_ref.shape[0]
        epl = LANES // dwin
        lane = lax.broadcasted_iota(jnp.int32, (1, LANES), 1)
        which = lane // dwin
        within = (lane % dwin).astype(F32)
        acc = jnp.zeros((tm, x_ref.shape[1]), F32)
        for gi, dup_ref in enumerate(dup_refs):
            pieces = []
            for p in range(per_group // epl):
                col, sl = None, None
                for w in range(epl):
                    e = gi * per_group + p * epl + w
                    cw = lax.convert_element_type(doff_ref[step * N_EXPERTS + e], F32) + within
                    sw = jnp.broadcast_to(slot_ref[:, e:e + 1], (tm, LANES))
                    col = cw if w == 0 else jnp.where(which == w, cw, col)
                    sl = sw if w == 0 else jnp.where(which == w, sw, sl)
                pieces.append(jnp.where(sl == col, 1.0, 0.0).astype(BF16))
            onehot = jnp.concatenate(pieces, axis=1)
            acc = acc + _dot(onehot, dup_ref[0, 0].reshape(per_group * dwin, x_ref.shape[1]))
        finish(slice(0, tm), acc)

    @pl.when(fit_ref[step] == 0)
    def _():
        align = min(win, LANES)
        for sc in range(cps):
            rows = slice(sc * MOE_CHUNK, (sc + 1) * MOE_CHUNK)
            base = (step * cps + sc) * N_EXPERTS
            acc = jnp.zeros((MOE_CHUNK, x_ref.shape[1]), F32)
            for e in range(N_EXPERTS):
                off = pl.multiple_of(woff_ref[base + e], align)
                cols = (lax.broadcasted_iota(jnp.int32, (1, win), 1) + off).astype(F32)
                onehot = jnp.where(slot_ref[rows, e:e + 1] == cols, 1.0, 0.0).astype(BF16)
                acc = acc + _dot(onehot, y_refs[e // per_group][e % per_group, pl.ds(off, win), :])
            finish(rows, acc)


def moe_combine(x, gate2, slot, ys, dups, woffs, doffs, fits, n_tok, final_gain=None):
    r, d = x.shape
    b = r // n_tok
    per_group = ys[0].shape[0]
    cap = ys[0].shape[1] // b
    dwin = dups[0].shape[3]
    n_chunks = n_tok // MOE_CHUNK
    cps = COMBINE_CHUNKS if n_chunks % COMBINE_CHUNKS == 0 else 1
    tm = cps * MOE_CHUNK
    steps = n_chunks // cps
    win = min(2 * MOE_CHUNK, cap)
    final = final_gain is not None
    need = jnp.max((1 - fits).reshape(b, steps), axis=1)
    ysel = lax.cummax(need * jnp.arange(b, dtype=jnp.int32))
    row_spec = lambda w: pl.BlockSpec((tm, w), lambda bi, c, *_: (bi * steps + c, 0))
    ins = [woffs, doffs, fits, ysel, x, gate2, slot] + list(dups) + list(ys)
    specs = [row_spec(d), _mod_spec(gate2, 1), row_spec(LANES)]
    specs += [pl.BlockSpec((1, 1, per_group, dwin, d), lambda bi, c, *_: (bi, c, 0, 0, 0))] * len(dups)
    specs += [pl.BlockSpec((per_group, cap, d), lambda bi, c, wo, do, ft, ys_: (0, ys_[bi], 0),
                           pipeline_mode=pl.Buffered(1))] * len(ys)
    if final:
        ins.append(final_gain.reshape(1, d))
        specs.append(pl.BlockSpec((1, d), lambda bi, c, *_: (0, 0)))
    return pl.pallas_call(
        functools.partial(_combine_kernel, win=win, dwin=dwin, n_chunks=n_chunks, cps=cps, n_groups=len(ys),
                          final=final),
        out_shape=jax.ShapeDtypeStruct((r, d), F32),
        grid_spec=pltpu.PrefetchScalarGridSpec(
            num_scalar_prefetch=4,
            grid=(b, steps),
            in_specs=specs,
            out_specs=row_spec(d),
        ),
        compiler_params=_cparams("parallel", "arbitrary"),
        name="moe_combine",
    )(*ins)


def moe_residual(x, shift, scale, gate2, w_router_padded, w_gate, w_up, w_down, layer, n_tok, final_gain=None):
    r, d = x.shape
    b = r // n_tok
    h, aff = router(x, shift, scale, w_router_padded, n_tok)
    idx, gates, slot, start = route(aff, n_tok)
    cap = idx.shape[-1]
    n_chunks = n_tok // MOE_CHUNK
    cps = COMBINE_CHUNKS if n_chunks % COMBINE_CHUNKS == 0 else 1
    steps = n_chunks // cps
    start = start[:, :, :N_EXPERTS]
    win = min(2 * MOE_CHUNK, cap)
    align = min(win, LANES)
    woffs = jnp.clip(start // align * align, 0, cap - win).reshape(-1)
    dwin = min(NARROW_WINDOW, cap)
    dalign = min(dwin, 16)
    s_start = start[:, ::cps]
    s_end = jnp.concatenate([s_start[:, 1:], jnp.full((b, 1, N_EXPERTS), cap, jnp.int32)], axis=1)
    doffs = jnp.clip(s_start // dalign * dalign, 0, cap - dwin)
    fits = jnp.all(s_end - doffs <= dwin, axis=-1).astype(jnp.int32).reshape(-1)
    doffs = doffs.reshape(-1)

    gidx = idx + (jnp.arange(b, dtype=jnp.int32) * n_tok)[:, None, None]
    gidx = jnp.transpose(gidx, (1, 0, 2)).reshape(N_EXPERTS, b * cap)
    gts = jnp.transpose(gates, (1, 0, 2)).reshape(N_EXPERTS, b * cap)
    per_group = N_EXPERTS // EXPERT_GROUPS
    ys, dups = [], []
    for g in range(EXPERT_GROUPS):
        grp = slice(g * per_group, (g + 1) * per_group)
        xs = h.at[gidx[grp]].get(mode="promise_in_bounds")
        y, dup = expert_ffn(xs, gts[grp], doffs, w_gate, w_up, w_down, layer, g * per_group, cap, steps, dwin)
        ys.append(y)
        dups.append(dup)
    return moe_combine(x, gate2, slot, ys, dups, woffs, doffs, fits, n_tok, final_gain)


def _head_rms(t, seg_ones):
    outs = []
    for j in range(t.shape[1] // LANES):
        blk = t[:, j * LANES:(j + 1) * LANES]
        hi, lo = _split_bf16(blk * blk)
        ss = _dot(hi, seg_ones) + _dot(lo, seg_ones)
        outs.append(blk * lax.rsqrt(ss * (1.0 / HEAD_DIM) + EPS))
    return jnp.concatenate(outs, axis=1)


def _rope(t, cos, sin_signed):
    w = t.shape[1]
    half = HEAD_DIM // 2
    lane = lax.broadcasted_iota(jnp.int32, t.shape, 1)
    partner = jnp.where(lane % HEAD_DIM < half, pltpu.roll(t, w - half, 1), pltpu.roll(t, half, 1))
    reps = w // cos.shape[1]
    return t * jnp.concatenate([cos] * reps, axis=1) + partner * jnp.concatenate([sin_signed] * reps, axis=1)


def _qkv_kernel(*refs, qd, kd, rope):
    if rope:
        x_ref, sh_ref, sc_ref, w_ref, qg_ref, kg_ref, cos_ref, sin_ref, q_ref, k_ref, v_ref = refs
    else:
        x_ref, sh_ref, sc_ref, w_ref, qg_ref, kg_ref, q_ref, k_ref, v_ref = refs
    h = _modulate(x_ref[...], sh_ref[0], sc_ref[0]).astype(BF16)
    z = _dot(h, w_ref[...])
    seg = (lax.broadcasted_iota(jnp.int32, (LANES, LANES), 0) // HEAD_DIM
           == lax.broadcasted_iota(jnp.int32, (LANES, LANES), 1) // HEAD_DIM).astype(BF16)
    q = _head_rms(z[:, :qd], seg) * qg_ref[...]
    k = _head_rms(z[:, qd:qd + kd], seg) * kg_ref[...]
    if rope:
        q = _rope(q, cos_ref[...], sin_ref[...])
        k = _rope(k, cos_ref[...], sin_ref[...])
    q_ref[...] = (q * (HEAD_DIM ** -0.5 * 1.4426950408889634)).astype(q_ref.dtype)
    k_ref[...] = k.astype(k_ref.dtype)
    v_ref[...] = z[:, qd + kd:].astype(v_ref.dtype)


def qkv_project(x, shift, scale, w_qkv, q_gain, k_gain, n_tok, rope_tables=None, tm=256):
    r, d = x.shape
    kd = N_KV_HEADS * HEAD_DIM
    qd = w_qkv.shape[1] - 2 * kd
    tm = _row_tile(n_tok, tm)
    tpb = n_tok // tm
    ins = [x, shift, scale, w_qkv, q_gain, k_gain]
    specs = [
        pl.BlockSpec((tm, d), lambda i: (i, 0)),
        _mod_spec(shift, tpb),
        _mod_spec(scale, tpb),
        pl.BlockSpec(w_qkv.shape, lambda i: (0, 0)),
        pl.BlockSpec((1, qd), lambda i: (0, 0)),
        pl.BlockSpec((1, kd), lambda i: (0, 0)),
    ]
    if rope_tables is not None:
        ins += list(rope_tables)
        specs += [pl.BlockSpec((tm, LANES), lambda i: (i % tpb, 0))] * 2
    return pl.pallas_call(
        functools.partial(_qkv_kernel, qd=qd, kd=kd, rope=rope_tables is not None),
        out_shape=[jax.ShapeDtypeStruct((r, qd), BF16), jax.ShapeDtypeStruct((r, kd), BF16),
                   jax.ShapeDtypeStruct((r, kd), BF16)],
        grid=(r // tm,),
        in_specs=specs,
        out_specs=[pl.BlockSpec((tm, qd), lambda i: (i, 0)), pl.BlockSpec((tm, kd), lambda i: (i, 0)),
                   pl.BlockSpec((tm, kd), lambda i: (i, 0))],
        compiler_params=_cparams("parallel"),
        name="qkv_project",
    )(*ins)


def _attn_kernel(q_ref, k_ref, vt_ref, o_ref, *scr, grp, bounded):
    k = k_ref[0, 0]
    vt = vt_ref[0, 0]
    tq = q_ref.shape[0]
    n_pairs = grp // 2
    outs = []

    def scores(pair):
        heads = (2 * pair, 2 * pair + 1)
        q2 = jnp.concatenate([q_ref[:, h * HEAD_DIM:(h + 1) * HEAD_DIM] for h in heads], axis=0)
        return lax.dot_general(k, q2, (((1,), (1,)), ((), ())), preferred_element_type=F32)

    if bounded:
        for pair in range(n_pairs):
            p = jnp.exp2(scores(pair))
            scr[pair][...] = p.astype(BF16)
            ot = _dot(vt[:HEAD_DIM], scr[pair][...]) / jnp.sum(p, axis=0, keepdims=True)
            o = jnp.concatenate([ot, ot], axis=0).T
            outs += [o[:tq, :HEAD_DIM], o[tq:, :HEAD_DIM]]
        p_refs = ()
    else:
        for pair in range(n_pairs):
            scr[2 * pair][...] = scores(pair)
        for pair in range(n_pairs):
            st_scr, p_scr = scr[2 * pair], scr[2 * pair + 1]
            p_scr[...] = jnp.exp2(st_scr[...] - jnp.max(st_scr[...], axis=0, keepdims=True)).astype(BF16)
        p_refs = scr[1::2]
    for p_scr in p_refs:
        ot = _dot(vt, p_scr[...])
        ot = ot[:HEAD_DIM] / ot[HEAD_DIM:]
        o = jnp.concatenate([ot, ot], axis=0).T
        outs += [o[:tq, :HEAD_DIM], o[tq:, :HEAD_DIM]]
    o_ref[...] = jnp.concatenate(outs, axis=1).astype(o_ref.dtype)


def _attention_call(q, k, vt, n_q, tq, bounded):
    r, qd = q.shape
    _, kvh, n_k, hd = k.shape
    grp = qd // (kvh * hd)
    tq = _row_tile(n_q, tq)
    tpb = n_q // tq
    dts = (BF16,) if bounded else (F32, BF16)
    return pl.pallas_call(
        functools.partial(_attn_kernel, grp=grp, bounded=bounded),
        out_shape=jax.ShapeDtypeStruct((r, qd), BF16),
        grid=(r // n_q, kvh, tpb),
        in_specs=[
            pl.BlockSpec((tq, grp * hd), lambda b, g, i: (b * tpb + i, g)),
            pl.BlockSpec((1, 1, n_k, hd), lambda b, g, i: (b, g, 0, 0)),
            pl.BlockSpec((1, 1, 2 * hd, n_k), lambda b, g, i: (b, g, 0, 0)),
        ],
        out_specs=pl.BlockSpec((tq, grp * hd), lambda b, g, i: (b * tpb + i, g)),
        scratch_shapes=[pltpu.VMEM((n_k, 2 * tq), dt) for _ in range(grp // 2) for dt in dts],
        compiler_params=_cparams("parallel", "parallel", "parallel"),
        name="attention_bounded" if bounded else "attention",
    )(q, k, vt)


SCORE_BOUND = 60.0


def attention(q, k, vt, n_q, score_bound, tq=256):
    return lax.cond(score_bound <= SCORE_BOUND,
                    lambda: _attention_call(q, k, vt, n_q, tq, True),
                    lambda: _attention_call(q, k, vt, n_q, tq, False))


def rope_tables(n_tok):
    t = jnp.arange(n_tok)
    row = (t // GRID_W).astype(F32)
    col = (t % GRID_W).astype(F32)
    n_freq = HEAD_DIM // 4
    inv = ROPE_THETA ** (-jnp.arange(n_freq, dtype=F32) / n_freq)
    ang = jnp.concatenate([row[:, None] * inv, col[:, None] * inv], axis=-1)
    cos, sin = jnp.cos(ang), jnp.sin(ang)
    reps = LANES // HEAD_DIM
    return jnp.tile(jnp.concatenate([cos, cos], -1), (1, reps)), jnp.tile(jnp.concatenate([-sin, sin], -1), (1, reps))


def _deinterleave_heads(w, n_heads):
    lead = w.shape[:-1]
    w = w.reshape(lead + (n_heads, HEAD_DIM // 2, 2))
    return jnp.swapaxes(w, -1, -2).reshape(lead + (n_heads * HEAD_DIM,))


def _split_kv_heads(t, b):
    return jnp.transpose(t.reshape(b, -1, N_KV_HEADS, HEAD_DIM), (0, 2, 1, 3))


def gqa_residual(x, xc, mods_l, mods_c, w_qkv, q_gain, k_gain, w_o, n_tok, n_ctx):
    b = x.shape[0] // n_tok
    kd = N_KV_HEADS * HEAD_DIM
    qd = w_qkv.shape[1] - 2 * kd
    n_qh = qd // HEAD_DIM
    w_perm = jnp.concatenate([_deinterleave_heads(w_qkv[:, :qd], n_qh),
                              _deinterleave_heads(w_qkv[:, qd:qd + kd], N_KV_HEADS), w_qkv[:, qd + kd:]], axis=1)
    w_perm = w_perm.astype(BF16)
    qg = jnp.tile(_deinterleave_heads(q_gain, 1), n_qh).reshape(1, qd)
    kg = jnp.tile(_deinterleave_heads(k_gain, 1), N_KV_HEADS).reshape(1, kd)
    sh_l, sc_l, g_l = mods_l
    sh_c, sc_c, g_c = mods_c
    q_l, k_l, v_l = qkv_project(x, sh_l, sc_l, w_perm, qg, kg, n_tok, rope_tables(n_tok))
    q_c, k_c, v_c = qkv_project(xc, sh_c, sc_c, w_perm, qg, kg, n_ctx)
    k_c4 = _split_kv_heads(k_c, b)
    k_all = jnp.concatenate([k_c4, _split_kv_heads(k_l, b)], axis=2)

    def values_t(v):
        vt = jnp.transpose(v.reshape(b, -1, N_KV_HEADS, HEAD_DIM), (0, 2, 3, 1))
        return jnp.concatenate([vt, jnp.ones_like(vt)], axis=2)

    vt_c = values_t(v_c)
    vt_all = jnp.concatenate([vt_c, values_t(v_l)], axis=3)
    score_bound = (1.01 * HEAD_DIM * HEAD_DIM ** -0.5 * 1.4426950408889634
                   * jnp.max(jnp.abs(q_gain)) * jnp.max(jnp.abs(k_gain)))
    o_l = attention(q_l, k_all, vt_all, n_tok, score_bound)
    o_c = attention(q_c, k_c4, vt_c, n_ctx, score_bound)
    w_o = w_o.astype(BF16)
    return mm_residual(o_l, w_o, x, g_l, n_tok), mm_residual(o_c, w_o, xc, g_c, n_ctx)


def _lru_kernel(xp_ref, x_ref, xn_ref, cw_ref, cb_ref, wa_ref, ba_ref, wx_ref, bx_ref, lam_ref, h0_ref,
                o_ref, hT_ref, a_scr, u_scr, carry_scr, *, tm, n_tok, tpb, reverse):
    step = pl.program_id(1)
    t_idx = (tpb - 1 - step) if reverse else step
    t0 = t_idx * tm
    rows = tm + 2 * SUBLANES

    @pl.when(step == 0)
    def _():
        carry_scr[...] = h0_ref[0]

    xe = jnp.concatenate([xp_ref[...], x_ref[...], xn_ref[...]], axis=0)
    pos = lax.broadcasted_iota(jnp.int32, (rows, 1), 0) + (t0 - SUBLANES)
    xe = jnp.where((pos >= 0) & (pos < n_tok), xe, 0.0)
    left = CONV_W // 2
    conv = cb_ref[...]
    for k in range(CONV_W):
        shift = (left - k) % rows
        tap = xe if shift == 0 else pltpu.roll(xe, shift, 0)
        conv = conv + tap * cw_ref[k:k + 1, :]
    xr = conv[SUBLANES:SUBLANES + tm]

    xb = xr.astype(BF16)
    bw = xr.shape[1] // LRU_BLOCKS
    ra, ia = [], []
    for j in range(LRU_BLOCKS):
        blk = xb[:, j * bw:(j + 1) * bw]
        ra.append(_dot(blk, wa_ref[j]))
        ia.append(_dot(blk, wx_ref[j]))
    r = _sigmoid(jnp.concatenate(ra, axis=1) + ba_ref[...])
    i = _sigmoid(jnp.concatenate(ia, axis=1) + bx_ref[...])
    log_a = -LRU_C * r * jnp.logaddexp(-lam_ref[...], 0.0)
    a = jnp.exp(log_a)
    a_scr[...] = a
    t = jnp.tanh(log_a)
    u_scr[...] = xr * i * jnp.sqrt(-2.0 * t / (1.0 - t))

    n_grp = tm // SUBLANES
    sub = lax.broadcasted_iota(jnp.int32, (SUBLANES, 1), 0)

    def group(j, carry):
        g = (n_grp - 1 - j) if reverse else j
        rws = pl.ds(pl.multiple_of(g * SUBLANES, SUBLANES), SUBLANES)
        ag, ug = a_scr[rws, :], u_scr[rws, :]
        s = 1
        while s < SUBLANES:
            if reverse:
                ok = sub < SUBLANES - s
                sh = SUBLANES - s
            else:
                ok = sub >= s
                sh = s
            u_prev = jnp.where(ok, pltpu.roll(ug, sh, 0), 0.0)
            a_prev = jnp.where(ok, pltpu.roll(ag, sh, 0), 1.0)
            ug = ug + ag * u_prev
            ag = ag * a_prev
            s *= 2
        hg = ug + ag * carry
        o_ref[rws, :] = hg
        return hg[0:1, :] if reverse else hg[SUBLANES - 1:SUBLANES, :]

    carry = lax.fori_loop(0, n_grp, group, carry_scr[...])
    carry_scr[...] = carry
    hT_ref[0] = carry


def lru_scan(xpre, conv_w, conv_b, wa, ba, wx, bx, lam, h0, n_tok, reverse, tm=256):
    r, w = xpre.shape
    b = r // n_tok
    tm = _row_tile(n_tok, tm)
    tpb = n_tok // tm
    hb = tm // SUBLANES
    last = r // SUBLANES - 1

    def tile(bi, s):
        return bi * tpb + ((tpb - 1 - s) if reverse else s)

    vec = pl.BlockSpec((1, w), lambda bi, s: (0, 0))
    blocks = pl.BlockSpec(wa.shape, lambda bi, s: (0, 0, 0))
    return pl.pallas_call(
        functools.partial(_lru_kernel, tm=tm, n_tok=n_tok, tpb=tpb, reverse=reverse),
        out_shape=[jax.ShapeDtypeStruct((r, w), F32), jax.ShapeDtypeStruct((b, 1, w), F32)],
        grid=(b, tpb),
        in_specs=[
            pl.BlockSpec((SUBLANES, w), lambda bi, s: (jnp.maximum(tile(bi, s) * hb - 1, 0), 0)),
            pl.BlockSpec((tm, w), lambda bi, s: (tile(bi, s), 0)),
            pl.BlockSpec((SUBLANES, w), lambda bi, s: (jnp.minimum((tile(bi, s) + 1) * hb, last), 0)),
            pl.BlockSpec((CONV_W, w), lambda bi, s: (0, 0)),
            vec, blocks, vec, blocks, vec, vec,
            pl.BlockSpec((1, 1, w), lambda bi, s: (bi, 0, 0)),
        ],
        out_specs=[pl.BlockSpec((tm, w), lambda bi, s: (tile(bi, s), 0)),
                   pl.BlockSpec((1, 1, w), lambda bi, s: (bi, 0, 0))],
        scratch_shapes=[pltpu.VMEM((tm, w), F32), pltpu.VMEM((tm, w), F32), pltpu.VMEM((1, w), F32)],
        compiler_params=_cparams("parallel", "arbitrary"),
        name="lru_scan",
    )(xpre, xpre, xpre, conv_w, conv_b.reshape(1, w), wa, ba.reshape(1, w), wx, bx.reshape(1, w),
      lam.reshape(1, w), h0)


def rglru_residual(x, xc, mods_l, mods_c, w_in, conv_w, conv_b, wa, ba, wx, bx, lam, w_out, n_tok, n_ctx):
    b = x.shape[0] // n_tok
    w = w_in.shape[1] // 2
    w_in = w_in.astype(BF16)
    sh_l, sc_l, g_l = mods_l
    sh_c, sc_c, g_c = mods_c
    y_l, xp_l = modmm(x, sh_l, sc_l, w_in, n_tok, (w, w), ("gelu", None), (F32, F32))
    y_c, xp_c = modmm(xc, sh_c, sc_c, w_in, n_ctx, (w, w), ("gelu", None), (F32, F32))
    zero = jnp.zeros((b, 1, w), F32)
    hs_l, hs_c = [], []
    for d in range(2):
        gate_w = (conv_w, conv_b, wa[d].astype(BF16), ba[d], wx[d].astype(BF16), bx[d], lam[d])
        hc, state = lru_scan(xp_c, *gate_w, zero, n_ctx, reverse=d == 1)
        hl, _ = lru_scan(xp_l, *gate_w, state, n_tok, reverse=d == 1)
        hs_c.append(hc)
        hs_l.append(hl)
    w_out = w_out.astype(BF16)
    return (mm_residual(tuple(hs_l), w_out, x, g_l, n_tok, b=y_l),
            mm_residual(tuple(hs_c), w_out, xc, g_c, n_ctx, b=y_c))


def _gmlp_kernel(x_ref, sh_ref, sc_ref, g_ref, win_ref, lng_ref, lnb_ref, ws_ref, bs_ref, wout_ref, o_ref, *, half):
    x = x_ref[...]
    h = _modulate(x, sh_ref[0], sc_ref[0]).astype(BF16)
    u = _gelu(_dot(h, win_ref[:, :half]))
    v = _gelu(_dot(h, win_ref[:, half:]))
    mu = jnp.mean(v, axis=-1, keepdims=True)
    vc = v - mu
    var = jnp.mean(vc * vc, axis=-1, keepdims=True)
    vn = (vc * lax.rsqrt(var + EPS) * lng_ref[...] + lnb_ref[...]).astype(BF16)
    gw = half // GMLP_GROUPS
    chunks = []
    for c in range(x.shape[0] // GMLP_CHUNK):
        rws = slice(c * GMLP_CHUNK, (c + 1) * GMLP_CHUNK)
        chunks.append(jnp.concatenate(
            [_dot(ws_ref[g], vn[rws, g * gw:(g + 1) * gw]) + bs_ref[g] for g in range(GMLP_GROUPS)], axis=1))
    v2 = jnp.concatenate(chunks, axis=0)
    o_ref[...] = x + g_ref[0] * _dot((u * v2).astype(BF16), wout_ref[...])


def gmlp_residual(x, shift, scale, gate, w_in, ln_g, ln_b, w_s, b_s, w_out, n_tok, tm=256):
    r, d = x.shape
    half = w_in.shape[1] // 2
    tm = _row_tile(n_tok, tm)
    tpb = n_tok // tm
    assert tm % GMLP_CHUNK == 0
    const2 = lambda i: (0, 0)
    const3 = lambda i: (0, 0, 0)
    once = pl.Buffered(1)
    return pl.pallas_call(
        functools.partial(_gmlp_kernel, half=half),
        out_shape=jax.ShapeDtypeStruct((r, d), F32),
        grid=(r // tm,),
        in_specs=[
            pl.BlockSpec((tm, d), lambda i: (i, 0)),
            _mod_spec(shift, tpb),
            _mod_spec(scale, tpb),
            _mod_spec(gate, tpb),
            pl.BlockSpec(w_in.shape, const2, pipeline_mode=once),
            pl.BlockSpec((1, half), const2),
            pl.BlockSpec((1, half), const2),
            pl.BlockSpec(w_s.shape, const3),
            pl.BlockSpec(b_s.shape + (1,), const3),
            pl.BlockSpec(w_out.shape, const2, pipeline_mode=once),
        ],
        out_specs=pl.BlockSpec((tm, d), lambda i: (i, 0)),
        compiler_params=_cparams("parallel"),
        name="gmlp_mixer",
    )(x, shift, scale, gate, w_in.astype(BF16), ln_g.reshape(1, half), ln_b.reshape(1, half),
      w_s.astype(BF16), b_s[..., None], w_out.astype(BF16))


def kernel(x, c, ctx, c_ctx, mod_w, mod_b, pool_w, pool_scale, lru_w_in, lru_conv_w, lru_conv_b, lru_wa, lru_ba,
           lru_wx, lru_bx, lru_lam, lru_w_out, attn_w_qkv, attn_q_gain, attn_k_gain, attn_w_o, gmlp_w_in, gmlp_ln_g,
           gmlp_ln_b, gmlp_w_s, gmlp_b_s, gmlp_w_out, moe_router, moe_w_gate, moe_w_up, moe_w_down, final_gain):
    b, n_tok, d = x.shape
    n_ctx = ctx.shape[1]
    depth = mod_w.shape[0]
    n_mixers = 4
    xl = x.reshape(b * n_tok, d)
    xc = ctx.reshape(b * n_ctx, d)

    pad = -(b + 1) % SUBLANES
    cvec = jnp.concatenate([c, c_ctx[None, :], jnp.zeros((pad, d), F32)], axis=0)
    mods = mod_vectors(cvec, mod_w, mod_b)

    for i in range(depth):
        m, j = i % n_mixers, i // n_mixers
        last = i == depth - 1
        ml = [mods[i, :b, k * d:(k + 1) * d].reshape(b, 1, d) for k in range(6)]
        mc = [mods[i, b:b + 1, k * d:(k + 1) * d].reshape(1, 1, d) for k in range(6)]
        if m == 0:
            pw = pool_w[j].astype(BF16)
            xl_new = pool_mixer_residual(xl, ml[0], ml[1], ml[2], pw, pool_scale[j], n_tok)
            if not last:
                xc = pool_mixer_residual(xc, mc[0], mc[1], mc[2], pw, pool_scale[j], n_ctx)
            xl = xl_new
        elif m == 1:
            xl, xc_new = rglru_residual(xl, xc, ml[:3], mc[:3], lru_w_in[j], lru_conv_w[j], lru_conv_b[j], lru_wa[j],
                                        lru_ba[j], lru_wx[j], lru_bx[j], lru_lam[j], lru_w_out[j], n_tok, n_ctx)
            xc = xc if last else xc_new
        elif m == 2:
            xl, xc_new = gqa_residual(xl, xc, ml[:3], mc[:3], attn_w_qkv[j], attn_q_gain[j], attn_k_gain[j],
                                      attn_w_o[j], n_tok, n_ctx)
            xc = xc if last else xc_new
        else:
            gargs = (gmlp_w_in[j], gmlp_ln_g[j], gmlp_ln_b[j], gmlp_w_s[j], gmlp_b_s[j], gmlp_w_out[j])
            xl_new = gmlp_residual(xl, ml[0], ml[1], ml[2], *gargs, n_tok)
            if not last:
                xc = gmlp_residual(xc, mc[0], mc[1], mc[2], *gargs, n_ctx)
            xl = xl_new
        wr = jnp.pad(moe_router[i], ((0, 0), (0, LANES - N_EXPERTS)))
        xl = moe_residual(xl, ml[3], ml[4], ml[5], wr, moe_w_gate, moe_w_up, moe_w_down, i, n_tok,
                          final_gain if last else None)
        if not last:
            xc = moe_residual(xc, mc[3], mc[4], mc[5], wr, moe_w_gate, moe_w_up, moe_w_down, i, n_ctx)
    return xl.reshape(b, n_tok, d)
```

```python
import functools

import jax
import jax.numpy as jnp
from jax import lax
from jax.experimental import pallas as pl
from jax.experimental.pallas import tpu as pltpu

F32 = jnp.float32
BF16 = jnp.bfloat16
EPS = 1e-6

N_EXPERTS = 16
CAPACITY_FACTOR = 2
POOL_WINDOWS = (2, 4, 8, 16)
POOL_HALO = 8
LRU_BLOCKS = 8
LRU_C = 8.0
CONV_W = 4
HEAD_DIM = 64
N_KV_HEADS = 4
GRID_W = 64
ROPE_THETA = 10000.0
GMLP_GROUPS = 4
GMLP_CHUNK = 128

LANES = 128
SUBLANES = 8
BF16_SUBLANES = 16
LOG2E = 1.4426950408889634
VMEM_LIMIT = 56 * 1024 * 1024


def _cparams(*sem):
    return pltpu.CompilerParams(dimension_semantics=sem, vmem_limit_bytes=VMEM_LIMIT)


def _modulate(x, shift, scale):
    ms = jnp.mean(x * x, axis=-1, keepdims=True)
    return x * lax.rsqrt(ms + EPS) * (1.0 + scale) + shift


def _split_bf16(a):
    hi = a.astype(BF16)
    lo = (a - hi.astype(F32)).astype(BF16)
    return hi, lo


def _dot(a, b):
    return jnp.dot(a, b, preferred_element_type=F32)


def _dot3(a, b):
    a_hi, a_lo = _split_bf16(a)
    b_hi, b_lo = _split_bf16(b)
    return _dot(a_hi, b_hi) + (_dot(a_hi, b_lo) + _dot(a_lo, b_hi))


def _gelu(x):
    return 0.5 * x * (1.0 + jnp.tanh(0.7978845608028654 * (x + 0.044715 * (x * x * x))))


def _silu(x):
    return x * (1.0 / (1.0 + jnp.exp(-x)))


def _sigmoid(x):
    return 0.5 * jnp.tanh(0.5 * x) + 0.5


def _mod_spec(mod, tiles_per_batch):
    d = mod.shape[-1]
    if mod.shape[0] == 1:
        return pl.BlockSpec((1, 1, d), lambda i, *_: (0, 0, 0))
    return pl.BlockSpec((1, 1, d), lambda i, *_: (i // tiles_per_batch, 0, 0))


def _row_tile(n, want):
    t = min(n, want)
    assert n % t == 0
    return t


def _mod_kernel(c_ref, w_ref, b_ref, o_ref):
    s = _silu(c_ref[...])
    o_ref[0] = _dot3(s, w_ref[0]) + b_ref[0]


def mod_vectors(cvec, mod_w, mod_b):
    depth, d, n6 = mod_w.shape
    tn = 1024
    return pl.pallas_call(
        _mod_kernel,
        out_shape=jax.ShapeDtypeStruct((depth, cvec.shape[0], n6), F32),
        grid=(depth, n6 // tn),
        in_specs=[
            pl.BlockSpec(cvec.shape, lambda l, j: (0, 0)),
            pl.BlockSpec((1, d, tn), lambda l, j: (l, 0, j)),
            pl.BlockSpec((1, 1, tn), lambda l, j: (l, 0, j)),
        ],
        out_specs=pl.BlockSpec((1, cvec.shape[0], tn), lambda l, j: (l, 0, j)),
        compiler_params=_cparams("parallel", "parallel"),
        name="mod_vectors",
    )(cvec, mod_w, mod_b.reshape(depth, 1, n6))


def _modmm_kernel(x_ref, sh_ref, sc_ref, w_ref, *o_refs, splits, acts):
    h = _modulate(x_ref[...], sh_ref[0], sc_ref[0]).astype(BF16)
    z = _dot(h, w_ref[...])
    off = 0
    for o_ref, width, act in zip(o_refs, splits, acts):
        part = z[:, off:off + width]
        if act == "gelu":
            part = _gelu(part)
        o_ref[...] = part.astype(o_ref.dtype)
        off += width


def modmm(x, shift, scale, w, n_tok, splits, acts, dtypes, tm=512):
    r, d = x.shape
    tm = _row_tile(n_tok, tm)
    tpb = n_tok // tm
    n = w.shape[1]
    assert sum(splits) == n
    return pl.pallas_call(
        functools.partial(_modmm_kernel, splits=splits, acts=acts),
        out_shape=[jax.ShapeDtypeStruct((r, s), dt) for s, dt in zip(splits, dtypes)],
        grid=(r // tm,),
        in_specs=[
            pl.BlockSpec((tm, d), lambda i: (i, 0)),
            _mod_spec(shift, tpb),
            _mod_spec(scale, tpb),
            pl.BlockSpec((d, n), lambda i: (0, 0)),
        ],
        out_specs=[pl.BlockSpec((tm, s), lambda i: (i, 0)) for s in splits],
        compiler_params=_cparams("parallel"),
        name="modmm",
    )(x, shift, scale, w)


def _mmres_kernel(*refs, n_sum, has_mul):
    a = refs[0][...]
    for a_ref in refs[1:n_sum]:
        a = a + a_ref[...]
    refs = refs[n_sum:]
    if has_mul:
        a = a.astype(F32) * refs[0][...].astype(F32)
        refs = refs[1:]
    w_ref, x_ref, g_ref, o_ref = refs
    o_ref[...] = x_ref[...] + g_ref[0] * _dot(a.astype(BF16), w_ref[...])


def mm_residual(a, w, x, gate, n_tok, b=None, tm=512):
    a = a if isinstance(a, (tuple, list)) else (a,)
    r, k = a[0].shape
    n = w.shape[1]
    tm = _row_tile(n_tok, tm)
    tpb = n_tok // tm
    ins = list(a) + ([b] if b is not None else []) + [w, x, gate]
    specs = [pl.BlockSpec((tm, k), lambda i: (i, 0))] * (len(a) + (b is not None))
    specs += [
        pl.BlockSpec((k, n), lambda i: (0, 0)),
        pl.BlockSpec((tm, n), lambda i: (i, 0)),
        _mod_spec(gate, tpb),
    ]
    return pl.pallas_call(
        functools.partial(_mmres_kernel, n_sum=len(a), has_mul=b is not None),
        out_shape=jax.ShapeDtypeStruct((r, n), F32),
        grid=(r // tm,),
        in_specs=specs,
        out_specs=pl.BlockSpec((tm, n), lambda i: (i, 0)),
        compiler_params=_cparams("parallel"),
        name="mm_residual",
    )(*ins)


def _pool_kernel(xp_ref, x_ref, xn_ref, sh_ref, sc_ref, g_ref, w_ref, ps_ref, o_ref, *, tm, n_tok, tpb):
    t0 = (pl.program_id(0) % tpb) * tm
    x = x_ref[...]
    rows = tm + 2 * POOL_HALO
    h = jnp.concatenate([
        jnp.where(t0 > 0, _modulate(xp_ref[...], sh_ref[0], sc_ref[0]), 0.0),
        _modulate(x, sh_ref[0], sc_ref[0]),
        jnp.where(t0 + tm < n_tok, _modulate(xn_ref[...], sh_ref[0], sc_ref[0]), 0.0)], axis=0)
    posc = lax.broadcasted_iota(jnp.int32, (tm, 1), 0) + t0
    gw = h.shape[1] // len(POOL_WINDOWS)
    outs = []
    for g, win in enumerate(POOL_WINDOWS):
        hg = h[:, g * gw:(g + 1) * gw]
        c = hg + pltpu.roll(hg, 1, 0)
        step = 1
        while 2 * step < win:
            c = pltpu.roll(c, step, 0) + pltpu.roll(c, rows - step, 0)
            step *= 2
        cnt = jnp.minimum(posc + (win - win // 2), n_tok) - jnp.maximum(posc - win // 2, 0)
        pooled = c[POOL_HALO:POOL_HALO + tm] / cnt.astype(F32) - hg[POOL_HALO:POOL_HALO + tm]
        outs.append(_dot(pooled.astype(BF16), w_ref[g]))
    y = jnp.concatenate(outs, axis=1) * ps_ref[...]
    o_ref[...] = x + g_ref[0] * y


def pool_mixer_residual(x, shift, scale, gate, w_pool, pool_scale, n_tok, tm=512):
    r, d = x.shape
    tm = _row_tile(n_tok, tm)
    tpb = n_tok // tm
    hb = tm // POOL_HALO
    last = r // POOL_HALO - 1
    groups, gw, _ = w_pool.shape
    return pl.pallas_call(
        functools.partial(_pool_kernel, tm=tm, n_tok=n_tok, tpb=tpb),
        out_shape=jax.ShapeDtypeStruct((r, d), F32),
        grid=(r // tm,),
        in_specs=[
            pl.BlockSpec((POOL_HALO, d), lambda i: (jnp.maximum(i * hb - 1, 0), 0)),
            pl.BlockSpec((tm, d), lambda i: (i, 0)),
            pl.BlockSpec((POOL_HALO, d), lambda i: (jnp.minimum((i + 1) * hb, last), 0)),
            _mod_spec(shift, tpb),
            _mod_spec(scale, tpb),
            _mod_spec(gate, tpb),
            pl.BlockSpec((groups, gw, gw), lambda i: (0, 0, 0)),
            pl.BlockSpec((1, d), lambda i: (0, 0)),
        ],
        out_specs=pl.BlockSpec((tm, d), lambda i: (i, 0)),
        compiler_params=_cparams("parallel"),
        name="pool_mixer",
    )(x, x, x, shift, scale, gate, w_pool, pool_scale.reshape(1, d))


def _router_kernel(x_ref, sh_ref, sc_ref, wr_ref, h_ref, aff_ref):
    h = _modulate(x_ref[...], sh_ref[0], sc_ref[0])
    h_ref[...] = h.astype(h_ref.dtype)
    logits = _dot3(h, wr_ref[...])
    lane = lax.broadcasted_iota(jnp.int32, logits.shape, 1)
    logits = jnp.where(lane < N_EXPERTS, logits, -jnp.inf)
    e = jnp.exp(logits - jnp.max(logits, axis=-1, keepdims=True))
    aff_ref[...] = e / jnp.sum(e, axis=-1, keepdims=True)


def router(x, shift, scale, w_router_padded, n_tok, tm=512):
    r, d = x.shape
    tm = _row_tile(n_tok, tm)
    tpb = n_tok // tm
    return pl.pallas_call(
        _router_kernel,
        out_shape=[jax.ShapeDtypeStruct((r, d), BF16), jax.ShapeDtypeStruct((r, LANES), F32)],
        grid=(r // tm,),
        in_specs=[
            pl.BlockSpec((tm, d), lambda i: (i, 0)),
            _mod_spec(shift, tpb),
            _mod_spec(scale, tpb),
            pl.BlockSpec((d, LANES), lambda i: (0, 0)),
        ],
        out_specs=[pl.BlockSpec((tm, d), lambda i: (i, 0)), pl.BlockSpec((tm, LANES), lambda i: (i, 0))],
        compiler_params=_cparams("parallel"),
        name="moe_router",
    )(x, shift, scale, w_router_padded)


MOE_CHUNK = 128
EXPERT_GROUPS = 2
COMBINE_CHUNKS = 2
NARROW_WINDOW = 64
PACK = LANES // N_EXPERTS


def _cumsum_rows(v, tb):
    n = v.shape[0]
    tri = (lax.broadcasted_iota(jnp.int32, (tb, tb), 0) >= lax.broadcasted_iota(jnp.int32, (tb, tb), 1)).astype(BF16)
    carry = jnp.zeros((1, v.shape[1]), F32)
    outs, starts = [], []
    for j in range(n // tb):
        starts.append(carry)
        c = _dot(tri, v[j * tb:(j + 1) * tb].astype(BF16)) + carry
        carry = c[tb - 1:tb, :]
        outs.append(c)
    return jnp.concatenate(outs, axis=0), jnp.concatenate(starts, axis=0)


def _route_kernel(aff_ref, affp_ref, idx_ref, gate_ref, slot_ref, start_ref, slot_scr, acc_scr, *, n, cap):
    def enough(cand):
        cnt = jnp.sum((pltpu.bitcast(affp_ref[...], jnp.int32) >= cand).astype(F32), axis=0, keepdims=True)
        shift = N_EXPERTS
        while shift < LANES:
            cnt = cnt + pltpu.roll(cnt, shift, 1)
            shift *= 2
        return cnt >= cap

    def two_bits(k, prefix):
        low = 28 - 2 * k
        c1, c2, c3 = (prefix | jnp.left_shift(jnp.int32(j), low) for j in (1, 2, 3))
        return jnp.where(enough(c3), c3, jnp.where(enough(c2), c2, jnp.where(enough(c1), c1, prefix)))

    top = jnp.full((1, LANES), 1 << 30, jnp.int32)
    thr = jnp.where(enough(top), top, jnp.zeros_like(top))
    thr = lax.fori_loop(0, 15, two_bits, thr)
    bits = pltpu.bitcast(aff_ref[...], jnp.int32)
    gt = bits > thr
    eq = bits == thr
    need = cap - jnp.sum(gt.astype(F32), axis=0, keepdims=True)
    eq_rank, _ = _cumsum_rows(eq.astype(F32), MOE_CHUNK)
    sel = gt | (eq & (eq_rank <= need))
    pos, starts = _cumsum_rows(sel.astype(F32), MOE_CHUNK)
    slot_scr[...] = jnp.where(sel, pos - 1.0, -1.0)
    slot_ref[...] = slot_scr[...]
    start_ref[0] = starts.astype(jnp.int32)

    slots = lax.broadcasted_iota(jnp.int32, (1, cap), 1).astype(F32)
    sub = lax.broadcasted_iota(jnp.int32, (SUBLANES, MOE_CHUNK), 0)
    local = lax.broadcasted_iota(jnp.int32, (SUBLANES, MOE_CHUNK), 1).astype(F32)
    acc_scr[...] = jnp.zeros(acc_scr.shape, F32)

    def chunk(c, carry):
        rows = pl.ds(pl.multiple_of(c * MOE_CHUNK, MOE_CHUNK), MOE_CHUNK)
        base = jnp.where(sub == 0, local, jnp.where(sub == 1, lax.convert_element_type(c, F32), 0.0))
        aff_t = aff_ref[rows, :].T
        for e in range(N_EXPERTS):
            onehot = jnp.where(slot_scr[rows, e:e + 1] == slots, 1.0, 0.0).astype(BF16)
            g = aff_t[e:e + 1, :]
            g_hi = g.astype(BF16).astype(F32)
            g_mid = (g - g_hi).astype(BF16).astype(F32)
            g_lo = g - g_hi - g_mid
            lhs = jnp.where(sub == 2, g_hi, jnp.where(sub == 3, g_mid, jnp.where(sub == 4, g_lo, base)))
            acc_scr[e] += _dot(lhs.astype(BF16), onehot)
        return carry

    lax.fori_loop(0, n // MOE_CHUNK, chunk, 0)
    for e in range(N_EXPERTS):
        idx_ref[0, e:e + 1, :] = (acc_scr[e, 0:1, :] + MOE_CHUNK * acc_scr[e, 1:2, :]).astype(jnp.int32)
        gate_ref[0, e:e + 1, :] = acc_scr[e, 2:3, :] + acc_scr[e, 3:4, :] + acc_scr[e, 4:5, :]


def route(aff, n_tok):
    r = aff.shape[0]
    b = r // n_tok
    cap = CAPACITY_FACTOR * n_tok // N_EXPERTS
    aff_packed = aff[:, :N_EXPERTS].reshape(r // PACK, LANES)
    n_chunks = n_tok // MOE_CHUNK
    return pl.pallas_call(
        functools.partial(_route_kernel, n=n_tok, cap=cap),
        out_shape=[jax.ShapeDtypeStruct((b, N_EXPERTS, cap), jnp.int32), jax.ShapeDtypeStruct((b, N_EXPERTS, cap), F32),
                   jax.ShapeDtypeStruct((r, LANES), F32), jax.ShapeDtypeStruct((b, n_chunks, LANES), jnp.int32)],
        grid=(b,),
        in_specs=[pl.BlockSpec((n_tok, LANES), lambda i: (i, 0)),
                  pl.BlockSpec((n_tok // PACK, LANES), lambda i: (i, 0))],
        out_specs=[pl.BlockSpec((1, N_EXPERTS, cap), lambda i: (i, 0, 0)),
                   pl.BlockSpec((1, N_EXPERTS, cap), lambda i: (i, 0, 0)),
                   pl.BlockSpec((n_tok, LANES), lambda i: (i, 0)),
                   pl.BlockSpec((1, n_chunks, LANES), lambda i: (i, 0, 0))],
        scratch_shapes=[pltpu.VMEM((n_tok, LANES), F32), pltpu.VMEM((N_EXPERTS, SUBLANES, cap), F32)],
        compiler_params=_cparams("parallel"),
        name="moe_route",
    )(aff, aff_packed)


def _ffn_kernel(doff_ref, x_ref, gt_ref, wg_ref, wu_ref, wd_ref, y_ref, dup_ref, wg_scr, wu_scr, wd_scr, y_scr, *,
                e0, cap, n_steps, dwin):
    @pl.when(pl.program_id(1) == 0)
    def _():
        wg_scr[...] = wg_ref[0, 0].astype(BF16)
        wu_scr[...] = wu_ref[0, 0].astype(BF16)
        wd_scr[...] = wd_ref[0, 0].astype(BF16)

    x = x_ref[0]
    a = _dot(x, wg_scr[...])
    u = _dot(x, wu_scr[...])
    hmid = (_silu(a) * u).astype(BF16)
    gate_cols = jnp.broadcast_to(gt_ref[0, 0], (LANES, x.shape[0])).T
    y = _dot(hmid, wd_scr[...])
    y_scr[...] = (y * jnp.concatenate([gate_cols] * (y.shape[1] // LANES), axis=1)).astype(BF16)
    y_ref[0] = y_scr[...]
    e = e0 + pl.program_id(0)
    align = min(dwin, BF16_SUBLANES)
    for bl in range(y_scr.shape[0] // cap):
        b = pl.program_id(1) * (y_scr.shape[0] // cap) + bl
        for s in range(n_steps):
            off = pl.multiple_of(doff_ref[(b * n_steps + s) * N_EXPERTS + e], align)
            dup_ref[bl, s, 0] = y_scr[pl.ds(bl * cap + off, dwin), :]


def expert_ffn(xs, gates, doffs, w_gate, w_up, w_down, layer, e0, cap, n_steps, dwin, tm=512):
    e, m, d = xs.shape
    ff = w_gate.shape[-1]
    tm = max(_row_tile(m, tm), cap)
    bpt = tm // cap
    return pl.pallas_call(
        functools.partial(_ffn_kernel, e0=e0, cap=cap, n_steps=n_steps, dwin=dwin),
        out_shape=[jax.ShapeDtypeStruct((e, m, d), BF16),
                   jax.ShapeDtypeStruct((m // cap, n_steps, e, dwin, d), BF16)],
        grid_spec=pltpu.PrefetchScalarGridSpec(
            num_scalar_prefetch=1,
            grid=(e, m // tm),
            in_specs=[
                pl.BlockSpec((1, tm, d), lambda k, i, off: (k, i, 0)),
                pl.BlockSpec((1, 1, 1, tm), lambda k, i, off: (k, i, 0, 0)),
                pl.BlockSpec((1, 1, d, ff), lambda k, i, off: (layer, e0 + k, 0, 0)),
                pl.BlockSpec((1, 1, d, ff), lambda k, i, off: (layer, e0 + k, 0, 0)),
                pl.BlockSpec((1, 1, ff, d), lambda k, i, off: (layer, e0 + k, 0, 0)),
            ],
            out_specs=[pl.BlockSpec((1, tm, d), lambda k, i, off: (k, i, 0)),
                       pl.BlockSpec((bpt, n_steps, 1, dwin, d), lambda k, i, off: (i, 0, k, 0, 0))],
            scratch_shapes=[pltpu.VMEM((d, ff), BF16), pltpu.VMEM((d, ff), BF16), pltpu.VMEM((ff, d), BF16),
                            pltpu.VMEM((tm, d), BF16)],
        ),
        compiler_params=_cparams("parallel", "arbitrary"),
        name="moe_expert_ffn",
    )(doffs, xs, gates.reshape(e, m // tm, 1, tm), w_gate, w_up, w_down)


def _combine_kernel(woff_ref, doff_ref, fit_ref, ysel_ref, x_ref, g_ref, slot_ref, *rest, win, dwin, n_chunks, cps,
                    n_groups, final):
    dup_refs, y_refs, rest = rest[:n_groups], rest[n_groups:2 * n_groups], rest[2 * n_groups:]
    if final:
        fg_ref, o_ref = rest
    else:
        (o_ref,) = rest
    per_group = N_EXPERTS // n_groups
    step = pl.program_id(0) * (n_chunks // cps) + pl.program_id(1)

    def finish(rows, acc):
        out = x_ref[rows, :] + g_ref[0] * acc
        if final:
            out = out * lax.rsqrt(jnp.mean(out * out, axis=-1, keepdims=True) + EPS) * fg_ref[...]
        o_ref[rows, :] = out

    @pl.when(fit_ref[step] == 1)
    def _():
        tm = x_ref.shape[0]
        epl = LANES // dwin
        lane = lax.broadcasted_iota(jnp.int32, (1, LANES), 1)
        which = lane // dwin
        within = (lane % dwin).astype(F32)
        acc = jnp.zeros((tm, x_ref.shape[1]), F32)
        for gi, dup_ref in enumerate(dup_refs):
            pieces = []
            for p in range(per_group // epl):
                col, sl = None, None
                for w in range(epl):
                    e = gi * per_group + p * epl + w
                    cw = lax.convert_element_type(doff_ref[step * N_EXPERTS + e], F32) + within
                    sw = jnp.broadcast_to(slot_ref[:, e:e + 1], (tm, LANES))
                    col = cw if w == 0 else jnp.where(which == w, cw, col)
                    sl = sw if w == 0 else jnp.where(which == w, sw, sl)
                pieces.append(jnp.where(sl == col, 1.0, 0.0).astype(BF16))
            onehot = jnp.concatenate(pieces, axis=1)
            acc = acc + _dot(onehot, dup_ref[0, 0].reshape(per_group * dwin, x_ref.shape[1]))
        finish(slice(0, tm), acc)

    @pl.when(fit_ref[step] == 0)
    def _():
        align = min(win, LANES)
        for sc in range(cps):
            rows = slice(sc * MOE_CHUNK, (sc + 1) * MOE_CHUNK)
            base = (step * cps + sc) * N_EXPERTS
            acc = jnp.zeros((MOE_CHUNK, x_ref.shape[1]), F32)
            for e in range(N_EXPERTS):
                off = pl.multiple_of(woff_ref[base + e], align)
                cols = (lax.broadcasted_iota(jnp.int32, (1, win), 1) + off).astype(F32)
                onehot = jnp.where(slot_ref[rows, e:e + 1] == cols, 1.0, 0.0).astype(BF16)
                acc = acc + _dot(onehot, y_refs[e // per_group][e % per_group, pl.ds(off, win), :])
            finish(rows, acc)


def moe_combine(x, gate2, slot, ys, dups, woffs, doffs, fits, n_tok, final_gain=None):
    r, d = x.shape
    b = r // n_tok
    per_group = ys[0].shape[0]
    cap = ys[0].shape[1] // b
    dwin = dups[0].shape[3]
    n_chunks = n_tok // MOE_CHUNK
    cps = COMBINE_CHUNKS if n_chunks % COMBINE_CHUNKS == 0 else 1
    tm = cps * MOE_CHUNK
    steps = n_chunks // cps
    win = min(2 * MOE_CHUNK, cap)
    final = final_gain is not None
    need = jnp.max((1 - fits).reshape(b, steps), axis=1)
    ysel = lax.cummax(need * jnp.arange(b, dtype=jnp.int32))
    row_spec = lambda w: pl.BlockSpec((tm, w), lambda bi, c, *_: (bi * steps + c, 0))
    ins = [woffs, doffs, fits, ysel, x, gate2, slot] + list(dups) + list(ys)
    specs = [row_spec(d), _mod_spec(gate2, 1), row_spec(LANES)]
    specs += [pl.BlockSpec((1, 1, per_group, dwin, d), lambda bi, c, *_: (bi, c, 0, 0, 0))] * len(dups)
    specs += [pl.BlockSpec((per_group, cap, d), lambda bi, c, wo, do, ft, ys_: (0, ys_[bi], 0),
                           pipeline_mode=pl.Buffered(1))] * len(ys)
    if final:
        ins.append(final_gain.reshape(1, d))
        specs.append(pl.BlockSpec((1, d), lambda bi, c, *_: (0, 0)))
    return pl.pallas_call(
        functools.partial(_combine_kernel, win=win, dwin=dwin, n_chunks=n_chunks, cps=cps, n_groups=len(ys),
                          final=final),
        out_shape=jax.ShapeDtypeStruct((r, d), F32),
        grid_spec=pltpu.PrefetchScalarGridSpec(
            num_scalar_prefetch=4,
            grid=(b, steps),
            in_specs=specs,
            out_specs=row_spec(d),
        ),
        compiler_params=_cparams("parallel", "arbitrary"),
        name="moe_combine",
    )(*ins)


def moe_residual(x, shift, scale, gate2, w_router_padded, w_gate, w_up, w_down, layer, n_tok, final_gain=None):
    r, d = x.shape
    b = r // n_tok
    h, aff = router(x, shift, scale, w_router_padded, n_tok)
    idx, gates, slot, start = route(aff, n_tok)
    cap = idx.shape[-1]
    n_chunks = n_tok // MOE_CHUNK
    cps = COMBINE_CHUNKS if n_chunks % COMBINE_CHUNKS == 0 else 1
    steps = n_chunks // cps
    start = start[:, :, :N_EXPERTS]
    win = min(2 * MOE_CHUNK, cap)
    align = min(win, LANES)
    woffs = jnp.clip(start // align * align, 0, cap - win).reshape(-1)
    dwin = min(NARROW_WINDOW, cap)
    dalign = min(dwin, BF16_SUBLANES)
    s_start = start[:, ::cps]
    s_end = jnp.concatenate([s_start[:, 1:], jnp.full((b, 1, N_EXPERTS), cap, jnp.int32)], axis=1)
    doffs = jnp.clip(s_start // dalign * dalign, 0, cap - dwin)
    fits = jnp.all(s_end - doffs <= dwin, axis=-1).astype(jnp.int32).reshape(-1)
    doffs = doffs.reshape(-1)

    gidx = idx + (jnp.arange(b, dtype=jnp.int32) * n_tok)[:, None, None]
    gidx = jnp.transpose(gidx, (1, 0, 2)).reshape(N_EXPERTS, b * cap)
    gts = jnp.transpose(gates, (1, 0, 2)).reshape(N_EXPERTS, b * cap)
    per_group = N_EXPERTS // EXPERT_GROUPS
    ys, dups = [], []
    for g in range(EXPERT_GROUPS):
        grp = slice(g * per_group, (g + 1) * per_group)
        xs = h.at[gidx[grp]].get(mode="promise_in_bounds")
        y, dup = expert_ffn(xs, gts[grp], doffs, w_gate, w_up, w_down, layer, g * per_group, cap, steps, dwin)
        ys.append(y)
        dups.append(dup)
    return moe_combine(x, gate2, slot, ys, dups, woffs, doffs, fits, n_tok, final_gain)


def _head_rms(t, seg_ones):
    outs = []
    for j in range(t.shape[1] // LANES):
        blk = t[:, j * LANES:(j + 1) * LANES]
        hi, lo = _split_bf16(blk * blk)
        ss = _dot(hi, seg_ones) + _dot(lo, seg_ones)
        outs.append(blk * lax.rsqrt(ss * (1.0 / HEAD_DIM) + EPS))
    return jnp.concatenate(outs, axis=1)


def _rope(t, cos, sin_signed):
    w = t.shape[1]
    half = HEAD_DIM // 2
    lane = lax.broadcasted_iota(jnp.int32, t.shape, 1)
    partner = jnp.where(lane % HEAD_DIM < half, pltpu.roll(t, w - half, 1), pltpu.roll(t, half, 1))
    reps = w // cos.shape[1]
    return t * jnp.concatenate([cos] * reps, axis=1) + partner * jnp.concatenate([sin_signed] * reps, axis=1)


def _qkv_kernel(*refs, qd, kd, rope):
    if rope:
        x_ref, sh_ref, sc_ref, w_ref, qg_ref, kg_ref, cos_ref, sin_ref, q_ref, k_ref, v_ref = refs
    else:
        x_ref, sh_ref, sc_ref, w_ref, qg_ref, kg_ref, q_ref, k_ref, v_ref = refs
    h = _modulate(x_ref[...], sh_ref[0], sc_ref[0]).astype(BF16)
    z = _dot(h, w_ref[...])
    seg = (lax.broadcasted_iota(jnp.int32, (LANES, LANES), 0) // HEAD_DIM
           == lax.broadcasted_iota(jnp.int32, (LANES, LANES), 1) // HEAD_DIM).astype(BF16)
    q = _head_rms(z[:, :qd], seg) * qg_ref[...]
    k = _head_rms(z[:, qd:qd + kd], seg) * kg_ref[...]
    if rope:
        q = _rope(q, cos_ref[...], sin_ref[...])
        k = _rope(k, cos_ref[...], sin_ref[...])
    q_ref[...] = (q * (HEAD_DIM ** -0.5 * LOG2E)).astype(q_ref.dtype)
    k_ref[...] = k.astype(k_ref.dtype)
    v_ref[...] = z[:, qd + kd:].astype(v_ref.dtype)


def qkv_project(x, shift, scale, w_qkv, q_gain, k_gain, n_tok, rope_tables=None, tm=512):
    r, d = x.shape
    kd = N_KV_HEADS * HEAD_DIM
    qd = w_qkv.shape[1] - 2 * kd
    tm = _row_tile(n_tok, tm)
    tpb = n_tok // tm
    ins = [x, shift, scale, w_qkv, q_gain, k_gain]
    specs = [
        pl.BlockSpec((tm, d), lambda i: (i, 0)),
        _mod_spec(shift, tpb),
        _mod_spec(scale, tpb),
        pl.BlockSpec(w_qkv.shape, lambda i: (0, 0)),
        pl.BlockSpec((1, qd), lambda i: (0, 0)),
        pl.BlockSpec((1, kd), lambda i: (0, 0)),
    ]
    if rope_tables is not None:
        ins += list(rope_tables)
        specs += [pl.BlockSpec((tm, LANES), lambda i: (i % tpb, 0))] * 2
    return pl.pallas_call(
        functools.partial(_qkv_kernel, qd=qd, kd=kd, rope=rope_tables is not None),
        out_shape=[jax.ShapeDtypeStruct((r, qd), BF16), jax.ShapeDtypeStruct((r, kd), BF16),
                   jax.ShapeDtypeStruct((r, kd), BF16)],
        grid=(r // tm,),
        in_specs=specs,
        out_specs=[pl.BlockSpec((tm, qd), lambda i: (i, 0)), pl.BlockSpec((tm, kd), lambda i: (i, 0)),
                   pl.BlockSpec((tm, kd), lambda i: (i, 0))],
        compiler_params=_cparams("parallel"),
        name="qkv_project",
    )(*ins)


def _attn_kernel(q_ref, k_ref, vt_ref, o_ref, *scr, grp, bounded):
    k = k_ref[0, 0]
    vt = vt_ref[0, 0]
    tq = q_ref.shape[0]
    n_pairs = grp // 2
    outs = []

    def scores(pair):
        heads = (2 * pair, 2 * pair + 1)
        q2 = jnp.concatenate([q_ref[:, h * HEAD_DIM:(h + 1) * HEAD_DIM] for h in heads], axis=0)
        return lax.dot_general(k, q2, (((1,), (1,)), ((), ())), preferred_element_type=F32)

    if bounded:
        for pair in range(n_pairs):
            p = jnp.exp2(scores(pair))
            scr[pair][...] = p.astype(BF16)
            ot = _dot(vt[:HEAD_DIM], scr[pair][...]) / jnp.sum(p, axis=0, keepdims=True)
            o = jnp.concatenate([ot, ot], axis=0).T
            outs += [o[:tq, :HEAD_DIM], o[tq:, :HEAD_DIM]]
        p_refs = ()
    else:
        for pair in range(n_pairs):
            scr[2 * pair][...] = scores(pair)
        for pair in range(n_pairs):
            st_scr, p_scr = scr[2 * pair], scr[2 * pair + 1]
            p_scr[...] = jnp.exp2(st_scr[...] - jnp.max(st_scr[...], axis=0, keepdims=True)).astype(BF16)
        p_refs = scr[1::2]
    for p_scr in p_refs:
        ot = _dot(vt, p_scr[...])
        ot = ot[:HEAD_DIM] / ot[HEAD_DIM:]
        o = jnp.concatenate([ot, ot], axis=0).T
        outs += [o[:tq, :HEAD_DIM], o[tq:, :HEAD_DIM]]
    o_ref[...] = jnp.concatenate(outs, axis=1).astype(o_ref.dtype)


def _attention_call(q, k, vt, n_q, tq, bounded):
    r, qd = q.shape
    _, kvh, n_k, hd = k.shape
    grp = qd // (kvh * hd)
    tq = _row_tile(n_q, tq)
    tpb = n_q // tq
    dts = (BF16,) if bounded else (F32, BF16)
    return pl.pallas_call(
        functools.partial(_attn_kernel, grp=grp, bounded=bounded),
        out_shape=jax.ShapeDtypeStruct((r, qd), BF16),
        grid=(r // n_q, kvh, tpb),
        in_specs=[
            pl.BlockSpec((tq, grp * hd), lambda b, g, i: (b * tpb + i, g)),
            pl.BlockSpec((1, 1, n_k, hd), lambda b, g, i: (b, g, 0, 0)),
            pl.BlockSpec((1, 1, 2 * hd, n_k), lambda b, g, i: (b, g, 0, 0)),
        ],
        out_specs=pl.BlockSpec((tq, grp * hd), lambda b, g, i: (b * tpb + i, g)),
        scratch_shapes=[pltpu.VMEM((n_k, 2 * tq), dt) for _ in range(grp // 2) for dt in dts],
        compiler_params=_cparams("parallel", "parallel", "parallel"),
        name="attention_bounded" if bounded else "attention",
    )(q, k, vt)


SCORE_BOUND = 60.0


def attention(q, k, vt, n_q, score_bound, tq_bounded=512, tq_exact=256):
    return lax.cond(score_bound <= SCORE_BOUND,
                    lambda: _attention_call(q, k, vt, n_q, tq_bounded, True),
                    lambda: _attention_call(q, k, vt, n_q, tq_exact, False))


def rope_tables(n_tok):
    t = jnp.arange(n_tok)
    row = (t // GRID_W).astype(F32)
    col = (t % GRID_W).astype(F32)
    n_freq = HEAD_DIM // 4
    inv = ROPE_THETA ** (-jnp.arange(n_freq, dtype=F32) / n_freq)
    ang = jnp.concatenate([row[:, None] * inv, col[:, None] * inv], axis=-1)
    cos, sin = jnp.cos(ang), jnp.sin(ang)
    reps = LANES // HEAD_DIM
    return jnp.tile(jnp.concatenate([cos, cos], -1), (1, reps)), jnp.tile(jnp.concatenate([-sin, sin], -1), (1, reps))


def _deinterleave_heads(w, n_heads):
    lead = w.shape[:-1]
    w = w.reshape(lead + (n_heads, HEAD_DIM // 2, 2))
    return jnp.swapaxes(w, -1, -2).reshape(lead + (n_heads * HEAD_DIM,))


def _split_kv_heads(t, b):
    return jnp.transpose(t.reshape(b, -1, N_KV_HEADS, HEAD_DIM), (0, 2, 1, 3))


def gqa_residual(x, xc, mods_l, mods_c, w_qkv, q_gain, k_gain, w_o, n_tok, n_ctx):
    b = x.shape[0] // n_tok
    kd = N_KV_HEADS * HEAD_DIM
    qd = w_qkv.shape[1] - 2 * kd
    n_qh = qd // HEAD_DIM
    w_perm = jnp.concatenate([_deinterleave_heads(w_qkv[:, :qd], n_qh),
                              _deinterleave_heads(w_qkv[:, qd:qd + kd], N_KV_HEADS), w_qkv[:, qd + kd:]], axis=1)
    w_perm = w_perm.astype(BF16)
    qg = jnp.tile(_deinterleave_heads(q_gain, 1), n_qh).reshape(1, qd)
    kg = jnp.tile(_deinterleave_heads(k_gain, 1), N_KV_HEADS).reshape(1, kd)
    sh_l, sc_l, g_l = mods_l
    sh_c, sc_c, g_c = mods_c
    q_l, k_l, v_l = qkv_project(x, sh_l, sc_l, w_perm, qg, kg, n_tok, rope_tables(n_tok))
    q_c, k_c, v_c = qkv_project(xc, sh_c, sc_c, w_perm, qg, kg, n_ctx)
    k_c4 = _split_kv_heads(k_c, b)
    k_all = jnp.concatenate([k_c4, _split_kv_heads(k_l, b)], axis=2)

    def values_t(v):
        vt = jnp.transpose(v.reshape(b, -1, N_KV_HEADS, HEAD_DIM), (0, 2, 3, 1))
        return jnp.concatenate([vt, jnp.ones_like(vt)], axis=2)

    vt_c = values_t(v_c)
    vt_all = jnp.concatenate([vt_c, values_t(v_l)], axis=3)
    score_bound = (1.01 * HEAD_DIM * HEAD_DIM ** -0.5 * LOG2E
                   * jnp.max(jnp.abs(q_gain)) * jnp.max(jnp.abs(k_gain)))
    o_l = attention(q_l, k_all, vt_all, n_tok, score_bound)
    o_c = attention(q_c, k_c4, vt_c, n_ctx, score_bound)
    w_o = w_o.astype(BF16)
    return mm_residual(o_l, w_o, x, g_l, n_tok), mm_residual(o_c, w_o, xc, g_c, n_ctx)


def _lru_kernel(xp_ref, x_ref, xn_ref, cw_ref, cb_ref, wa_ref, ba_ref, wx_ref, bx_ref, lam_ref, h0_ref,
                o_ref, hT_ref, a_scr, u_scr, carry_scr, *, tm, n_tok, tpb, reverse):
    step = pl.program_id(1)
    t_idx = (tpb - 1 - step) if reverse else step
    t0 = t_idx * tm
    rows = tm + 2 * SUBLANES

    @pl.when(step == 0)
    def _():
        carry_scr[...] = h0_ref[0]

    xe = jnp.concatenate([jnp.where(t0 > 0, xp_ref[...], 0.0), x_ref[...],
                          jnp.where(t0 + tm < n_tok, xn_ref[...], 0.0)], axis=0)
    left = CONV_W // 2
    conv = cb_ref[...]
    for k in range(CONV_W):
        shift = (left - k) % rows
        tap = xe if shift == 0 else pltpu.roll(xe, shift, 0)
        conv = conv + tap * cw_ref[k:k + 1, :]
    xr = conv[SUBLANES:SUBLANES + tm]

    xb = xr.astype(BF16)
    bw = xr.shape[1] // LRU_BLOCKS
    ra, ia = [], []
    for j in range(LRU_BLOCKS):
        blk = xb[:, j * bw:(j + 1) * bw]
        ra.append(_dot(blk, wa_ref[j]))
        ia.append(_dot(blk, wx_ref[j]))
    r = _sigmoid(jnp.concatenate(ra, axis=1) + ba_ref[...])
    i = _sigmoid(jnp.concatenate(ia, axis=1) + bx_ref[...])
    log_a = -LRU_C * r * jnp.logaddexp(-lam_ref[...], 0.0)
    a = jnp.exp(log_a)
    a_scr[...] = a
    t = jnp.tanh(log_a)
    u_scr[...] = xr * i * jnp.sqrt(-2.0 * t / (1.0 - t))

    n_grp = tm // SUBLANES
    sub = lax.broadcasted_iota(jnp.int32, (SUBLANES, 1), 0)

    def group(j, carry):
        g = (n_grp - 1 - j) if reverse else j
        rws = pl.ds(pl.multiple_of(g * SUBLANES, SUBLANES), SUBLANES)
        ag, ug = a_scr[rws, :], u_scr[rws, :]
        s = 1
        while s < SUBLANES:
            if reverse:
                ok = sub < SUBLANES - s
                sh = SUBLANES - s
            else:
                ok = sub >= s
                sh = s
            u_prev = jnp.where(ok, pltpu.roll(ug, sh, 0), 0.0)
            a_prev = jnp.where(ok, pltpu.roll(ag, sh, 0), 1.0)
            ug = ug + ag * u_prev
            ag = ag * a_prev
            s *= 2
        hg = ug + ag * carry
        o_ref[rws, :] = hg
        return hg[0:1, :] if reverse else hg[SUBLANES - 1:SUBLANES, :]

    carry = lax.fori_loop(0, n_grp, group, carry_scr[...])
    carry_scr[...] = carry
    hT_ref[0] = carry


def lru_scan(xpre, conv_w, conv_b, wa, ba, wx, bx, lam, h0, n_tok, reverse, tm=512):
    r, w = xpre.shape
    b = r // n_tok
    tm = _row_tile(n_tok, tm)
    tpb = n_tok // tm
    hb = tm // SUBLANES
    last = r // SUBLANES - 1

    def tile(bi, s):
        return bi * tpb + ((tpb - 1 - s) if reverse else s)

    vec = pl.BlockSpec((1, w), lambda bi, s: (0, 0))
    blocks = pl.BlockSpec(wa.shape, lambda bi, s: (0, 0, 0))
    return pl.pallas_call(
        functools.partial(_lru_kernel, tm=tm, n_tok=n_tok, tpb=tpb, reverse=reverse),
        out_shape=[jax.ShapeDtypeStruct((r, w), F32), jax.ShapeDtypeStruct((b, 1, w), F32)],
        grid=(b, tpb),
        in_specs=[
            pl.BlockSpec((SUBLANES, w), lambda bi, s: (jnp.maximum(tile(bi, s) * hb - 1, 0), 0)),
            pl.BlockSpec((tm, w), lambda bi, s: (tile(bi, s), 0)),
            pl.BlockSpec((SUBLANES, w), lambda bi, s: (jnp.minimum((tile(bi, s) + 1) * hb, last), 0)),
            pl.BlockSpec((CONV_W, w), lambda bi, s: (0, 0)),
            vec, blocks, vec, blocks, vec, vec,
            pl.BlockSpec((1, 1, w), lambda bi, s: (bi, 0, 0)),
        ],
        out_specs=[pl.BlockSpec((tm, w), lambda bi, s: (tile(bi, s), 0)),
                   pl.BlockSpec((1, 1, w), lambda bi, s: (bi, 0, 0))],
        scratch_shapes=[pltpu.VMEM((tm, w), F32), pltpu.VMEM((tm, w), F32), pltpu.VMEM((1, w), F32)],
        compiler_params=_cparams("parallel", "arbitrary"),
        name="lru_scan",
    )(xpre, xpre, xpre, conv_w, conv_b.reshape(1, w), wa, ba.reshape(1, w), wx, bx.reshape(1, w),
      lam.reshape(1, w), h0)


def rglru_residual(x, xc, mods_l, mods_c, w_in, conv_w, conv_b, wa, ba, wx, bx, lam, w_out, n_tok, n_ctx):
    b = x.shape[0] // n_tok
    w = w_in.shape[1] // 2
    w_in = w_in.astype(BF16)
    sh_l, sc_l, g_l = mods_l
    sh_c, sc_c, g_c = mods_c
    y_l, xp_l = modmm(x, sh_l, sc_l, w_in, n_tok, (w, w), ("gelu", None), (F32, F32))
    y_c, xp_c = modmm(xc, sh_c, sc_c, w_in, n_ctx, (w, w), ("gelu", None), (F32, F32))
    zero = jnp.zeros((b, 1, w), F32)
    hs_l, hs_c = [], []
    for d in range(2):
        gate_w = (conv_w, conv_b, wa[d].astype(BF16), ba[d], wx[d].astype(BF16), bx[d], lam[d])
        hc, state = lru_scan(xp_c, *gate_w, zero, n_ctx, reverse=d == 1)
        hl, _ = lru_scan(xp_l, *gate_w, state, n_tok, reverse=d == 1)
        hs_c.append(hc)
        hs_l.append(hl)
    w_out = w_out.astype(BF16)
    return (mm_residual(tuple(hs_l), w_out, x, g_l, n_tok, b=y_l),
            mm_residual(tuple(hs_c), w_out, xc, g_c, n_ctx, b=y_c))


def _gmlp_kernel(x_ref, sh_ref, sc_ref, g_ref, win_ref, lng_ref, lnb_ref, ws_ref, bs_ref, wout_ref, o_ref, *, half):
    x = x_ref[...]
    h = _modulate(x, sh_ref[0], sc_ref[0]).astype(BF16)
    u = _gelu(_dot(h, win_ref[:, :half]))
    v = _gelu(_dot(h, win_ref[:, half:]))
    mu = jnp.mean(v, axis=-1, keepdims=True)
    vc = v - mu
    var = jnp.mean(vc * vc, axis=-1, keepdims=True)
    vn = (vc * lax.rsqrt(var + EPS) * lng_ref[...] + lnb_ref[...]).astype(BF16)
    gw = half // GMLP_GROUPS
    chunks = []
    for c in range(x.shape[0] // GMLP_CHUNK):
        rws = slice(c * GMLP_CHUNK, (c + 1) * GMLP_CHUNK)
        chunks.append(jnp.concatenate(
            [_dot(ws_ref[g], vn[rws, g * gw:(g + 1) * gw]) + bs_ref[g] for g in range(GMLP_GROUPS)], axis=1))
    v2 = jnp.concatenate(chunks, axis=0)
    o_ref[...] = x + g_ref[0] * _dot((u * v2).astype(BF16), wout_ref[...])


def gmlp_residual(x, shift, scale, gate, w_in, ln_g, ln_b, w_s, b_s, w_out, n_tok, tm=256):
    r, d = x.shape
    half = w_in.shape[1] // 2
    tm = _row_tile(n_tok, tm)
    tpb = n_tok // tm
    assert tm % GMLP_CHUNK == 0
    const2 = lambda i: (0, 0)
    const3 = lambda i: (0, 0, 0)
    once = pl.Buffered(1)
    return pl.pallas_call(
        functools.partial(_gmlp_kernel, half=half),
        out_shape=jax.ShapeDtypeStruct((r, d), F32),
        grid=(r // tm,),
        in_specs=[
            pl.BlockSpec((tm, d), lambda i: (i, 0)),
            _mod_spec(shift, tpb),
            _mod_spec(scale, tpb),
            _mod_spec(gate, tpb),
            pl.BlockSpec(w_in.shape, const2, pipeline_mode=once),
            pl.BlockSpec((1, half), const2),
            pl.BlockSpec((1, half), const2),
            pl.BlockSpec(w_s.shape, const3),
            pl.BlockSpec(b_s.shape + (1,), const3),
            pl.BlockSpec(w_out.shape, const2, pipeline_mode=once),
        ],
        out_specs=pl.BlockSpec((tm, d), lambda i: (i, 0)),
        compiler_params=_cparams("parallel"),
        name="gmlp_mixer",
    )(x, shift, scale, gate, w_in.astype(BF16), ln_g.reshape(1, half), ln_b.reshape(1, half),
      w_s.astype(BF16), b_s[..., None], w_out.astype(BF16))


def kernel(x, c, ctx, c_ctx, mod_w, mod_b, pool_w, pool_scale, lru_w_in, lru_conv_w, lru_conv_b, lru_wa, lru_ba,
           lru_wx, lru_bx, lru_lam, lru_w_out, attn_w_qkv, attn_q_gain, attn_k_gain, attn_w_o, gmlp_w_in, gmlp_ln_g,
           gmlp_ln_b, gmlp_w_s, gmlp_b_s, gmlp_w_out, moe_router, moe_w_gate, moe_w_up, moe_w_down, final_gain):
    b, n_tok, d = x.shape
    n_ctx = ctx.shape[1]
    depth = mod_w.shape[0]
    n_mixers = 4
    xl = x.reshape(b * n_tok, d)
    xc = ctx.reshape(b * n_ctx, d)

    pad = -(b + 1) % SUBLANES
    cvec = jnp.concatenate([c, c_ctx[None, :], jnp.zeros((pad, d), F32)], axis=0)
    mods = mod_vectors(cvec, mod_w, mod_b)

    for i in range(depth):
        m, j = i % n_mixers, i // n_mixers
        last = i == depth - 1
        ml = [mods[i, :b, k * d:(k + 1) * d].reshape(b, 1, d) for k in range(6)]
        mc = [mods[i, b:b + 1, k * d:(k + 1) * d].reshape(1, 1, d) for k in range(6)]
        if m == 0:
            pw = pool_w[j].astype(BF16)
            xl_new = pool_mixer_residual(xl, ml[0], ml[1], ml[2], pw, pool_scale[j], n_tok)
            if not last:
                xc = pool_mixer_residual(xc, mc[0], mc[1], mc[2], pw, pool_scale[j], n_ctx)
            xl = xl_new
        elif m == 1:
            xl, xc_new = rglru_residual(xl, xc, ml[:3], mc[:3], lru_w_in[j], lru_conv_w[j], lru_conv_b[j], lru_wa[j],
                                        lru_ba[j], lru_wx[j], lru_bx[j], lru_lam[j], lru_w_out[j], n_tok, n_ctx)
            xc = xc if last else xc_new
        elif m == 2:
            xl, xc_new = gqa_residual(xl, xc, ml[:3], mc[:3], attn_w_qkv[j], attn_q_gain[j], attn_k_gain[j],
                                      attn_w_o[j], n_tok, n_ctx)
            xc = xc if last else xc_new
        else:
            gargs = (gmlp_w_in[j], gmlp_ln_g[j], gmlp_ln_b[j], gmlp_w_s[j], gmlp_b_s[j], gmlp_w_out[j])
            xl_new = gmlp_residual(xl, ml[0], ml[1], ml[2], *gargs, n_tok)
            if not last:
                xc = gmlp_residual(xc, mc[0], mc[1], mc[2], *gargs, n_ctx)
            xl = xl_new
        wr = jnp.pad(moe_router[i], ((0, 0), (0, LANES - N_EXPERTS)))
        xl = moe_residual(xl, ml[3], ml[4], ml[5], wr, moe_w_gate, moe_w_up, moe_w_down, i, n_tok,
                          final_gain if last else None)
        if not last:
            xc = moe_residual(xc, mc[3], mc[4], mc[5], wr, moe_w_gate, moe_w_up, moe_w_down, i, n_ctx)
    return xl.reshape(b, n_tok, d)
```

```python
import functools

import jax
import jax.numpy as jnp
from jax import lax
from jax.experimental import pallas as pl
from jax.experimental.pallas import tpu as pltpu

F32 = jnp.float32
BF16 = jnp.bfloat16
EPS = 1e-6

N_EXPERTS = 16
CAPACITY_FACTOR = 2
POOL_WINDOWS = (2, 4, 8, 16)
POOL_HALO = 8
LRU_BLOCKS = 8
LRU_C = 8.0
CONV_W = 4
HEAD_DIM = 64
N_KV_HEADS = 4
GRID_W = 64
ROPE_THETA = 10000.0
GMLP_GROUPS = 4
GMLP_CHUNK = 128

LANES = 128
SUBLANES = 8
BF16_SUBLANES = 16
LOG2E = 1.4426950408889634
VMEM_LIMIT = 56 * 1024 * 1024


def _cparams(*sem):
    return pltpu.CompilerParams(dimension_semantics=sem, vmem_limit_bytes=VMEM_LIMIT)


def _modulate(x, shift, scale):
    ms = jnp.mean(x * x, axis=-1, keepdims=True)
    return x * lax.rsqrt(ms + EPS) * (1.0 + scale) + shift


def _split_bf16(a):
    hi = a.astype(BF16)
    lo = (a - hi.astype(F32)).astype(BF16)
    return hi, lo


def _dot(a, b):
    return jnp.dot(a, b, preferred_element_type=F32)


def _dot3(a, b):
    a_hi, a_lo = _split_bf16(a)
    b_hi, b_lo = _split_bf16(b)
    return _dot(a_hi, b_hi) + (_dot(a_hi, b_lo) + _dot(a_lo, b_hi))


def _gelu(x):
    return 0.5 * x * (1.0 + jnp.tanh(0.7978845608028654 * (x + 0.044715 * (x * x * x))))


def _silu(x):
    return x * (1.0 / (1.0 + jnp.exp(-x)))


def _sigmoid(x):
    return 0.5 * jnp.tanh(0.5 * x) + 0.5


def _mod_spec(mod, tiles_per_batch):
    d = mod.shape[-1]
    if mod.shape[0] == 1:
        return pl.BlockSpec((1, 1, d), lambda i, *_: (0, 0, 0))
    return pl.BlockSpec((1, 1, d), lambda i, *_: (i // tiles_per_batch, 0, 0))


def _row_tile(n, want):
    t = min(n, want)
    assert n % t == 0
    return t


def _mod_kernel(c_ref, w_ref, b_ref, o_ref):
    s = _silu(c_ref[...])
    o_ref[0] = _dot3(s, w_ref[0]) + b_ref[0]


def mod_vectors(cvec, mod_w, mod_b):
    depth, d, n6 = mod_w.shape
    tn = 1024
    return pl.pallas_call(
        _mod_kernel,
        out_shape=jax.ShapeDtypeStruct((depth, cvec.shape[0], n6), F32),
        grid=(depth, n6 // tn),
        in_specs=[
            pl.BlockSpec(cvec.shape, lambda l, j: (0, 0)),
            pl.BlockSpec((1, d, tn), lambda l, j: (l, 0, j)),
            pl.BlockSpec((1, 1, tn), lambda l, j: (l, 0, j)),
        ],
        out_specs=pl.BlockSpec((1, cvec.shape[0], tn), lambda l, j: (l, 0, j)),
        compiler_params=_cparams("parallel", "parallel"),
        name="mod_vectors",
    )(cvec, mod_w, mod_b.reshape(depth, 1, n6))


def _modmm_kernel(x_ref, sh_ref, sc_ref, w_ref, *o_refs, splits, acts):
    h = _modulate(x_ref[...], sh_ref[0], sc_ref[0]).astype(BF16)
    z = _dot(h, w_ref[...])
    off = 0
    for o_ref, width, act in zip(o_refs, splits, acts):
        part = z[:, off:off + width]
        if act == "gelu":
            part = _gelu(part)
        o_ref[...] = part.astype(o_ref.dtype)
        off += width


def modmm(x, shift, scale, w, n_tok, splits, acts, dtypes, tm=512):
    r, d = x.shape
    tm = _row_tile(n_tok, tm)
    tpb = n_tok // tm
    n = w.shape[1]
    assert sum(splits) == n
    return pl.pallas_call(
        functools.partial(_modmm_kernel, splits=splits, acts=acts),
        out_shape=[jax.ShapeDtypeStruct((r, s), dt) for s, dt in zip(splits, dtypes)],
        grid=(r // tm,),
        in_specs=[
            pl.BlockSpec((tm, d), lambda i: (i, 0)),
            _mod_spec(shift, tpb),
            _mod_spec(scale, tpb),
            pl.BlockSpec((d, n), lambda i: (0, 0)),
        ],
        out_specs=[pl.BlockSpec((tm, s), lambda i: (i, 0)) for s in splits],
        compiler_params=_cparams("parallel"),
        name="modmm",
    )(x, shift, scale, w)


def _mmres_kernel(*refs, n_sum, has_mul):
    a = refs[0][...]
    for a_ref in refs[1:n_sum]:
        a = a + a_ref[...]
    refs = refs[n_sum:]
    if has_mul:
        a = a.astype(F32) * refs[0][...].astype(F32)
        refs = refs[1:]
    w_ref, x_ref, g_ref, o_ref = refs
    o_ref[...] = x_ref[...] + g_ref[0] * _dot(a.astype(BF16), w_ref[...])


def mm_residual(a, w, x, gate, n_tok, b=None, tm=512):
    a = a if isinstance(a, (tuple, list)) else (a,)
    r, k = a[0].shape
    n = w.shape[1]
    tm = _row_tile(n_tok, tm)
    tpb = n_tok // tm
    ins = list(a) + ([b] if b is not None else []) + [w, x, gate]
    specs = [pl.BlockSpec((tm, k), lambda i: (i, 0))] * (len(a) + (b is not None))
    specs += [
        pl.BlockSpec((k, n), lambda i: (0, 0)),
        pl.BlockSpec((tm, n), lambda i: (i, 0)),
        _mod_spec(gate, tpb),
    ]
    return pl.pallas_call(
        functools.partial(_mmres_kernel, n_sum=len(a), has_mul=b is not None),
        out_shape=jax.ShapeDtypeStruct((r, n), F32),
        grid=(r // tm,),
        in_specs=specs,
        out_specs=pl.BlockSpec((tm, n), lambda i: (i, 0)),
        compiler_params=_cparams("parallel"),
        name="mm_residual",
    )(*ins)


def _pool_kernel(xp_ref, x_ref, xn_ref, sh_ref, sc_ref, g_ref, w_ref, ps_ref, o_ref, *, tm, n_tok, tpb):
    t0 = (pl.program_id(0) % tpb) * tm
    x = x_ref[...]
    rows = tm + 2 * POOL_HALO
    h = jnp.concatenate([
        jnp.where(t0 > 0, _modulate(xp_ref[...], sh_ref[0], sc_ref[0]), 0.0),
        _modulate(x, sh_ref[0], sc_ref[0]),
        jnp.where(t0 + tm < n_tok, _modulate(xn_ref[...], sh_ref[0], sc_ref[0]), 0.0)], axis=0)
    posc = lax.broadcasted_iota(jnp.int32, (tm, 1), 0) + t0
    gw = h.shape[1] // len(POOL_WINDOWS)
    outs = []
    for g, win in enumerate(POOL_WINDOWS):
        hg = h[:, g * gw:(g + 1) * gw]
        c = hg + pltpu.roll(hg, 1, 0)
        step = 1
        while 2 * step < win:
            c = pltpu.roll(c, step, 0) + pltpu.roll(c, rows - step, 0)
            step *= 2
        cnt = jnp.minimum(posc + (win - win // 2), n_tok) - jnp.maximum(posc - win // 2, 0)
        pooled = c[POOL_HALO:POOL_HALO + tm] / cnt.astype(F32) - hg[POOL_HALO:POOL_HALO + tm]
        outs.append(_dot(pooled.astype(BF16), w_ref[g]))
    y = jnp.concatenate(outs, axis=1) * ps_ref[...]
    o_ref[...] = x + g_ref[0] * y


def pool_mixer_residual(x, shift, scale, gate, w_pool, pool_scale, n_tok, tm=512):
    r, d = x.shape
    tm = _row_tile(n_tok, tm)
    tpb = n_tok // tm
    hb = tm // POOL_HALO
    last = r // POOL_HALO - 1
    groups, gw, _ = w_pool.shape
    return pl.pallas_call(
        functools.partial(_pool_kernel, tm=tm, n_tok=n_tok, tpb=tpb),
        out_shape=jax.ShapeDtypeStruct((r, d), F32),
        grid=(r // tm,),
        in_specs=[
            pl.BlockSpec((POOL_HALO, d), lambda i: (jnp.maximum(i * hb - 1, 0), 0)),
            pl.BlockSpec((tm, d), lambda i: (i, 0)),
            pl.BlockSpec((POOL_HALO, d), lambda i: (jnp.minimum((i + 1) * hb, last), 0)),
            _mod_spec(shift, tpb),
            _mod_spec(scale, tpb),
            _mod_spec(gate, tpb),
            pl.BlockSpec((groups, gw, gw), lambda i: (0, 0, 0)),
            pl.BlockSpec((1, d), lambda i: (0, 0)),
        ],
        out_specs=pl.BlockSpec((tm, d), lambda i: (i, 0)),
        compiler_params=_cparams("parallel"),
        name="pool_mixer",
    )(x, x, x, shift, scale, gate, w_pool, pool_scale.reshape(1, d))


def _router_kernel(x_ref, sh_ref, sc_ref, wr_ref, h_ref, aff_ref):
    h = _modulate(x_ref[...], sh_ref[0], sc_ref[0])
    h_ref[...] = h.astype(h_ref.dtype)
    logits = _dot3(h, wr_ref[...])
    lane = lax.broadcasted_iota(jnp.int32, logits.shape, 1)
    logits = jnp.where(lane < N_EXPERTS, logits, -jnp.inf)
    e = jnp.exp(logits - jnp.max(logits, axis=-1, keepdims=True))
    aff_ref[...] = e / jnp.sum(e, axis=-1, keepdims=True)


def router(x, shift, scale, w_router_padded, n_tok, tm=512):
    r, d = x.shape
    tm = _row_tile(n_tok, tm)
    tpb = n_tok // tm
    return pl.pallas_call(
        _router_kernel,
        out_shape=[jax.ShapeDtypeStruct((r, d), BF16), jax.ShapeDtypeStruct((r, LANES), F32)],
        grid=(r // tm,),
        in_specs=[
            pl.BlockSpec((tm, d), lambda i: (i, 0)),
            _mod_spec(shift, tpb),
            _mod_spec(scale, tpb),
            pl.BlockSpec((d, LANES), lambda i: (0, 0)),
        ],
        out_specs=[pl.BlockSpec((tm, d), lambda i: (i, 0)), pl.BlockSpec((tm, LANES), lambda i: (i, 0))],
        compiler_params=_cparams("parallel"),
        name="moe_router",
    )(x, shift, scale, w_router_padded)


MOE_CHUNK = 128
EXPERT_GROUPS = 2
COMBINE_CHUNKS = 2
NARROW_WINDOW = 64
PACK = LANES // N_EXPERTS


def _cumsum_rows(v, tb):
    n = v.shape[0]
    tri = (lax.broadcasted_iota(jnp.int32, (tb, tb), 0) >= lax.broadcasted_iota(jnp.int32, (tb, tb), 1)).astype(BF16)
    carry = jnp.zeros((1, v.shape[1]), F32)
    outs, starts = [], []
    for j in range(n // tb):
        starts.append(carry)
        c = _dot(tri, v[j * tb:(j + 1) * tb].astype(BF16)) + carry
        carry = c[tb - 1:tb, :]
        outs.append(c)
    return jnp.concatenate(outs, axis=0), jnp.concatenate(starts, axis=0)


def _route_kernel(aff_ref, affp_ref, idx_ref, gate_ref, slot_ref, start_ref, slot_scr, acc_scr, *, n, cap):
    def enough(cand):
        cnt = jnp.sum((affp_ref[...] >= pltpu.bitcast(cand, F32)).astype(F32), axis=0, keepdims=True)
        shift = N_EXPERTS
        while shift < LANES:
            cnt = cnt + pltpu.roll(cnt, shift, 1)
            shift *= 2
        return cnt >= cap

    def two_bits(k, prefix):
        low = 28 - 2 * k
        c1, c2, c3 = (prefix | jnp.left_shift(jnp.int32(j), low) for j in (1, 2, 3))
        return jnp.where(enough(c3), c3, jnp.where(enough(c2), c2, jnp.where(enough(c1), c1, prefix)))

    top = jnp.full((1, LANES), 1 << 30, jnp.int32)
    thr = jnp.where(enough(top), top, jnp.zeros_like(top))
    thr = pltpu.bitcast(lax.fori_loop(0, 15, two_bits, thr), F32)
    aff = aff_ref[...]
    gt = aff > thr
    eq = aff == thr
    need = cap - jnp.sum(gt.astype(F32), axis=0, keepdims=True)
    eq_rank, _ = _cumsum_rows(eq.astype(F32), MOE_CHUNK)
    sel = gt | (eq & (eq_rank <= need))
    pos, starts = _cumsum_rows(sel.astype(F32), MOE_CHUNK)
    slot_scr[...] = jnp.where(sel, pos - 1.0, -1.0)
    slot_ref[...] = slot_scr[...]
    start_ref[0] = starts.astype(jnp.int32)

    slots = lax.broadcasted_iota(jnp.int32, (1, cap), 1).astype(F32)
    sub = lax.broadcasted_iota(jnp.int32, (SUBLANES, MOE_CHUNK), 0)
    local = lax.broadcasted_iota(jnp.int32, (SUBLANES, MOE_CHUNK), 1).astype(F32)
    acc_scr[...] = jnp.zeros(acc_scr.shape, F32)

    def chunk(c, carry):
        rows = pl.ds(pl.multiple_of(c * MOE_CHUNK, MOE_CHUNK), MOE_CHUNK)
        base = jnp.where(sub == 0, local, jnp.where(sub == 1, lax.convert_element_type(c, F32), 0.0))
        aff_t = aff_ref[rows, :].T
        for e in range(N_EXPERTS):
            onehot = jnp.where(slot_scr[rows, e:e + 1] == slots, 1.0, 0.0).astype(BF16)
            g = aff_t[e:e + 1, :]
            g_hi = g.astype(BF16).astype(F32)
            g_mid = (g - g_hi).astype(BF16).astype(F32)
            g_lo = g - g_hi - g_mid
            lhs = jnp.where(sub == 2, g_hi, jnp.where(sub == 3, g_mid, jnp.where(sub == 4, g_lo, base)))
            acc_scr[e] += _dot(lhs.astype(BF16), onehot)
        return carry

    lax.fori_loop(0, n // MOE_CHUNK, chunk, 0)
    for e in range(N_EXPERTS):
        idx_ref[0, e:e + 1, :] = (acc_scr[e, 0:1, :] + MOE_CHUNK * acc_scr[e, 1:2, :]).astype(jnp.int32)
        gate_ref[0, e:e + 1, :] = acc_scr[e, 2:3, :] + acc_scr[e, 3:4, :] + acc_scr[e, 4:5, :]


def route(aff, n_tok):
    r = aff.shape[0]
    b = r // n_tok
    cap = CAPACITY_FACTOR * n_tok // N_EXPERTS
    aff_packed = aff[:, :N_EXPERTS].reshape(r // PACK, LANES)
    n_chunks = n_tok // MOE_CHUNK
    return pl.pallas_call(
        functools.partial(_route_kernel, n=n_tok, cap=cap),
        out_shape=[jax.ShapeDtypeStruct((b, N_EXPERTS, cap), jnp.int32), jax.ShapeDtypeStruct((b, N_EXPERTS, cap), F32),
                   jax.ShapeDtypeStruct((r, LANES), F32), jax.ShapeDtypeStruct((b, n_chunks, LANES), jnp.int32)],
        grid=(b,),
        in_specs=[pl.BlockSpec((n_tok, LANES), lambda i: (i, 0)),
                  pl.BlockSpec((n_tok // PACK, LANES), lambda i: (i, 0))],
        out_specs=[pl.BlockSpec((1, N_EXPERTS, cap), lambda i: (i, 0, 0)),
                   pl.BlockSpec((1, N_EXPERTS, cap), lambda i: (i, 0, 0)),
                   pl.BlockSpec((n_tok, LANES), lambda i: (i, 0)),
                   pl.BlockSpec((1, n_chunks, LANES), lambda i: (i, 0, 0))],
        scratch_shapes=[pltpu.VMEM((n_tok, LANES), F32), pltpu.VMEM((N_EXPERTS, SUBLANES, cap), F32)],
        compiler_params=_cparams("parallel"),
        name="moe_route",
    )(aff, aff_packed)


def _ffn_kernel(doff_ref, x_ref, gt_ref, wg_ref, wu_ref, wd_ref, y_ref, dup_ref, wg_scr, wu_scr, wd_scr, y_scr, *,
                e0, cap, n_steps, dwin):
    @pl.when(pl.program_id(1) == 0)
    def _():
        wg_scr[...] = wg_ref[0, 0].astype(BF16)
        wu_scr[...] = wu_ref[0, 0].astype(BF16)
        wd_scr[...] = wd_ref[0, 0].astype(BF16)

    x = x_ref[0]
    a = _dot(x, wg_scr[...])
    u = _dot(x, wu_scr[...])
    hmid = (_silu(a) * u).astype(BF16)
    gate_cols = jnp.broadcast_to(gt_ref[0, 0], (LANES, x.shape[0])).T
    y = _dot(hmid, wd_scr[...])
    y_scr[...] = (y * jnp.concatenate([gate_cols] * (y.shape[1] // LANES), axis=1)).astype(BF16)
    y_ref[0] = y_scr[...]
    e = e0 + pl.program_id(0)
    align = min(dwin, BF16_SUBLANES)
    for bl in range(y_scr.shape[0] // cap):
        b = pl.program_id(1) * (y_scr.shape[0] // cap) + bl
        for s in range(n_steps):
            off = pl.multiple_of(doff_ref[(b * n_steps + s) * N_EXPERTS + e], align)
            dup_ref[bl, s, 0] = y_scr[pl.ds(bl * cap + off, dwin), :]


def expert_ffn(xs, gates, doffs, w_gate, w_up, w_down, layer, e0, cap, n_steps, dwin, tm=512):
    e, m, d = xs.shape
    ff = w_gate.shape[-1]
    tm = max(_row_tile(m, tm), cap)
    bpt = tm // cap
    return pl.pallas_call(
        functools.partial(_ffn_kernel, e0=e0, cap=cap, n_steps=n_steps, dwin=dwin),
        out_shape=[jax.ShapeDtypeStruct((e, m, d), BF16),
                   jax.ShapeDtypeStruct((m // cap, n_steps, e, dwin, d), BF16)],
        grid_spec=pltpu.PrefetchScalarGridSpec(
            num_scalar_prefetch=1,
            grid=(e, m // tm),
            in_specs=[
                pl.BlockSpec((1, tm, d), lambda k, i, off: (k, i, 0)),
                pl.BlockSpec((1, 1, 1, tm), lambda k, i, off: (k, i, 0, 0)),
                pl.BlockSpec((1, 1, d, ff), lambda k, i, off: (layer, e0 + k, 0, 0)),
                pl.BlockSpec((1, 1, d, ff), lambda k, i, off: (layer, e0 + k, 0, 0)),
                pl.BlockSpec((1, 1, ff, d), lambda k, i, off: (layer, e0 + k, 0, 0)),
            ],
            out_specs=[pl.BlockSpec((1, tm, d), lambda k, i, off: (k, i, 0)),
                       pl.BlockSpec((bpt, n_steps, 1, dwin, d), lambda k, i, off: (i, 0, k, 0, 0))],
            scratch_shapes=[pltpu.VMEM((d, ff), BF16), pltpu.VMEM((d, ff), BF16), pltpu.VMEM((ff, d), BF16),
                            pltpu.VMEM((tm, d), BF16)],
        ),
        compiler_params=_cparams("parallel", "arbitrary"),
        name="moe_expert_ffn",
    )(doffs, xs, gates.reshape(e, m // tm, 1, tm), w_gate, w_up, w_down)


def _combine_kernel(woff_ref, doff_ref, fit_ref, ysel_ref, x_ref, g_ref, slot_ref, *rest, win, dwin, n_chunks, cps,
                    n_groups, final):
    dup_refs, y_refs, rest = rest[:n_groups], rest[n_groups:2 * n_groups], rest[2 * n_groups:]
    if final:
        fg_ref, o_ref = rest
    else:
        (o_ref,) = rest
    per_group = N_EXPERTS // n_groups
    step = pl.program_id(0) * (n_chunks // cps) + pl.program_id(1)

    def finish(rows, acc):
        out = x_ref[rows, :] + g_ref[0] * acc
        if final:
            out = out * lax.rsqrt(jnp.mean(out * out, axis=-1, keepdims=True) + EPS) * fg_ref[...]
        o_ref[rows, :] = out

    @pl.when(fit_ref[step] == 1)
    def _():
        tm = x_ref.shape[0]
        epl = LANES // dwin
        lane = lax.broadcasted_iota(jnp.int32, (1, LANES), 1)
        which = lane // dwin
        within = (lane % dwin).astype(F32)
        acc = jnp.zeros((tm, x_ref.shape[1]), F32)
        for gi, dup_ref in enumerate(dup_refs):
            pieces = []
            for p in range(per_group // epl):
                col, sl = None, None
                for w in range(epl):
                    e = gi * per_group + p * epl + w
                    cw = lax.convert_element_type(doff_ref[step * N_EXPERTS + e], F32) + within
                    sw = jnp.broadcast_to(slot_ref[:, e:e + 1], (tm, LANES))
                    col = cw if w == 0 else jnp.where(which == w, cw, col)
                    sl = sw if w == 0 else jnp.where(which == w, sw, sl)
                pieces.append(jnp.where(sl == col, 1.0, 0.0).astype(BF16))
            onehot = jnp.concatenate(pieces, axis=1)
            acc = acc + _dot(onehot, dup_ref[0, 0].reshape(per_group * dwin, x_ref.shape[1]))
        finish(slice(0, tm), acc)

    @pl.when(fit_ref[step] == 0)
    def _():
        align = min(win, LANES)
        for sc in range(cps):
            rows = slice(sc * MOE_CHUNK, (sc + 1) * MOE_CHUNK)
            base = (step * cps + sc) * N_EXPERTS
            acc = jnp.zeros((MOE_CHUNK, x_ref.shape[1]), F32)
            for e in range(N_EXPERTS):
                off = pl.multiple_of(woff_ref[base + e], align)
                cols = (lax.broadcasted_iota(jnp.int32, (1, win), 1) + off).astype(F32)
                onehot = jnp.where(slot_ref[rows, e:e + 1] == cols, 1.0, 0.0).astype(BF16)
                acc = acc + _dot(onehot, y_refs[e // per_group][e % per_group, pl.ds(off, win), :])
            finish(rows, acc)


def moe_combine(x, gate2, slot, ys, dups, woffs, doffs, fits, n_tok, final_gain=None):
    r, d = x.shape
    b = r // n_tok
    per_group = ys[0].shape[0]
    cap = ys[0].shape[1] // b
    dwin = dups[0].shape[3]
    n_chunks = n_tok // MOE_CHUNK
    cps = COMBINE_CHUNKS if n_chunks % COMBINE_CHUNKS == 0 else 1
    tm = cps * MOE_CHUNK
    steps = n_chunks // cps
    win = min(2 * MOE_CHUNK, cap)
    final = final_gain is not None
    need = jnp.max((1 - fits).reshape(b, steps), axis=1)
    ysel = lax.cummax(need * jnp.arange(b, dtype=jnp.int32))
    row_spec = lambda w: pl.BlockSpec((tm, w), lambda bi, c, *_: (bi * steps + c, 0))
    ins = [woffs, doffs, fits, ysel, x, gate2, slot] + list(dups) + list(ys)
    specs = [row_spec(d), _mod_spec(gate2, 1), row_spec(LANES)]
    specs += [pl.BlockSpec((1, 1, per_group, dwin, d), lambda bi, c, *_: (bi, c, 0, 0, 0))] * len(dups)
    specs += [pl.BlockSpec((per_group, cap, d), lambda bi, c, wo, do, ft, ys_: (0, ys_[bi], 0),
                           pipeline_mode=pl.Buffered(1))] * len(ys)
    if final:
        ins.append(final_gain.reshape(1, d))
        specs.append(pl.BlockSpec((1, d), lambda bi, c, *_: (0, 0)))
    return pl.pallas_call(
        functools.partial(_combine_kernel, win=win, dwin=dwin, n_chunks=n_chunks, cps=cps, n_groups=len(ys),
                          final=final),
        out_shape=jax.ShapeDtypeStruct((r, d), F32),
        grid_spec=pltpu.PrefetchScalarGridSpec(
            num_scalar_prefetch=4,
            grid=(b, steps),
            in_specs=specs,
            out_specs=row_spec(d),
        ),
        compiler_params=_cparams("parallel", "arbitrary"),
        name="moe_combine",
    )(*ins)


def moe_residual(x, shift, scale, gate2, w_router_padded, w_gate, w_up, w_down, layer, n_tok, final_gain=None):
    r, d = x.shape
    b = r // n_tok
    h, aff = router(x, shift, scale, w_router_padded, n_tok)
    idx, gates, slot, start = route(aff, n_tok)
    cap = idx.shape[-1]
    n_chunks = n_tok // MOE_CHUNK
    cps = COMBINE_CHUNKS if n_chunks % COMBINE_CHUNKS == 0 else 1
    steps = n_chunks // cps
    start = start[:, :, :N_EXPERTS]
    win = min(2 * MOE_CHUNK, cap)
    align = min(win, LANES)
    woffs = jnp.clip(start // align * align, 0, cap - win).reshape(-1)
    dwin = min(NARROW_WINDOW, cap)
    dalign = min(dwin, BF16_SUBLANES)
    s_start = start[:, ::cps]
    s_end = jnp.concatenate([s_start[:, 1:], jnp.full((b, 1, N_EXPERTS), cap, jnp.int32)], axis=1)
    doffs = jnp.clip(s_start // dalign * dalign, 0, cap - dwin)
    fits = jnp.all(s_end - doffs <= dwin, axis=-1).astype(jnp.int32).reshape(-1)
    doffs = doffs.reshape(-1)

    gidx = idx + (jnp.arange(b, dtype=jnp.int32) * n_tok)[:, None, None]
    gidx = jnp.transpose(gidx, (1, 0, 2)).reshape(N_EXPERTS, b * cap)
    gts = jnp.transpose(gates, (1, 0, 2)).reshape(N_EXPERTS, b * cap)
    per_group = N_EXPERTS // EXPERT_GROUPS
    ys, dups = [], []
    for g in range(EXPERT_GROUPS):
        grp = slice(g * per_group, (g + 1) * per_group)
        xs = h.at[gidx[grp]].get(mode="promise_in_bounds")
        y, dup = expert_ffn(xs, gts[grp], doffs, w_gate, w_up, w_down, layer, g * per_group, cap, steps, dwin)
        ys.append(y)
        dups.append(dup)
    return moe_combine(x, gate2, slot, ys, dups, woffs, doffs, fits, n_tok, final_gain)


def _head_rms(t, seg_ones):
    outs = []
    for j in range(t.shape[1] // LANES):
        blk = t[:, j * LANES:(j + 1) * LANES]
        hi, lo = _split_bf16(blk * blk)
        ss = _dot(hi, seg_ones) + _dot(lo, seg_ones)
        outs.append(blk * lax.rsqrt(ss * (1.0 / HEAD_DIM) + EPS))
    return jnp.concatenate(outs, axis=1)


def _rope(t, cos, sin_signed):
    w = t.shape[1]
    half = HEAD_DIM // 2
    lane = lax.broadcasted_iota(jnp.int32, t.shape, 1)
    partner = jnp.where(lane % HEAD_DIM < half, pltpu.roll(t, w - half, 1), pltpu.roll(t, half, 1))
    reps = w // cos.shape[1]
    return t * jnp.concatenate([cos] * reps, axis=1) + partner * jnp.concatenate([sin_signed] * reps, axis=1)


def _qkv_kernel(*refs, qd, kd, rope):
    if rope:
        x_ref, sh_ref, sc_ref, w_ref, qg_ref, kg_ref, cos_ref, sin_ref, q_ref, k_ref, v_ref = refs
    else:
        x_ref, sh_ref, sc_ref, w_ref, qg_ref, kg_ref, q_ref, k_ref, v_ref = refs
    h = _modulate(x_ref[...], sh_ref[0], sc_ref[0]).astype(BF16)
    z = _dot(h, w_ref[...])
    seg = (lax.broadcasted_iota(jnp.int32, (LANES, LANES), 0) // HEAD_DIM
           == lax.broadcasted_iota(jnp.int32, (LANES, LANES), 1) // HEAD_DIM).astype(BF16)
    q = _head_rms(z[:, :qd], seg) * qg_ref[...]
    k = _head_rms(z[:, qd:qd + kd], seg) * kg_ref[...]
    if rope:
        q = _rope(q, cos_ref[...], sin_ref[...])
        k = _rope(k, cos_ref[...], sin_ref[...])
    q_ref[...] = (q * (HEAD_DIM ** -0.5 * LOG2E)).astype(q_ref.dtype)
    k_ref[...] = k.astype(k_ref.dtype)
    v_ref[...] = z[:, qd + kd:].astype(v_ref.dtype)


def qkv_project(x, shift, scale, w_qkv, q_gain, k_gain, n_tok, rope_tables=None, tm=512):
    r, d = x.shape
    kd = N_KV_HEADS * HEAD_DIM
    qd = w_qkv.shape[1] - 2 * kd
    tm = _row_tile(n_tok, tm)
    tpb = n_tok // tm
    ins = [x, shift, scale, w_qkv, q_gain, k_gain]
    specs = [
        pl.BlockSpec((tm, d), lambda i: (i, 0)),
        _mod_spec(shift, tpb),
        _mod_spec(scale, tpb),
        pl.BlockSpec(w_qkv.shape, lambda i: (0, 0)),
        pl.BlockSpec((1, qd), lambda i: (0, 0)),
        pl.BlockSpec((1, kd), lambda i: (0, 0)),
    ]
    if rope_tables is not None:
        ins += list(rope_tables)
        specs += [pl.BlockSpec((tm, LANES), lambda i: (i % tpb, 0))] * 2
    return pl.pallas_call(
        functools.partial(_qkv_kernel, qd=qd, kd=kd, rope=rope_tables is not None),
        out_shape=[jax.ShapeDtypeStruct((r, qd), BF16), jax.ShapeDtypeStruct((r, kd), BF16),
                   jax.ShapeDtypeStruct((r, kd), BF16)],
        grid=(r // tm,),
        in_specs=specs,
        out_specs=[pl.BlockSpec((tm, qd), lambda i: (i, 0)), pl.BlockSpec((tm, kd), lambda i: (i, 0)),
                   pl.BlockSpec((tm, kd), lambda i: (i, 0))],
        compiler_params=_cparams("parallel"),
        name="qkv_project",
    )(*ins)


def _attn_kernel(q_ref, k_ref, vt_ref, o_ref, *scr, grp, bounded):
    k = k_ref[0, 0]
    vt = vt_ref[0, 0]
    tq = q_ref.shape[0]
    n_pairs = grp // 2
    outs = []

    def scores(pair):
        heads = (2 * pair, 2 * pair + 1)
        q2 = jnp.concatenate([q_ref[:, h * HEAD_DIM:(h + 1) * HEAD_DIM] for h in heads], axis=0)
        return lax.dot_general(k, q2, (((1,), (1,)), ((), ())), preferred_element_type=F32)

    if bounded:
        for pair in range(n_pairs):
            p = jnp.exp2(scores(pair))
            scr[pair][...] = p.astype(BF16)
            ot = _dot(vt[:HEAD_DIM], scr[pair][...]) / jnp.sum(p, axis=0, keepdims=True)
            o = jnp.concatenate([ot, ot], axis=0).T
            outs += [o[:tq, :HEAD_DIM], o[tq:, :HEAD_DIM]]
        p_refs = ()
    else:
        for pair in range(n_pairs):
            scr[2 * pair][...] = scores(pair)
        for pair in range(n_pairs):
            st_scr, p_scr = scr[2 * pair], scr[2 * pair + 1]
            p_scr[...] = jnp.exp2(st_scr[...] - jnp.max(st_scr[...], axis=0, keepdims=True)).astype(BF16)
        p_refs = scr[1::2]
    for p_scr in p_refs:
        ot = _dot(vt, p_scr[...])
        ot = ot[:HEAD_DIM] / ot[HEAD_DIM:]
        o = jnp.concatenate([ot, ot], axis=0).T
        outs += [o[:tq, :HEAD_DIM], o[tq:, :HEAD_DIM]]
    o_ref[...] = jnp.concatenate(outs, axis=1).astype(o_ref.dtype)


def _attention_call(q, k, vt, n_q, tq, bounded):
    r, qd = q.shape
    _, kvh, n_k, hd = k.shape
    grp = qd // (kvh * hd)
    tq = _row_tile(n_q, tq)
    tpb = n_q // tq
    dts = (BF16,) if bounded else (F32, BF16)
    return pl.pallas_call(
        functools.partial(_attn_kernel, grp=grp, bounded=bounded),
        out_shape=jax.ShapeDtypeStruct((r, qd), BF16),
        grid=(r // n_q, kvh, tpb),
        in_specs=[
            pl.BlockSpec((tq, grp * hd), lambda b, g, i: (b * tpb + i, g)),
            pl.BlockSpec((1, 1, n_k, hd), lambda b, g, i: (b, g, 0, 0)),
            pl.BlockSpec((1, 1, 2 * hd, n_k), lambda b, g, i: (b, g, 0, 0)),
        ],
        out_specs=pl.BlockSpec((tq, grp * hd), lambda b, g, i: (b * tpb + i, g)),
        scratch_shapes=[pltpu.VMEM((n_k, 2 * tq), dt) for _ in range(grp // 2) for dt in dts],
        compiler_params=_cparams("parallel", "parallel", "parallel"),
        name="attention_bounded" if bounded else "attention",
    )(q, k, vt)


SCORE_BOUND = 60.0


def attention(q, k, vt, n_q, score_bound, tq_bounded=512, tq_exact=256):
    return lax.cond(score_bound <= SCORE_BOUND,
                    lambda: _attention_call(q, k, vt, n_q, tq_bounded, True),
                    lambda: _attention_call(q, k, vt, n_q, tq_exact, False))


def rope_tables(n_tok):
    t = jnp.arange(n_tok)
    row = (t // GRID_W).astype(F32)
    col = (t % GRID_W).astype(F32)
    n_freq = HEAD_DIM // 4
    inv = ROPE_THETA ** (-jnp.arange(n_freq, dtype=F32) / n_freq)
    ang = jnp.concatenate([row[:, None] * inv, col[:, None] * inv], axis=-1)
    cos, sin = jnp.cos(ang), jnp.sin(ang)
    reps = LANES // HEAD_DIM
    return jnp.tile(jnp.concatenate([cos, cos], -1), (1, reps)), jnp.tile(jnp.concatenate([-sin, sin], -1), (1, reps))


def _deinterleave_heads(w, n_heads):
    lead = w.shape[:-1]
    w = w.reshape(lead + (n_heads, HEAD_DIM // 2, 2))
    return jnp.swapaxes(w, -1, -2).reshape(lead + (n_heads * HEAD_DIM,))


def _split_kv_heads(t, b):
    return jnp.transpose(t.reshape(b, -1, N_KV_HEADS, HEAD_DIM), (0, 2, 1, 3))


def gqa_residual(x, xc, mods_l, mods_c, w_qkv, q_gain, k_gain, w_o, n_tok, n_ctx):
    b = x.shape[0] // n_tok
    kd = N_KV_HEADS * HEAD_DIM
    qd = w_qkv.shape[1] - 2 * kd
    n_qh = qd // HEAD_DIM
    w_perm = jnp.concatenate([_deinterleave_heads(w_qkv[:, :qd], n_qh),
                              _deinterleave_heads(w_qkv[:, qd:qd + kd], N_KV_HEADS), w_qkv[:, qd + kd:]], axis=1)
    w_perm = w_perm.astype(BF16)
    qg = jnp.tile(_deinterleave_heads(q_gain, 1), n_qh).reshape(1, qd)
    kg = jnp.tile(_deinterleave_heads(k_gain, 1), N_KV_HEADS).reshape(1, kd)
    sh_l, sc_l, g_l = mods_l
    sh_c, sc_c, g_c = mods_c
    q_l, k_l, v_l = qkv_project(x, sh_l, sc_l, w_perm, qg, kg, n_tok, rope_tables(n_tok))
    q_c, k_c, v_c = qkv_project(xc, sh_c, sc_c, w_perm, qg, kg, n_ctx)
    k_c4 = _split_kv_heads(k_c, b)
    k_all = jnp.concatenate([k_c4, _split_kv_heads(k_l, b)], axis=2)

    def values_t(v):
        vt = jnp.transpose(v.reshape(b, -1, N_KV_HEADS, HEAD_DIM), (0, 2, 3, 1))
        return jnp.concatenate([vt, jnp.ones_like(vt)], axis=2)

    vt_c = values_t(v_c)
    vt_all = jnp.concatenate([vt_c, values_t(v_l)], axis=3)
    score_bound = (1.01 * HEAD_DIM * HEAD_DIM ** -0.5 * LOG2E
                   * jnp.max(jnp.abs(q_gain)) * jnp.max(jnp.abs(k_gain)))
    o_l = attention(q_l, k_all, vt_all, n_tok, score_bound)
    o_c = attention(q_c, k_c4, vt_c, n_ctx, score_bound)
    w_o = w_o.astype(BF16)
    return mm_residual(o_l, w_o, x, g_l, n_tok), mm_residual(o_c, w_o, xc, g_c, n_ctx)


def _lru_kernel(xp_ref, x_ref, xn_ref, cw_ref, cb_ref, wa_ref, ba_ref, wx_ref, bx_ref, lam_ref, h0_ref,
                o_ref, hT_ref, a_scr, u_scr, carry_scr, *, tm, n_tok, tpb, reverse):
    step = pl.program_id(1)
    t_idx = (tpb - 1 - step) if reverse else step
    t0 = t_idx * tm
    rows = tm + 2 * SUBLANES

    @pl.when(step == 0)
    def _():
        carry_scr[...] = h0_ref[0]

    xe = jnp.concatenate([jnp.where(t0 > 0, xp_ref[...], 0.0), x_ref[...],
                          jnp.where(t0 + tm < n_tok, xn_ref[...], 0.0)], axis=0)
    left = CONV_W // 2
    conv = cb_ref[...]
    for k in range(CONV_W):
        shift = (left - k) % rows
        tap = xe if shift == 0 else pltpu.roll(xe, shift, 0)
        conv = conv + tap * cw_ref[k:k + 1, :]
    xr = conv[SUBLANES:SUBLANES + tm]

    xb = xr.astype(BF16)
    bw = xr.shape[1] // LRU_BLOCKS
    ra, ia = [], []
    for j in range(LRU_BLOCKS):
        blk = xb[:, j * bw:(j + 1) * bw]
        ra.append(_dot(blk, wa_ref[j]))
        ia.append(_dot(blk, wx_ref[j]))
    r = _sigmoid(jnp.concatenate(ra, axis=1) + ba_ref[...])
    i = _sigmoid(jnp.concatenate(ia, axis=1) + bx_ref[...])
    log_a = -LRU_C * r * jnp.logaddexp(-lam_ref[...], 0.0)
    a = jnp.exp(log_a)
    a_scr[...] = a
    t = jnp.tanh(log_a)
    u_scr[...] = xr * i * jnp.sqrt(-2.0 * t / (1.0 - t))

    n_grp = tm // SUBLANES
    sub = lax.broadcasted_iota(jnp.int32, (SUBLANES, 1), 0)

    def group(j, carry):
        g = (n_grp - 1 - j) if reverse else j
        rws = pl.ds(pl.multiple_of(g * SUBLANES, SUBLANES), SUBLANES)
        ag, ug = a_scr[rws, :], u_scr[rws, :]
        s = 1
        while s < SUBLANES:
            if reverse:
                ok = sub < SUBLANES - s
                sh = SUBLANES - s
            else:
                ok = sub >= s
                sh = s
            u_prev = jnp.where(ok, pltpu.roll(ug, sh, 0), 0.0)
            a_prev = jnp.where(ok, pltpu.roll(ag, sh, 0), 1.0)
            ug = ug + ag * u_prev
            ag = ag * a_prev
            s *= 2
        hg = ug + ag * carry
        o_ref[rws, :] = hg
        return hg[0:1, :] if reverse else hg[SUBLANES - 1:SUBLANES, :]

    carry = lax.fori_loop(0, n_grp, group, carry_scr[...])
    carry_scr[...] = carry
    hT_ref[0] = carry


def lru_scan(xpre, conv_w, conv_b, wa, ba, wx, bx, lam, h0, n_tok, reverse, tm=512):
    r, w = xpre.shape
    b = r // n_tok
    tm = _row_tile(n_tok, tm)
    tpb = n_tok // tm
    hb = tm // SUBLANES
    last = r // SUBLANES - 1

    def tile(bi, s):
        return bi * tpb + ((tpb - 1 - s) if reverse else s)

    vec = pl.BlockSpec((1, w), lambda bi, s: (0, 0))
    blocks = pl.BlockSpec(wa.shape, lambda bi, s: (0, 0, 0))
    return pl.pallas_call(
        functools.partial(_lru_kernel, tm=tm, n_tok=n_tok, tpb=tpb, reverse=reverse),
        out_shape=[jax.ShapeDtypeStruct((r, w), F32), jax.ShapeDtypeStruct((b, 1, w), F32)],
        grid=(b, tpb),
        in_specs=[
            pl.BlockSpec((SUBLANES, w), lambda bi, s: (jnp.maximum(tile(bi, s) * hb - 1, 0), 0)),
            pl.BlockSpec((tm, w), lambda bi, s: (tile(bi, s), 0)),
            pl.BlockSpec((SUBLANES, w), lambda bi, s: (jnp.minimum((tile(bi, s) + 1) * hb, last), 0)),
            pl.BlockSpec((CONV_W, w), lambda bi, s: (0, 0)),
            vec, blocks, vec, blocks, vec, vec,
            pl.BlockSpec((1, 1, w), lambda bi, s: (bi, 0, 0)),
        ],
        out_specs=[pl.BlockSpec((tm, w), lambda bi, s: (tile(bi, s), 0)),
                   pl.BlockSpec((1, 1, w), lambda bi, s: (bi, 0, 0))],
        scratch_shapes=[pltpu.VMEM((tm, w), F32), pltpu.VMEM((tm, w), F32), pltpu.VMEM((1, w), F32)],
        compiler_params=_cparams("parallel", "arbitrary"),
        name="lru_scan",
    )(xpre, xpre, xpre, conv_w, conv_b.reshape(1, w), wa, ba.reshape(1, w), wx, bx.reshape(1, w),
      lam.reshape(1, w), h0)


def rglru_residual(x, xc, mods_l, mods_c, w_in, conv_w, conv_b, wa, ba, wx, bx, lam, w_out, n_tok, n_ctx):
    b = x.shape[0] // n_tok
    w = w_in.shape[1] // 2
    w_in = w_in.astype(BF16)
    sh_l, sc_l, g_l = mods_l
    sh_c, sc_c, g_c = mods_c
    y_l, xp_l = modmm(x, sh_l, sc_l, w_in, n_tok, (w, w), ("gelu", None), (F32, F32))
    y_c, xp_c = modmm(xc, sh_c, sc_c, w_in, n_ctx, (w, w), ("gelu", None), (F32, F32))
    zero = jnp.zeros((b, 1, w), F32)
    hs_l, hs_c = [], []
    for d in range(2):
        gate_w = (conv_w, conv_b, wa[d].astype(BF16), ba[d], wx[d].astype(BF16), bx[d], lam[d])
        hc, state = lru_scan(xp_c, *gate_w, zero, n_ctx, reverse=d == 1)
        hl, _ = lru_scan(xp_l, *gate_w, state, n_tok, reverse=d == 1)
        hs_c.append(hc)
        hs_l.append(hl)
    w_out = w_out.astype(BF16)
    return (mm_residual(tuple(hs_l), w_out, x, g_l, n_tok, b=y_l),
            mm_residual(tuple(hs_c), w_out, xc, g_c, n_ctx, b=y_c))


def _gmlp_kernel(x_ref, sh_ref, sc_ref, g_ref, win_ref, lng_ref, lnb_ref, ws_ref, bs_ref, wout_ref, o_ref, *, half):
    x = x_ref[...]
    h = _modulate(x, sh_ref[0], sc_ref[0]).astype(BF16)
    u = _gelu(_dot(h, win_ref[:, :half]))
    v = _gelu(_dot(h, win_ref[:, half:]))
    mu = jnp.mean(v, axis=-1, keepdims=True)
    vc = v - mu
    var = jnp.mean(vc * vc, axis=-1, keepdims=True)
    vn = (vc * lax.rsqrt(var + EPS) * lng_ref[...] + lnb_ref[...]).astype(BF16)
    gw = half // GMLP_GROUPS
    chunks = []
    for c in range(x.shape[0] // GMLP_CHUNK):
        rws = slice(c * GMLP_CHUNK, (c + 1) * GMLP_CHUNK)
        chunks.append(jnp.concatenate(
            [_dot(ws_ref[g], vn[rws, g * gw:(g + 1) * gw]) + bs_ref[g] for g in range(GMLP_GROUPS)], axis=1))
    v2 = jnp.concatenate(chunks, axis=0)
    o_ref[...] = x + g_ref[0] * _dot((u * v2).astype(BF16), wout_ref[...])


def gmlp_residual(x, shift, scale, gate, w_in, ln_g, ln_b, w_s, b_s, w_out, n_tok, tm=256):
    r, d = x.shape
    half = w_in.shape[1] // 2
    tm = _row_tile(n_tok, tm)
    tpb = n_tok // tm
    assert tm % GMLP_CHUNK == 0
    const2 = lambda i: (0, 0)
    const3 = lambda i: (0, 0, 0)
    once = pl.Buffered(1)
    return pl.pallas_call(
        functools.partial(_gmlp_kernel, half=half),
        out_shape=jax.ShapeDtypeStruct((r, d), F32),
        grid=(r // tm,),
        in_specs=[
            pl.BlockSpec((tm, d), lambda i: (i, 0)),
            _mod_spec(shift, tpb),
            _mod_spec(scale, tpb),
            _mod_spec(gate, tpb),
            pl.BlockSpec(w_in.shape, const2, pipeline_mode=once),
            pl.BlockSpec((1, half), const2),
            pl.BlockSpec((1, half), const2),
            pl.BlockSpec(w_s.shape, const3),
            pl.BlockSpec(b_s.shape + (1,), const3),
            pl.BlockSpec(w_out.shape, const2, pipeline_mode=once),
        ],
        out_specs=pl.BlockSpec((tm, d), lambda i: (i, 0)),
        compiler_params=_cparams("parallel"),
        name="gmlp_mixer",
    )(x, shift, scale, gate, w_in.astype(BF16), ln_g.reshape(1, half), ln_b.reshape(1, half),
      w_s.astype(BF16), b_s[..., None], w_out.astype(BF16))


def kernel(x, c, ctx, c_ctx, mod_w, mod_b, pool_w, pool_scale, lru_w_in, lru_conv_w, lru_conv_b, lru_wa, lru_ba,
           lru_wx, lru_bx, lru_lam, lru_w_out, attn_w_qkv, attn_q_gain, attn_k_gain, attn_w_o, gmlp_w_in, gmlp_ln_g,
           gmlp_ln_b, gmlp_w_s, gmlp_b_s, gmlp_w_out, moe_router, moe_w_gate, moe_w_up, moe_w_down, final_gain):
    b, n_tok, d = x.shape
    n_ctx = ctx.shape[1]
    depth = mod_w.shape[0]
    n_mixers = 4
    xl = x.reshape(b * n_tok, d)
    xc = ctx.reshape(b * n_ctx, d)

    pad = -(b + 1) % SUBLANES
    cvec = jnp.concatenate([c, c_ctx[None, :], jnp.zeros((pad, d), F32)], axis=0)
    mods = mod_vectors(cvec, mod_w, mod_b)

    for i in range(depth):
        m, j = i % n_mixers, i // n_mixers
        last = i == depth - 1
        ml = [mods[i, :b, k * d:(k + 1) * d].reshape(b, 1, d) for k in range(6)]
        mc = [mods[i, b:b + 1, k * d:(k + 1) * d].reshape(1, 1, d) for k in range(6)]
        if m == 0:
            pw = pool_w[j].astype(BF16)
            xl_new = pool_mixer_residual(xl, ml[0], ml[1], ml[2], pw, pool_scale[j], n_tok)
            if not last:
                xc = pool_mixer_residual(xc, mc[0], mc[1], mc[2], pw, pool_scale[j], n_ctx)
            xl = xl_new
        elif m == 1:
            xl, xc_new = rglru_residual(xl, xc, ml[:3], mc[:3], lru_w_in[j], lru_conv_w[j], lru_conv_b[j], lru_wa[j],
                                        lru_ba[j], lru_wx[j], lru_bx[j], lru_lam[j], lru_w_out[j], n_tok, n_ctx)
            xc = xc if last else xc_new
        elif m == 2:
            xl, xc_new = gqa_residual(xl, xc, ml[:3], mc[:3], attn_w_qkv[j], attn_q_gain[j], attn_k_gain[j],
                                      attn_w_o[j], n_tok, n_ctx)
            xc = xc if last else xc_new
        else:
            gargs = (gmlp_w_in[j], gmlp_ln_g[j], gmlp_ln_b[j], gmlp_w_s[j], gmlp_b_s[j], gmlp_w_out[j])
            xl_new = gmlp_residual(xl, ml[0], ml[1], ml[2], *gargs, n_tok)
            if not last:
                xc = gmlp_residual(xc, mc[0], mc[1], mc[2], *gargs, n_ctx)
            xl = xl_new
        wr = jnp.pad(moe_router[i], ((0, 0), (0, LANES - N_EXPERTS)))
        xl = moe_residual(xl, ml[3], ml[4], ml[5], wr, moe_w_gate, moe_w_up, moe_w_down, i, n_tok,
                          final_gain if last else None)
        if not last:
            xc = moe_residual(xc, mc[3], mc[4], mc[5], wr, moe_w_gate, moe_w_up, moe_w_down, i, n_ctx)
    return xl.reshape(b, n_tok, d)
```

```python
import functools

import jax
import jax.numpy as jnp
from jax import lax
from jax.experimental import pallas as pl
from jax.experimental.pallas import tpu as pltpu

F32 = jnp.float32
BF16 = jnp.bfloat16
EPS = 1e-6

N_EXPERTS = 16
CAPACITY_FACTOR = 2
POOL_WINDOWS = (2, 4, 8, 16)
POOL_HALO = 8
LRU_BLOCKS = 8
LRU_C = 8.0
CONV_W = 4
HEAD_DIM = 64
N_KV_HEADS = 4
GRID_W = 64
ROPE_THETA = 10000.0
GMLP_GROUPS = 4
GMLP_CHUNK = 128

LANES = 128
SUBLANES = 8
BF16_SUBLANES = 16
LOG2E = 1.4426950408889634
VMEM_LIMIT = 56 * 1024 * 1024


def _cparams(*sem):
    return pltpu.CompilerParams(dimension_semantics=sem, vmem_limit_bytes=VMEM_LIMIT)


def _modulate(x, shift, scale):
    ms = jnp.mean(x * x, axis=-1, keepdims=True)
    return x * lax.rsqrt(ms + EPS) * (1.0 + scale) + shift


def _split_bf16(a):
    hi = a.astype(BF16)
    lo = (a - hi.astype(F32)).astype(BF16)
    return hi, lo


def _dot(a, b):
    return jnp.dot(a, b, preferred_element_type=F32)


def _dot3(a, b):
    a_hi, a_lo = _split_bf16(a)
    b_hi, b_lo = _split_bf16(b)
    return _dot(a_hi, b_hi) + (_dot(a_hi, b_lo) + _dot(a_lo, b_hi))


def _gelu(x):
    return 0.5 * x * (1.0 + jnp.tanh(0.7978845608028654 * (x + 0.044715 * (x * x * x))))


def _silu(x):
    return x * (1.0 / (1.0 + jnp.exp(-x)))


def _sigmoid(x):
    return 0.5 * jnp.tanh(0.5 * x) + 0.5


def _mod_spec(mod, tiles_per_batch):
    d = mod.shape[-1]
    if mod.shape[0] == 1:
        return pl.BlockSpec((1, 1, d), lambda i, *_: (0, 0, 0))
    return pl.BlockSpec((1, 1, d), lambda i, *_: (i // tiles_per_batch, 0, 0))


def _row_tile(n, want):
    t = min(n, want)
    assert n % t == 0
    return t


def _mod_kernel(c_ref, w_ref, b_ref, o_ref):
    s = _silu(c_ref[...])
    o_ref[0] = _dot3(s, w_ref[0]) + b_ref[0]


def mod_vectors(cvec, mod_w, mod_b):
    depth, d, n6 = mod_w.shape
    tn = 1024
    return pl.pallas_call(
        _mod_kernel,
        out_shape=jax.ShapeDtypeStruct((depth, cvec.shape[0], n6), F32),
        grid=(depth, n6 // tn),
        in_specs=[
            pl.BlockSpec(cvec.shape, lambda l, j: (0, 0)),
            pl.BlockSpec((1, d, tn), lambda l, j: (l, 0, j)),
            pl.BlockSpec((1, 1, tn), lambda l, j: (l, 0, j)),
        ],
        out_specs=pl.BlockSpec((1, cvec.shape[0], tn), lambda l, j: (l, 0, j)),
        compiler_params=_cparams("parallel", "parallel"),
        name="mod_vectors",
    )(cvec, mod_w, mod_b.reshape(depth, 1, n6))


def _modmm_kernel(x_ref, sh_ref, sc_ref, w_ref, *o_refs, splits, acts):
    h = _modulate(x_ref[...], sh_ref[0], sc_ref[0]).astype(BF16)
    z = _dot(h, w_ref[...])
    off = 0
    for o_ref, width, act in zip(o_refs, splits, acts):
        part = z[:, off:off + width]
        if act == "gelu":
            part = _gelu(part)
        o_ref[...] = part.astype(o_ref.dtype)
        off += width


def modmm(x, shift, scale, w, n_tok, splits, acts, dtypes, tm=512):
    r, d = x.shape
    tm = _row_tile(n_tok, tm)
    tpb = n_tok // tm
    n = w.shape[1]
    assert sum(splits) == n
    return pl.pallas_call(
        functools.partial(_modmm_kernel, splits=splits, acts=acts),
        out_shape=[jax.ShapeDtypeStruct((r, s), dt) for s, dt in zip(splits, dtypes)],
        grid=(r // tm,),
        in_specs=[
            pl.BlockSpec((tm, d), lambda i: (i, 0)),
            _mod_spec(shift, tpb),
            _mod_spec(scale, tpb),
            pl.BlockSpec((d, n), lambda i: (0, 0)),
        ],
        out_specs=[pl.BlockSpec((tm, s), lambda i: (i, 0)) for s in splits],
        compiler_params=_cparams("parallel"),
        name="modmm",
    )(x, shift, scale, w)


def _mmres_kernel(*refs, n_sum, has_mul):
    a = refs[0][...]
    for a_ref in refs[1:n_sum]:
        a = a + a_ref[...]
    refs = refs[n_sum:]
    if has_mul:
        a = a.astype(F32) * refs[0][...].astype(F32)
        refs = refs[1:]
    w_ref, x_ref, g_ref, o_ref = refs
    o_ref[...] = x_ref[...] + g_ref[0] * _dot(a.astype(BF16), w_ref[...])


def mm_residual(a, w, x, gate, n_tok, b=None, tm=512):
    a = a if isinstance(a, (tuple, list)) else (a,)
    r, k = a[0].shape
    n = w.shape[1]
    tm = _row_tile(n_tok, tm)
    tpb = n_tok // tm
    ins = list(a) + ([b] if b is not None else []) + [w, x, gate]
    specs = [pl.BlockSpec((tm, k), lambda i: (i, 0))] * (len(a) + (b is not None))
    specs += [
        pl.BlockSpec((k, n), lambda i: (0, 0)),
        pl.BlockSpec((tm, n), lambda i: (i, 0)),
        _mod_spec(gate, tpb),
    ]
    return pl.pallas_call(
        functools.partial(_mmres_kernel, n_sum=len(a), has_mul=b is not None),
        out_shape=jax.ShapeDtypeStruct((r, n), F32),
        grid=(r // tm,),
        in_specs=specs,
        out_specs=pl.BlockSpec((tm, n), lambda i: (i, 0)),
        compiler_params=_cparams("parallel"),
        name="mm_residual",
    )(*ins)


def _pool_kernel(xp_ref, x_ref, xn_ref, sh_ref, sc_ref, g_ref, w_ref, ps_ref, o_ref, *, tm, n_tok, tpb):
    t0 = (pl.program_id(0) % tpb) * tm
    x = x_ref[...]
    rows = tm + 2 * POOL_HALO
    h = jnp.concatenate([
        jnp.where(t0 > 0, _modulate(xp_ref[...], sh_ref[0], sc_ref[0]), 0.0),
        _modulate(x, sh_ref[0], sc_ref[0]),
        jnp.where(t0 + tm < n_tok, _modulate(xn_ref[...], sh_ref[0], sc_ref[0]), 0.0)], axis=0)
    posc = lax.broadcasted_iota(jnp.int32, (tm, 1), 0) + t0
    gw = h.shape[1] // len(POOL_WINDOWS)
    outs = []
    for g, win in enumerate(POOL_WINDOWS):
        hg = h[:, g * gw:(g + 1) * gw]
        c = hg + pltpu.roll(hg, 1, 0)
        step = 1
        while 2 * step < win:
            c = pltpu.roll(c, step, 0) + pltpu.roll(c, rows - step, 0)
            step *= 2
        cnt = jnp.minimum(posc + (win - win // 2), n_tok) - jnp.maximum(posc - win // 2, 0)
        pooled = c[POOL_HALO:POOL_HALO + tm] / cnt.astype(F32) - hg[POOL_HALO:POOL_HALO + tm]
        outs.append(_dot(pooled.astype(BF16), w_ref[g]))
    y = jnp.concatenate(outs, axis=1) * ps_ref[...]
    o_ref[...] = x + g_ref[0] * y


def pool_mixer_residual(x, shift, scale, gate, w_pool, pool_scale, n_tok, tm=512):
    r, d = x.shape
    tm = _row_tile(n_tok, tm)
    tpb = n_tok // tm
    hb = tm // POOL_HALO
    last = r // POOL_HALO - 1
    groups, gw, _ = w_pool.shape
    return pl.pallas_call(
        functools.partial(_pool_kernel, tm=tm, n_tok=n_tok, tpb=tpb),
        out_shape=jax.ShapeDtypeStruct((r, d), F32),
        grid=(r // tm,),
        in_specs=[
            pl.BlockSpec((POOL_HALO, d), lambda i: (jnp.maximum(i * hb - 1, 0), 0)),
            pl.BlockSpec((tm, d), lambda i: (i, 0)),
            pl.BlockSpec((POOL_HALO, d), lambda i: (jnp.minimum((i + 1) * hb, last), 0)),
            _mod_spec(shift, tpb),
            _mod_spec(scale, tpb),
            _mod_spec(gate, tpb),
            pl.BlockSpec((groups, gw, gw), lambda i: (0, 0, 0)),
            pl.BlockSpec((1, d), lambda i: (0, 0)),
        ],
        out_specs=pl.BlockSpec((tm, d), lambda i: (i, 0)),
        compiler_params=_cparams("parallel"),
        name="pool_mixer",
    )(x, x, x, shift, scale, gate, w_pool, pool_scale.reshape(1, d))


def _router_kernel(x_ref, sh_ref, sc_ref, wr_ref, h_ref, aff_ref):
    h = _modulate(x_ref[...], sh_ref[0], sc_ref[0])
    h_ref[...] = h.astype(h_ref.dtype)
    logits = _dot3(h, wr_ref[...])
    lane = lax.broadcasted_iota(jnp.int32, logits.shape, 1)
    logits = jnp.where(lane < N_EXPERTS, logits, -jnp.inf)
    e = jnp.exp(logits - jnp.max(logits, axis=-1, keepdims=True))
    aff_ref[...] = e / jnp.sum(e, axis=-1, keepdims=True)


def router(x, shift, scale, w_router_padded, n_tok, tm=512):
    r, d = x.shape
    tm = _row_tile(n_tok, tm)
    tpb = n_tok // tm
    return pl.pallas_call(
        _router_kernel,
        out_shape=[jax.ShapeDtypeStruct((r, d), BF16), jax.ShapeDtypeStruct((r, LANES), F32)],
        grid=(r // tm,),
        in_specs=[
            pl.BlockSpec((tm, d), lambda i: (i, 0)),
            _mod_spec(shift, tpb),
            _mod_spec(scale, tpb),
            pl.BlockSpec((d, LANES), lambda i: (0, 0)),
        ],
        out_specs=[pl.BlockSpec((tm, d), lambda i: (i, 0)), pl.BlockSpec((tm, LANES), lambda i: (i, 0))],
        compiler_params=_cparams("parallel"),
        name="moe_router",
    )(x, shift, scale, w_router_padded)


MOE_CHUNK = 128
EXPERT_GROUPS = 2
COMBINE_CHUNKS = 2
NARROW_WINDOW = 64
PACK = LANES // N_EXPERTS


def _cumsum_rows(v, tb):
    n = v.shape[0]
    tri = (lax.broadcasted_iota(jnp.int32, (tb, tb), 0) >= lax.broadcasted_iota(jnp.int32, (tb, tb), 1)).astype(BF16)
    carry = jnp.zeros((1, v.shape[1]), F32)
    outs, starts = [], []
    for j in range(n // tb):
        starts.append(carry)
        c = _dot(tri, v[j * tb:(j + 1) * tb].astype(BF16)) + carry
        carry = c[tb - 1:tb, :]
        outs.append(c)
    return jnp.concatenate(outs, axis=0), jnp.concatenate(starts, axis=0)


def _route_kernel(aff_ref, affp_ref, idx_ref, gate_ref, slot_ref, start_ref, slot_scr, acc_scr, *, n, cap):
    def enough(cand):
        cnt = jnp.sum((affp_ref[...] >= pltpu.bitcast(cand, F32)).astype(F32), axis=0, keepdims=True)
        shift = N_EXPERTS
        while shift < LANES:
            cnt = cnt + pltpu.roll(cnt, shift, 1)
            shift *= 2
        return cnt >= cap

    def two_bits(k, prefix):
        low = 28 - 2 * k
        c1, c2, c3 = (prefix | jnp.left_shift(jnp.int32(j), low) for j in (1, 2, 3))
        return jnp.where(enough(c3), c3, jnp.where(enough(c2), c2, jnp.where(enough(c1), c1, prefix)))

    top = jnp.full((1, LANES), 1 << 30, jnp.int32)
    thr = jnp.where(enough(top), top, jnp.zeros_like(top))
    thr = pltpu.bitcast(lax.fori_loop(0, 15, two_bits, thr), F32)
    aff = aff_ref[...]
    gt = aff > thr
    eq = aff == thr
    need = cap - jnp.sum(gt.astype(F32), axis=0, keepdims=True)
    eq_rank, _ = _cumsum_rows(eq.astype(F32), MOE_CHUNK)
    sel = gt | (eq & (eq_rank <= need))
    pos, starts = _cumsum_rows(sel.astype(F32), MOE_CHUNK)
    slot_scr[...] = jnp.where(sel, pos - 1.0, -1.0)
    slot_ref[...] = slot_scr[...]
    start_ref[0] = starts.astype(jnp.int32)

    slots = lax.broadcasted_iota(jnp.int32, (1, cap), 1).astype(F32)
    sub = lax.broadcasted_iota(jnp.int32, (SUBLANES, MOE_CHUNK), 0)
    local = lax.broadcasted_iota(jnp.int32, (SUBLANES, MOE_CHUNK), 1).astype(F32)
    acc_scr[...] = jnp.zeros(acc_scr.shape, F32)

    def chunk(c, carry):
        rows = pl.ds(pl.multiple_of(c * MOE_CHUNK, MOE_CHUNK), MOE_CHUNK)
        base = jnp.where(sub == 0, local, jnp.where(sub == 1, lax.convert_element_type(c, F32), 0.0))
        aff_t = aff_ref[rows, :].T
        for e in range(N_EXPERTS):
            onehot = jnp.where(slot_scr[rows, e:e + 1] == slots, 1.0, 0.0).astype(BF16)
            g = aff_t[e:e + 1, :]
            g_hi = g.astype(BF16).astype(F32)
            g_mid = (g - g_hi).astype(BF16).astype(F32)
            g_lo = g - g_hi - g_mid
            lhs = jnp.where(sub == 2, g_hi, jnp.where(sub == 3, g_mid, jnp.where(sub == 4, g_lo, base)))
            acc_scr[e] += _dot(lhs.astype(BF16), onehot)
        return carry

    lax.fori_loop(0, n // MOE_CHUNK, chunk, 0)
    for e in range(N_EXPERTS):
        idx_ref[0, e:e + 1, :] = (acc_scr[e, 0:1, :] + MOE_CHUNK * acc_scr[e, 1:2, :]).astype(jnp.int32)
        gate_ref[0, e:e + 1, :] = acc_scr[e, 2:3, :] + acc_scr[e, 3:4, :] + acc_scr[e, 4:5, :]


def route(aff, n_tok):
    r = aff.shape[0]
    b = r // n_tok
    cap = CAPACITY_FACTOR * n_tok // N_EXPERTS
    aff_packed = aff[:, :N_EXPERTS].reshape(r // PACK, LANES)
    n_chunks = n_tok // MOE_CHUNK
    return pl.pallas_call(
        functools.partial(_route_kernel, n=n_tok, cap=cap),
        out_shape=[jax.ShapeDtypeStruct((b, N_EXPERTS, cap), jnp.int32), jax.ShapeDtypeStruct((b, N_EXPERTS, cap), F32),
                   jax.ShapeDtypeStruct((r, LANES), F32), jax.ShapeDtypeStruct((b, n_chunks, LANES), jnp.int32)],
        grid=(b,),
        in_specs=[pl.BlockSpec((n_tok, LANES), lambda i: (i, 0)),
                  pl.BlockSpec((n_tok // PACK, LANES), lambda i: (i, 0))],
        out_specs=[pl.BlockSpec((1, N_EXPERTS, cap), lambda i: (i, 0, 0)),
                   pl.BlockSpec((1, N_EXPERTS, cap), lambda i: (i, 0, 0)),
                   pl.BlockSpec((n_tok, LANES), lambda i: (i, 0)),
                   pl.BlockSpec((1, n_chunks, LANES), lambda i: (i, 0, 0))],
        scratch_shapes=[pltpu.VMEM((n_tok, LANES), F32), pltpu.VMEM((N_EXPERTS, SUBLANES, cap), F32)],
        compiler_params=_cparams("parallel"),
        name="moe_route",
    )(aff, aff_packed)


def _ffn_kernel(doff_ref, x_ref, gt_ref, wg_ref, wu_ref, wd_ref, y_ref, dup_ref, wg_scr, wu_scr, wd_scr, y_scr, *,
                e0, cap, n_steps, dwin):
    @pl.when(pl.program_id(1) == 0)
    def _():
        wg_scr[...] = wg_ref[0, 0].astype(BF16)
        wu_scr[...] = wu_ref[0, 0].astype(BF16)
        wd_scr[...] = wd_ref[0, 0].astype(BF16)

    x = x_ref[0]
    a = _dot(x, wg_scr[...])
    u = _dot(x, wu_scr[...])
    hmid = (_silu(a) * u).astype(BF16)
    gate_cols = jnp.broadcast_to(gt_ref[0, 0], (LANES, x.shape[0])).T
    y = _dot(hmid, wd_scr[...])
    y_scr[...] = (y * jnp.concatenate([gate_cols] * (y.shape[1] // LANES), axis=1)).astype(BF16)
    y_ref[0] = y_scr[...]
    e = e0 + pl.program_id(0)
    align = min(dwin, BF16_SUBLANES)
    for bl in range(y_scr.shape[0] // cap):
        b = pl.program_id(1) * (y_scr.shape[0] // cap) + bl
        for s in range(n_steps):
            off = pl.multiple_of(doff_ref[(b * n_steps + s) * N_EXPERTS + e], align)
            dup_ref[bl, s, 0] = y_scr[pl.ds(bl * cap + off, dwin), :]


def expert_ffn(xs, gates, doffs, w_gate, w_up, w_down, layer, e0, cap, n_steps, dwin, tm=512):
    e, m, d = xs.shape
    ff = w_gate.shape[-1]
    tm = max(_row_tile(m, tm), cap)
    bpt = tm // cap
    return pl.pallas_call(
        functools.partial(_ffn_kernel, e0=e0, cap=cap, n_steps=n_steps, dwin=dwin),
        out_shape=[jax.ShapeDtypeStruct((e, m, d), BF16),
                   jax.ShapeDtypeStruct((m // cap, n_steps, e, dwin, d), BF16)],
        grid_spec=pltpu.PrefetchScalarGridSpec(
            num_scalar_prefetch=1,
            grid=(e, m // tm),
            in_specs=[
                pl.BlockSpec((1, tm, d), lambda k, i, off: (k, i, 0)),
                pl.BlockSpec((1, 1, 1, tm), lambda k, i, off: (k, i, 0, 0)),
                pl.BlockSpec((1, 1, d, ff), lambda k, i, off: (layer, e0 + k, 0, 0)),
                pl.BlockSpec((1, 1, d, ff), lambda k, i, off: (layer, e0 + k, 0, 0)),
                pl.BlockSpec((1, 1, ff, d), lambda k, i, off: (layer, e0 + k, 0, 0)),
            ],
            out_specs=[pl.BlockSpec((1, tm, d), lambda k, i, off: (k, i, 0)),
                       pl.BlockSpec((bpt, n_steps, 1, dwin, d), lambda k, i, off: (i, 0, k, 0, 0))],
            scratch_shapes=[pltpu.VMEM((d, ff), BF16), pltpu.VMEM((d, ff), BF16), pltpu.VMEM((ff, d), BF16),
                            pltpu.VMEM((tm, d), BF16)],
        ),
        compiler_params=_cparams("parallel", "arbitrary"),
        name="moe_expert_ffn",
    )(doffs, xs, gates.reshape(e, m // tm, 1, tm), w_gate, w_up, w_down)


def _combine_kernel(woff_ref, doff_ref, fit_ref, ysel_ref, x_ref, g_ref, slot_ref, *rest, win, dwin, n_chunks, cps,
                    n_groups, final):
    dup_refs, y_refs, rest = rest[:n_groups], rest[n_groups:2 * n_groups], rest[2 * n_groups:]
    if final:
        fg_ref, o_ref = rest
    else:
        (o_ref,) = rest
    per_group = N_EXPERTS // n_groups
    step = pl.program_id(0) * (n_chunks // cps) + pl.program_id(1)

    def finish(rows, acc):
        out = x_ref[rows, :] + g_ref[0] * acc
        if final:
            out = out * lax.rsqrt(jnp.mean(out * out, axis=-1, keepdims=True) + EPS) * fg_ref[...]
        o_ref[rows, :] = out

    @pl.when(fit_ref[step] == 1)
    def _():
        tm = x_ref.shape[0]
        epl = LANES // dwin
        lane = lax.broadcasted_iota(jnp.int32, (1, LANES), 1)
        which = lane // dwin
        within = (lane % dwin).astype(F32)
        acc = jnp.zeros((tm, x_ref.shape[1]), F32)
        for gi, dup_ref in enumerate(dup_refs):
            pieces = []
            for p in range(per_group // epl):
                col, sl = None, None
                for w in range(epl):
                    e = gi * per_group + p * epl + w
                    cw = lax.convert_element_type(doff_ref[step * N_EXPERTS + e], F32) + within
                    sw = jnp.broadcast_to(slot_ref[:, e:e + 1], (tm, LANES))
                    col = cw if w == 0 else jnp.where(which == w, cw, col)
                    sl = sw if w == 0 else jnp.where(which == w, sw, sl)
                pieces.append(jnp.where(sl == col, 1.0, 0.0).astype(BF16))
            onehot = jnp.concatenate(pieces, axis=1)
            acc = acc + _dot(onehot, dup_ref[0, 0].reshape(per_group * dwin, x_ref.shape[1]))
        finish(slice(0, tm), acc)

    @pl.when(fit_ref[step] == 0)
    def _():
        align = min(win, LANES)
        for sc in range(cps):
            rows = slice(sc * MOE_CHUNK, (sc + 1) * MOE_CHUNK)
            base = (step * cps + sc) * N_EXPERTS
            acc = jnp.zeros((MOE_CHUNK, x_ref.shape[1]), F32)
            for e in range(N_EXPERTS):
                off = pl.multiple_of(woff_ref[base + e], align)
                cols = (lax.broadcasted_iota(jnp.int32, (1, win), 1) + off).astype(F32)
                onehot = jnp.where(slot_ref[rows, e:e + 1] == cols, 1.0, 0.0).astype(BF16)
                acc = acc + _dot(onehot, y_refs[e // per_group][e % per_group, pl.ds(off, win), :])
            finish(rows, acc)


def moe_combine(x, gate2, slot, ys, dups, woffs, doffs, fits, n_tok, final_gain=None):
    r, d = x.shape
    b = r // n_tok
    per_group = ys[0].shape[0]
    cap = ys[0].shape[1] // b
    dwin = dups[0].shape[3]
    n_chunks = n_tok // MOE_CHUNK
    cps = COMBINE_CHUNKS if n_chunks % COMBINE_CHUNKS == 0 else 1
    tm = cps * MOE_CHUNK
    steps = n_chunks // cps
    win = min(2 * MOE_CHUNK, cap)
    final = final_gain is not None
    need = jnp.max((1 - fits).reshape(b, steps), axis=1)
    ysel = lax.cummax(need * jnp.arange(b, dtype=jnp.int32))
    row_spec = lambda w: pl.BlockSpec((tm, w), lambda bi, c, *_: (bi * steps + c, 0))
    ins = [woffs, doffs, fits, ysel, x, gate2, slot] + list(dups) + list(ys)
    specs = [row_spec(d), _mod_spec(gate2, 1), row_spec(LANES)]
    specs += [pl.BlockSpec((1, 1, per_group, dwin, d), lambda bi, c, *_: (bi, c, 0, 0, 0))] * len(dups)
    specs += [pl.BlockSpec((per_group, cap, d), lambda bi, c, wo, do, ft, ys_: (0, ys_[bi], 0),
                           pipeline_mode=pl.Buffered(1))] * len(ys)
    if final:
        ins.append(final_gain.reshape(1, d))
        specs.append(pl.BlockSpec((1, d), lambda bi, c, *_: (0, 0)))
    return pl.pallas_call(
        functools.partial(_combine_kernel, win=win, dwin=dwin, n_chunks=n_chunks, cps=cps, n_groups=len(ys),
                          final=final),
        out_shape=jax.ShapeDtypeStruct((r, d), F32),
        grid_spec=pltpu.PrefetchScalarGridSpec(
            num_scalar_prefetch=4,
            grid=(b, steps),
            in_specs=specs,
            out_specs=row_spec(d),
        ),
        compiler_params=_cparams("parallel", "arbitrary"),
        name="moe_combine",
    )(*ins)


def moe_residual(x, shift, scale, gate2, w_router_padded, w_gate, w_up, w_down, layer, n_tok, final_gain=None):
    r, d = x.shape
    b = r // n_tok
    h, aff = router(x, shift, scale, w_router_padded, n_tok)
    idx, gates, slot, start = route(aff, n_tok)
    cap = idx.shape[-1]
    n_chunks = n_tok // MOE_CHUNK
    cps = COMBINE_CHUNKS if n_chunks % COMBINE_CHUNKS == 0 else 1
    steps = n_chunks // cps
    start = start[:, :, :N_EXPERTS]
    win = min(2 * MOE_CHUNK, cap)
    align = min(win, LANES)
    woffs = jnp.clip(start // align * align, 0, cap - win).reshape(-1)
    dwin = min(NARROW_WINDOW, cap)
    dalign = min(dwin, BF16_SUBLANES)
    s_start = start[:, ::cps]
    s_end = jnp.concatenate([s_start[:, 1:], jnp.full((b, 1, N_EXPERTS), cap, jnp.int32)], axis=1)
    doffs = jnp.clip(s_start // dalign * dalign, 0, cap - dwin)
    fits = jnp.all(s_end - doffs <= dwin, axis=-1).astype(jnp.int32).reshape(-1)
    doffs = doffs.reshape(-1)

    gidx = idx + (jnp.arange(b, dtype=jnp.int32) * n_tok)[:, None, None]
    gidx = jnp.transpose(gidx, (1, 0, 2)).reshape(N_EXPERTS, b * cap)
    gts = jnp.transpose(gates, (1, 0, 2)).reshape(N_EXPERTS, b * cap)
    per_group = N_EXPERTS // EXPERT_GROUPS
    ys, dups = [], []
    for g in range(EXPERT_GROUPS):
        grp = slice(g * per_group, (g + 1) * per_group)
        xs = h.at[gidx[grp]].get(mode="promise_in_bounds")
        y, dup = expert_ffn(xs, gts[grp], doffs, w_gate, w_up, w_down, layer, g * per_group, cap, steps, dwin)
        ys.append(y)
        dups.append(dup)
    return moe_combine(x, gate2, slot, ys, dups, woffs, doffs, fits, n_tok, final_gain)


def _head_rms(t, seg_ones):
    outs = []
    for j in range(t.shape[1] // LANES):
        blk = t[:, j * LANES:(j + 1) * LANES]
        hi, lo = _split_bf16(blk * blk)
        ss = _dot(hi, seg_ones) + _dot(lo, seg_ones)
        outs.append(blk * lax.rsqrt(ss * (1.0 / HEAD_DIM) + EPS))
    return jnp.concatenate(outs, axis=1)


def _rope(t, cos, sin_signed):
    w = t.shape[1]
    half = HEAD_DIM // 2
    lane = lax.broadcasted_iota(jnp.int32, t.shape, 1)
    partner = jnp.where(lane % HEAD_DIM < half, pltpu.roll(t, w - half, 1), pltpu.roll(t, half, 1))
    reps = w // cos.shape[1]
    return t * jnp.concatenate([cos] * reps, axis=1) + partner * jnp.concatenate([sin_signed] * reps, axis=1)


def _qkv_kernel(*refs, qd, kd, rope):
    if rope:
        x_ref, sh_ref, sc_ref, w_ref, qg_ref, kg_ref, cos_ref, sin_ref, q_ref, k_ref, v_ref = refs
    else:
        x_ref, sh_ref, sc_ref, w_ref, qg_ref, kg_ref, q_ref, k_ref, v_ref = refs
    h = _modulate(x_ref[...], sh_ref[0], sc_ref[0]).astype(BF16)
    z = _dot(h, w_ref[...])
    seg = (lax.broadcasted_iota(jnp.int32, (LANES, LANES), 0) // HEAD_DIM
           == lax.broadcasted_iota(jnp.int32, (LANES, LANES), 1) // HEAD_DIM).astype(BF16)
    q = _head_rms(z[:, :qd], seg) * qg_ref[...]
    k = _head_rms(z[:, qd:qd + kd], seg) * kg_ref[...]
    if rope:
        q = _rope(q, cos_ref[...], sin_ref[...])
        k = _rope(k, cos_ref[...], sin_ref[...])
    q_ref[...] = (q * (HEAD_DIM ** -0.5 * LOG2E)).astype(q_ref.dtype)
    k_ref[...] = k.astype(k_ref.dtype)
    v_ref[...] = z[:, qd + kd:].astype(v_ref.dtype)


def qkv_project(x, shift, scale, w_qkv, q_gain, k_gain, n_tok, rope_tables=None, tm=512):
    r, d = x.shape
    kd = N_KV_HEADS * HEAD_DIM
    qd = w_qkv.shape[1] - 2 * kd
    tm = _row_tile(n_tok, tm)
    tpb = n_tok // tm
    ins = [x, shift, scale, w_qkv, q_gain, k_gain]
    specs = [
        pl.BlockSpec((tm, d), lambda i: (i, 0)),
        _mod_spec(shift, tpb),
        _mod_spec(scale, tpb),
        pl.BlockSpec(w_qkv.shape, lambda i: (0, 0)),
        pl.BlockSpec((1, qd), lambda i: (0, 0)),
        pl.BlockSpec((1, kd), lambda i: (0, 0)),
    ]
    if rope_tables is not None:
        ins += list(rope_tables)
        specs += [pl.BlockSpec((tm, LANES), lambda i: (i % tpb, 0))] * 2
    return pl.pallas_call(
        functools.partial(_qkv_kernel, qd=qd, kd=kd, rope=rope_tables is not None),
        out_shape=[jax.ShapeDtypeStruct((r, qd), BF16), jax.ShapeDtypeStruct((r, kd), BF16),
                   jax.ShapeDtypeStruct((r, kd), BF16)],
        grid=(r // tm,),
        in_specs=specs,
        out_specs=[pl.BlockSpec((tm, qd), lambda i: (i, 0)), pl.BlockSpec((tm, kd), lambda i: (i, 0)),
                   pl.BlockSpec((tm, kd), lambda i: (i, 0))],
        compiler_params=_cparams("parallel"),
        name="qkv_project",
    )(*ins)


def _attn_kernel(q_ref, k_ref, vt_ref, o_ref, *scr, grp, bounded):
    k = k_ref[0, 0]
    vt = vt_ref[0, 0]
    tq = q_ref.shape[0]
    n_pairs = grp // 2
    outs = []

    def scores(pair):
        heads = (2 * pair, 2 * pair + 1)
        q2 = jnp.concatenate([q_ref[:, h * HEAD_DIM:(h + 1) * HEAD_DIM] for h in heads], axis=0)
        return lax.dot_general(k, q2, (((1,), (1,)), ((), ())), preferred_element_type=F32)

    if bounded:
        for pair in range(n_pairs):
            p = jnp.exp2(scores(pair))
            scr[pair][...] = p.astype(BF16)
            ot = _dot(vt, scr[pair][...]) / jnp.sum(p, axis=0, keepdims=True)
            o = jnp.concatenate([ot, ot], axis=0).T
            outs += [o[:tq, :HEAD_DIM], o[tq:, :HEAD_DIM]]
        p_refs = ()
    else:
        for pair in range(n_pairs):
            scr[2 * pair][...] = scores(pair)
        for pair in range(n_pairs):
            st_scr, p_scr = scr[2 * pair], scr[2 * pair + 1]
            p_scr[...] = jnp.exp2(st_scr[...] - jnp.max(st_scr[...], axis=0, keepdims=True)).astype(BF16)
        p_refs = scr[1::2]
    for p_scr in p_refs:
        ot = _dot(vt, p_scr[...])
        ot = ot[:HEAD_DIM] / ot[HEAD_DIM:]
        o = jnp.concatenate([ot, ot], axis=0).T
        outs += [o[:tq, :HEAD_DIM], o[tq:, :HEAD_DIM]]
    o_ref[...] = jnp.concatenate(outs, axis=1).astype(o_ref.dtype)


def _attention_call(q, k, vt, n_q, tq, bounded):
    r, qd = q.shape
    _, kvh, n_k, hd = k.shape
    grp = qd // (kvh * hd)
    tq = _row_tile(n_q, tq)
    tpb = n_q // tq
    dts = (BF16,) if bounded else (F32, BF16)
    if not bounded:
        vt = jnp.concatenate([vt, jnp.ones_like(vt)], axis=2)
    return pl.pallas_call(
        functools.partial(_attn_kernel, grp=grp, bounded=bounded),
        out_shape=jax.ShapeDtypeStruct((r, qd), BF16),
        grid=(r // n_q, kvh, tpb),
        in_specs=[
            pl.BlockSpec((tq, grp * hd), lambda b, g, i: (b * tpb + i, g)),
            pl.BlockSpec((1, 1, n_k, hd), lambda b, g, i: (b, g, 0, 0)),
            pl.BlockSpec((1, 1, vt.shape[2], n_k), lambda b, g, i: (b, g, 0, 0)),
        ],
        out_specs=pl.BlockSpec((tq, grp * hd), lambda b, g, i: (b * tpb + i, g)),
        scratch_shapes=[pltpu.VMEM((n_k, 2 * tq), dt) for _ in range(grp // 2) for dt in dts],
        compiler_params=_cparams("parallel", "parallel", "parallel"),
        name="attention_bounded" if bounded else "attention",
    )(q, k, vt)


SCORE_BOUND = 60.0


def attention(q, k, vt, n_q, score_bound, tq_bounded=512, tq_exact=256):
    return lax.cond(score_bound <= SCORE_BOUND,
                    lambda: _attention_call(q, k, vt, n_q, tq_bounded, True),
                    lambda: _attention_call(q, k, vt, n_q, tq_exact, False))


def rope_tables(n_tok):
    t = jnp.arange(n_tok)
    row = (t // GRID_W).astype(F32)
    col = (t % GRID_W).astype(F32)
    n_freq = HEAD_DIM // 4
    inv = ROPE_THETA ** (-jnp.arange(n_freq, dtype=F32) / n_freq)
    ang = jnp.concatenate([row[:, None] * inv, col[:, None] * inv], axis=-1)
    cos, sin = jnp.cos(ang), jnp.sin(ang)
    reps = LANES // HEAD_DIM
    return jnp.tile(jnp.concatenate([cos, cos], -1), (1, reps)), jnp.tile(jnp.concatenate([-sin, sin], -1), (1, reps))


def _deinterleave_heads(w, n_heads):
    lead = w.shape[:-1]
    w = w.reshape(lead + (n_heads, HEAD_DIM // 2, 2))
    return jnp.swapaxes(w, -1, -2).reshape(lead + (n_heads * HEAD_DIM,))


def _split_kv_heads(t, b):
    return jnp.transpose(t.reshape(b, -1, N_KV_HEADS, HEAD_DIM), (0, 2, 1, 3))


def gqa_residual(x, xc, mods_l, mods_c, w_qkv, q_gain, k_gain, w_o, n_tok, n_ctx):
    b = x.shape[0] // n_tok
    kd = N_KV_HEADS * HEAD_DIM
    qd = w_qkv.shape[1] - 2 * kd
    n_qh = qd // HEAD_DIM
    w_perm = jnp.concatenate([_deinterleave_heads(w_qkv[:, :qd], n_qh),
                              _deinterleave_heads(w_qkv[:, qd:qd + kd], N_KV_HEADS), w_qkv[:, qd + kd:]], axis=1)
    w_perm = w_perm.astype(BF16)
    qg = jnp.tile(_deinterleave_heads(q_gain, 1), n_qh).reshape(1, qd)
    kg = jnp.tile(_deinterleave_heads(k_gain, 1), N_KV_HEADS).reshape(1, kd)
    sh_l, sc_l, g_l = mods_l
    sh_c, sc_c, g_c = mods_c
    q_l, k_l, v_l = qkv_project(x, sh_l, sc_l, w_perm, qg, kg, n_tok, rope_tables(n_tok))
    q_c, k_c, v_c = qkv_project(xc, sh_c, sc_c, w_perm, qg, kg, n_ctx)
    k_c4 = _split_kv_heads(k_c, b)
    k_all = jnp.concatenate([k_c4, _split_kv_heads(k_l, b)], axis=2)

    def values_t(v):
        return jnp.transpose(v.reshape(b, -1, N_KV_HEADS, HEAD_DIM), (0, 2, 3, 1))

    vt_c = values_t(v_c)
    vt_all = jnp.concatenate([vt_c, values_t(v_l)], axis=3)
    score_bound = (1.01 * HEAD_DIM * HEAD_DIM ** -0.5 * LOG2E
                   * jnp.max(jnp.abs(q_gain)) * jnp.max(jnp.abs(k_gain)))
    o_l = attention(q_l, k_all, vt_all, n_tok, score_bound)
    o_c = attention(q_c, k_c4, vt_c, n_ctx, score_bound)
    w_o = w_o.astype(BF16)
    return mm_residual(o_l, w_o, x, g_l, n_tok), mm_residual(o_c, w_o, xc, g_c, n_ctx)


def _lru_kernel(xp_ref, x_ref, xn_ref, cw_ref, cb_ref, wa_ref, ba_ref, wx_ref, bx_ref, lam_ref, h0_ref,
                o_ref, hT_ref, a_scr, u_scr, carry_scr, *, tm, n_tok, tpb, reverse):
    step = pl.program_id(1)
    t_idx = (tpb - 1 - step) if reverse else step
    t0 = t_idx * tm
    rows = tm + 2 * SUBLANES

    @pl.when(step == 0)
    def _():
        carry_scr[...] = h0_ref[0]

    xe = jnp.concatenate([jnp.where(t0 > 0, xp_ref[...], 0.0), x_ref[...],
                          jnp.where(t0 + tm < n_tok, xn_ref[...], 0.0)], axis=0)
    left = CONV_W // 2
    conv = cb_ref[...]
    for k in range(CONV_W):
        shift = (left - k) % rows
        tap = xe if shift == 0 else pltpu.roll(xe, shift, 0)
        conv = conv + tap * cw_ref[k:k + 1, :]
    xr = conv[SUBLANES:SUBLANES + tm]

    xb = xr.astype(BF16)
    bw = xr.shape[1] // LRU_BLOCKS
    ra, ia = [], []
    for j in range(LRU_BLOCKS):
        blk = xb[:, j * bw:(j + 1) * bw]
        ra.append(_dot(blk, wa_ref[j]))
        ia.append(_dot(blk, wx_ref[j]))
    r = _sigmoid(jnp.concatenate(ra, axis=1) + ba_ref[...])
    i = _sigmoid(jnp.concatenate(ia, axis=1) + bx_ref[...])
    log_a = -LRU_C * r * jnp.logaddexp(-lam_ref[...], 0.0)
    a = jnp.exp(log_a)
    a_scr[...] = a
    t = jnp.tanh(log_a)
    u_scr[...] = xr * i * jnp.sqrt(-2.0 * t / (1.0 - t))

    n_grp = tm // SUBLANES
    sub = lax.broadcasted_iota(jnp.int32, (SUBLANES, 1), 0)

    def group(j, carry):
        g = (n_grp - 1 - j) if reverse else j
        rws = pl.ds(pl.multiple_of(g * SUBLANES, SUBLANES), SUBLANES)
        ag, ug = a_scr[rws, :], u_scr[rws, :]
        s = 1
        while s < SUBLANES:
            if reverse:
                ok = sub < SUBLANES - s
                sh = SUBLANES - s
            else:
                ok = sub >= s
                sh = s
            u_prev = jnp.where(ok, pltpu.roll(ug, sh, 0), 0.0)
            a_prev = jnp.where(ok, pltpu.roll(ag, sh, 0), 1.0)
            ug = ug + ag * u_prev
            ag = ag * a_prev
            s *= 2
        hg = ug + ag * carry
        o_ref[rws, :] = hg
        return hg[0:1, :] if reverse else hg[SUBLANES - 1:SUBLANES, :]

    carry = lax.fori_loop(0, n_grp, group, carry_scr[...])
    carry_scr[...] = carry
    hT_ref[0] = carry


def lru_scan(xpre, conv_w, conv_b, wa, ba, wx, bx, lam, h0, n_tok, reverse, tm=512):
    r, w = xpre.shape
    b = r // n_tok
    tm = _row_tile(n_tok, tm)
    tpb = n_tok // tm
    hb = tm // SUBLANES
    last = r // SUBLANES - 1

    def tile(bi, s):
        return bi * tpb + ((tpb - 1 - s) if reverse else s)

    vec = pl.BlockSpec((1, w), lambda bi, s: (0, 0))
    blocks = pl.BlockSpec(wa.shape, lambda bi, s: (0, 0, 0))
    return pl.pallas_call(
        functools.partial(_lru_kernel, tm=tm, n_tok=n_tok, tpb=tpb, reverse=reverse),
        out_shape=[jax.ShapeDtypeStruct((r, w), F32), jax.ShapeDtypeStruct((b, 1, w), F32)],
        grid=(b, tpb),
        in_specs=[
            pl.BlockSpec((SUBLANES, w), lambda bi, s: (jnp.maximum(tile(bi, s) * hb - 1, 0), 0)),
            pl.BlockSpec((tm, w), lambda bi, s: (tile(bi, s), 0)),
            pl.BlockSpec((SUBLANES, w), lambda bi, s: (jnp.minimum((tile(bi, s) + 1) * hb, last), 0)),
            pl.BlockSpec((CONV_W, w), lambda bi, s: (0, 0)),
            vec, blocks, vec, blocks, vec, vec,
            pl.BlockSpec((1, 1, w), lambda bi, s: (bi, 0, 0)),
        ],
        out_specs=[pl.BlockSpec((tm, w), lambda bi, s: (tile(bi, s), 0)),
                   pl.BlockSpec((1, 1, w), lambda bi, s: (bi, 0, 0))],
        scratch_shapes=[pltpu.VMEM((tm, w), F32), pltpu.VMEM((tm, w), F32), pltpu.VMEM((1, w), F32)],
        compiler_params=_cparams("parallel", "arbitrary"),
        name="lru_scan",
    )(xpre, xpre, xpre, conv_w, conv_b.reshape(1, w), wa, ba.reshape(1, w), wx, bx.reshape(1, w),
      lam.reshape(1, w), h0)


def rglru_residual(x, xc, mods_l, mods_c, w_in, conv_w, conv_b, wa, ba, wx, bx, lam, w_out, n_tok, n_ctx):
    b = x.shape[0] // n_tok
    w = w_in.shape[1] // 2
    w_in = w_in.astype(BF16)
    sh_l, sc_l, g_l = mods_l
    sh_c, sc_c, g_c = mods_c
    y_l, xp_l = modmm(x, sh_l, sc_l, w_in, n_tok, (w, w), ("gelu", None), (F32, F32))
    y_c, xp_c = modmm(xc, sh_c, sc_c, w_in, n_ctx, (w, w), ("gelu", None), (F32, F32))
    zero = jnp.zeros((b, 1, w), F32)
    hs_l, hs_c = [], []
    for d in range(2):
        gate_w = (conv_w, conv_b, wa[d].astype(BF16), ba[d], wx[d].astype(BF16), bx[d], lam[d])
        hc, state = lru_scan(xp_c, *gate_w, zero, n_ctx, reverse=d == 1)
        hl, _ = lru_scan(xp_l, *gate_w, state, n_tok, reverse=d == 1)
        hs_c.append(hc)
        hs_l.append(hl)
    w_out = w_out.astype(BF16)
    return (mm_residual(tuple(hs_l), w_out, x, g_l, n_tok, b=y_l),
            mm_residual(tuple(hs_c), w_out, xc, g_c, n_ctx, b=y_c))


def _gmlp_kernel(x_ref, sh_ref, sc_ref, g_ref, win_ref, lng_ref, lnb_ref, ws_ref, bs_ref, wout_ref, o_ref, *, half):
    x = x_ref[...]
    h = _modulate(x, sh_ref[0], sc_ref[0]).astype(BF16)
    u = _gelu(_dot(h, win_ref[:, :half]))
    v = _gelu(_dot(h, win_ref[:, half:]))
    mu = jnp.mean(v, axis=-1, keepdims=True)
    vc = v - mu
    var = jnp.mean(vc * vc, axis=-1, keepdims=True)
    vn = (vc * lax.rsqrt(var + EPS) * lng_ref[...] + lnb_ref[...]).astype(BF16)
    gw = half // GMLP_GROUPS
    chunks = []
    for c in range(x.shape[0] // GMLP_CHUNK):
        rws = slice(c * GMLP_CHUNK, (c + 1) * GMLP_CHUNK)
        chunks.append(jnp.concatenate(
            [_dot(ws_ref[g], vn[rws, g * gw:(g + 1) * gw]) + bs_ref[g] for g in range(GMLP_GROUPS)], axis=1))
    v2 = jnp.concatenate(chunks, axis=0)
    o_ref[...] = x + g_ref[0] * _dot((u * v2).astype(BF16), wout_ref[...])


def gmlp_residual(x, shift, scale, gate, w_in, ln_g, ln_b, w_s, b_s, w_out, n_tok, tm=256):
    r, d = x.shape
    half = w_in.shape[1] // 2
    tm = _row_tile(n_tok, tm)
    tpb = n_tok // tm
    assert tm % GMLP_CHUNK == 0
    const2 = lambda i: (0, 0)
    const3 = lambda i: (0, 0, 0)
    once = pl.Buffered(1)
    return pl.pallas_call(
        functools.partial(_gmlp_kernel, half=half),
        out_shape=jax.ShapeDtypeStruct((r, d), F32),
        grid=(r // tm,),
        in_specs=[
            pl.BlockSpec((tm, d), lambda i: (i, 0)),
            _mod_spec(shift, tpb),
            _mod_spec(scale, tpb),
            _mod_spec(gate, tpb),
            pl.BlockSpec(w_in.shape, const2, pipeline_mode=once),
            pl.BlockSpec((1, half), const2),
            pl.BlockSpec((1, half), const2),
            pl.BlockSpec(w_s.shape, const3),
            pl.BlockSpec(b_s.shape + (1,), const3),
            pl.BlockSpec(w_out.shape, const2, pipeline_mode=once),
        ],
        out_specs=pl.BlockSpec((tm, d), lambda i: (i, 0)),
        compiler_params=_cparams("parallel"),
        name="gmlp_mixer",
    )(x, shift, scale, gate, w_in.astype(BF16), ln_g.reshape(1, half), ln_b.reshape(1, half),
      w_s.astype(BF16), b_s[..., None], w_out.astype(BF16))


def kernel(x, c, ctx, c_ctx, mod_w, mod_b, pool_w, pool_scale, lru_w_in, lru_conv_w, lru_conv_b, lru_wa, lru_ba,
           lru_wx, lru_bx, lru_lam, lru_w_out, attn_w_qkv, attn_q_gain, attn_k_gain, attn_w_o, gmlp_w_in, gmlp_ln_g,
           gmlp_ln_b, gmlp_w_s, gmlp_b_s, gmlp_w_out, moe_router, moe_w_gate, moe_w_up, moe_w_down, final_gain):
    b, n_tok, d = x.shape
    n_ctx = ctx.shape[1]
    depth = mod_w.shape[0]
    n_mixers = 4
    xl = x.reshape(b * n_tok, d)
    xc = ctx.reshape(b * n_ctx, d)

    pad = -(b + 1) % SUBLANES
    cvec = jnp.concatenate([c, c_ctx[None, :], jnp.zeros((pad, d), F32)], axis=0)
    mods = mod_vectors(cvec, mod_w, mod_b)

    for i in range(depth):
        m, j = i % n_mixers, i // n_mixers
        last = i == depth - 1
        ml = [mods[i, :b, k * d:(k + 1) * d].reshape(b, 1, d) for k in range(6)]
        mc = [mods[i, b:b + 1, k * d:(k + 1) * d].reshape(1, 1, d) for k in range(6)]
        if m == 0:
            pw = pool_w[j].astype(BF16)
            xl_new = pool_mixer_residual(xl, ml[0], ml[1], ml[2], pw, pool_scale[j], n_tok)
            if not last:
                xc = pool_mixer_residual(xc, mc[0], mc[1], mc[2], pw, pool_scale[j], n_ctx)
            xl = xl_new
        elif m == 1:
            xl, xc_new = rglru_residual(xl, xc, ml[:3], mc[:3], lru_w_in[j], lru_conv_w[j], lru_conv_b[j], lru_wa[j],
                                        lru_ba[j], lru_wx[j], lru_bx[j], lru_lam[j], lru_w_out[j], n_tok, n_ctx)
            xc = xc if last else xc_new
        elif m == 2:
            xl, xc_new = gqa_residual(xl, xc, ml[:3], mc[:3], attn_w_qkv[j], attn_q_gain[j], attn_k_gain[j],
                                      attn_w_o[j], n_tok, n_ctx)
            xc = xc if last else xc_new
        else:
            gargs = (gmlp_w_in[j], gmlp_ln_g[j], gmlp_ln_b[j], gmlp_w_s[j], gmlp_b_s[j], gmlp_w_out[j])
            xl_new = gmlp_residual(xl, ml[0], ml[1], ml[2], *gargs, n_tok)
            if not last:
                xc = gmlp_residual(xc, mc[0], mc[1], mc[2], *gargs, n_ctx)
            xl = xl_new
        wr = jnp.pad(moe_router[i], ((0, 0), (0, LANES - N_EXPERTS)))
        xl = moe_residual(xl, ml[3], ml[4], ml[5], wr, moe_w_gate, moe_w_up, moe_w_down, i, n_tok,
                          final_gain if last else None)
        if not last:
            xc = moe_residual(xc, mc[3], mc[4], mc[5], wr, moe_w_gate, moe_w_up, moe_w_down, i, n_ctx)
    return xl.reshape(b, n_tok, d)
```

```python
import functools

import jax
import jax.numpy as jnp
from jax import lax
from jax.experimental import pallas as pl
from jax.experimental.pallas import tpu as pltpu

F32 = jnp.float32
BF16 = jnp.bfloat16
EPS = 1e-6

N_EXPERTS = 16
CAPACITY_FACTOR = 2
POOL_WINDOWS = (2, 4, 8, 16)
POOL_HALO = 8
LRU_BLOCKS = 8
LRU_C = 8.0
CONV_W = 4
HEAD_DIM = 64
N_KV_HEADS = 4
GRID_W = 64
ROPE_THETA = 10000.0
GMLP_GROUPS = 4
GMLP_CHUNK = 128

LANES = 128
SUBLANES = 8
BF16_SUBLANES = 16
LOG2E = 1.4426950408889634
VMEM_LIMIT = 56 * 1024 * 1024


def _cparams(*sem):
    return pltpu.CompilerParams(dimension_semantics=sem, vmem_limit_bytes=VMEM_LIMIT)


def _modulate(x, shift, scale):
    ms = jnp.mean(x * x, axis=-1, keepdims=True)
    return x * lax.rsqrt(ms + EPS) * (1.0 + scale) + shift


def _split_bf16(a):
    hi = a.astype(BF16)
    lo = (a - hi.astype(F32)).astype(BF16)
    return hi, lo


def _dot(a, b):
    return jnp.dot(a, b, preferred_element_type=F32)


def _dot3(a, b):
    a_hi, a_lo = _split_bf16(a)
    b_hi, b_lo = _split_bf16(b)
    return _dot(a_hi, b_hi) + (_dot(a_hi, b_lo) + _dot(a_lo, b_hi))


def _gelu(x):
    return 0.5 * x * (1.0 + jnp.tanh(0.7978845608028654 * (x + 0.044715 * (x * x * x))))


def _silu(x):
    return x * (1.0 / (1.0 + jnp.exp(-x)))


def _sigmoid(x):
    return 0.5 * jnp.tanh(0.5 * x) + 0.5


def _mod_spec(mod, tiles_per_batch):
    d = mod.shape[-1]
    if mod.shape[0] == 1:
        return pl.BlockSpec((1, 1, d), lambda i, *_: (0, 0, 0))
    return pl.BlockSpec((1, 1, d), lambda i, *_: (i // tiles_per_batch, 0, 0))


def _row_tile(n, want):
    t = min(n, want)
    assert n % t == 0
    return t


def _mod_kernel(c_ref, w_ref, b_ref, o_ref):
    s = _silu(c_ref[...])
    o_ref[0] = _dot3(s, w_ref[0]) + b_ref[0]


def mod_vectors(cvec, mod_w, mod_b):
    depth, d, n6 = mod_w.shape
    tn = 1024
    return pl.pallas_call(
        _mod_kernel,
        out_shape=jax.ShapeDtypeStruct((depth, cvec.shape[0], n6), F32),
        grid=(depth, n6 // tn),
        in_specs=[
            pl.BlockSpec(cvec.shape, lambda l, j: (0, 0)),
            pl.BlockSpec((1, d, tn), lambda l, j: (l, 0, j)),
            pl.BlockSpec((1, 1, tn), lambda l, j: (l, 0, j)),
        ],
        out_specs=pl.BlockSpec((1, cvec.shape[0], tn), lambda l, j: (l, 0, j)),
        compiler_params=_cparams("parallel", "parallel"),
        name="mod_vectors",
    )(cvec, mod_w, mod_b.reshape(depth, 1, n6))


def _modmm_kernel(x_ref, sh_ref, sc_ref, w_ref, *o_refs, splits, acts):
    h = _modulate(x_ref[...], sh_ref[0], sc_ref[0]).astype(BF16)
    z = _dot(h, w_ref[...])
    off = 0
    for o_ref, width, act in zip(o_refs, splits, acts):
        part = z[:, off:off + width]
        if act == "gelu":
            part = _gelu(part)
        o_ref[...] = part.astype(o_ref.dtype)
        off += width


def modmm(x, shift, scale, w, n_tok, splits, acts, dtypes, tm=512):
    r, d = x.shape
    tm = _row_tile(n_tok, tm)
    tpb = n_tok // tm
    n = w.shape[1]
    assert sum(splits) == n
    return pl.pallas_call(
        functools.partial(_modmm_kernel, splits=splits, acts=acts),
        out_shape=[jax.ShapeDtypeStruct((r, s), dt) for s, dt in zip(splits, dtypes)],
        grid=(r // tm,),
        in_specs=[
            pl.BlockSpec((tm, d), lambda i: (i, 0)),
            _mod_spec(shift, tpb),
            _mod_spec(scale, tpb),
            pl.BlockSpec((d, n), lambda i: (0, 0)),
        ],
        out_specs=[pl.BlockSpec((tm, s), lambda i: (i, 0)) for s in splits],
        compiler_params=_cparams("parallel"),
        name="modmm",
    )(x, shift, scale, w)


def _mmres_kernel(*refs, n_sum, has_mul):
    a = refs[0][...]
    for a_ref in refs[1:n_sum]:
        a = a + a_ref[...]
    refs = refs[n_sum:]
    if has_mul:
        a = a.astype(F32) * refs[0][...].astype(F32)
        refs = refs[1:]
    w_ref, x_ref, g_ref, o_ref = refs
    o_ref[...] = x_ref[...] + g_ref[0] * _dot(a.astype(BF16), w_ref[...])


def mm_residual(a, w, x, gate, n_tok, b=None, tm=1024):
    a = a if isinstance(a, (tuple, list)) else (a,)
    r, k = a[0].shape
    n = w.shape[1]
    tm = _row_tile(n_tok, tm)
    tpb = n_tok // tm
    ins = list(a) + ([b] if b is not None else []) + [w, x, gate]
    specs = [pl.BlockSpec((tm, k), lambda i: (i, 0))] * (len(a) + (b is not None))
    specs += [
        pl.BlockSpec((k, n), lambda i: (0, 0)),
        pl.BlockSpec((tm, n), lambda i: (i, 0)),
        _mod_spec(gate, tpb),
    ]
    return pl.pallas_call(
        functools.partial(_mmres_kernel, n_sum=len(a), has_mul=b is not None),
        out_shape=jax.ShapeDtypeStruct((r, n), F32),
        grid=(r // tm,),
        in_specs=specs,
        out_specs=pl.BlockSpec((tm, n), lambda i: (i, 0)),
        compiler_params=_cparams("parallel"),
        name="mm_residual",
    )(*ins)


def _pool_kernel(xp_ref, x_ref, xn_ref, sh_ref, sc_ref, g_ref, w_ref, ps_ref, o_ref, *, tm, n_tok, tpb):
    t0 = (pl.program_id(0) % tpb) * tm
    x = x_ref[...]
    rows = tm + 2 * POOL_HALO
    h = jnp.concatenate([
        jnp.where(t0 > 0, _modulate(xp_ref[...], sh_ref[0], sc_ref[0]), 0.0),
        _modulate(x, sh_ref[0], sc_ref[0]),
        jnp.where(t0 + tm < n_tok, _modulate(xn_ref[...], sh_ref[0], sc_ref[0]), 0.0)], axis=0)
    posc = lax.broadcasted_iota(jnp.int32, (tm, 1), 0) + t0
    gw = h.shape[1] // len(POOL_WINDOWS)
    outs = []
    for g, win in enumerate(POOL_WINDOWS):
        hg = h[:, g * gw:(g + 1) * gw]
        c = hg + pltpu.roll(hg, 1, 0)
        step = 1
        while 2 * step < win:
            c = pltpu.roll(c, step, 0) + pltpu.roll(c, rows - step, 0)
            step *= 2
        cnt = jnp.minimum(posc + (win - win // 2), n_tok) - jnp.maximum(posc - win // 2, 0)
        pooled = c[POOL_HALO:POOL_HALO + tm] / cnt.astype(F32) - hg[POOL_HALO:POOL_HALO + tm]
        outs.append(_dot(pooled.astype(BF16), w_ref[g]))
    y = jnp.concatenate(outs, axis=1) * ps_ref[...]
    o_ref[...] = x + g_ref[0] * y


def pool_mixer_residual(x, shift, scale, gate, w_pool, pool_scale, n_tok, tm=512):
    r, d = x.shape
    tm = _row_tile(n_tok, tm)
    tpb = n_tok // tm
    hb = tm // POOL_HALO
    last = r // POOL_HALO - 1
    groups, gw, _ = w_pool.shape
    return pl.pallas_call(
        functools.partial(_pool_kernel, tm=tm, n_tok=n_tok, tpb=tpb),
        out_shape=jax.ShapeDtypeStruct((r, d), F32),
        grid=(r // tm,),
        in_specs=[
            pl.BlockSpec((POOL_HALO, d), lambda i: (jnp.maximum(i * hb - 1, 0), 0)),
            pl.BlockSpec((tm, d), lambda i: (i, 0)),
            pl.BlockSpec((POOL_HALO, d), lambda i: (jnp.minimum((i + 1) * hb, last), 0)),
            _mod_spec(shift, tpb),
            _mod_spec(scale, tpb),
            _mod_spec(gate, tpb),
            pl.BlockSpec((groups, gw, gw), lambda i: (0, 0, 0)),
            pl.BlockSpec((1, d), lambda i: (0, 0)),
        ],
        out_specs=pl.BlockSpec((tm, d), lambda i: (i, 0)),
        compiler_params=_cparams("parallel"),
        name="pool_mixer",
    )(x, x, x, shift, scale, gate, w_pool, pool_scale.reshape(1, d))


def _router_kernel(x_ref, sh_ref, sc_ref, wr_ref, h_ref, aff_ref):
    h = _modulate(x_ref[...], sh_ref[0], sc_ref[0])
    h_ref[...] = h.astype(h_ref.dtype)
    logits = _dot3(h, wr_ref[...])
    lane = lax.broadcasted_iota(jnp.int32, logits.shape, 1)
    logits = jnp.where(lane < N_EXPERTS, logits, -jnp.inf)
    e = jnp.exp(logits - jnp.max(logits, axis=-1, keepdims=True))
    aff_ref[...] = e / jnp.sum(e, axis=-1, keepdims=True)


def router(x, shift, scale, w_router_padded, n_tok, tm=1024):
    r, d = x.shape
    tm = _row_tile(n_tok, tm)
    tpb = n_tok // tm
    return pl.pallas_call(
        _router_kernel,
        out_shape=[jax.ShapeDtypeStruct((r, d), BF16), jax.ShapeDtypeStruct((r, LANES), F32)],
        grid=(r // tm,),
        in_specs=[
            pl.BlockSpec((tm, d), lambda i: (i, 0)),
            _mod_spec(shift, tpb),
            _mod_spec(scale, tpb),
            pl.BlockSpec((d, LANES), lambda i: (0, 0)),
        ],
        out_specs=[pl.BlockSpec((tm, d), lambda i: (i, 0)), pl.BlockSpec((tm, LANES), lambda i: (i, 0))],
        compiler_params=_cparams("parallel"),
        name="moe_router",
    )(x, shift, scale, w_router_padded)


MOE_CHUNK = 128
EXPERT_GROUPS = 2
COMBINE_CHUNKS = 2
NARROW_WINDOW = 64
PACK = LANES // N_EXPERTS


def _cumsum_rows(v, tb):
    n = v.shape[0]
    tri = (lax.broadcasted_iota(jnp.int32, (tb, tb), 0) >= lax.broadcasted_iota(jnp.int32, (tb, tb), 1)).astype(BF16)
    carry = jnp.zeros((1, v.shape[1]), F32)
    outs, starts = [], []
    for j in range(n // tb):
        starts.append(carry)
        c = _dot(tri, v[j * tb:(j + 1) * tb].astype(BF16)) + carry
        carry = c[tb - 1:tb, :]
        outs.append(c)
    return jnp.concatenate(outs, axis=0), jnp.concatenate(starts, axis=0)


def _route_kernel(aff_ref, affp_ref, idx_ref, gate_ref, slot_ref, start_ref, slot_scr, acc_scr, *, n, cap):
    def enough(cand):
        cnt = jnp.sum((affp_ref[...] >= pltpu.bitcast(cand, F32)).astype(F32), axis=0, keepdims=True)
        shift = N_EXPERTS
        while shift < LANES:
            cnt = cnt + pltpu.roll(cnt, shift, 1)
            shift *= 2
        return cnt >= cap

    def two_bits(k, prefix):
        low = 28 - 2 * k
        c1, c2, c3 = (prefix | jnp.left_shift(jnp.int32(j), low) for j in (1, 2, 3))
        return jnp.where(enough(c3), c3, jnp.where(enough(c2), c2, jnp.where(enough(c1), c1, prefix)))

    top = jnp.full((1, LANES), 1 << 30, jnp.int32)
    thr = jnp.where(enough(top), top, jnp.zeros_like(top))
    thr = pltpu.bitcast(lax.fori_loop(0, 15, two_bits, thr), F32)
    aff = aff_ref[...]
    gt = aff > thr
    eq = aff == thr
    need = cap - jnp.sum(gt.astype(F32), axis=0, keepdims=True)
    eq_rank, _ = _cumsum_rows(eq.astype(F32), MOE_CHUNK)
    sel = gt | (eq & (eq_rank <= need))
    pos, starts = _cumsum_rows(sel.astype(F32), MOE_CHUNK)
    slot_scr[...] = jnp.where(sel, pos - 1.0, -1.0)
    slot_ref[...] = slot_scr[...]
    start_ref[0] = starts.astype(jnp.int32)

    slots = lax.broadcasted_iota(jnp.int32, (1, cap), 1).astype(F32)
    sub = lax.broadcasted_iota(jnp.int32, (SUBLANES, MOE_CHUNK), 0)
    local = lax.broadcasted_iota(jnp.int32, (SUBLANES, MOE_CHUNK), 1).astype(F32)
    acc_scr[...] = jnp.zeros(acc_scr.shape, F32)

    def chunk(c, carry):
        rows = pl.ds(pl.multiple_of(c * MOE_CHUNK, MOE_CHUNK), MOE_CHUNK)
        base = jnp.where(sub == 0, local, jnp.where(sub == 1, lax.convert_element_type(c, F32), 0.0))
        aff_t = aff_ref[rows, :].T
        for e in range(N_EXPERTS):
            onehot = jnp.where(slot_scr[rows, e:e + 1] == slots, 1.0, 0.0).astype(BF16)
            g = aff_t[e:e + 1, :]
            g_hi = g.astype(BF16).astype(F32)
            g_mid = (g - g_hi).astype(BF16).astype(F32)
            g_lo = g - g_hi - g_mid
            lhs = jnp.where(sub == 2, g_hi, jnp.where(sub == 3, g_mid, jnp.where(sub == 4, g_lo, base)))
            acc_scr[e] += _dot(lhs.astype(BF16), onehot)
        return carry

    lax.fori_loop(0, n // MOE_CHUNK, chunk, 0)
    for e in range(N_EXPERTS):
        idx_ref[0, e:e + 1, :] = (acc_scr[e, 0:1, :] + MOE_CHUNK * acc_scr[e, 1:2, :]).astype(jnp.int32)
        gate_ref[0, e:e + 1, :] = acc_scr[e, 2:3, :] + acc_scr[e, 3:4, :] + acc_scr[e, 4:5, :]


def route(aff, n_tok):
    r = aff.shape[0]
    b = r // n_tok
    cap = CAPACITY_FACTOR * n_tok // N_EXPERTS
    aff_packed = aff[:, :N_EXPERTS].reshape(r // PACK, LANES)
    n_chunks = n_tok // MOE_CHUNK
    return pl.pallas_call(
        functools.partial(_route_kernel, n=n_tok, cap=cap),
        out_shape=[jax.ShapeDtypeStruct((b, N_EXPERTS, cap), jnp.int32), jax.ShapeDtypeStruct((b, N_EXPERTS, cap), F32),
                   jax.ShapeDtypeStruct((r, LANES), F32), jax.ShapeDtypeStruct((b, n_chunks, LANES), jnp.int32)],
        grid=(b,),
        in_specs=[pl.BlockSpec((n_tok, LANES), lambda i: (i, 0)),
                  pl.BlockSpec((n_tok // PACK, LANES), lambda i: (i, 0))],
        out_specs=[pl.BlockSpec((1, N_EXPERTS, cap), lambda i: (i, 0, 0)),
                   pl.BlockSpec((1, N_EXPERTS, cap), lambda i: (i, 0, 0)),
                   pl.BlockSpec((n_tok, LANES), lambda i: (i, 0)),
                   pl.BlockSpec((1, n_chunks, LANES), lambda i: (i, 0, 0))],
        scratch_shapes=[pltpu.VMEM((n_tok, LANES), F32), pltpu.VMEM((N_EXPERTS, SUBLANES, cap), F32)],
        compiler_params=_cparams("parallel"),
        name="moe_route",
    )(aff, aff_packed)


def _ffn_kernel(doff_ref, x_ref, gt_ref, wg_ref, wu_ref, wd_ref, y_ref, dup_ref, wg_scr, wu_scr, wd_scr, y_scr, *,
                e0, cap, n_steps, dwin):
    @pl.when(pl.program_id(1) == 0)
    def _():
        wg_scr[...] = wg_ref[0, 0].astype(BF16)
        wu_scr[...] = wu_ref[0, 0].astype(BF16)
        wd_scr[...] = wd_ref[0, 0].astype(BF16)

    x = x_ref[0]
    a = _dot(x, wg_scr[...])
    u = _dot(x, wu_scr[...])
    hmid = (_silu(a) * u).astype(BF16)
    gate_cols = jnp.broadcast_to(gt_ref[0, 0], (LANES, x.shape[0])).T
    y = _dot(hmid, wd_scr[...])
    y_scr[...] = (y * jnp.concatenate([gate_cols] * (y.shape[1] // LANES), axis=1)).astype(BF16)
    y_ref[0] = y_scr[...]
    e = e0 + pl.program_id(0)
    align = min(dwin, BF16_SUBLANES)
    for bl in range(y_scr.shape[0] // cap):
        b = pl.program_id(1) * (y_scr.shape[0] // cap) + bl
        for s in range(n_steps):
            off = pl.multiple_of(doff_ref[(b * n_steps + s) * N_EXPERTS + e], align)
            dup_ref[bl, s, 0] = y_scr[pl.ds(bl * cap + off, dwin), :]


def expert_ffn(xs, gates, doffs, w_gate, w_up, w_down, layer, e0, cap, n_steps, dwin, tm=512):
    e, m, d = xs.shape
    ff = w_gate.shape[-1]
    tm = max(_row_tile(m, tm), cap)
    bpt = tm // cap
    return pl.pallas_call(
        functools.partial(_ffn_kernel, e0=e0, cap=cap, n_steps=n_steps, dwin=dwin),
        out_shape=[jax.ShapeDtypeStruct((e, m, d), BF16),
                   jax.ShapeDtypeStruct((m // cap, n_steps, e, dwin, d), BF16)],
        grid_spec=pltpu.PrefetchScalarGridSpec(
            num_scalar_prefetch=1,
            grid=(e, m // tm),
            in_specs=[
                pl.BlockSpec((1, tm, d), lambda k, i, off: (k, i, 0)),
                pl.BlockSpec((1, 1, 1, tm), lambda k, i, off: (k, i, 0, 0)),
                pl.BlockSpec((1, 1, d, ff), lambda k, i, off: (layer, e0 + k, 0, 0)),
                pl.BlockSpec((1, 1, d, ff), lambda k, i, off: (layer, e0 + k, 0, 0)),
                pl.BlockSpec((1, 1, ff, d), lambda k, i, off: (layer, e0 + k, 0, 0)),
            ],
            out_specs=[pl.BlockSpec((1, tm, d), lambda k, i, off: (k, i, 0)),
                       pl.BlockSpec((bpt, n_steps, 1, dwin, d), lambda k, i, off: (i, 0, k, 0, 0))],
            scratch_shapes=[pltpu.VMEM((d, ff), BF16), pltpu.VMEM((d, ff), BF16), pltpu.VMEM((ff, d), BF16),
                            pltpu.VMEM((tm, d), BF16)],
        ),
        compiler_params=_cparams("parallel", "arbitrary"),
        name="moe_expert_ffn",
    )(doffs, xs, gates.reshape(e, m // tm, 1, tm), w_gate, w_up, w_down)


def _combine_kernel(woff_ref, doff_ref, fit_ref, ysel_ref, x_ref, g_ref, slot_ref, *rest, win, dwin, n_chunks, cps,
                    n_groups, final):
    dup_refs, y_refs, rest = rest[:n_groups], rest[n_groups:2 * n_groups], rest[2 * n_groups:]
    if final:
        fg_ref, o_ref = rest
    else:
        (o_ref,) = rest
    per_group = N_EXPERTS // n_groups
    step = pl.program_id(0) * (n_chunks // cps) + pl.program_id(1)

    def finish(rows, acc):
        out = x_ref[rows, :] + g_ref[0] * acc
        if final:
            out = out * lax.rsqrt(jnp.mean(out * out, axis=-1, keepdims=True) + EPS) * fg_ref[...]
        o_ref[rows, :] = out

    @pl.when(fit_ref[step] == 1)
    def _():
        tm = x_ref.shape[0]
        epl = LANES // dwin
        lane = lax.broadcasted_iota(jnp.int32, (1, LANES), 1)
        which = lane // dwin
        within = (lane % dwin).astype(F32)
        acc = jnp.zeros((tm, x_ref.shape[1]), F32)
        for gi, dup_ref in enumerate(dup_refs):
            pieces = []
            for p in range(per_group // epl):
                col, sl = None, None
                for w in range(epl):
                    e = gi * per_group + p * epl + w
                    cw = lax.convert_element_type(doff_ref[step * N_EXPERTS + e], F32) + within
                    sw = jnp.broadcast_to(slot_ref[:, e:e + 1], (tm, LANES))
                    col = cw if w == 0 else jnp.where(which == w, cw, col)
                    sl = sw if w == 0 else jnp.where(which == w, sw, sl)
                pieces.append(jnp.where(sl == col, 1.0, 0.0).astype(BF16))
            onehot = jnp.concatenate(pieces, axis=1)
            acc = acc + _dot(onehot, dup_ref[0, 0].reshape(per_group * dwin, x_ref.shape[1]))
        finish(slice(0, tm), acc)

    @pl.when(fit_ref[step] == 0)
    def _():
        align = min(win, LANES)
        for sc in range(cps):
            rows = slice(sc * MOE_CHUNK, (sc + 1) * MOE_CHUNK)
            base = (step * cps + sc) * N_EXPERTS
            acc = jnp.zeros((MOE_CHUNK, x_ref.shape[1]), F32)
            for e in range(N_EXPERTS):
                off = pl.multiple_of(woff_ref[base + e], align)
                cols = (lax.broadcasted_iota(jnp.int32, (1, win), 1) + off).astype(F32)
                onehot = jnp.where(slot_ref[rows, e:e + 1] == cols, 1.0, 0.0).astype(BF16)
                acc = acc + _dot(onehot, y_refs[e // per_group][e % per_group, pl.ds(off, win), :])
            finish(rows, acc)


def moe_combine(x, gate2, slot, ys, dups, woffs, doffs, fits, n_tok, final_gain=None):
    r, d = x.shape
    b = r // n_tok
    per_group = ys[0].shape[0]
    cap = ys[0].shape[1] // b
    dwin = dups[0].shape[3]
    n_chunks = n_tok // MOE_CHUNK
    cps = COMBINE_CHUNKS if n_chunks % COMBINE_CHUNKS == 0 else 1
    tm = cps * MOE_CHUNK
    steps = n_chunks // cps
    win = min(2 * MOE_CHUNK, cap)
    final = final_gain is not None
    need = jnp.max((1 - fits).reshape(b, steps), axis=1)
    ysel = lax.cummax(need * jnp.arange(b, dtype=jnp.int32))
    row_spec = lambda w: pl.BlockSpec((tm, w), lambda bi, c, *_: (bi * steps + c, 0))
    ins = [woffs, doffs, fits, ysel, x, gate2, slot] + list(dups) + list(ys)
    specs = [row_spec(d), _mod_spec(gate2, 1), row_spec(LANES)]
    specs += [pl.BlockSpec((1, 1, per_group, dwin, d), lambda bi, c, *_: (bi, c, 0, 0, 0))] * len(dups)
    specs += [pl.BlockSpec((per_group, cap, d), lambda bi, c, wo, do, ft, ys_: (0, ys_[bi], 0),
                           pipeline_mode=pl.Buffered(1))] * len(ys)
    if final:
        ins.append(final_gain.reshape(1, d))
        specs.append(pl.BlockSpec((1, d), lambda bi, c, *_: (0, 0)))
    return pl.pallas_call(
        functools.partial(_combine_kernel, win=win, dwin=dwin, n_chunks=n_chunks, cps=cps, n_groups=len(ys),
                          final=final),
        out_shape=jax.ShapeDtypeStruct((r, d), F32),
        grid_spec=pltpu.PrefetchScalarGridSpec(
            num_scalar_prefetch=4,
            grid=(b, steps),
            in_specs=specs,
            out_specs=row_spec(d),
        ),
        compiler_params=_cparams("parallel", "arbitrary"),
        name="moe_combine",
    )(*ins)


def moe_residual(x, shift, scale, gate2, w_router_padded, w_gate, w_up, w_down, layer, n_tok, final_gain=None):
    r, d = x.shape
    b = r // n_tok
    h, aff = router(x, shift, scale, w_router_padded, n_tok)
    idx, gates, slot, start = route(aff, n_tok)
    cap = idx.shape[-1]
    n_chunks = n_tok // MOE_CHUNK
    cps = COMBINE_CHUNKS if n_chunks % COMBINE_CHUNKS == 0 else 1
    steps = n_chunks // cps
    start = start[:, :, :N_EXPERTS]
    win = min(2 * MOE_CHUNK, cap)
    align = min(win, LANES)
    woffs = jnp.clip(start // align * align, 0, cap - win).reshape(-1)
    dwin = min(NARROW_WINDOW, cap)
    dalign = min(dwin, BF16_SUBLANES)
    s_start = start[:, ::cps]
    s_end = jnp.concatenate([s_start[:, 1:], jnp.full((b, 1, N_EXPERTS), cap, jnp.int32)], axis=1)
    doffs = jnp.clip(s_start // dalign * dalign, 0, cap - dwin)
    fits = jnp.all(s_end - doffs <= dwin, axis=-1).astype(jnp.int32).reshape(-1)
    doffs = doffs.reshape(-1)

    gidx = idx + (jnp.arange(b, dtype=jnp.int32) * n_tok)[:, None, None]
    gidx = jnp.transpose(gidx, (1, 0, 2)).reshape(N_EXPERTS, b * cap)
    gts = jnp.transpose(gates, (1, 0, 2)).reshape(N_EXPERTS, b * cap)
    per_group = N_EXPERTS // EXPERT_GROUPS
    ys, dups = [], []
    for g in range(EXPERT_GROUPS):
        grp = slice(g * per_group, (g + 1) * per_group)
        xs = h.at[gidx[grp]].get(mode="promise_in_bounds")
        y, dup = expert_ffn(xs, gts[grp], doffs, w_gate, w_up, w_down, layer, g * per_group, cap, steps, dwin)
        ys.append(y)
        dups.append(dup)
    return moe_combine(x, gate2, slot, ys, dups, woffs, doffs, fits, n_tok, final_gain)


def _head_rms(t, seg_ones):
    outs = []
    for j in range(t.shape[1] // LANES):
        blk = t[:, j * LANES:(j + 1) * LANES]
        hi, lo = _split_bf16(blk * blk)
        ss = _dot(hi, seg_ones) + _dot(lo, seg_ones)
        outs.append(blk * lax.rsqrt(ss * (1.0 / HEAD_DIM) + EPS))
    return jnp.concatenate(outs, axis=1)


def _rope(t, cos, sin_signed):
    w = t.shape[1]
    half = HEAD_DIM // 2
    lane = lax.broadcasted_iota(jnp.int32, t.shape, 1)
    partner = jnp.where(lane % HEAD_DIM < half, pltpu.roll(t, w - half, 1), pltpu.roll(t, half, 1))
    reps = w // cos.shape[1]
    return t * jnp.concatenate([cos] * reps, axis=1) + partner * jnp.concatenate([sin_signed] * reps, axis=1)


def _qkv_kernel(*refs, qd, kd, rope):
    if rope:
        x_ref, sh_ref, sc_ref, w_ref, qg_ref, kg_ref, cos_ref, sin_ref, q_ref, k_ref, v_ref = refs
    else:
        x_ref, sh_ref, sc_ref, w_ref, qg_ref, kg_ref, q_ref, k_ref, v_ref = refs
    h = _modulate(x_ref[...], sh_ref[0], sc_ref[0]).astype(BF16)
    z = _dot(h, w_ref[...])
    seg = (lax.broadcasted_iota(jnp.int32, (LANES, LANES), 0) // HEAD_DIM
           == lax.broadcasted_iota(jnp.int32, (LANES, LANES), 1) // HEAD_DIM).astype(BF16)
    q = _head_rms(z[:, :qd], seg) * qg_ref[...]
    k = _head_rms(z[:, qd:qd + kd], seg) * kg_ref[...]
    if rope:
        q = _rope(q, cos_ref[...], sin_ref[...])
        k = _rope(k, cos_ref[...], sin_ref[...])
    q_ref[...] = (q * (HEAD_DIM ** -0.5 * LOG2E)).astype(q_ref.dtype)
    k_ref[...] = k.astype(k_ref.dtype)
    v_ref[...] = z[:, qd + kd:].astype(v_ref.dtype)


def qkv_project(x, shift, scale, w_qkv, q_gain, k_gain, n_tok, rope_tables=None, tm=512):
    r, d = x.shape
    kd = N_KV_HEADS * HEAD_DIM
    qd = w_qkv.shape[1] - 2 * kd
    tm = _row_tile(n_tok, tm)
    tpb = n_tok // tm
    ins = [x, shift, scale, w_qkv, q_gain, k_gain]
    specs = [
        pl.BlockSpec((tm, d), lambda i: (i, 0)),
        _mod_spec(shift, tpb),
        _mod_spec(scale, tpb),
        pl.BlockSpec(w_qkv.shape, lambda i: (0, 0)),
        pl.BlockSpec((1, qd), lambda i: (0, 0)),
        pl.BlockSpec((1, kd), lambda i: (0, 0)),
    ]
    if rope_tables is not None:
        ins += list(rope_tables)
        specs += [pl.BlockSpec((tm, LANES), lambda i: (i % tpb, 0))] * 2
    return pl.pallas_call(
        functools.partial(_qkv_kernel, qd=qd, kd=kd, rope=rope_tables is not None),
        out_shape=[jax.ShapeDtypeStruct((r, qd), BF16), jax.ShapeDtypeStruct((r, kd), BF16),
                   jax.ShapeDtypeStruct((r, kd), BF16)],
        grid=(r // tm,),
        in_specs=specs,
        out_specs=[pl.BlockSpec((tm, qd), lambda i: (i, 0)), pl.BlockSpec((tm, kd), lambda i: (i, 0)),
                   pl.BlockSpec((tm, kd), lambda i: (i, 0))],
        compiler_params=_cparams("parallel"),
        name="qkv_project",
    )(*ins)


def _attn_kernel(q_ref, k_ref, vt_ref, o_ref, *scr, grp, bounded):
    k = k_ref[0, 0]
    vt = vt_ref[0, 0]
    tq = q_ref.shape[0]
    n_pairs = grp // 2
    outs = []

    def scores(pair):
        heads = (2 * pair, 2 * pair + 1)
        q2 = jnp.concatenate([q_ref[:, h * HEAD_DIM:(h + 1) * HEAD_DIM] for h in heads], axis=0)
        return lax.dot_general(k, q2, (((1,), (1,)), ((), ())), preferred_element_type=F32)

    if bounded:
        for pair in range(n_pairs):
            p = jnp.exp2(scores(pair))
            scr[pair][...] = p.astype(BF16)
            ot = _dot(vt, scr[pair][...]) / jnp.sum(p, axis=0, keepdims=True)
            o = jnp.concatenate([ot, ot], axis=0).T
            outs += [o[:tq, :HEAD_DIM], o[tq:, :HEAD_DIM]]
        p_refs = ()
    else:
        for pair in range(n_pairs):
            scr[2 * pair][...] = scores(pair)
        for pair in range(n_pairs):
            st_scr, p_scr = scr[2 * pair], scr[2 * pair + 1]
            p_scr[...] = jnp.exp2(st_scr[...] - jnp.max(st_scr[...], axis=0, keepdims=True)).astype(BF16)
        p_refs = scr[1::2]
    for p_scr in p_refs:
        ot = _dot(vt, p_scr[...])
        ot = ot[:HEAD_DIM] / ot[HEAD_DIM:]
        o = jnp.concatenate([ot, ot], axis=0).T
        outs += [o[:tq, :HEAD_DIM], o[tq:, :HEAD_DIM]]
    o_ref[...] = jnp.concatenate(outs, axis=1).astype(o_ref.dtype)


def _attention_call(q, k, vt, n_q, tq, bounded):
    r, qd = q.shape
    _, kvh, n_k, hd = k.shape
    grp = qd // (kvh * hd)
    tq = _row_tile(n_q, tq)
    tpb = n_q // tq
    dts = (BF16,) if bounded else (F32, BF16)
    if not bounded:
        vt = jnp.concatenate([vt, jnp.ones_like(vt)], axis=2)
    return pl.pallas_call(
        functools.partial(_attn_kernel, grp=grp, bounded=bounded),
        out_shape=jax.ShapeDtypeStruct((r, qd), BF16),
        grid=(r // n_q, kvh, tpb),
        in_specs=[
            pl.BlockSpec((tq, grp * hd), lambda b, g, i: (b * tpb + i, g)),
            pl.BlockSpec((1, 1, n_k, hd), lambda b, g, i: (b, g, 0, 0)),
            pl.BlockSpec((1, 1, vt.shape[2], n_k), lambda b, g, i: (b, g, 0, 0)),
        ],
        out_specs=pl.BlockSpec((tq, grp * hd), lambda b, g, i: (b * tpb + i, g)),
        scratch_shapes=[pltpu.VMEM((n_k, 2 * tq), dt) for _ in range(grp // 2) for dt in dts],
        compiler_params=_cparams("parallel", "parallel", "parallel"),
        name="attention_bounded" if bounded else "attention",
    )(q, k, vt)


SCORE_BOUND = 60.0


def attention(q, k, vt, n_q, score_bound, tq_bounded=512, tq_exact=256):
    return lax.cond(score_bound <= SCORE_BOUND,
                    lambda: _attention_call(q, k, vt, n_q, tq_bounded, True),
                    lambda: _attention_call(q, k, vt, n_q, tq_exact, False))


def rope_tables(n_tok):
    t = jnp.arange(n_tok)
    row = (t // GRID_W).astype(F32)
    col = (t % GRID_W).astype(F32)
    n_freq = HEAD_DIM // 4
    inv = ROPE_THETA ** (-jnp.arange(n_freq, dtype=F32) / n_freq)
    ang = jnp.concatenate([row[:, None] * inv, col[:, None] * inv], axis=-1)
    cos, sin = jnp.cos(ang), jnp.sin(ang)
    reps = LANES // HEAD_DIM
    return jnp.tile(jnp.concatenate([cos, cos], -1), (1, reps)), jnp.tile(jnp.concatenate([-sin, sin], -1), (1, reps))


def _deinterleave_heads(w, n_heads):
    lead = w.shape[:-1]
    w = w.reshape(lead + (n_heads, HEAD_DIM // 2, 2))
    return jnp.swapaxes(w, -1, -2).reshape(lead + (n_heads * HEAD_DIM,))


def _split_kv_heads(t, b):
    return jnp.transpose(t.reshape(b, -1, N_KV_HEADS, HEAD_DIM), (0, 2, 1, 3))


def gqa_residual(x, xc, mods_l, mods_c, w_qkv, q_gain, k_gain, w_o, n_tok, n_ctx):
    b = x.shape[0] // n_tok
    kd = N_KV_HEADS * HEAD_DIM
    qd = w_qkv.shape[1] - 2 * kd
    n_qh = qd // HEAD_DIM
    w_perm = jnp.concatenate([_deinterleave_heads(w_qkv[:, :qd], n_qh),
                              _deinterleave_heads(w_qkv[:, qd:qd + kd], N_KV_HEADS), w_qkv[:, qd + kd:]], axis=1)
    w_perm = w_perm.astype(BF16)
    qg = jnp.tile(_deinterleave_heads(q_gain, 1), n_qh).reshape(1, qd)
    kg = jnp.tile(_deinterleave_heads(k_gain, 1), N_KV_HEADS).reshape(1, kd)
    sh_l, sc_l, g_l = mods_l
    sh_c, sc_c, g_c = mods_c
    q_l, k_l, v_l = qkv_project(x, sh_l, sc_l, w_perm, qg, kg, n_tok, rope_tables(n_tok))
    q_c, k_c, v_c = qkv_project(xc, sh_c, sc_c, w_perm, qg, kg, n_ctx)
    k_c4 = _split_kv_heads(k_c, b)
    k_all = jnp.concatenate([k_c4, _split_kv_heads(k_l, b)], axis=2)

    def values_t(v):
        return jnp.transpose(v.reshape(b, -1, N_KV_HEADS, HEAD_DIM), (0, 2, 3, 1))

    vt_c = values_t(v_c)
    vt_all = jnp.concatenate([vt_c, values_t(v_l)], axis=3)
    score_bound = (1.01 * HEAD_DIM * HEAD_DIM ** -0.5 * LOG2E
                   * jnp.max(jnp.abs(q_gain)) * jnp.max(jnp.abs(k_gain)))
    o_l = attention(q_l, k_all, vt_all, n_tok, score_bound)
    o_c = attention(q_c, k_c4, vt_c, n_ctx, score_bound)
    w_o = w_o.astype(BF16)
    return mm_residual(o_l, w_o, x, g_l, n_tok), mm_residual(o_c, w_o, xc, g_c, n_ctx)


def _lru_kernel(xp_ref, x_ref, xn_ref, cw_ref, cb_ref, wa_ref, ba_ref, wx_ref, bx_ref, lam_ref, h0_ref,
                o_ref, hT_ref, a_scr, u_scr, carry_scr, *, tm, n_tok, tpb, reverse):
    step = pl.program_id(1)
    t_idx = (tpb - 1 - step) if reverse else step
    t0 = t_idx * tm
    rows = tm + 2 * SUBLANES

    @pl.when(step == 0)
    def _():
        carry_scr[...] = h0_ref[0]

    xe = jnp.concatenate([jnp.where(t0 > 0, xp_ref[...], 0.0), x_ref[...],
                          jnp.where(t0 + tm < n_tok, xn_ref[...], 0.0)], axis=0)
    left = CONV_W // 2
    conv = cb_ref[...]
    for k in range(CONV_W):
        shift = (left - k) % rows
        tap = xe if shift == 0 else pltpu.roll(xe, shift, 0)
        conv = conv + tap * cw_ref[k:k + 1, :]
    xr = conv[SUBLANES:SUBLANES + tm]

    xb = xr.astype(BF16)
    bw = xr.shape[1] // LRU_BLOCKS
    ra, ia = [], []
    for j in range(LRU_BLOCKS):
        blk = xb[:, j * bw:(j + 1) * bw]
        ra.append(_dot(blk, wa_ref[j]))
        ia.append(_dot(blk, wx_ref[j]))
    r = _sigmoid(jnp.concatenate(ra, axis=1) + ba_ref[...])
    i = _sigmoid(jnp.concatenate(ia, axis=1) + bx_ref[...])
    log_a = -LRU_C * r * jnp.logaddexp(-lam_ref[...], 0.0)
    a = jnp.exp(log_a)
    a_scr[...] = a
    t = jnp.tanh(log_a)
    u_scr[...] = xr * i * jnp.sqrt(-2.0 * t / (1.0 - t))

    n_grp = tm // SUBLANES
    sub = lax.broadcasted_iota(jnp.int32, (SUBLANES, 1), 0)

    def group(j, carry):
        g = (n_grp - 1 - j) if reverse else j
        rws = pl.ds(pl.multiple_of(g * SUBLANES, SUBLANES), SUBLANES)
        ag, ug = a_scr[rws, :], u_scr[rws, :]
        s = 1
        while s < SUBLANES:
            if reverse:
                ok = sub < SUBLANES - s
                sh = SUBLANES - s
            else:
                ok = sub >= s
                sh = s
            u_prev = jnp.where(ok, pltpu.roll(ug, sh, 0), 0.0)
            a_prev = jnp.where(ok, pltpu.roll(ag, sh, 0), 1.0)
            ug = ug + ag * u_prev
            ag = ag * a_prev
            s *= 2
        hg = ug + ag * carry
        o_ref[rws, :] = hg
        return hg[0:1, :] if reverse else hg[SUBLANES - 1:SUBLANES, :]

    carry = lax.fori_loop(0, n_grp, group, carry_scr[...])
    carry_scr[...] = carry
    hT_ref[0] = carry


def lru_scan(xpre, conv_w, conv_b, wa, ba, wx, bx, lam, h0, n_tok, reverse, tm=512):
    r, w = xpre.shape
    b = r // n_tok
    tm = _row_tile(n_tok, tm)
    tpb = n_tok // tm
    hb = tm // SUBLANES
    last = r // SUBLANES - 1

    def tile(bi, s):
        return bi * tpb + ((tpb - 1 - s) if reverse else s)

    vec = pl.BlockSpec((1, w), lambda bi, s: (0, 0))
    blocks = pl.BlockSpec(wa.shape, lambda bi, s: (0, 0, 0))
    return pl.pallas_call(
        functools.partial(_lru_kernel, tm=tm, n_tok=n_tok, tpb=tpb, reverse=reverse),
        out_shape=[jax.ShapeDtypeStruct((r, w), F32), jax.ShapeDtypeStruct((b, 1, w), F32)],
        grid=(b, tpb),
        in_specs=[
            pl.BlockSpec((SUBLANES, w), lambda bi, s: (jnp.maximum(tile(bi, s) * hb - 1, 0), 0)),
            pl.BlockSpec((tm, w), lambda bi, s: (tile(bi, s), 0)),
            pl.BlockSpec((SUBLANES, w), lambda bi, s: (jnp.minimum((tile(bi, s) + 1) * hb, last), 0)),
            pl.BlockSpec((CONV_W, w), lambda bi, s: (0, 0)),
            vec, blocks, vec, blocks, vec, vec,
            pl.BlockSpec((1, 1, w), lambda bi, s: (bi, 0, 0)),
        ],
        out_specs=[pl.BlockSpec((tm, w), lambda bi, s: (tile(bi, s), 0)),
                   pl.BlockSpec((1, 1, w), lambda bi, s: (bi, 0, 0))],
        scratch_shapes=[pltpu.VMEM((tm, w), F32), pltpu.VMEM((tm, w), F32), pltpu.VMEM((1, w), F32)],
        compiler_params=_cparams("parallel", "arbitrary"),
        name="lru_scan",
    )(xpre, xpre, xpre, conv_w, conv_b.reshape(1, w), wa, ba.reshape(1, w), wx, bx.reshape(1, w),
      lam.reshape(1, w), h0)


def rglru_residual(x, xc, mods_l, mods_c, w_in, conv_w, conv_b, wa, ba, wx, bx, lam, w_out, n_tok, n_ctx):
    b = x.shape[0] // n_tok
    w = w_in.shape[1] // 2
    w_in = w_in.astype(BF16)
    sh_l, sc_l, g_l = mods_l
    sh_c, sc_c, g_c = mods_c
    y_l, xp_l = modmm(x, sh_l, sc_l, w_in, n_tok, (w, w), ("gelu", None), (F32, F32))
    y_c, xp_c = modmm(xc, sh_c, sc_c, w_in, n_ctx, (w, w), ("gelu", None), (F32, F32))
    zero = jnp.zeros((b, 1, w), F32)
    hs_l, hs_c = [], []
    for d in range(2):
        gate_w = (conv_w, conv_b, wa[d].astype(BF16), ba[d], wx[d].astype(BF16), bx[d], lam[d])
        hc, state = lru_scan(xp_c, *gate_w, zero, n_ctx, reverse=d == 1)
        hl, _ = lru_scan(xp_l, *gate_w, state, n_tok, reverse=d == 1)
        hs_c.append(hc)
        hs_l.append(hl)
    w_out = w_out.astype(BF16)
    return (mm_residual(tuple(hs_l), w_out, x, g_l, n_tok, b=y_l),
            mm_residual(tuple(hs_c), w_out, xc, g_c, n_ctx, b=y_c))


def _gmlp_kernel(x_ref, sh_ref, sc_ref, g_ref, win_ref, lng_ref, lnb_ref, ws_ref, bs_ref, wout_ref, o_ref, *, half):
    x = x_ref[...]
    h = _modulate(x, sh_ref[0], sc_ref[0]).astype(BF16)
    u = _gelu(_dot(h, win_ref[:, :half]))
    v = _gelu(_dot(h, win_ref[:, half:]))
    mu = jnp.mean(v, axis=-1, keepdims=True)
    vc = v - mu
    var = jnp.mean(vc * vc, axis=-1, keepdims=True)
    vn = (vc * lax.rsqrt(var + EPS) * lng_ref[...] + lnb_ref[...]).astype(BF16)
    gw = half // GMLP_GROUPS
    chunks = []
    for c in range(x.shape[0] // GMLP_CHUNK):
        rws = slice(c * GMLP_CHUNK, (c + 1) * GMLP_CHUNK)
        chunks.append(jnp.concatenate(
            [_dot(ws_ref[g], vn[rws, g * gw:(g + 1) * gw]) + bs_ref[g] for g in range(GMLP_GROUPS)], axis=1))
    v2 = jnp.concatenate(chunks, axis=0)
    o_ref[...] = x + g_ref[0] * _dot((u * v2).astype(BF16), wout_ref[...])


def gmlp_residual(x, shift, scale, gate, w_in, ln_g, ln_b, w_s, b_s, w_out, n_tok, tm=256):
    r, d = x.shape
    half = w_in.shape[1] // 2
    tm = _row_tile(n_tok, tm)
    tpb = n_tok // tm
    assert tm % GMLP_CHUNK == 0
    const2 = lambda i: (0, 0)
    const3 = lambda i: (0, 0, 0)
    once = pl.Buffered(1)
    return pl.pallas_call(
        functools.partial(_gmlp_kernel, half=half),
        out_shape=jax.ShapeDtypeStruct((r, d), F32),
        grid=(r // tm,),
        in_specs=[
            pl.BlockSpec((tm, d), lambda i: (i, 0)),
            _mod_spec(shift, tpb),
            _mod_spec(scale, tpb),
            _mod_spec(gate, tpb),
            pl.BlockSpec(w_in.shape, const2, pipeline_mode=once),
            pl.BlockSpec((1, half), const2),
            pl.BlockSpec((1, half), const2),
            pl.BlockSpec(w_s.shape, const3),
            pl.BlockSpec(b_s.shape + (1,), const3),
            pl.BlockSpec(w_out.shape, const2, pipeline_mode=once),
        ],
        out_specs=pl.BlockSpec((tm, d), lambda i: (i, 0)),
        compiler_params=_cparams("parallel"),
        name="gmlp_mixer",
    )(x, shift, scale, gate, w_in.astype(BF16), ln_g.reshape(1, half), ln_b.reshape(1, half),
      w_s.astype(BF16), b_s[..., None], w_out.astype(BF16))


def kernel(x, c, ctx, c_ctx, mod_w, mod_b, pool_w, pool_scale, lru_w_in, lru_conv_w, lru_conv_b, lru_wa, lru_ba,
           lru_wx, lru_bx, lru_lam, lru_w_out, attn_w_qkv, attn_q_gain, attn_k_gain, attn_w_o, gmlp_w_in, gmlp_ln_g,
           gmlp_ln_b, gmlp_w_s, gmlp_b_s, gmlp_w_out, moe_router, moe_w_gate, moe_w_up, moe_w_down, final_gain):
    b, n_tok, d = x.shape
    n_ctx = ctx.shape[1]
    depth = mod_w.shape[0]
    n_mixers = 4
    xl = x.reshape(b * n_tok, d)
    xc = ctx.reshape(b * n_ctx, d)

    pad = -(b + 1) % SUBLANES
    cvec = jnp.concatenate([c, c_ctx[None, :], jnp.zeros((pad, d), F32)], axis=0)
    mods = mod_vectors(cvec, mod_w, mod_b)

    for i in range(depth):
        m, j = i % n_mixers, i // n_mixers
        last = i == depth - 1
        ml = [mods[i, :b, k * d:(k + 1) * d].reshape(b, 1, d) for k in range(6)]
        mc = [mods[i, b:b + 1, k * d:(k + 1) * d].reshape(1, 1, d) for k in range(6)]
        if m == 0:
            pw = pool_w[j].astype(BF16)
            xl_new = pool_mixer_residual(xl, ml[0], ml[1], ml[2], pw, pool_scale[j], n_tok)
            if not last:
                xc = pool_mixer_residual(xc, mc[0], mc[1], mc[2], pw, pool_scale[j], n_ctx)
            xl = xl_new
        elif m == 1:
            xl, xc_new = rglru_residual(xl, xc, ml[:3], mc[:3], lru_w_in[j], lru_conv_w[j], lru_conv_b[j], lru_wa[j],
                                        lru_ba[j], lru_wx[j], lru_bx[j], lru_lam[j], lru_w_out[j], n_tok, n_ctx)
            xc = xc if last else xc_new
        elif m == 2:
            xl, xc_new = gqa_residual(xl, xc, ml[:3], mc[:3], attn_w_qkv[j], attn_q_gain[j], attn_k_gain[j],
                                      attn_w_o[j], n_tok, n_ctx)
            xc = xc if last else xc_new
        else:
            gargs = (gmlp_w_in[j], gmlp_ln_g[j], gmlp_ln_b[j], gmlp_w_s[j], gmlp_b_s[j], gmlp_w_out[j])
            xl_new = gmlp_residual(xl, ml[0], ml[1], ml[2], *gargs, n_tok)
            if not last:
                xc = gmlp_residual(xc, mc[0], mc[1], mc[2], *gargs, n_ctx)
            xl = xl_new
        wr = jnp.pad(moe_router[i], ((0, 0), (0, LANES - N_EXPERTS)))
        xl = moe_residual(xl, ml[3], ml[4], ml[5], wr, moe_w_gate, moe_w_up, moe_w_down, i, n_tok,
                          final_gain if last else None)
        if not last:
            xc = moe_residual(xc, mc[3], mc[4], mc[5], wr, moe_w_gate, moe_w_up, moe_w_down, i, n_ctx)
    return xl.reshape(b, n_tok, d)
```

```python
import functools

import jax
import jax.numpy as jnp
from jax import lax
from jax.experimental import pallas as pl
from jax.experimental.pallas import tpu as pltpu

F32 = jnp.float32
BF16 = jnp.bfloat16
EPS = 1e-6

N_EXPERTS = 16
CAPACITY_FACTOR = 2
POOL_WINDOWS = (2, 4, 8, 16)
POOL_HALO = 8
LRU_BLOCKS = 8
LRU_C = 8.0
CONV_W = 4
HEAD_DIM = 64
N_KV_HEADS = 4
GRID_W = 64
ROPE_THETA = 10000.0
GMLP_GROUPS = 4
GMLP_CHUNK = 128

LANES = 128
SUBLANES = 8
BF16_SUBLANES = 16
LOG2E = 1.4426950408889634
VMEM_LIMIT = 56 * 1024 * 1024


def _cparams(*sem):
    return pltpu.CompilerParams(dimension_semantics=sem, vmem_limit_bytes=VMEM_LIMIT)


def _modulate(x, shift, scale):
    ms = jnp.mean(x * x, axis=-1, keepdims=True)
    return x * lax.rsqrt(ms + EPS) * (1.0 + scale) + shift


def _split_bf16(a):
    hi = a.astype(BF16)
    lo = (a - hi.astype(F32)).astype(BF16)
    return hi, lo


def _dot(a, b):
    return jnp.dot(a, b, preferred_element_type=F32)


def _dot3(a, b):
    a_hi, a_lo = _split_bf16(a)
    b_hi, b_lo = _split_bf16(b)
    return _dot(a_hi, b_hi) + (_dot(a_hi, b_lo) + _dot(a_lo, b_hi))


def _gelu(x):
    return 0.5 * x * (1.0 + jnp.tanh(0.7978845608028654 * (x + 0.044715 * (x * x * x))))


def _silu(x):
    return x * (1.0 / (1.0 + jnp.exp(-x)))


def _sigmoid(x):
    return 0.5 * jnp.tanh(0.5 * x) + 0.5


def _mod_spec(mod, tiles_per_batch):
    d = mod.shape[-1]
    if mod.shape[0] == 1:
        return pl.BlockSpec((1, 1, d), lambda i, *_: (0, 0, 0))
    return pl.BlockSpec((1, 1, d), lambda i, *_: (i // tiles_per_batch, 0, 0))


def _row_tile(n, want):
    t = min(n, want)
    assert n % t == 0
    return t


def _mod_kernel(c_ref, w_ref, b_ref, o_ref):
    s = _silu(c_ref[...])
    o_ref[0] = _dot3(s, w_ref[0]) + b_ref[0]


def mod_vectors(cvec, mod_w, mod_b):
    depth, d, n6 = mod_w.shape
    tn = 1024
    return pl.pallas_call(
        _mod_kernel,
        out_shape=jax.ShapeDtypeStruct((depth, cvec.shape[0], n6), F32),
        grid=(depth, n6 // tn),
        in_specs=[
            pl.BlockSpec(cvec.shape, lambda l, j: (0, 0)),
            pl.BlockSpec((1, d, tn), lambda l, j: (l, 0, j)),
            pl.BlockSpec((1, 1, tn), lambda l, j: (l, 0, j)),
        ],
        out_specs=pl.BlockSpec((1, cvec.shape[0], tn), lambda l, j: (l, 0, j)),
        compiler_params=_cparams("parallel", "parallel"),
        name="mod_vectors",
    )(cvec, mod_w, mod_b.reshape(depth, 1, n6))


def _modmm_kernel(x_ref, sh_ref, sc_ref, w_ref, *o_refs, splits, acts):
    h = _modulate(x_ref[...], sh_ref[0], sc_ref[0]).astype(BF16)
    z = _dot(h, w_ref[...])
    off = 0
    for o_ref, width, act in zip(o_refs, splits, acts):
        part = z[:, off:off + width]
        if act == "gelu":
            part = _gelu(part)
        o_ref[...] = part.astype(o_ref.dtype)
        off += width


def modmm(x, shift, scale, w, n_tok, splits, acts, dtypes, tm=1024):
    r, d = x.shape
    tm = _row_tile(n_tok, tm)
    tpb = n_tok // tm
    n = w.shape[1]
    assert sum(splits) == n
    return pl.pallas_call(
        functools.partial(_modmm_kernel, splits=splits, acts=acts),
        out_shape=[jax.ShapeDtypeStruct((r, s), dt) for s, dt in zip(splits, dtypes)],
        grid=(r // tm,),
        in_specs=[
            pl.BlockSpec((tm, d), lambda i: (i, 0)),
            _mod_spec(shift, tpb),
            _mod_spec(scale, tpb),
            pl.BlockSpec((d, n), lambda i: (0, 0)),
        ],
        out_specs=[pl.BlockSpec((tm, s), lambda i: (i, 0)) for s in splits],
        compiler_params=_cparams("parallel"),
        name="modmm",
    )(x, shift, scale, w)


def _mmres_kernel(*refs, n_sum, has_mul):
    a = refs[0][...]
    for a_ref in refs[1:n_sum]:
        a = a + a_ref[...]
    refs = refs[n_sum:]
    if has_mul:
        a = a.astype(F32) * refs[0][...].astype(F32)
        refs = refs[1:]
    w_ref, x_ref, g_ref, o_ref = refs
    o_ref[...] = x_ref[...] + g_ref[0] * _dot(a.astype(BF16), w_ref[...])


def mm_residual(a, w, x, gate, n_tok, b=None, tm=1024):
    a = a if isinstance(a, (tuple, list)) else (a,)
    r, k = a[0].shape
    n = w.shape[1]
    tm = _row_tile(n_tok, tm)
    tpb = n_tok // tm
    ins = list(a) + ([b] if b is not None else []) + [w, x, gate]
    specs = [pl.BlockSpec((tm, k), lambda i: (i, 0))] * (len(a) + (b is not None))
    specs += [
        pl.BlockSpec((k, n), lambda i: (0, 0)),
        pl.BlockSpec((tm, n), lambda i: (i, 0)),
        _mod_spec(gate, tpb),
    ]
    return pl.pallas_call(
        functools.partial(_mmres_kernel, n_sum=len(a), has_mul=b is not None),
        out_shape=jax.ShapeDtypeStruct((r, n), F32),
        grid=(r // tm,),
        in_specs=specs,
        out_specs=pl.BlockSpec((tm, n), lambda i: (i, 0)),
        compiler_params=_cparams("parallel"),
        name="mm_residual",
    )(*ins)


def _pool_kernel(xp_ref, x_ref, xn_ref, sh_ref, sc_ref, g_ref, w_ref, ps_ref, o_ref, *, tm, n_tok, tpb):
    t0 = (pl.program_id(0) % tpb) * tm
    x = x_ref[...]
    rows = tm + 2 * POOL_HALO
    h = jnp.concatenate([
        jnp.where(t0 > 0, _modulate(xp_ref[...], sh_ref[0], sc_ref[0]), 0.0),
        _modulate(x, sh_ref[0], sc_ref[0]),
        jnp.where(t0 + tm < n_tok, _modulate(xn_ref[...], sh_ref[0], sc_ref[0]), 0.0)], axis=0)
    posc = lax.broadcasted_iota(jnp.int32, (tm, 1), 0) + t0
    gw = h.shape[1] // len(POOL_WINDOWS)
    outs = []
    for g, win in enumerate(POOL_WINDOWS):
        hg = h[:, g * gw:(g + 1) * gw]
        c = hg + pltpu.roll(hg, 1, 0)
        step = 1
        while 2 * step < win:
            c = pltpu.roll(c, step, 0) + pltpu.roll(c, rows - step, 0)
            step *= 2
        cnt = jnp.minimum(posc + (win - win // 2), n_tok) - jnp.maximum(posc - win // 2, 0)
        pooled = c[POOL_HALO:POOL_HALO + tm] / cnt.astype(F32) - hg[POOL_HALO:POOL_HALO + tm]
        outs.append(_dot(pooled.astype(BF16), w_ref[g]))
    y = jnp.concatenate(outs, axis=1) * ps_ref[...]
    o_ref[...] = x + g_ref[0] * y


def pool_mixer_residual(x, shift, scale, gate, w_pool, pool_scale, n_tok, tm=1024):
    r, d = x.shape
    tm = _row_tile(n_tok, tm)
    tpb = n_tok // tm
    hb = tm // POOL_HALO
    last = r // POOL_HALO - 1
    groups, gw, _ = w_pool.shape
    return pl.pallas_call(
        functools.partial(_pool_kernel, tm=tm, n_tok=n_tok, tpb=tpb),
        out_shape=jax.ShapeDtypeStruct((r, d), F32),
        grid=(r // tm,),
        in_specs=[
            pl.BlockSpec((POOL_HALO, d), lambda i: (jnp.maximum(i * hb - 1, 0), 0)),
            pl.BlockSpec((tm, d), lambda i: (i, 0)),
            pl.BlockSpec((POOL_HALO, d), lambda i: (jnp.minimum((i + 1) * hb, last), 0)),
            _mod_spec(shift, tpb),
            _mod_spec(scale, tpb),
            _mod_spec(gate, tpb),
            pl.BlockSpec((groups, gw, gw), lambda i: (0, 0, 0)),
            pl.BlockSpec((1, d), lambda i: (0, 0)),
        ],
        out_specs=pl.BlockSpec((tm, d), lambda i: (i, 0)),
        compiler_params=_cparams("parallel"),
        name="pool_mixer",
    )(x, x, x, shift, scale, gate, w_pool, pool_scale.reshape(1, d))


def _router_kernel(x_ref, sh_ref, sc_ref, wr_ref, h_ref, aff_ref):
    h = _modulate(x_ref[...], sh_ref[0], sc_ref[0])
    h_ref[...] = h.astype(h_ref.dtype)
    logits = _dot3(h, wr_ref[...])
    lane = lax.broadcasted_iota(jnp.int32, logits.shape, 1)
    logits = jnp.where(lane < N_EXPERTS, logits, -jnp.inf)
    e = jnp.exp(logits - jnp.max(logits, axis=-1, keepdims=True))
    aff_ref[...] = e / jnp.sum(e, axis=-1, keepdims=True)


def router(x, shift, scale, w_router_padded, n_tok, tm=1024):
    r, d = x.shape
    tm = _row_tile(n_tok, tm)
    tpb = n_tok // tm
    return pl.pallas_call(
        _router_kernel,
        out_shape=[jax.ShapeDtypeStruct((r, d), BF16), jax.ShapeDtypeStruct((r, LANES), F32)],
        grid=(r // tm,),
        in_specs=[
            pl.BlockSpec((tm, d), lambda i: (i, 0)),
            _mod_spec(shift, tpb),
            _mod_spec(scale, tpb),
            pl.BlockSpec((d, LANES), lambda i: (0, 0)),
        ],
        out_specs=[pl.BlockSpec((tm, d), lambda i: (i, 0)), pl.BlockSpec((tm, LANES), lambda i: (i, 0))],
        compiler_params=_cparams("parallel"),
        name="moe_router",
    )(x, shift, scale, w_router_padded)


MOE_CHUNK = 128
EXPERT_GROUPS = 2
COMBINE_CHUNKS = 2
NARROW_WINDOW = 64
PACK = LANES // N_EXPERTS


def _cumsum_rows(v, tb):
    n = v.shape[0]
    tri = (lax.broadcasted_iota(jnp.int32, (tb, tb), 0) >= lax.broadcasted_iota(jnp.int32, (tb, tb), 1)).astype(BF16)
    carry = jnp.zeros((1, v.shape[1]), F32)
    outs, starts = [], []
    for j in range(n // tb):
        starts.append(carry)
        c = _dot(tri, v[j * tb:(j + 1) * tb].astype(BF16)) + carry
        carry = c[tb - 1:tb, :]
        outs.append(c)
    return jnp.concatenate(outs, axis=0), jnp.concatenate(starts, axis=0)


def _route_kernel(aff_ref, affp_ref, idx_ref, gate_ref, slot_ref, start_ref, slot_scr, acc_scr, *, n, cap):
    def enough(cand):
        cnt = jnp.sum((affp_ref[...] >= pltpu.bitcast(cand, F32)).astype(F32), axis=0, keepdims=True)
        shift = N_EXPERTS
        while shift < LANES:
            cnt = cnt + pltpu.roll(cnt, shift, 1)
            shift *= 2
        return cnt >= cap

    def two_bits(k, prefix):
        low = 28 - 2 * k
        c1, c2, c3 = (prefix | jnp.left_shift(jnp.int32(j), low) for j in (1, 2, 3))
        return jnp.where(enough(c3), c3, jnp.where(enough(c2), c2, jnp.where(enough(c1), c1, prefix)))

    top = jnp.full((1, LANES), 1 << 30, jnp.int32)
    thr = jnp.where(enough(top), top, jnp.zeros_like(top))
    thr = pltpu.bitcast(lax.fori_loop(0, 15, two_bits, thr), F32)
    aff = aff_ref[...]
    gt = aff > thr
    eq = aff == thr
    need = cap - jnp.sum(gt.astype(F32), axis=0, keepdims=True)
    eq_rank, _ = _cumsum_rows(eq.astype(F32), MOE_CHUNK)
    sel = gt | (eq & (eq_rank <= need))
    pos, starts = _cumsum_rows(sel.astype(F32), MOE_CHUNK)
    slot_scr[...] = jnp.where(sel, pos - 1.0, -1.0)
    slot_ref[...] = slot_scr[...]
    start_ref[0] = starts.astype(jnp.int32)

    slots = lax.broadcasted_iota(jnp.int32, (1, cap), 1).astype(F32)
    sub = lax.broadcasted_iota(jnp.int32, (SUBLANES, MOE_CHUNK), 0)
    local = lax.broadcasted_iota(jnp.int32, (SUBLANES, MOE_CHUNK), 1).astype(F32)
    acc_scr[...] = jnp.zeros(acc_scr.shape, F32)

    def chunk(c, carry):
        rows = pl.ds(pl.multiple_of(c * MOE_CHUNK, MOE_CHUNK), MOE_CHUNK)
        base = jnp.where(sub == 0, local, jnp.where(sub == 1, lax.convert_element_type(c, F32), 0.0))
        aff_t = aff_ref[rows, :].T
        for e in range(N_EXPERTS):
            onehot = jnp.where(slot_scr[rows, e:e + 1] == slots, 1.0, 0.0).astype(BF16)
            g = aff_t[e:e + 1, :]
            g_hi = g.astype(BF16).astype(F32)
            g_mid = (g - g_hi).astype(BF16).astype(F32)
            g_lo = g - g_hi - g_mid
            lhs = jnp.where(sub == 2, g_hi, jnp.where(sub == 3, g_mid, jnp.where(sub == 4, g_lo, base)))
            acc_scr[e] += _dot(lhs.astype(BF16), onehot)
        return carry

    lax.fori_loop(0, n // MOE_CHUNK, chunk, 0)
    for e in range(N_EXPERTS):
        idx_ref[0, e:e + 1, :] = (acc_scr[e, 0:1, :] + MOE_CHUNK * acc_scr[e, 1:2, :]).astype(jnp.int32)
        gate_ref[0, e:e + 1, :] = acc_scr[e, 2:3, :] + acc_scr[e, 3:4, :] + acc_scr[e, 4:5, :]


def route(aff, n_tok):
    r = aff.shape[0]
    b = r // n_tok
    cap = CAPACITY_FACTOR * n_tok // N_EXPERTS
    aff_packed = aff[:, :N_EXPERTS].reshape(r // PACK, LANES)
    n_chunks = n_tok // MOE_CHUNK
    return pl.pallas_call(
        functools.partial(_route_kernel, n=n_tok, cap=cap),
        out_shape=[jax.ShapeDtypeStruct((b, N_EXPERTS, cap), jnp.int32), jax.ShapeDtypeStruct((b, N_EXPERTS, cap), F32),
                   jax.ShapeDtypeStruct((r, LANES), F32), jax.ShapeDtypeStruct((b, n_chunks, LANES), jnp.int32)],
        grid=(b,),
        in_specs=[pl.BlockSpec((n_tok, LANES), lambda i: (i, 0)),
                  pl.BlockSpec((n_tok // PACK, LANES), lambda i: (i, 0))],
        out_specs=[pl.BlockSpec((1, N_EXPERTS, cap), lambda i: (i, 0, 0)),
                   pl.BlockSpec((1, N_EXPERTS, cap), lambda i: (i, 0, 0)),
                   pl.BlockSpec((n_tok, LANES), lambda i: (i, 0)),
                   pl.BlockSpec((1, n_chunks, LANES), lambda i: (i, 0, 0))],
        scratch_shapes=[pltpu.VMEM((n_tok, LANES), F32), pltpu.VMEM((N_EXPERTS, SUBLANES, cap), F32)],
        compiler_params=_cparams("parallel"),
        name="moe_route",
    )(aff, aff_packed)


def _ffn_kernel(doff_ref, x_ref, gt_ref, wg_ref, wu_ref, wd_ref, y_ref, dup_ref, wg_scr, wu_scr, wd_scr, y_scr, *,
                e0, cap, n_steps, dwin):
    @pl.when(pl.program_id(1) == 0)
    def _():
        wg_scr[...] = wg_ref[0, 0].astype(BF16)
        wu_scr[...] = wu_ref[0, 0].astype(BF16)
        wd_scr[...] = wd_ref[0, 0].astype(BF16)

    x = x_ref[0]
    a = _dot(x, wg_scr[...])
    u = _dot(x, wu_scr[...])
    hmid = (_silu(a) * u).astype(BF16)
    gate_cols = jnp.broadcast_to(gt_ref[0, 0], (LANES, x.shape[0])).T
    y = _dot(hmid, wd_scr[...])
    y_scr[...] = (y * jnp.concatenate([gate_cols] * (y.shape[1] // LANES), axis=1)).astype(BF16)
    y_ref[0] = y_scr[...]
    e = e0 + pl.program_id(0)
    align = min(dwin, BF16_SUBLANES)
    for bl in range(y_scr.shape[0] // cap):
        b = pl.program_id(1) * (y_scr.shape[0] // cap) + bl
        for s in range(n_steps):
            off = pl.multiple_of(doff_ref[(b * n_steps + s) * N_EXPERTS + e], align)
            dup_ref[bl, s, 0] = y_scr[pl.ds(bl * cap + off, dwin), :]


def expert_ffn(xs, gates, doffs, w_gate, w_up, w_down, layer, e0, cap, n_steps, dwin, tm=512):
    e, m, d = xs.shape
    ff = w_gate.shape[-1]
    tm = max(_row_tile(m, tm), cap)
    bpt = tm // cap
    return pl.pallas_call(
        functools.partial(_ffn_kernel, e0=e0, cap=cap, n_steps=n_steps, dwin=dwin),
        out_shape=[jax.ShapeDtypeStruct((e, m, d), BF16),
                   jax.ShapeDtypeStruct((m // cap, n_steps, e, dwin, d), BF16)],
        grid_spec=pltpu.PrefetchScalarGridSpec(
            num_scalar_prefetch=1,
            grid=(e, m // tm),
            in_specs=[
                pl.BlockSpec((1, tm, d), lambda k, i, off: (k, i, 0)),
                pl.BlockSpec((1, 1, 1, tm), lambda k, i, off: (k, i, 0, 0)),
                pl.BlockSpec((1, 1, d, ff), lambda k, i, off: (layer, e0 + k, 0, 0)),
                pl.BlockSpec((1, 1, d, ff), lambda k, i, off: (layer, e0 + k, 0, 0)),
                pl.BlockSpec((1, 1, ff, d), lambda k, i, off: (layer, e0 + k, 0, 0)),
            ],
            out_specs=[pl.BlockSpec((1, tm, d), lambda k, i, off: (k, i, 0)),
                       pl.BlockSpec((bpt, n_steps, 1, dwin, d), lambda k, i, off: (i, 0, k, 0, 0))],
            scratch_shapes=[pltpu.VMEM((d, ff), BF16), pltpu.VMEM((d, ff), BF16), pltpu.VMEM((ff, d), BF16),
                            pltpu.VMEM((tm, d), BF16)],
        ),
        compiler_params=_cparams("parallel", "arbitrary"),
        name="moe_expert_ffn",
    )(doffs, xs, gates.reshape(e, m // tm, 1, tm), w_gate, w_up, w_down)


def _combine_kernel(woff_ref, doff_ref, fit_ref, ysel_ref, x_ref, g_ref, slot_ref, *rest, win, dwin, n_chunks, cps,
                    n_groups, final):
    dup_refs, y_refs, rest = rest[:n_groups], rest[n_groups:2 * n_groups], rest[2 * n_groups:]
    if final:
        fg_ref, o_ref = rest
    else:
        (o_ref,) = rest
    per_group = N_EXPERTS // n_groups
    step = pl.program_id(0) * (n_chunks // cps) + pl.program_id(1)

    def finish(rows, acc):
        out = x_ref[rows, :] + g_ref[0] * acc
        if final:
            out = out * lax.rsqrt(jnp.mean(out * out, axis=-1, keepdims=True) + EPS) * fg_ref[...]
        o_ref[rows, :] = out

    @pl.when(fit_ref[step] == 1)
    def _():
        tm = x_ref.shape[0]
        epl = LANES // dwin
        lane = lax.broadcasted_iota(jnp.int32, (1, LANES), 1)
        which = lane // dwin
        within = (lane % dwin).astype(F32)
        acc = jnp.zeros((tm, x_ref.shape[1]), F32)
        for gi, dup_ref in enumerate(dup_refs):
            pieces = []
            for p in range(per_group // epl):
                col, sl = None, None
                for w in range(epl):
                    e = gi * per_group + p * epl + w
                    cw = lax.convert_element_type(doff_ref[step * N_EXPERTS + e], F32) + within
                    sw = jnp.broadcast_to(slot_ref[:, e:e + 1], (tm, LANES))
                    col = cw if w == 0 else jnp.where(which == w, cw, col)
                    sl = sw if w == 0 else jnp.where(which == w, sw, sl)
                pieces.append(jnp.where(sl == col, 1.0, 0.0).astype(BF16))
            onehot = jnp.concatenate(pieces, axis=1)
            acc = acc + _dot(onehot, dup_ref[0, 0].reshape(per_group * dwin, x_ref.shape[1]))
        finish(slice(0, tm), acc)

    @pl.when(fit_ref[step] == 0)
    def _():
        align = min(win, LANES)
        for sc in range(cps):
            rows = slice(sc * MOE_CHUNK, (sc + 1) * MOE_CHUNK)
            base = (step * cps + sc) * N_EXPERTS
            acc = jnp.zeros((MOE_CHUNK, x_ref.shape[1]), F32)
            for e in range(N_EXPERTS):
                off = pl.multiple_of(woff_ref[base + e], align)
                cols = (lax.broadcasted_iota(jnp.int32, (1, win), 1) + off).astype(F32)
                onehot = jnp.where(slot_ref[rows, e:e + 1] == cols, 1.0, 0.0).astype(BF16)
                acc = acc + _dot(onehot, y_refs[e // per_group][e % per_group, pl.ds(off, win), :])
            finish(rows, acc)


def moe_combine(x, gate2, slot, ys, dups, woffs, doffs, fits, n_tok, final_gain=None):
    r, d = x.shape
    b = r // n_tok
    per_group = ys[0].shape[0]
    cap = ys[0].shape[1] // b
    dwin = dups[0].shape[3]
    n_chunks = n_tok // MOE_CHUNK
    cps = COMBINE_CHUNKS if n_chunks % COMBINE_CHUNKS == 0 else 1
    tm = cps * MOE_CHUNK
    steps = n_chunks // cps
    win = min(2 * MOE_CHUNK, cap)
    final = final_gain is not None
    need = jnp.max((1 - fits).reshape(b, steps), axis=1)
    ysel = lax.cummax(need * jnp.arange(b, dtype=jnp.int32))
    row_spec = lambda w: pl.BlockSpec((tm, w), lambda bi, c, *_: (bi * steps + c, 0))
    ins = [woffs, doffs, fits, ysel, x, gate2, slot] + list(dups) + list(ys)
    specs = [row_spec(d), _mod_spec(gate2, 1), row_spec(LANES)]
    specs += [pl.BlockSpec((1, 1, per_group, dwin, d), lambda bi, c, *_: (bi, c, 0, 0, 0))] * len(dups)
    specs += [pl.BlockSpec((per_group, cap, d), lambda bi, c, wo, do, ft, ys_: (0, ys_[bi], 0),
                           pipeline_mode=pl.Buffered(1))] * len(ys)
    if final:
        ins.append(final_gain.reshape(1, d))
        specs.append(pl.BlockSpec((1, d), lambda bi, c, *_: (0, 0)))
    return pl.pallas_call(
        functools.partial(_combine_kernel, win=win, dwin=dwin, n_chunks=n_chunks, cps=cps, n_groups=len(ys),
                          final=final),
        out_shape=jax.ShapeDtypeStruct((r, d), F32),
        grid_spec=pltpu.PrefetchScalarGridSpec(
            num_scalar_prefetch=4,
            grid=(b, steps),
            in_specs=specs,
            out_specs=row_spec(d),
        ),
        compiler_params=_cparams("parallel", "arbitrary"),
        name="moe_combine",
    )(*ins)


def moe_residual(x, shift, scale, gate2, w_router_padded, w_gate, w_up, w_down, layer, n_tok, final_gain=None):
    r, d = x.shape
    b = r // n_tok
    h, aff = router(x, shift, scale, w_router_padded, n_tok)
    idx, gates, slot, start = route(aff, n_tok)
    cap = idx.shape[-1]
    n_chunks = n_tok // MOE_CHUNK
    cps = COMBINE_CHUNKS if n_chunks % COMBINE_CHUNKS == 0 else 1
    steps = n_chunks // cps
    start = start[:, :, :N_EXPERTS]
    win = min(2 * MOE_CHUNK, cap)
    align = min(win, LANES)
    woffs = jnp.clip(start // align * align, 0, cap - win).reshape(-1)
    dwin = min(NARROW_WINDOW, cap)
    dalign = min(dwin, BF16_SUBLANES)
    s_start = start[:, ::cps]
    s_end = jnp.concatenate([s_start[:, 1:], jnp.full((b, 1, N_EXPERTS), cap, jnp.int32)], axis=1)
    doffs = jnp.clip(s_start // dalign * dalign, 0, cap - dwin)
    fits = jnp.all(s_end - doffs <= dwin, axis=-1).astype(jnp.int32).reshape(-1)
    doffs = doffs.reshape(-1)

    gidx = idx + (jnp.arange(b, dtype=jnp.int32) * n_tok)[:, None, None]
    gidx = jnp.transpose(gidx, (1, 0, 2)).reshape(N_EXPERTS, b * cap)
    gts = jnp.transpose(gates, (1, 0, 2)).reshape(N_EXPERTS, b * cap)
    per_group = N_EXPERTS // EXPERT_GROUPS
    ys, dups = [], []
    for g in range(EXPERT_GROUPS):
        grp = slice(g * per_group, (g + 1) * per_group)
        xs = h.at[gidx[grp]].get(mode="promise_in_bounds")
        y, dup = expert_ffn(xs, gts[grp], doffs, w_gate, w_up, w_down, layer, g * per_group, cap, steps, dwin)
        ys.append(y)
        dups.append(dup)
    return moe_combine(x, gate2, slot, ys, dups, woffs, doffs, fits, n_tok, final_gain)


def _head_rms(t, seg_ones):
    outs = []
    for j in range(t.shape[1] // LANES):
        blk = t[:, j * LANES:(j + 1) * LANES]
        hi, lo = _split_bf16(blk * blk)
        ss = _dot(hi, seg_ones) + _dot(lo, seg_ones)
        outs.append(blk * lax.rsqrt(ss * (1.0 / HEAD_DIM) + EPS))
    return jnp.concatenate(outs, axis=1)


def _rope(t, cos, sin_signed):
    w = t.shape[1]
    half = HEAD_DIM // 2
    lane = lax.broadcasted_iota(jnp.int32, t.shape, 1)
    partner = jnp.where(lane % HEAD_DIM < half, pltpu.roll(t, w - half, 1), pltpu.roll(t, half, 1))
    reps = w // cos.shape[1]
    return t * jnp.concatenate([cos] * reps, axis=1) + partner * jnp.concatenate([sin_signed] * reps, axis=1)


def _qkv_kernel(*refs, qd, kd, rope):
    if rope:
        x_ref, sh_ref, sc_ref, w_ref, qg_ref, kg_ref, cos_ref, sin_ref, q_ref, k_ref, v_ref = refs
    else:
        x_ref, sh_ref, sc_ref, w_ref, qg_ref, kg_ref, q_ref, k_ref, v_ref = refs
    h = _modulate(x_ref[...], sh_ref[0], sc_ref[0]).astype(BF16)
    z = _dot(h, w_ref[...])
    seg = (lax.broadcasted_iota(jnp.int32, (LANES, LANES), 0) // HEAD_DIM
           == lax.broadcasted_iota(jnp.int32, (LANES, LANES), 1) // HEAD_DIM).astype(BF16)
    q = _head_rms(z[:, :qd], seg) * qg_ref[...]
    k = _head_rms(z[:, qd:qd + kd], seg) * kg_ref[...]
    if rope:
        q = _rope(q, cos_ref[...], sin_ref[...])
        k = _rope(k, cos_ref[...], sin_ref[...])
    q_ref[...] = (q * (HEAD_DIM ** -0.5 * LOG2E)).astype(q_ref.dtype)
    k_ref[...] = k.astype(k_ref.dtype)
    v_ref[...] = z[:, qd + kd:].astype(v_ref.dtype)


def qkv_project(x, shift, scale, w_qkv, q_gain, k_gain, n_tok, rope_tables=None, tm=1024):
    r, d = x.shape
    kd = N_KV_HEADS * HEAD_DIM
    qd = w_qkv.shape[1] - 2 * kd
    tm = _row_tile(n_tok, tm)
    tpb = n_tok // tm
    ins = [x, shift, scale, w_qkv, q_gain, k_gain]
    specs = [
        pl.BlockSpec((tm, d), lambda i: (i, 0)),
        _mod_spec(shift, tpb),
        _mod_spec(scale, tpb),
        pl.BlockSpec(w_qkv.shape, lambda i: (0, 0)),
        pl.BlockSpec((1, qd), lambda i: (0, 0)),
        pl.BlockSpec((1, kd), lambda i: (0, 0)),
    ]
    if rope_tables is not None:
        ins += list(rope_tables)
        specs += [pl.BlockSpec((tm, LANES), lambda i: (i % tpb, 0))] * 2
    return pl.pallas_call(
        functools.partial(_qkv_kernel, qd=qd, kd=kd, rope=rope_tables is not None),
        out_shape=[jax.ShapeDtypeStruct((r, qd), BF16), jax.ShapeDtypeStruct((r, kd), BF16),
                   jax.ShapeDtypeStruct((r, kd), BF16)],
        grid=(r // tm,),
        in_specs=specs,
        out_specs=[pl.BlockSpec((tm, qd), lambda i: (i, 0)), pl.BlockSpec((tm, kd), lambda i: (i, 0)),
                   pl.BlockSpec((tm, kd), lambda i: (i, 0))],
        compiler_params=_cparams("parallel"),
        name="qkv_project",
    )(*ins)


def _attn_kernel(q_ref, k_ref, vt_ref, o_ref, *scr, grp, bounded):
    k = k_ref[0, 0]
    vt = vt_ref[0, 0]
    tq = q_ref.shape[0]
    n_pairs = grp // 2
    outs = []

    def scores(pair):
        heads = (2 * pair, 2 * pair + 1)
        q2 = jnp.concatenate([q_ref[:, h * HEAD_DIM:(h + 1) * HEAD_DIM] for h in heads], axis=0)
        return lax.dot_general(k, q2, (((1,), (1,)), ((), ())), preferred_element_type=F32)

    if bounded:
        for pair in range(n_pairs):
            p = jnp.exp2(scores(pair))
            scr[pair][...] = p.astype(BF16)
            ot = _dot(vt, scr[pair][...]) / jnp.sum(p, axis=0, keepdims=True)
            o = jnp.concatenate([ot, ot], axis=0).T
            outs += [o[:tq, :HEAD_DIM], o[tq:, :HEAD_DIM]]
        p_refs = ()
    else:
        for pair in range(n_pairs):
            scr[2 * pair][...] = scores(pair)
        for pair in range(n_pairs):
            st_scr, p_scr = scr[2 * pair], scr[2 * pair + 1]
            p_scr[...] = jnp.exp2(st_scr[...] - jnp.max(st_scr[...], axis=0, keepdims=True)).astype(BF16)
        p_refs = scr[1::2]
    for p_scr in p_refs:
        ot = _dot(vt, p_scr[...])
        ot = ot[:HEAD_DIM] / ot[HEAD_DIM:]
        o = jnp.concatenate([ot, ot], axis=0).T
        outs += [o[:tq, :HEAD_DIM], o[tq:, :HEAD_DIM]]
    o_ref[...] = jnp.concatenate(outs, axis=1).astype(o_ref.dtype)


def _attention_call(q, k, vt, n_q, tq, bounded):
    r, qd = q.shape
    _, kvh, n_k, hd = k.shape
    grp = qd // (kvh * hd)
    tq = _row_tile(n_q, tq)
    tpb = n_q // tq
    dts = (BF16,) if bounded else (F32, BF16)
    if not bounded:
        vt = jnp.concatenate([vt, jnp.ones_like(vt)], axis=2)
    return pl.pallas_call(
        functools.partial(_attn_kernel, grp=grp, bounded=bounded),
        out_shape=jax.ShapeDtypeStruct((r, qd), BF16),
        grid=(r // n_q, kvh, tpb),
        in_specs=[
            pl.BlockSpec((tq, grp * hd), lambda b, g, i: (b * tpb + i, g)),
            pl.BlockSpec((1, 1, n_k, hd), lambda b, g, i: (b, g, 0, 0)),
            pl.BlockSpec((1, 1, vt.shape[2], n_k), lambda b, g, i: (b, g, 0, 0)),
        ],
        out_specs=pl.BlockSpec((tq, grp * hd), lambda b, g, i: (b * tpb + i, g)),
        scratch_shapes=[pltpu.VMEM((n_k, 2 * tq), dt) for _ in range(grp // 2) for dt in dts],
        compiler_params=_cparams("parallel", "parallel", "parallel"),
        name="attention_bounded" if bounded else "attention",
    )(q, k, vt)


SCORE_BOUND = 60.0


def attention(q, k, vt, n_q, score_bound, tq_bounded=512, tq_exact=256):
    return lax.cond(score_bound <= SCORE_BOUND,
                    lambda: _attention_call(q, k, vt, n_q, tq_bounded, True),
                    lambda: _attention_call(q, k, vt, n_q, tq_exact, False))


def rope_tables(n_tok):
    t = jnp.arange(n_tok)
    row = (t // GRID_W).astype(F32)
    col = (t % GRID_W).astype(F32)
    n_freq = HEAD_DIM // 4
    inv = ROPE_THETA ** (-jnp.arange(n_freq, dtype=F32) / n_freq)
    ang = jnp.concatenate([row[:, None] * inv, col[:, None] * inv], axis=-1)
    cos, sin = jnp.cos(ang), jnp.sin(ang)
    reps = LANES // HEAD_DIM
    return jnp.tile(jnp.concatenate([cos, cos], -1), (1, reps)), jnp.tile(jnp.concatenate([-sin, sin], -1), (1, reps))


def _deinterleave_heads(w, n_heads):
    lead = w.shape[:-1]
    w = w.reshape(lead + (n_heads, HEAD_DIM // 2, 2))
    return jnp.swapaxes(w, -1, -2).reshape(lead + (n_heads * HEAD_DIM,))


def _split_kv_heads(t, b):
    return jnp.transpose(t.reshape(b, -1, N_KV_HEADS, HEAD_DIM), (0, 2, 1, 3))


def gqa_residual(x, xc, mods_l, mods_c, w_qkv, q_gain, k_gain, w_o, n_tok, n_ctx):
    b = x.shape[0] // n_tok
    kd = N_KV_HEADS * HEAD_DIM
    qd = w_qkv.shape[1] - 2 * kd
    n_qh = qd // HEAD_DIM
    w_perm = jnp.concatenate([_deinterleave_heads(w_qkv[:, :qd], n_qh),
                              _deinterleave_heads(w_qkv[:, qd:qd + kd], N_KV_HEADS), w_qkv[:, qd + kd:]], axis=1)
    w_perm = w_perm.astype(BF16)
    qg = jnp.tile(_deinterleave_heads(q_gain, 1), n_qh).reshape(1, qd)
    kg = jnp.tile(_deinterleave_heads(k_gain, 1), N_KV_HEADS).reshape(1, kd)
    sh_l, sc_l, g_l = mods_l
    sh_c, sc_c, g_c = mods_c
    q_l, k_l, v_l = qkv_project(x, sh_l, sc_l, w_perm, qg, kg, n_tok, rope_tables(n_tok))
    q_c, k_c, v_c = qkv_project(xc, sh_c, sc_c, w_perm, qg, kg, n_ctx)
    k_c4 = _split_kv_heads(k_c, b)
    k_all = jnp.concatenate([k_c4, _split_kv_heads(k_l, b)], axis=2)

    def values_t(v):
        return jnp.transpose(v.reshape(b, -1, N_KV_HEADS, HEAD_DIM), (0, 2, 3, 1))

    vt_c = values_t(v_c)
    vt_all = jnp.concatenate([vt_c, values_t(v_l)], axis=3)
    score_bound = (1.01 * HEAD_DIM * HEAD_DIM ** -0.5 * LOG2E
                   * jnp.max(jnp.abs(q_gain)) * jnp.max(jnp.abs(k_gain)))
    o_l = attention(q_l, k_all, vt_all, n_tok, score_bound)
    o_c = attention(q_c, k_c4, vt_c, n_ctx, score_bound)
    w_o = w_o.astype(BF16)
    return mm_residual(o_l, w_o, x, g_l, n_tok), mm_residual(o_c, w_o, xc, g_c, n_ctx)


def _lru_kernel(xp_ref, x_ref, xn_ref, cw_ref, cb_ref, wa_ref, ba_ref, wx_ref, bx_ref, lam_ref, h0_ref,
                o_ref, hT_ref, a_scr, u_scr, carry_scr, *, tm, n_tok, tpb, reverse):
    step = pl.program_id(1)
    t_idx = (tpb - 1 - step) if reverse else step
    t0 = t_idx * tm
    rows = tm + 2 * SUBLANES

    @pl.when(step == 0)
    def _():
        carry_scr[...] = h0_ref[0]

    xe = jnp.concatenate([jnp.where(t0 > 0, xp_ref[...], 0.0), x_ref[...],
                          jnp.where(t0 + tm < n_tok, xn_ref[...], 0.0)], axis=0)
    left = CONV_W // 2
    conv = cb_ref[...]
    for k in range(CONV_W):
        shift = (left - k) % rows
        tap = xe if shift == 0 else pltpu.roll(xe, shift, 0)
        conv = conv + tap * cw_ref[k:k + 1, :]
    xr = conv[SUBLANES:SUBLANES + tm]

    xb = xr.astype(BF16)
    bw = xr.shape[1] // LRU_BLOCKS
    ra, ia = [], []
    for j in range(LRU_BLOCKS):
        blk = xb[:, j * bw:(j + 1) * bw]
        ra.append(_dot(blk, wa_ref[j]))
        ia.append(_dot(blk, wx_ref[j]))
    r = _sigmoid(jnp.concatenate(ra, axis=1) + ba_ref[...])
    i = _sigmoid(jnp.concatenate(ia, axis=1) + bx_ref[...])
    log_a = -LRU_C * r * jnp.logaddexp(-lam_ref[...], 0.0)
    a = jnp.exp(log_a)
    a_scr[...] = a
    t = jnp.tanh(log_a)
    u_scr[...] = xr * i * jnp.sqrt(-2.0 * t / (1.0 - t))

    n_grp = tm // SUBLANES
    sub = lax.broadcasted_iota(jnp.int32, (SUBLANES, 1), 0)

    def group(j, carry):
        g = (n_grp - 1 - j) if reverse else j
        rws = pl.ds(pl.multiple_of(g * SUBLANES, SUBLANES), SUBLANES)
        ag, ug = a_scr[rws, :], u_scr[rws, :]
        s = 1
        while s < SUBLANES:
            if reverse:
                ok = sub < SUBLANES - s
                sh = SUBLANES - s
            else:
                ok = sub >= s
                sh = s
            u_prev = jnp.where(ok, pltpu.roll(ug, sh, 0), 0.0)
            a_prev = jnp.where(ok, pltpu.roll(ag, sh, 0), 1.0)
            ug = ug + ag * u_prev
            ag = ag * a_prev
            s *= 2
        hg = ug + ag * carry
        o_ref[rws, :] = hg
        return hg[0:1, :] if reverse else hg[SUBLANES - 1:SUBLANES, :]

    carry = lax.fori_loop(0, n_grp, group, carry_scr[...])
    carry_scr[...] = carry
    hT_ref[0] = carry


def lru_scan(xpre, conv_w, conv_b, wa, ba, wx, bx, lam, h0, n_tok, reverse, tm=1024):
    r, w = xpre.shape
    b = r // n_tok
    tm = _row_tile(n_tok, tm)
    tpb = n_tok // tm
    hb = tm // SUBLANES
    last = r // SUBLANES - 1

    def tile(bi, s):
        return bi * tpb + ((tpb - 1 - s) if reverse else s)

    vec = pl.BlockSpec((1, w), lambda bi, s: (0, 0))
    blocks = pl.BlockSpec(wa.shape, lambda bi, s: (0, 0, 0))
    return pl.pallas_call(
        functools.partial(_lru_kernel, tm=tm, n_tok=n_tok, tpb=tpb, reverse=reverse),
        out_shape=[jax.ShapeDtypeStruct((r, w), F32), jax.ShapeDtypeStruct((b, 1, w), F32)],
        grid=(b, tpb),
        in_specs=[
            pl.BlockSpec((SUBLANES, w), lambda bi, s: (jnp.maximum(tile(bi, s) * hb - 1, 0), 0)),
            pl.BlockSpec((tm, w), lambda bi, s: (tile(bi, s), 0)),
            pl.BlockSpec((SUBLANES, w), lambda bi, s: (jnp.minimum((tile(bi, s) + 1) * hb, last), 0)),
            pl.BlockSpec((CONV_W, w), lambda bi, s: (0, 0)),
            vec, blocks, vec, blocks, vec, vec,
            pl.BlockSpec((1, 1, w), lambda bi, s: (bi, 0, 0)),
        ],
        out_specs=[pl.BlockSpec((tm, w), lambda bi, s: (tile(bi, s), 0)),
                   pl.BlockSpec((1, 1, w), lambda bi, s: (bi, 0, 0))],
        scratch_shapes=[pltpu.VMEM((tm, w), F32), pltpu.VMEM((tm, w), F32), pltpu.VMEM((1, w), F32)],
        compiler_params=_cparams("parallel", "arbitrary"),
        name="lru_scan",
    )(xpre, xpre, xpre, conv_w, conv_b.reshape(1, w), wa, ba.reshape(1, w), wx, bx.reshape(1, w),
      lam.reshape(1, w), h0)


def rglru_residual(x, xc, mods_l, mods_c, w_in, conv_w, conv_b, wa, ba, wx, bx, lam, w_out, n_tok, n_ctx):
    b = x.shape[0] // n_tok
    w = w_in.shape[1] // 2
    w_in = w_in.astype(BF16)
    sh_l, sc_l, g_l = mods_l
    sh_c, sc_c, g_c = mods_c
    y_l, xp_l = modmm(x, sh_l, sc_l, w_in, n_tok, (w, w), ("gelu", None), (F32, F32))
    y_c, xp_c = modmm(xc, sh_c, sc_c, w_in, n_ctx, (w, w), ("gelu", None), (F32, F32))
    zero = jnp.zeros((b, 1, w), F32)
    hs_l, hs_c = [], []
    for d in range(2):
        gate_w = (conv_w, conv_b, wa[d].astype(BF16), ba[d], wx[d].astype(BF16), bx[d], lam[d])
        hc, state = lru_scan(xp_c, *gate_w, zero, n_ctx, reverse=d == 1)
        hl, _ = lru_scan(xp_l, *gate_w, state, n_tok, reverse=d == 1)
        hs_c.append(hc)
        hs_l.append(hl)
    w_out = w_out.astype(BF16)
    return (mm_residual(tuple(hs_l), w_out, x, g_l, n_tok, b=y_l),
            mm_residual(tuple(hs_c), w_out, xc, g_c, n_ctx, b=y_c))


def _gmlp_kernel(x_ref, sh_ref, sc_ref, g_ref, win_ref, lng_ref, lnb_ref, ws_ref, bs_ref, wout_ref, o_ref, *, half):
    x = x_ref[...]
    h = _modulate(x, sh_ref[0], sc_ref[0]).astype(BF16)
    u = _gelu(_dot(h, win_ref[:, :half]))
    v = _gelu(_dot(h, win_ref[:, half:]))
    mu = jnp.mean(v, axis=-1, keepdims=True)
    vc = v - mu
    var = jnp.mean(vc * vc, axis=-1, keepdims=True)
    vn = (vc * lax.rsqrt(var + EPS) * lng_ref[...] + lnb_ref[...]).astype(BF16)
    gw = half // GMLP_GROUPS
    chunks = []
    for c in range(x.shape[0] // GMLP_CHUNK):
        rws = slice(c * GMLP_CHUNK, (c + 1) * GMLP_CHUNK)
        chunks.append(jnp.concatenate(
            [_dot(ws_ref[g], vn[rws, g * gw:(g + 1) * gw]) + bs_ref[g] for g in range(GMLP_GROUPS)], axis=1))
    v2 = jnp.concatenate(chunks, axis=0)
    o_ref[...] = x + g_ref[0] * _dot((u * v2).astype(BF16), wout_ref[...])


def gmlp_residual(x, shift, scale, gate, w_in, ln_g, ln_b, w_s, b_s, w_out, n_tok, tm=256):
    r, d = x.shape
    half = w_in.shape[1] // 2
    tm = _row_tile(n_tok, tm)
    tpb = n_tok // tm
    assert tm % GMLP_CHUNK == 0
    const2 = lambda i: (0, 0)
    const3 = lambda i: (0, 0, 0)
    once = pl.Buffered(1)
    return pl.pallas_call(
        functools.partial(_gmlp_kernel, half=half),
        out_shape=jax.ShapeDtypeStruct((r, d), F32),
        grid=(r // tm,),
        in_specs=[
            pl.BlockSpec((tm, d), lambda i: (i, 0)),
            _mod_spec(shift, tpb),
            _mod_spec(scale, tpb),
            _mod_spec(gate, tpb),
            pl.BlockSpec(w_in.shape, const2, pipeline_mode=once),
            pl.BlockSpec((1, half), const2),
            pl.BlockSpec((1, half), const2),
            pl.BlockSpec(w_s.shape, const3),
            pl.BlockSpec(b_s.shape + (1,), const3),
            pl.BlockSpec(w_out.shape, const2, pipeline_mode=once),
        ],
        out_specs=pl.BlockSpec((tm, d), lambda i: (i, 0)),
        compiler_params=_cparams("parallel"),
        name="gmlp_mixer",
    )(x, shift, scale, gate, w_in.astype(BF16), ln_g.reshape(1, half), ln_b.reshape(1, half),
      w_s.astype(BF16), b_s[..., None], w_out.astype(BF16))


def kernel(x, c, ctx, c_ctx, mod_w, mod_b, pool_w, pool_scale, lru_w_in, lru_conv_w, lru_conv_b, lru_wa, lru_ba,
           lru_wx, lru_bx, lru_lam, lru_w_out, attn_w_qkv, attn_q_gain, attn_k_gain, attn_w_o, gmlp_w_in, gmlp_ln_g,
           gmlp_ln_b, gmlp_w_s, gmlp_b_s, gmlp_w_out, moe_router, moe_w_gate, moe_w_up, moe_w_down, final_gain):
    b, n_tok, d = x.shape
    n_ctx = ctx.shape[1]
    depth = mod_w.shape[0]
    n_mixers = 4
    xl = x.reshape(b * n_tok, d)
    xc = ctx.reshape(b * n_ctx, d)

    pad = -(b + 1) % SUBLANES
    cvec = jnp.concatenate([c, c_ctx[None, :], jnp.zeros((pad, d), F32)], axis=0)
    mods = mod_vectors(cvec, mod_w, mod_b)

    for i in range(depth):
        m, j = i % n_mixers, i // n_mixers
        last = i == depth - 1
        ml = [mods[i, :b, k * d:(k + 1) * d].reshape(b, 1, d) for k in range(6)]
        mc = [mods[i, b:b + 1, k * d:(k + 1) * d].reshape(1, 1, d) for k in range(6)]
        if m == 0:
            pw = pool_w[j].astype(BF16)
            xl_new = pool_mixer_residual(xl, ml[0], ml[1], ml[2], pw, pool_scale[j], n_tok)
            if not last:
                xc = pool_mixer_residual(xc, mc[0], mc[1], mc[2], pw, pool_scale[j], n_ctx)
            xl = xl_new
        elif m == 1:
            xl, xc_new = rglru_residual(xl, xc, ml[:3], mc[:3], lru_w_in[j], lru_conv_w[j], lru_conv_b[j], lru_wa[j],
                                        lru_ba[j], lru_wx[j], lru_bx[j], lru_lam[j], lru_w_out[j], n_tok, n_ctx)
            xc = xc if last else xc_new
        elif m == 2:
            xl, xc_new = gqa_residual(xl, xc, ml[:3], mc[:3], attn_w_qkv[j], attn_q_gain[j], attn_k_gain[j],
                                      attn_w_o[j], n_tok, n_ctx)
            xc = xc if last else xc_new
        else:
            gargs = (gmlp_w_in[j], gmlp_ln_g[j], gmlp_ln_b[j], gmlp_w_s[j], gmlp_b_s[j], gmlp_w_out[j])
            xl_new = gmlp_residual(xl, ml[0], ml[1], ml[2], *gargs, n_tok)
            if not last:
                xc = gmlp_residual(xc, mc[0], mc[1], mc[2], *gargs, n_ctx)
            xl = xl_new
        wr = jnp.pad(moe_router[i], ((0, 0), (0, LANES - N_EXPERTS)))
        xl = moe_residual(xl, ml[3], ml[4], ml[5], wr, moe_w_gate, moe_w_up, moe_w_down, i, n_tok,
                          final_gain if last else None)
        if not last:
            xc = moe_residual(xc, mc[3], mc[4], mc[5], wr, moe_w_gate, moe_w_up, moe_w_down, i, n_ctx)
    return xl.reshape(b, n_tok, d)
```

```python
import functools

import jax
import jax.numpy as jnp
from jax import lax
from jax.experimental import pallas as pl
from jax.experimental.pallas import tpu as pltpu

F32 = jnp.float32
BF16 = jnp.bfloat16
EPS = 1e-6

N_EXPERTS = 16
CAPACITY_FACTOR = 2
POOL_WINDOWS = (2, 4, 8, 16)
POOL_HALO = 8
LRU_BLOCKS = 8
LRU_C = 8.0
CONV_W = 4
HEAD_DIM = 64
N_KV_HEADS = 4
GRID_W = 64
ROPE_THETA = 10000.0
GMLP_GROUPS = 4
GMLP_CHUNK = 128

LANES = 128
SUBLANES = 8
BF16_SUBLANES = 16
LOG2E = 1.4426950408889634
VMEM_LIMIT = 56 * 1024 * 1024


def _cparams(*sem):
    return pltpu.CompilerParams(dimension_semantics=sem, vmem_limit_bytes=VMEM_LIMIT)


def _modulate(x, shift, scale):
    ms = jnp.mean(x * x, axis=-1, keepdims=True)
    return x * lax.rsqrt(ms + EPS) * (1.0 + scale) + shift


def _split_bf16(a):
    hi = a.astype(BF16)
    lo = (a - hi.astype(F32)).astype(BF16)
    return hi, lo


def _dot(a, b):
    return jnp.dot(a, b, preferred_element_type=F32)


def _dot3(a, b):
    a_hi, a_lo = _split_bf16(a)
    b_hi, b_lo = _split_bf16(b)
    return _dot(a_hi, b_hi) + (_dot(a_hi, b_lo) + _dot(a_lo, b_hi))


def _gelu(x):
    return 0.5 * x * (1.0 + jnp.tanh(0.7978845608028654 * (x + 0.044715 * (x * x * x))))


def _silu(x):
    return x * (1.0 / (1.0 + jnp.exp(-x)))


def _sigmoid(x):
    return 0.5 * jnp.tanh(0.5 * x) + 0.5


def _mod_spec(mod, tiles_per_batch):
    d = mod.shape[-1]
    if mod.shape[0] == 1:
        return pl.BlockSpec((1, 1, d), lambda i, *_: (0, 0, 0))
    return pl.BlockSpec((1, 1, d), lambda i, *_: (i // tiles_per_batch, 0, 0))


def _row_tile(n, want):
    t = min(n, want)
    assert n % t == 0
    return t


def _mod_kernel(c_ref, w_ref, b_ref, o_ref):
    s = _silu(c_ref[...])
    o_ref[0] = _dot3(s, w_ref[0]) + b_ref[0]


def mod_vectors(cvec, mod_w, mod_b):
    depth, d, n6 = mod_w.shape
    tn = 1024
    return pl.pallas_call(
        _mod_kernel,
        out_shape=jax.ShapeDtypeStruct((depth, cvec.shape[0], n6), F32),
        grid=(depth, n6 // tn),
        in_specs=[
            pl.BlockSpec(cvec.shape, lambda l, j: (0, 0)),
            pl.BlockSpec((1, d, tn), lambda l, j: (l, 0, j)),
            pl.BlockSpec((1, 1, tn), lambda l, j: (l, 0, j)),
        ],
        out_specs=pl.BlockSpec((1, cvec.shape[0], tn), lambda l, j: (l, 0, j)),
        compiler_params=_cparams("parallel", "parallel"),
        name="mod_vectors",
    )(cvec, mod_w, mod_b.reshape(depth, 1, n6))


def _modmm_kernel(x_ref, sh_ref, sc_ref, w_ref, *o_refs, splits, acts):
    h = _modulate(x_ref[...], sh_ref[0], sc_ref[0]).astype(BF16)
    z = _dot(h, w_ref[...])
    off = 0
    for o_ref, width, act in zip(o_refs, splits, acts):
        part = z[:, off:off + width]
        if act == "gelu":
            part = _gelu(part)
        o_ref[...] = part.astype(o_ref.dtype)
        off += width


def modmm(x, shift, scale, w, n_tok, splits, acts, dtypes, tm=512):
    r, d = x.shape
    tm = _row_tile(n_tok, tm)
    tpb = n_tok // tm
    n = w.shape[1]
    assert sum(splits) == n
    return pl.pallas_call(
        functools.partial(_modmm_kernel, splits=splits, acts=acts),
        out_shape=[jax.ShapeDtypeStruct((r, s), dt) for s, dt in zip(splits, dtypes)],
        grid=(r // tm,),
        in_specs=[
            pl.BlockSpec((tm, d), lambda i: (i, 0)),
            _mod_spec(shift, tpb),
            _mod_spec(scale, tpb),
            pl.BlockSpec((d, n), lambda i: (0, 0)),
        ],
        out_specs=[pl.BlockSpec((tm, s), lambda i: (i, 0)) for s in splits],
        compiler_params=_cparams("parallel"),
        name="modmm",
    )(x, shift, scale, w)


def _mmres_kernel(*refs, n_sum, has_mul):
    a = refs[0][...]
    for a_ref in refs[1:n_sum]:
        a = a + a_ref[...]
    refs = refs[n_sum:]
    if has_mul:
        a = a.astype(F32) * refs[0][...].astype(F32)
        refs = refs[1:]
    w_ref, x_ref, g_ref, o_ref = refs
    o_ref[...] = x_ref[...] + g_ref[0] * _dot(a.astype(BF16), w_ref[...])


def mm_residual(a, w, x, gate, n_tok, b=None, tm=1024):
    a = a if isinstance(a, (tuple, list)) else (a,)
    r, k = a[0].shape
    n = w.shape[1]
    tm = _row_tile(n_tok, tm)
    tpb = n_tok // tm
    ins = list(a) + ([b] if b is not None else []) + [w, x, gate]
    specs = [pl.BlockSpec((tm, k), lambda i: (i, 0))] * (len(a) + (b is not None))
    specs += [
        pl.BlockSpec((k, n), lambda i: (0, 0)),
        pl.BlockSpec((tm, n), lambda i: (i, 0)),
        _mod_spec(gate, tpb),
    ]
    return pl.pallas_call(
        functools.partial(_mmres_kernel, n_sum=len(a), has_mul=b is not None),
        out_shape=jax.ShapeDtypeStruct((r, n), F32),
        grid=(r // tm,),
        in_specs=specs,
        out_specs=pl.BlockSpec((tm, n), lambda i: (i, 0)),
        compiler_params=_cparams("parallel"),
        name="mm_residual",
    )(*ins)


def _pool_kernel(xp_ref, x_ref, xn_ref, sh_ref, sc_ref, g_ref, w_ref, ps_ref, o_ref, *, tm, n_tok, tpb):
    t0 = (pl.program_id(0) % tpb) * tm
    x = x_ref[...]
    rows = tm + 2 * POOL_HALO
    h = jnp.concatenate([
        jnp.where(t0 > 0, _modulate(xp_ref[...], sh_ref[0], sc_ref[0]), 0.0),
        _modulate(x, sh_ref[0], sc_ref[0]),
        jnp.where(t0 + tm < n_tok, _modulate(xn_ref[...], sh_ref[0], sc_ref[0]), 0.0)], axis=0)
    posc = lax.broadcasted_iota(jnp.int32, (tm, 1), 0) + t0
    gw = h.shape[1] // len(POOL_WINDOWS)
    outs = []
    for g, win in enumerate(POOL_WINDOWS):
        hg = h[:, g * gw:(g + 1) * gw]
        c = hg + pltpu.roll(hg, 1, 0)
        step = 1
        while 2 * step < win:
            c = pltpu.roll(c, step, 0) + pltpu.roll(c, rows - step, 0)
            step *= 2
        cnt = jnp.minimum(posc + (win - win // 2), n_tok) - jnp.maximum(posc - win // 2, 0)
        pooled = c[POOL_HALO:POOL_HALO + tm] / cnt.astype(F32) - hg[POOL_HALO:POOL_HALO + tm]
        outs.append(_dot(pooled.astype(BF16), w_ref[g]))
    y = jnp.concatenate(outs, axis=1) * ps_ref[...]
    o_ref[...] = x + g_ref[0] * y


def pool_mixer_residual(x, shift, scale, gate, w_pool, pool_scale, n_tok, tm=512):
    r, d = x.shape
    tm = _row_tile(n_tok, tm)
    tpb = n_tok // tm
    hb = tm // POOL_HALO
    last = r // POOL_HALO - 1
    groups, gw, _ = w_pool.shape
    return pl.pallas_call(
        functools.partial(_pool_kernel, tm=tm, n_tok=n_tok, tpb=tpb),
        out_shape=jax.ShapeDtypeStruct((r, d), F32),
        grid=(r // tm,),
        in_specs=[
            pl.BlockSpec((POOL_HALO, d), lambda i: (jnp.maximum(i * hb - 1, 0), 0)),
            pl.BlockSpec((tm, d), lambda i: (i, 0)),
            pl.BlockSpec((POOL_HALO, d), lambda i: (jnp.minimum((i + 1) * hb, last), 0)),
            _mod_spec(shift, tpb),
            _mod_spec(scale, tpb),
            _mod_spec(gate, tpb),
            pl.BlockSpec((groups, gw, gw), lambda i: (0, 0, 0)),
            pl.BlockSpec((1, d), lambda i: (0, 0)),
        ],
        out_specs=pl.BlockSpec((tm, d), lambda i: (i, 0)),
        compiler_params=_cparams("parallel"),
        name="pool_mixer",
    )(x, x, x, shift, scale, gate, w_pool, pool_scale.reshape(1, d))


def _router_kernel(x_ref, sh_ref, sc_ref, wr_ref, h_ref, aff_ref):
    h = _modulate(x_ref[...], sh_ref[0], sc_ref[0])
    h_ref[...] = h.astype(h_ref.dtype)
    logits = _dot3(h, wr_ref[...])
    lane = lax.broadcasted_iota(jnp.int32, logits.shape, 1)
    logits = jnp.where(lane < N_EXPERTS, logits, -jnp.inf)
    e = jnp.exp(logits - jnp.max(logits, axis=-1, keepdims=True))
    aff_ref[...] = e / jnp.sum(e, axis=-1, keepdims=True)


def router(x, shift, scale, w_router_padded, n_tok, tm=1024):
    r, d = x.shape
    tm = _row_tile(n_tok, tm)
    tpb = n_tok // tm
    return pl.pallas_call(
        _router_kernel,
        out_shape=[jax.ShapeDtypeStruct((r, d), BF16), jax.ShapeDtypeStruct((r, LANES), F32)],
        grid=(r // tm,),
        in_specs=[
            pl.BlockSpec((tm, d), lambda i: (i, 0)),
            _mod_spec(shift, tpb),
            _mod_spec(scale, tpb),
            pl.BlockSpec((d, LANES), lambda i: (0, 0)),
        ],
        out_specs=[pl.BlockSpec((tm, d), lambda i: (i, 0)), pl.BlockSpec((tm, LANES), lambda i: (i, 0))],
        compiler_params=_cparams("parallel"),
        name="moe_router",
    )(x, shift, scale, w_router_padded)


MOE_CHUNK = 128
EXPERT_GROUPS = 2
COMBINE_CHUNKS = 2
NARROW_WINDOW = 64
PACK = LANES // N_EXPERTS


def _cumsum_rows(v, tb):
    n = v.shape[0]
    tri = (lax.broadcasted_iota(jnp.int32, (tb, tb), 0) >= lax.broadcasted_iota(jnp.int32, (tb, tb), 1)).astype(BF16)
    carry = jnp.zeros((1, v.shape[1]), F32)
    outs, starts = [], []
    for j in range(n // tb):
        starts.append(carry)
        c = _dot(tri, v[j * tb:(j + 1) * tb].astype(BF16)) + carry
        carry = c[tb - 1:tb, :]
        outs.append(c)
    return jnp.concatenate(outs, axis=0), jnp.concatenate(starts, axis=0)


def _route_kernel(aff_ref, affp_ref, idx_ref, gate_ref, slot_ref, start_ref, slot_scr, acc_scr, *, n, cap):
    def enough(cand):
        cnt = jnp.sum((affp_ref[...] >= pltpu.bitcast(cand, F32)).astype(F32), axis=0, keepdims=True)
        shift = N_EXPERTS
        while shift < LANES:
            cnt = cnt + pltpu.roll(cnt, shift, 1)
            shift *= 2
        return cnt >= cap

    def two_bits(k, prefix):
        low = 28 - 2 * k
        c1, c2, c3 = (prefix | jnp.left_shift(jnp.int32(j), low) for j in (1, 2, 3))
        return jnp.where(enough(c3), c3, jnp.where(enough(c2), c2, jnp.where(enough(c1), c1, prefix)))

    top = jnp.full((1, LANES), 1 << 30, jnp.int32)
    thr = jnp.where(enough(top), top, jnp.zeros_like(top))
    thr = pltpu.bitcast(lax.fori_loop(0, 15, two_bits, thr), F32)
    aff = aff_ref[...]
    gt = aff > thr
    eq = aff == thr
    need = cap - jnp.sum(gt.astype(F32), axis=0, keepdims=True)
    eq_rank, _ = _cumsum_rows(eq.astype(F32), MOE_CHUNK)
    sel = gt | (eq & (eq_rank <= need))
    pos, starts = _cumsum_rows(sel.astype(F32), MOE_CHUNK)
    slot_scr[...] = jnp.where(sel, pos - 1.0, -1.0)
    slot_ref[...] = slot_scr[...]
    start_ref[0] = starts.astype(jnp.int32)

    slots = lax.broadcasted_iota(jnp.int32, (1, cap), 1).astype(F32)
    sub = lax.broadcasted_iota(jnp.int32, (SUBLANES, MOE_CHUNK), 0)
    local = lax.broadcasted_iota(jnp.int32, (SUBLANES, MOE_CHUNK), 1).astype(F32)
    acc_scr[...] = jnp.zeros(acc_scr.shape, F32)

    def chunk(c, carry):
        rows = pl.ds(pl.multiple_of(c * MOE_CHUNK, MOE_CHUNK), MOE_CHUNK)
        base = jnp.where(sub == 0, local, jnp.where(sub == 1, lax.convert_element_type(c, F32), 0.0))
        aff_t = aff_ref[rows, :].T
        for e in range(N_EXPERTS):
            onehot = jnp.where(slot_scr[rows, e:e + 1] == slots, 1.0, 0.0).astype(BF16)
            g = aff_t[e:e + 1, :]
            g_hi = g.astype(BF16).astype(F32)
            g_mid = (g - g_hi).astype(BF16).astype(F32)
            g_lo = g - g_hi - g_mid
            lhs = jnp.where(sub == 2, g_hi, jnp.where(sub == 3, g_mid, jnp.where(sub == 4, g_lo, base)))
            acc_scr[e] += _dot(lhs.astype(BF16), onehot)
        return carry

    lax.fori_loop(0, n // MOE_CHUNK, chunk, 0)
    for e in range(N_EXPERTS):
        idx_ref[0, e:e + 1, :] = (acc_scr[e, 0:1, :] + MOE_CHUNK * acc_scr[e, 1:2, :]).astype(jnp.int32)
        gate_ref[0, e:e + 1, :] = acc_scr[e, 2:3, :] + acc_scr[e, 3:4, :] + acc_scr[e, 4:5, :]


def route(aff, n_tok):
    r = aff.shape[0]
    b = r // n_tok
    cap = CAPACITY_FACTOR * n_tok // N_EXPERTS
    aff_packed = aff[:, :N_EXPERTS].reshape(r // PACK, LANES)
    n_chunks = n_tok // MOE_CHUNK
    return pl.pallas_call(
        functools.partial(_route_kernel, n=n_tok, cap=cap),
        out_shape=[jax.ShapeDtypeStruct((b, N_EXPERTS, cap), jnp.int32), jax.ShapeDtypeStruct((b, N_EXPERTS, cap), F32),
                   jax.ShapeDtypeStruct((r, LANES), F32), jax.ShapeDtypeStruct((b, n_chunks, LANES), jnp.int32)],
        grid=(b,),
        in_specs=[pl.BlockSpec((n_tok, LANES), lambda i: (i, 0)),
                  pl.BlockSpec((n_tok // PACK, LANES), lambda i: (i, 0))],
        out_specs=[pl.BlockSpec((1, N_EXPERTS, cap), lambda i: (i, 0, 0)),
                   pl.BlockSpec((1, N_EXPERTS, cap), lambda i: (i, 0, 0)),
                   pl.BlockSpec((n_tok, LANES), lambda i: (i, 0)),
                   pl.BlockSpec((1, n_chunks, LANES), lambda i: (i, 0, 0))],
        scratch_shapes=[pltpu.VMEM((n_tok, LANES), F32), pltpu.VMEM((N_EXPERTS, SUBLANES, cap), F32)],
        compiler_params=_cparams("parallel"),
        name="moe_route",
    )(aff, aff_packed)


def _ffn_kernel(doff_ref, x_ref, gt_ref, wg_ref, wu_ref, wd_ref, y_ref, dup_ref, wg_scr, wu_scr, wd_scr, y_scr, *,
                e0, cap, n_steps, dwin):
    @pl.when(pl.program_id(1) == 0)
    def _():
        wg_scr[...] = wg_ref[0, 0].astype(BF16)
        wu_scr[...] = wu_ref[0, 0].astype(BF16)
        wd_scr[...] = wd_ref[0, 0].astype(BF16)

    x = x_ref[0]
    a = _dot(x, wg_scr[...])
    u = _dot(x, wu_scr[...])
    hmid = (_silu(a) * u).astype(BF16)
    gate_cols = jnp.broadcast_to(gt_ref[0, 0], (LANES, x.shape[0])).T
    y = _dot(hmid, wd_scr[...])
    y_scr[...] = (y * jnp.concatenate([gate_cols] * (y.shape[1] // LANES), axis=1)).astype(BF16)
    y_ref[0] = y_scr[...]
    e = e0 + pl.program_id(0)
    align = min(dwin, BF16_SUBLANES)
    for bl in range(y_scr.shape[0] // cap):
        b = pl.program_id(1) * (y_scr.shape[0] // cap) + bl
        for s in range(n_steps):
            off = pl.multiple_of(doff_ref[(b * n_steps + s) * N_EXPERTS + e], align)
            dup_ref[bl, s, 0] = y_scr[pl.ds(bl * cap + off, dwin), :]


def expert_ffn(xs, gates, doffs, w_gate, w_up, w_down, layer, e0, cap, n_steps, dwin, tm=512):
    e, m, d = xs.shape
    ff = w_gate.shape[-1]
    tm = max(_row_tile(m, tm), cap)
    bpt = tm // cap
    return pl.pallas_call(
        functools.partial(_ffn_kernel, e0=e0, cap=cap, n_steps=n_steps, dwin=dwin),
        out_shape=[jax.ShapeDtypeStruct((e, m, d), BF16),
                   jax.ShapeDtypeStruct((m // cap, n_steps, e, dwin, d), BF16)],
        grid_spec=pltpu.PrefetchScalarGridSpec(
            num_scalar_prefetch=1,
            grid=(e, m // tm),
            in_specs=[
                pl.BlockSpec((1, tm, d), lambda k, i, off: (k, i, 0)),
                pl.BlockSpec((1, 1, 1, tm), lambda k, i, off: (k, i, 0, 0)),
                pl.BlockSpec((1, 1, d, ff), lambda k, i, off: (layer, e0 + k, 0, 0)),
                pl.BlockSpec((1, 1, d, ff), lambda k, i, off: (layer, e0 + k, 0, 0)),
                pl.BlockSpec((1, 1, ff, d), lambda k, i, off: (layer, e0 + k, 0, 0)),
            ],
            out_specs=[pl.BlockSpec((1, tm, d), lambda k, i, off: (k, i, 0)),
                       pl.BlockSpec((bpt, n_steps, 1, dwin, d), lambda k, i, off: (i, 0, k, 0, 0))],
            scratch_shapes=[pltpu.VMEM((d, ff), BF16), pltpu.VMEM((d, ff), BF16), pltpu.VMEM((ff, d), BF16),
                            pltpu.VMEM((tm, d), BF16)],
        ),
        compiler_params=_cparams("parallel", "arbitrary"),
        name="moe_expert_ffn",
    )(doffs, xs, gates.reshape(e, m // tm, 1, tm), w_gate, w_up, w_down)


def _combine_kernel(woff_ref, doff_ref, fit_ref, ysel_ref, x_ref, g_ref, slot_ref, *rest, win, dwin, n_chunks, cps,
                    n_groups, final):
    dup_refs, y_refs, rest = rest[:n_groups], rest[n_groups:2 * n_groups], rest[2 * n_groups:]
    if final:
        fg_ref, o_ref = rest
    else:
        (o_ref,) = rest
    per_group = N_EXPERTS // n_groups
    step = pl.program_id(0) * (n_chunks // cps) + pl.program_id(1)

    def finish(rows, acc):
        out = x_ref[rows, :] + g_ref[0] * acc
        if final:
            out = out * lax.rsqrt(jnp.mean(out * out, axis=-1, keepdims=True) + EPS) * fg_ref[...]
        o_ref[rows, :] = out

    @pl.when(fit_ref[step] == 1)
    def _():
        tm = x_ref.shape[0]
        epl = LANES // dwin
        lane = lax.broadcasted_iota(jnp.int32, (1, LANES), 1)
        which = lane // dwin
        within = (lane % dwin).astype(F32)
        acc = jnp.zeros((tm, x_ref.shape[1]), F32)
        for gi, dup_ref in enumerate(dup_refs):
            pieces = []
            for p in range(per_group // epl):
                col, sl = None, None
                for w in range(epl):
                    e = gi * per_group + p * epl + w
                    cw = lax.convert_element_type(doff_ref[step * N_EXPERTS + e], F32) + within
                    sw = jnp.broadcast_to(slot_ref[:, e:e + 1], (tm, LANES))
                    col = cw if w == 0 else jnp.where(which == w, cw, col)
                    sl = sw if w == 0 else jnp.where(which == w, sw, sl)
                pieces.append(jnp.where(sl == col, 1.0, 0.0).astype(BF16))
            onehot = jnp.concatenate(pieces, axis=1)
            acc = acc + _dot(onehot, dup_ref[0, 0].reshape(per_group * dwin, x_ref.shape[1]))
        finish(slice(0, tm), acc)

    @pl.when(fit_ref[step] == 0)
    def _():
        align = min(win, LANES)
        for sc in range(cps):
            rows = slice(sc * MOE_CHUNK, (sc + 1) * MOE_CHUNK)
            base = (step * cps + sc) * N_EXPERTS
            acc = jnp.zeros((MOE_CHUNK, x_ref.shape[1]), F32)
            for e in range(N_EXPERTS):
                off = pl.multiple_of(woff_ref[base + e], align)
                cols = (lax.broadcasted_iota(jnp.int32, (1, win), 1) + off).astype(F32)
                onehot = jnp.where(slot_ref[rows, e:e + 1] == cols, 1.0, 0.0).astype(BF16)
                acc = acc + _dot(onehot, y_refs[e // per_group][e % per_group, pl.ds(off, win), :])
            finish(rows, acc)


def moe_combine(x, gate2, slot, ys, dups, woffs, doffs, fits, n_tok, final_gain=None):
    r, d = x.shape
    b = r // n_tok
    per_group = ys[0].shape[0]
    cap = ys[0].shape[1] // b
    dwin = dups[0].shape[3]
    n_chunks = n_tok // MOE_CHUNK
    cps = COMBINE_CHUNKS if n_chunks % COMBINE_CHUNKS == 0 else 1
    tm = cps * MOE_CHUNK
    steps = n_chunks // cps
    win = min(2 * MOE_CHUNK, cap)
    final = final_gain is not None
    need = jnp.max((1 - fits).reshape(b, steps), axis=1)
    ysel = lax.cummax(need * jnp.arange(b, dtype=jnp.int32))
    row_spec = lambda w: pl.BlockSpec((tm, w), lambda bi, c, *_: (bi * steps + c, 0))
    ins = [woffs, doffs, fits, ysel, x, gate2, slot] + list(dups) + list(ys)
    specs = [row_spec(d), _mod_spec(gate2, 1), row_spec(LANES)]
    specs += [pl.BlockSpec((1, 1, per_group, dwin, d), lambda bi, c, *_: (bi, c, 0, 0, 0))] * len(dups)
    specs += [pl.BlockSpec((per_group, cap, d), lambda bi, c, wo, do, ft, ys_: (0, ys_[bi], 0),
                           pipeline_mode=pl.Buffered(1))] * len(ys)
    if final:
        ins.append(final_gain.reshape(1, d))
        specs.append(pl.BlockSpec((1, d), lambda bi, c, *_: (0, 0)))
    return pl.pallas_call(
        functools.partial(_combine_kernel, win=win, dwin=dwin, n_chunks=n_chunks, cps=cps, n_groups=len(ys),
                          final=final),
        out_shape=jax.ShapeDtypeStruct((r, d), F32),
        grid_spec=pltpu.PrefetchScalarGridSpec(
            num_scalar_prefetch=4,
            grid=(b, steps),
            in_specs=specs,
            out_specs=row_spec(d),
        ),
        compiler_params=_cparams("parallel", "arbitrary"),
        name="moe_combine",
    )(*ins)


def moe_residual(x, shift, scale, gate2, w_router_padded, w_gate, w_up, w_down, layer, n_tok, final_gain=None):
    r, d = x.shape
    b = r // n_tok
    h, aff = router(x, shift, scale, w_router_padded, n_tok)
    idx, gates, slot, start = route(aff, n_tok)
    cap = idx.shape[-1]
    n_chunks = n_tok // MOE_CHUNK
    cps = COMBINE_CHUNKS if n_chunks % COMBINE_CHUNKS == 0 else 1
    steps = n_chunks // cps
    start = start[:, :, :N_EXPERTS]
    win = min(2 * MOE_CHUNK, cap)
    align = min(win, LANES)
    woffs = jnp.clip(start // align * align, 0, cap - win).reshape(-1)
    dwin = min(NARROW_WINDOW, cap)
    dalign = min(dwin, BF16_SUBLANES)
    s_start = start[:, ::cps]
    s_end = jnp.concatenate([s_start[:, 1:], jnp.full((b, 1, N_EXPERTS), cap, jnp.int32)], axis=1)
    doffs = jnp.clip(s_start // dalign * dalign, 0, cap - dwin)
    fits = jnp.all(s_end - doffs <= dwin, axis=-1).astype(jnp.int32).reshape(-1)
    doffs = doffs.reshape(-1)

    gidx = idx + (jnp.arange(b, dtype=jnp.int32) * n_tok)[:, None, None]
    gidx = jnp.transpose(gidx, (1, 0, 2)).reshape(N_EXPERTS, b * cap)
    gts = jnp.transpose(gates, (1, 0, 2)).reshape(N_EXPERTS, b * cap)
    per_group = N_EXPERTS // EXPERT_GROUPS
    ys, dups = [], []
    for g in range(EXPERT_GROUPS):
        grp = slice(g * per_group, (g + 1) * per_group)
        xs = h.at[gidx[grp]].get(mode="promise_in_bounds")
        y, dup = expert_ffn(xs, gts[grp], doffs, w_gate, w_up, w_down, layer, g * per_group, cap, steps, dwin)
        ys.append(y)
        dups.append(dup)
    return moe_combine(x, gate2, slot, ys, dups, woffs, doffs, fits, n_tok, final_gain)


def _head_rms(t, seg_ones):
    outs = []
    for j in range(t.shape[1] // LANES):
        blk = t[:, j * LANES:(j + 1) * LANES]
        hi, lo = _split_bf16(blk * blk)
        ss = _dot(hi, seg_ones) + _dot(lo, seg_ones)
        outs.append(blk * lax.rsqrt(ss * (1.0 / HEAD_DIM) + EPS))
    return jnp.concatenate(outs, axis=1)


def _rope(t, cos, sin_signed):
    w = t.shape[1]
    half = HEAD_DIM // 2
    lane = lax.broadcasted_iota(jnp.int32, t.shape, 1)
    partner = jnp.where(lane % HEAD_DIM < half, pltpu.roll(t, w - half, 1), pltpu.roll(t, half, 1))
    reps = w // cos.shape[1]
    return t * jnp.concatenate([cos] * reps, axis=1) + partner * jnp.concatenate([sin_signed] * reps, axis=1)


def _qkv_kernel(*refs, qd, kd, rope):
    if rope:
        x_ref, sh_ref, sc_ref, w_ref, qg_ref, kg_ref, cos_ref, sin_ref, q_ref, k_ref, v_ref = refs
    else:
        x_ref, sh_ref, sc_ref, w_ref, qg_ref, kg_ref, q_ref, k_ref, v_ref = refs
    h = _modulate(x_ref[...], sh_ref[0], sc_ref[0]).astype(BF16)
    z = _dot(h, w_ref[...])
    seg = (lax.broadcasted_iota(jnp.int32, (LANES, LANES), 0) // HEAD_DIM
           == lax.broadcasted_iota(jnp.int32, (LANES, LANES), 1) // HEAD_DIM).astype(BF16)
    q = _head_rms(z[:, :qd], seg) * qg_ref[...]
    k = _head_rms(z[:, qd:qd + kd], seg) * kg_ref[...]
    if rope:
        q = _rope(q, cos_ref[...], sin_ref[...])
        k = _rope(k, cos_ref[...], sin_ref[...])
    q_ref[...] = (q * (HEAD_DIM ** -0.5 * LOG2E)).astype(q_ref.dtype)
    k_ref[...] = k.astype(k_ref.dtype)
    v_ref[...] = z[:, qd + kd:].astype(v_ref.dtype)


def qkv_project(x, shift, scale, w_qkv, q_gain, k_gain, n_tok, rope_tables=None, tm=512):
    r, d = x.shape
    kd = N_KV_HEADS * HEAD_DIM
    qd = w_qkv.shape[1] - 2 * kd
    tm = _row_tile(n_tok, tm)
    tpb = n_tok // tm
    ins = [x, shift, scale, w_qkv, q_gain, k_gain]
    specs = [
        pl.BlockSpec((tm, d), lambda i: (i, 0)),
        _mod_spec(shift, tpb),
        _mod_spec(scale, tpb),
        pl.BlockSpec(w_qkv.shape, lambda i: (0, 0)),
        pl.BlockSpec((1, qd), lambda i: (0, 0)),
        pl.BlockSpec((1, kd), lambda i: (0, 0)),
    ]
    if rope_tables is not None:
        ins += list(rope_tables)
        specs += [pl.BlockSpec((tm, LANES), lambda i: (i % tpb, 0))] * 2
    return pl.pallas_call(
        functools.partial(_qkv_kernel, qd=qd, kd=kd, rope=rope_tables is not None),
        out_shape=[jax.ShapeDtypeStruct((r, qd), BF16), jax.ShapeDtypeStruct((r, kd), BF16),
                   jax.ShapeDtypeStruct((r, kd), BF16)],
        grid=(r // tm,),
        in_specs=specs,
        out_specs=[pl.BlockSpec((tm, qd), lambda i: (i, 0)), pl.BlockSpec((tm, kd), lambda i: (i, 0)),
                   pl.BlockSpec((tm, kd), lambda i: (i, 0))],
        compiler_params=_cparams("parallel"),
        name="qkv_project",
    )(*ins)


def _attn_kernel(q_ref, k_ref, vt_ref, o_ref, *scr, grp, bounded):
    k = k_ref[0, 0]
    vt = vt_ref[0, 0]
    tq = q_ref.shape[0]
    n_pairs = grp // 2
    outs = []

    def scores(pair):
        heads = (2 * pair, 2 * pair + 1)
        q2 = jnp.concatenate([q_ref[:, h * HEAD_DIM:(h + 1) * HEAD_DIM] for h in heads], axis=0)
        return lax.dot_general(k, q2, (((1,), (1,)), ((), ())), preferred_element_type=F32)

    if bounded:
        for pair in range(n_pairs):
            p = jnp.exp2(scores(pair))
            scr[pair][...] = p.astype(BF16)
            ot = _dot(vt, scr[pair][...]) / jnp.sum(p, axis=0, keepdims=True)
            o = jnp.concatenate([ot, ot], axis=0).T
            outs += [o[:tq, :HEAD_DIM], o[tq:, :HEAD_DIM]]
        p_refs = ()
    else:
        for pair in range(n_pairs):
            scr[2 * pair][...] = scores(pair)
        for pair in range(n_pairs):
            st_scr, p_scr = scr[2 * pair], scr[2 * pair + 1]
            p_scr[...] = jnp.exp2(st_scr[...] - jnp.max(st_scr[...], axis=0, keepdims=True)).astype(BF16)
        p_refs = scr[1::2]
    for p_scr in p_refs:
        ot = _dot(vt, p_scr[...])
        ot = ot[:HEAD_DIM] / ot[HEAD_DIM:]
        o = jnp.concatenate([ot, ot], axis=0).T
        outs += [o[:tq, :HEAD_DIM], o[tq:, :HEAD_DIM]]
    o_ref[...] = jnp.concatenate(outs, axis=1).astype(o_ref.dtype)


def _attention_call(q, k, vt, n_q, tq, bounded):
    r, qd = q.shape
    _, kvh, n_k, hd = k.shape
    grp = qd // (kvh * hd)
    tq = _row_tile(n_q, tq)
    tpb = n_q // tq
    dts = (BF16,) if bounded else (F32, BF16)
    if not bounded:
        vt = jnp.concatenate([vt, jnp.ones_like(vt)], axis=2)
    return pl.pallas_call(
        functools.partial(_attn_kernel, grp=grp, bounded=bounded),
        out_shape=jax.ShapeDtypeStruct((r, qd), BF16),
        grid=(r // n_q, kvh, tpb),
        in_specs=[
            pl.BlockSpec((tq, grp * hd), lambda b, g, i: (b * tpb + i, g)),
            pl.BlockSpec((1, 1, n_k, hd), lambda b, g, i: (b, g, 0, 0)),
            pl.BlockSpec((1, 1, vt.shape[2], n_k), lambda b, g, i: (b, g, 0, 0)),
        ],
        out_specs=pl.BlockSpec((tq, grp * hd), lambda b, g, i: (b * tpb + i, g)),
        scratch_shapes=[pltpu.VMEM((n_k, 2 * tq), dt) for _ in range(grp // 2) for dt in dts],
        compiler_params=_cparams("parallel", "parallel", "parallel"),
        name="attention_bounded" if bounded else "attention",
    )(q, k, vt)


SCORE_BOUND = 60.0


def attention(q, k, vt, n_q, score_bound, tq_bounded=1024, tq_exact=256):
    return lax.cond(score_bound <= SCORE_BOUND,
                    lambda: _attention_call(q, k, vt, n_q, tq_bounded, True),
                    lambda: _attention_call(q, k, vt, n_q, tq_exact, False))


def rope_tables(n_tok):
    t = jnp.arange(n_tok)
    row = (t // GRID_W).astype(F32)
    col = (t % GRID_W).astype(F32)
    n_freq = HEAD_DIM // 4
    inv = ROPE_THETA ** (-jnp.arange(n_freq, dtype=F32) / n_freq)
    ang = jnp.concatenate([row[:, None] * inv, col[:, None] * inv], axis=-1)
    cos, sin = jnp.cos(ang), jnp.sin(ang)
    reps = LANES // HEAD_DIM
    return jnp.tile(jnp.concatenate([cos, cos], -1), (1, reps)), jnp.tile(jnp.concatenate([-sin, sin], -1), (1, reps))


def _deinterleave_heads(w, n_heads):
    lead = w.shape[:-1]
    w = w.reshape(lead + (n_heads, HEAD_DIM // 2, 2))
    return jnp.swapaxes(w, -1, -2).reshape(lead + (n_heads * HEAD_DIM,))


def _split_kv_heads(t, b):
    return jnp.transpose(t.reshape(b, -1, N_KV_HEADS, HEAD_DIM), (0, 2, 1, 3))


def gqa_residual(x, xc, mods_l, mods_c, w_qkv, q_gain, k_gain, w_o, n_tok, n_ctx):
    b = x.shape[0] // n_tok
    kd = N_KV_HEADS * HEAD_DIM
    qd = w_qkv.shape[1] - 2 * kd
    n_qh = qd // HEAD_DIM
    w_perm = jnp.concatenate([_deinterleave_heads(w_qkv[:, :qd], n_qh),
                              _deinterleave_heads(w_qkv[:, qd:qd + kd], N_KV_HEADS), w_qkv[:, qd + kd:]], axis=1)
    w_perm = w_perm.astype(BF16)
    qg = jnp.tile(_deinterleave_heads(q_gain, 1), n_qh).reshape(1, qd)
    kg = jnp.tile(_deinterleave_heads(k_gain, 1), N_KV_HEADS).reshape(1, kd)
    sh_l, sc_l, g_l = mods_l
    sh_c, sc_c, g_c = mods_c
    q_l, k_l, v_l = qkv_project(x, sh_l, sc_l, w_perm, qg, kg, n_tok, rope_tables(n_tok))
    q_c, k_c, v_c = qkv_project(xc, sh_c, sc_c, w_perm, qg, kg, n_ctx)
    k_c4 = _split_kv_heads(k_c, b)
    k_all = jnp.concatenate([k_c4, _split_kv_heads(k_l, b)], axis=2)

    def values_t(v):
        return jnp.transpose(v.reshape(b, -1, N_KV_HEADS, HEAD_DIM), (0, 2, 3, 1))

    vt_c = values_t(v_c)
    vt_all = jnp.concatenate([vt_c, values_t(v_l)], axis=3)
    score_bound = (1.01 * HEAD_DIM * HEAD_DIM ** -0.5 * LOG2E
                   * jnp.max(jnp.abs(q_gain)) * jnp.max(jnp.abs(k_gain)))
    o_l = attention(q_l, k_all, vt_all, n_tok, score_bound)
    o_c = attention(q_c, k_c4, vt_c, n_ctx, score_bound)
    w_o = w_o.astype(BF16)
    return mm_residual(o_l, w_o, x, g_l, n_tok), mm_residual(o_c, w_o, xc, g_c, n_ctx)


def _lru_kernel(xp_ref, x_ref, xn_ref, cw_ref, cb_ref, wa_ref, ba_ref, wx_ref, bx_ref, lam_ref, h0_ref,
                o_ref, hT_ref, a_scr, u_scr, carry_scr, *, tm, n_tok, tpb, reverse):
    step = pl.program_id(1)
    t_idx = (tpb - 1 - step) if reverse else step
    t0 = t_idx * tm
    rows = tm + 2 * SUBLANES

    @pl.when(step == 0)
    def _():
        carry_scr[...] = h0_ref[0]

    xe = jnp.concatenate([jnp.where(t0 > 0, xp_ref[...], 0.0), x_ref[...],
                          jnp.where(t0 + tm < n_tok, xn_ref[...], 0.0)], axis=0)
    left = CONV_W // 2
    conv = cb_ref[...]
    for k in range(CONV_W):
        shift = (left - k) % rows
        tap = xe if shift == 0 else pltpu.roll(xe, shift, 0)
        conv = conv + tap * cw_ref[k:k + 1, :]
    xr = conv[SUBLANES:SUBLANES + tm]

    xb = xr.astype(BF16)
    bw = xr.shape[1] // LRU_BLOCKS
    ra, ia = [], []
    for j in range(LRU_BLOCKS):
        blk = xb[:, j * bw:(j + 1) * bw]
        ra.append(_dot(blk, wa_ref[j]))
        ia.append(_dot(blk, wx_ref[j]))
    r = _sigmoid(jnp.concatenate(ra, axis=1) + ba_ref[...])
    i = _sigmoid(jnp.concatenate(ia, axis=1) + bx_ref[...])
    log_a = -LRU_C * r * jnp.logaddexp(-lam_ref[...], 0.0)
    a = jnp.exp(log_a)
    a_scr[...] = a
    t = jnp.tanh(log_a)
    u_scr[...] = xr * i * jnp.sqrt(-2.0 * t / (1.0 - t))

    n_grp = tm // SUBLANES
    sub = lax.broadcasted_iota(jnp.int32, (SUBLANES, 1), 0)

    def group(j, carry):
        g = (n_grp - 1 - j) if reverse else j
        rws = pl.ds(pl.multiple_of(g * SUBLANES, SUBLANES), SUBLANES)
        ag, ug = a_scr[rws, :], u_scr[rws, :]
        s = 1
        while s < SUBLANES:
            if reverse:
                ok = sub < SUBLANES - s
                sh = SUBLANES - s
            else:
                ok = sub >= s
                sh = s
            u_prev = jnp.where(ok, pltpu.roll(ug, sh, 0), 0.0)
            a_prev = jnp.where(ok, pltpu.roll(ag, sh, 0), 1.0)
            ug = ug + ag * u_prev
            ag = ag * a_prev
            s *= 2
        hg = ug + ag * carry
        o_ref[rws, :] = hg
        return hg[0:1, :] if reverse else hg[SUBLANES - 1:SUBLANES, :]

    carry = lax.fori_loop(0, n_grp, group, carry_scr[...])
    carry_scr[...] = carry
    hT_ref[0] = carry


def lru_scan(xpre, conv_w, conv_b, wa, ba, wx, bx, lam, h0, n_tok, reverse, tm=512):
    r, w = xpre.shape
    b = r // n_tok
    tm = _row_tile(n_tok, tm)
    tpb = n_tok // tm
    hb = tm // SUBLANES
    last = r // SUBLANES - 1

    def tile(bi, s):
        return bi * tpb + ((tpb - 1 - s) if reverse else s)

    vec = pl.BlockSpec((1, w), lambda bi, s: (0, 0))
    blocks = pl.BlockSpec(wa.shape, lambda bi, s: (0, 0, 0))
    return pl.pallas_call(
        functools.partial(_lru_kernel, tm=tm, n_tok=n_tok, tpb=tpb, reverse=reverse),
        out_shape=[jax.ShapeDtypeStruct((r, w), F32), jax.ShapeDtypeStruct((b, 1, w), F32)],
        grid=(b, tpb),
        in_specs=[
            pl.BlockSpec((SUBLANES, w), lambda bi, s: (jnp.maximum(tile(bi, s) * hb - 1, 0), 0)),
            pl.BlockSpec((tm, w), lambda bi, s: (tile(bi, s), 0)),
            pl.BlockSpec((SUBLANES, w), lambda bi, s: (jnp.minimum((tile(bi, s) + 1) * hb, last), 0)),
            pl.BlockSpec((CONV_W, w), lambda bi, s: (0, 0)),
            vec, blocks, vec, blocks, vec, vec,
            pl.BlockSpec((1, 1, w), lambda bi, s: (bi, 0, 0)),
        ],
        out_specs=[pl.BlockSpec((tm, w), lambda bi, s: (tile(bi, s), 0)),
                   pl.BlockSpec((1, 1, w), lambda bi, s: (bi, 0, 0))],
        scratch_shapes=[pltpu.VMEM((tm, w), F32), pltpu.VMEM((tm, w), F32), pltpu.VMEM((1, w), F32)],
        compiler_params=_cparams("parallel", "arbitrary"),
        name="lru_scan",
    )(xpre, xpre, xpre, conv_w, conv_b.reshape(1, w), wa, ba.reshape(1, w), wx, bx.reshape(1, w),
      lam.reshape(1, w), h0)


def rglru_residual(x, xc, mods_l, mods_c, w_in, conv_w, conv_b, wa, ba, wx, bx, lam, w_out, n_tok, n_ctx):
    b = x.shape[0] // n_tok
    w = w_in.shape[1] // 2
    w_in = w_in.astype(BF16)
    sh_l, sc_l, g_l = mods_l
    sh_c, sc_c, g_c = mods_c
    y_l, xp_l = modmm(x, sh_l, sc_l, w_in, n_tok, (w, w), ("gelu", None), (F32, F32))
    y_c, xp_c = modmm(xc, sh_c, sc_c, w_in, n_ctx, (w, w), ("gelu", None), (F32, F32))
    zero = jnp.zeros((b, 1, w), F32)
    hs_l, hs_c = [], []
    for d in range(2):
        gate_w = (conv_w, conv_b, wa[d].astype(BF16), ba[d], wx[d].astype(BF16), bx[d], lam[d])
        hc, state = lru_scan(xp_c, *gate_w, zero, n_ctx, reverse=d == 1)
        hl, _ = lru_scan(xp_l, *gate_w, state, n_tok, reverse=d == 1)
        hs_c.append(hc)
        hs_l.append(hl)
    w_out = w_out.astype(BF16)
    return (mm_residual(tuple(hs_l), w_out, x, g_l, n_tok, b=y_l),
            mm_residual(tuple(hs_c), w_out, xc, g_c, n_ctx, b=y_c))


def _gmlp_kernel(x_ref, sh_ref, sc_ref, g_ref, win_ref, lng_ref, lnb_ref, ws_ref, bs_ref, wout_ref, o_ref, *, half):
    x = x_ref[...]
    h = _modulate(x, sh_ref[0], sc_ref[0]).astype(BF16)
    u = _gelu(_dot(h, win_ref[:, :half]))
    v = _gelu(_dot(h, win_ref[:, half:]))
    mu = jnp.mean(v, axis=-1, keepdims=True)
    vc = v - mu
    var = jnp.mean(vc * vc, axis=-1, keepdims=True)
    vn = (vc * lax.rsqrt(var + EPS) * lng_ref[...] + lnb_ref[...]).astype(BF16)
    gw = half // GMLP_GROUPS
    chunks = []
    for c in range(x.shape[0] // GMLP_CHUNK):
        rws = slice(c * GMLP_CHUNK, (c + 1) * GMLP_CHUNK)
        chunks.append(jnp.concatenate(
            [_dot(ws_ref[g], vn[rws, g * gw:(g + 1) * gw]) + bs_ref[g] for g in range(GMLP_GROUPS)], axis=1))
    v2 = jnp.concatenate(chunks, axis=0)
    o_ref[...] = x + g_ref[0] * _dot((u * v2).astype(BF16), wout_ref[...])


def gmlp_residual(x, shift, scale, gate, w_in, ln_g, ln_b, w_s, b_s, w_out, n_tok, tm=512):
    r, d = x.shape
    half = w_in.shape[1] // 2
    tm = _row_tile(n_tok, tm)
    tpb = n_tok // tm
    assert tm % GMLP_CHUNK == 0
    const2 = lambda i: (0, 0)
    const3 = lambda i: (0, 0, 0)
    once = pl.Buffered(1)
    return pl.pallas_call(
        functools.partial(_gmlp_kernel, half=half),
        out_shape=jax.ShapeDtypeStruct((r, d), F32),
        grid=(r // tm,),
        in_specs=[
            pl.BlockSpec((tm, d), lambda i: (i, 0)),
            _mod_spec(shift, tpb),
            _mod_spec(scale, tpb),
            _mod_spec(gate, tpb),
            pl.BlockSpec(w_in.shape, const2, pipeline_mode=once),
            pl.BlockSpec((1, half), const2),
            pl.BlockSpec((1, half), const2),
            pl.BlockSpec(w_s.shape, const3),
            pl.BlockSpec(b_s.shape + (1,), const3),
            pl.BlockSpec(w_out.shape, const2, pipeline_mode=once),
        ],
        out_specs=pl.BlockSpec((tm, d), lambda i: (i, 0)),
        compiler_params=_cparams("parallel"),
        name="gmlp_mixer",
    )(x, shift, scale, gate, w_in.astype(BF16), ln_g.reshape(1, half), ln_b.reshape(1, half),
      w_s.astype(BF16), b_s[..., None], w_out.astype(BF16))


def kernel(x, c, ctx, c_ctx, mod_w, mod_b, pool_w, pool_scale, lru_w_in, lru_conv_w, lru_conv_b, lru_wa, lru_ba,
           lru_wx, lru_bx, lru_lam, lru_w_out, attn_w_qkv, attn_q_gain, attn_k_gain, attn_w_o, gmlp_w_in, gmlp_ln_g,
           gmlp_ln_b, gmlp_w_s, gmlp_b_s, gmlp_w_out, moe_router, moe_w_gate, moe_w_up, moe_w_down, final_gain):
    b, n_tok, d = x.shape
    n_ctx = ctx.shape[1]
    depth = mod_w.shape[0]
    n_mixers = 4
    xl = x.reshape(b * n_tok, d)
    xc = ctx.reshape(b * n_ctx, d)

    pad = -(b + 1) % SUBLANES
    cvec = jnp.concatenate([c, c_ctx[None, :], jnp.zeros((pad, d), F32)], axis=0)
    mods = mod_vectors(cvec, mod_w, mod_b)

    for i in range(depth):
        m, j = i % n_mixers, i // n_mixers
        last = i == depth - 1
        ml = [mods[i, :b, k * d:(k + 1) * d].reshape(b, 1, d) for k in range(6)]
        mc = [mods[i, b:b + 1, k * d:(k + 1) * d].reshape(1, 1, d) for k in range(6)]
        if m == 0:
            pw = pool_w[j].astype(BF16)
            xl_new = pool_mixer_residual(xl, ml[0], ml[1], ml[2], pw, pool_scale[j], n_tok)
            if not last:
                xc = pool_mixer_residual(xc, mc[0], mc[1], mc[2], pw, pool_scale[j], n_ctx)
            xl = xl_new
        elif m == 1:
            xl, xc_new = rglru_residual(xl, xc, ml[:3], mc[:3], lru_w_in[j], lru_conv_w[j], lru_conv_b[j], lru_wa[j],
                                        lru_ba[j], lru_wx[j], lru_bx[j], lru_lam[j], lru_w_out[j], n_tok, n_ctx)
            xc = xc if last else xc_new
        elif m == 2:
            xl, xc_new = gqa_residual(xl, xc, ml[:3], mc[:3], attn_w_qkv[j], attn_q_gain[j], attn_k_gain[j],
                                      attn_w_o[j], n_tok, n_ctx)
            xc = xc if last else xc_new
        else:
            gargs = (gmlp_w_in[j], gmlp_ln_g[j], gmlp_ln_b[j], gmlp_w_s[j], gmlp_b_s[j], gmlp_w_out[j])
            xl_new = gmlp_residual(xl, ml[0], ml[1], ml[2], *gargs, n_tok)
            if not last:
                xc = gmlp_residual(xc, mc[0], mc[1], mc[2], *gargs, n_ctx)
            xl = xl_new
        wr = jnp.pad(moe_router[i], ((0, 0), (0, LANES - N_EXPERTS)))
        xl = moe_residual(xl, ml[3], ml[4], ml[5], wr, moe_w_gate, moe_w_up, moe_w_down, i, n_tok,
                          final_gain if last else None)
        if not last:
            xc = moe_residual(xc, mc[3], mc[4], mc[5], wr, moe_w_gate, moe_w_up, moe_w_down, i, n_ctx)
    return xl.reshape(b, n_tok, d)
```
